```python
import math
import jax, jax.numpy as jnp
from jax import lax
import numpy as np

D_MODEL = 2048
BATCH = 16
SEQ = 256
DEPTH = 4
DEC_BATCH = 2
DEC_SEQ = 1024
PAST_LEN = 256

GRID_W = 64
Q_BLOCK = 128
TOKEN_BLOCK = 128
ROPE_BASE = 10000.0
EPS = 1e-6

MLA_HEADS = 8
MLA_NOPE = 128
MLA_ROPE = 64
MLA_V = 128
MLA_Q_RANK = 512
MLA_KV_RANK = 256

POOL_WINDOWS = (2, 4, 8, 16)
POOL_GROUP = 128
POOL_WIDTH = POOL_GROUP * len(POOL_WINDOWS)

DIFF_HEADS = 4
DIFF_QK = 64
DIFF_V = 2 * DIFF_QK

N_BRANCH = 3

PEER_HEADS = 8
PEER_KEYS = 128
PEER_EXPERTS = PEER_KEYS * PEER_KEYS
PEER_TOPK = 16
PEER_QDIM = 256
PEER_HALF = PEER_QDIM // 2

W_Q_A = MLA_Q_RANK
W_KV_A = MLA_KV_RANK + MLA_ROPE
W_DQ = DIFF_HEADS * 2 * DIFF_QK
W_DK = DIFF_HEADS * 2 * DIFF_QK
W_DV = DIFF_HEADS * DIFF_V
W_GATE = N_BRANCH * D_MODEL
IN_WIDTH = W_Q_A + W_KV_A + POOL_WIDTH + W_DQ + W_DK + W_DV + W_GATE
IN_SPLITS = (W_Q_A, W_Q_A + W_KV_A, W_Q_A + W_KV_A + POOL_WIDTH, W_Q_A + W_KV_A + POOL_WIDTH + W_DQ, W_Q_A + W_KV_A + POOL_WIDTH + W_DQ + W_DK, W_Q_A + W_KV_A + POOL_WIDTH + W_DQ + W_DK + W_DV)

kernel_name = 'hybrid_flow_trunk_ctx_and_denoise'


def rmsnorm(x, g):
    xf = x.astype(jnp.float32)
    y = xf * lax.rsqrt(jnp.mean(xf * xf, axis=-1, keepdims=True) + EPS)
    return (y * g.astype(jnp.float32)).astype(x.dtype)


def rope_1d(x, pos):
    half = x.shape[-1] // 2
    inv = ROPE_BASE ** (-jnp.arange(half, dtype=jnp.float32) / half)
    ang = pos.astype(jnp.float32)[:, None] * inv
    cos = jnp.cos(ang)[None, :, None, :]
    sin = jnp.sin(ang)[None, :, None, :]
    xf = x.astype(jnp.float32)
    x1, x2 = xf[..., :half], xf[..., half:]
    return jnp.concatenate([x1 * cos - x2 * sin, x1 * sin + x2 * cos], axis=-1).astype(x.dtype)


def rope_2d(x, pos):
    if pos is None:
        return x
    row, col = pos
    r = x.shape[-1] // 2
    return jnp.concatenate([rope_1d(x[..., :r], row), rope_1d(x[..., r:], col)], axis=-1)


def attention(q, k, v, scale):
    B, Tq, H, dk = q.shape
    blk = min(Q_BLOCK, Tq)
    nb = Tq // blk
    qb = q.reshape(B, nb, blk, H, dk).swapaxes(0, 1)

    def one(qi):
        s = jnp.einsum('bqhd,bkhd->bhqk', qi, k, preferred_element_type=jnp.float32) * scale
        p = jax.nn.softmax(s, axis=-1).astype(v.dtype)
        return jnp.einsum('bhqk,bkhd->bqhd', p, v)

    o = lax.map(one, qb)
    return o.swapaxes(0, 1).reshape(B, Tq, H, v.shape[-1])


def mla_branch(qa, kva, lp, pos, ctx_ckv, ctx_krope):
    B, T, _ = qa.shape
    q = (rmsnorm(qa, lp['g_qnorm']) @ lp['w_qb']).reshape(B, T, MLA_HEADS, MLA_NOPE + MLA_ROPE)
    q_nope, q_rope = q[..., :MLA_NOPE], rope_2d(q[..., MLA_NOPE:], pos)
    ckv = rmsnorm(kva[..., :MLA_KV_RANK], lp['g_kvnorm'])
    krope = kva[..., MLA_KV_RANK:][:, :, None, :]
    krope_rot = rope_2d(krope, pos)
    if ctx_ckv is None:
        ckv_all, krope_all = ckv, krope_rot
    else:
        ckv_all = jnp.concatenate([ctx_ckv, ckv], axis=1)
        krope_all = jnp.concatenate([ctx_krope[:, :, None, :], krope_rot], axis=1)
    kv = (ckv_all @ lp['w_kvb']).reshape(B, ckv_all.shape[1], MLA_HEADS, MLA_NOPE + MLA_V)
    Tk = kv.shape[1]
    k = jnp.concatenate([kv[..., :MLA_NOPE], jnp.broadcast_to(krope_all, (B, Tk, MLA_HEADS, MLA_ROPE))], axis=-1)
    o = attention(jnp.concatenate([q_nope, q_rope], axis=-1), k, kv[..., MLA_NOPE:], (MLA_NOPE + MLA_ROPE) ** -0.5)
    return o.reshape(B, T, MLA_HEADS * MLA_V), ckv, krope[:, :, 0, :]


def multi_scale_pool(z, w_pool, pool_scale):
    B, T, _ = z.shape
    zf = z.astype(jnp.float32)
    csum = jnp.concatenate([jnp.zeros((B, 1, POOL_WIDTH), jnp.float32), jnp.cumsum(zf, axis=1)], axis=1)
    t = jnp.arange(T)
    parts = []
    for gi, w in enumerate(POOL_WINDOWS):
        lo = jnp.maximum(t - w // 2, 0)
        hi = jnp.minimum(t + (w - 1) // 2, T - 1)
        sl = slice(gi * POOL_GROUP, (gi + 1) * POOL_GROUP)
        cg = csum[..., sl]
        mean = (cg[:, hi + 1] - cg[:, lo]) / (hi - lo + 1).astype(jnp.float32)[None, :, None]
        parts.append(mean - zf[..., sl])
    d = jnp.stack(parts, axis=2).astype(z.dtype)
    y = jnp.einsum('btgc,gcd->btgd', d, w_pool).reshape(B, T, POOL_WIDTH)
    return y * pool_scale


def diff_branch(dq, dk, dv, lp, l, pos, ctx_k, ctx_v):
    B, T, _ = dq.shape
    q = rope_2d(dq.reshape(B, T, 2 * DIFF_HEADS, DIFF_QK), pos).reshape(B, T, DIFF_HEADS, 2 * DIFF_QK)
    k_raw = dk.reshape(B, T, DIFF_HEADS, 2 * DIFF_QK)
    k = rope_2d(k_raw.reshape(B, T, 2 * DIFF_HEADS, DIFF_QK), pos).reshape(B, T, DIFF_HEADS, 2 * DIFF_QK)
    v = dv.reshape(B, T, DIFF_HEADS, DIFF_V)
    if ctx_k is None:
        v_all = v
    else:
        k = jnp.concatenate([ctx_k, k], axis=1)
        v_all = jnp.concatenate([ctx_v, v], axis=1)
    lam_init = 0.8 - 0.6 * math.exp(-0.3 * l)
    lv = lp['diff_lambda'].astype(jnp.float32)
    lam = jnp.exp(jnp.sum(lv[0] * lv[1])) - jnp.exp(jnp.sum(lv[2] * lv[3])) + lam_init
    scale = DIFF_QK ** -0.5
    a1 = attention(q[..., :DIFF_QK], k[..., :DIFF_QK], v_all, scale)
    a2 = attention(q[..., DIFF_QK:], k[..., DIFF_QK:], v_all, scale)
    o = a1.astype(jnp.float32) - lam * a2.astype(jnp.float32)
    o = rmsnorm(o, lp['g_diffnorm']) * (1.0 - lam_init)
    return o.reshape(B, T, DIFF_HEADS * DIFF_V).astype(dq.dtype), k_raw, v


def peer_ffn(h, lp):
    B, T, D = h.shape
    n = B * T
    hf = h.reshape(n, D)
    q = (hf @ lp['w_peer_q']).reshape(n, PEER_HEADS, 2, PEER_HALF)
    s = jnp.einsum('nhpc,pkc->nhpk', q, lp['peer_subkeys'], preferred_element_type=jnp.float32)
    s1, i1 = lax.top_k(s[:, :, 0], PEER_TOPK)
    s2, i2 = lax.top_k(s[:, :, 1], PEER_TOPK)
    cand = (s1[..., :, None] + s2[..., None, :]).reshape(n, PEER_HEADS, PEER_TOPK * PEER_TOPK)
    cidx = (i1[..., :, None] * PEER_KEYS + i2[..., None, :]).reshape(n, PEER_HEADS, PEER_TOPK * PEER_TOPK)
    best, sel = lax.top_k(cand, PEER_TOPK)
    idx = jnp.take_along_axis(cidx, sel, axis=-1)
    g = jax.nn.softmax(best, axis=-1)
    nb = n // TOKEN_BLOCK
    u_tab, v_tab = lp['peer_u'], lp['peer_v']

    def one(args):
        ht, it, gt = args
        act = jax.nn.gelu(jnp.einsum('tkd,td->tk', u_tab[it], ht, preferred_element_type=jnp.float32), approximate=False)
        return jnp.einsum('tk,tkd->td', (gt * act).astype(h.dtype), v_tab[it])

    out = lax.map(one, (hf.reshape(nb, TOKEN_BLOCK, D), idx.reshape(nb, TOKEN_BLOCK, PEER_HEADS * PEER_TOPK), g.reshape(nb, TOKEN_BLOCK, PEER_HEADS * PEER_TOPK)))
    return out.reshape(B, T, D)


def trunk_layer(x, cond, l, pos, ctx, lp):
    B, T, _ = x.shape
    mod = (jax.nn.silu(cond) @ lp['w_mod'] + lp['b_mod'])[:, None, :]
    sh1, sc1, ga1, sh2, sc2, ga2 = jnp.split(mod, 6, axis=-1)
    h = rmsnorm(x, lp['g_norm1']) * (1 + sc1) + sh1
    qa, kva, pz, dq, dk, dv, gl = jnp.split(h @ lp['w_in'], IN_SPLITS, axis=-1)
    if ctx is None:
        c_ckv = c_krope = c_k = c_v = None
    else:
        c_ckv, c_krope, c_k, c_v = ctx
    mla_o, ckv, krope = mla_branch(qa, kva, lp, pos, c_ckv, c_krope)
    pool_o = multi_scale_pool(pz, lp['w_pool'], lp['pool_scale'])
    diff_o, k_raw, v_raw = diff_branch(dq, dk, dv, lp, l, pos, c_k, c_v)
    gates = jax.nn.sigmoid(gl.reshape(B, T, N_BRANCH, D_MODEL))
    merged = (gates[:, :, 0] * (mla_o @ lp['w_br_mla'])
              + gates[:, :, 1] * (pool_o @ lp['w_br_pool'])
              + gates[:, :, 2] * (diff_o @ lp['w_br_diff']))
    x = x + ga1 * (merged @ lp['w_out'])
    h2 = rmsnorm(x, lp['g_norm2']) * (1 + sc2) + sh2
    x = x + ga2 * peer_ffn(h2, lp)
    return x, (ckv, krope, k_raw, v_raw)


def setup_inputs(seed: int = 0) -> dict:
    key = jax.random.key(seed)
    ks = jax.random.split(key, 32)
    D = D_MODEL

    def nrm(k, shape, scale=1.0):
        return jax.random.normal(k, shape, jnp.float32) * scale

    def gain(k, shape):
        return 1.0 + 0.02 * jax.random.normal(k, shape, jnp.float32)

    return {
        'x_prompt': nrm(ks[0], (BATCH, SEQ, D)),
        'x_sample': nrm(ks[1], (DEC_BATCH, DEC_SEQ, D)),
        'cache_mla_ckv': nrm(ks[2], (DEC_BATCH, DEPTH, PAST_LEN, MLA_KV_RANK)),
        'cache_mla_krope': nrm(ks[3], (DEC_BATCH, DEPTH, PAST_LEN, MLA_ROPE)),
        'cache_diff_k': nrm(ks[4], (DEC_BATCH, DEPTH, PAST_LEN, DIFF_HEADS, 2 * DIFF_QK)),
        'cache_diff_v': nrm(ks[5], (DEC_BATCH, DEPTH, PAST_LEN, DIFF_HEADS, DIFF_V)),
        'c': nrm(ks[6], (DEC_BATCH, D)),
        'c_ctx': nrm(ks[7], (D,)),
        'w_mod': nrm(ks[8], (DEPTH, D, 6 * D), 0.5 * D ** -0.5),
        'b_mod': nrm(ks[9], (DEPTH, 6 * D), 0.02),
        'g_norm1': gain(ks[10], (DEPTH, D)),
        'w_in': nrm(ks[11], (DEPTH, D, IN_WIDTH), D ** -0.5),
        'g_qnorm': gain(ks[12], (DEPTH, MLA_Q_RANK)),
        'w_qb': nrm(ks[13], (DEPTH, MLA_Q_RANK, MLA_HEADS * (MLA_NOPE + MLA_ROPE)), MLA_Q_RANK ** -0.5),
        'g_kvnorm': gain(ks[14], (DEPTH, MLA_KV_RANK)),
        'w_kvb': nrm(ks[15], (DEPTH, MLA_KV_RANK, MLA_HEADS * (MLA_NOPE + MLA_V)), MLA_KV_RANK ** -0.5),
        'w_pool': nrm(ks[16], (DEPTH, len(POOL_WINDOWS), POOL_GROUP, POOL_GROUP), POOL_GROUP ** -0.5),
        'pool_scale': gain(ks[17], (DEPTH, POOL_WIDTH)),
        'diff_lambda': nrm(ks[18], (DEPTH, 4, DIFF_QK), 0.1),
        'g_diffnorm': gain(ks[19], (DEPTH, DIFF_V)),
        'w_br_mla': nrm(ks[20], (DEPTH, MLA_HEADS * MLA_V, D), (MLA_HEADS * MLA_V) ** -0.5),
        'w_br_pool': nrm(ks[21], (DEPTH, POOL_WIDTH, D), POOL_WIDTH ** -0.5),
        'w_br_diff': nrm(ks[22], (DEPTH, DIFF_HEADS * DIFF_V, D), (DIFF_HEADS * DIFF_V) ** -0.5),
        'w_out': nrm(ks[23], (DEPTH, D, D), D ** -0.5),
        'g_norm2': gain(ks[24], (DEPTH, D)),
        'w_peer_q': nrm(ks[25], (DEPTH, D, PEER_HEADS * PEER_QDIM), D ** -0.5),
        'peer_subkeys': nrm(ks[26], (DEPTH, 2, PEER_KEYS, PEER_HALF), PEER_HALF ** -0.5),
        'peer_u': nrm(ks[27], (DEPTH, PEER_EXPERTS, D), D ** -0.5),
        'peer_v': nrm(ks[28], (DEPTH, PEER_EXPERTS, D), PEER_HEADS ** -0.5),
        'g_final': gain(ks[29], (D,)),
    }


def reference(x_prompt, x_sample, cache_mla_ckv, cache_mla_krope, cache_diff_k, cache_diff_v, c, c_ctx,
              w_mod, b_mod, g_norm1, w_in, g_qnorm, w_qb, g_kvnorm, w_kvb, w_pool, pool_scale,
              diff_lambda, g_diffnorm, w_br_mla, w_br_pool, w_br_diff, w_out, g_norm2,
              w_peer_q, peer_subkeys, peer_u, peer_v, g_final):
    rows = x_sample.shape[1] // GRID_W
    pos = (jnp.repeat(jnp.arange(rows), GRID_W), jnp.tile(jnp.arange(GRID_W), rows))
    xp, xs = x_prompt, x_sample
    st_ckv, st_krope, st_k, st_v = [], [], [], []
    for l in range(DEPTH):
        lp = {
            'w_mod': w_mod[l], 'b_mod': b_mod[l], 'g_norm1': g_norm1[l], 'w_in': w_in[l],
            'g_qnorm': g_qnorm[l], 'w_qb': w_qb[l], 'g_kvnorm': g_kvnorm[l], 'w_kvb': w_kvb[l],
            'w_pool': w_pool[l], 'pool_scale': pool_scale[l], 'diff_lambda': diff_lambda[l],
            'g_diffnorm': g_diffnorm[l], 'w_br_mla': w_br_mla[l], 'w_br_pool': w_br_pool[l],
            'w_br_diff': w_br_diff[l], 'w_out': w_out[l], 'g_norm2': g_norm2[l],
            'w_peer_q': w_peer_q[l], 'peer_subkeys': peer_subkeys[l], 'peer_u': peer_u[l], 'peer_v': peer_v[l],
        }
        xp, st = trunk_layer(xp, c_ctx[None, :], l, None, None, lp)
        st_ckv.append(st[0])
        st_krope.append(st[1])
        st_k.append(st[2])
        st_v.append(st[3])
        ctx = (cache_mla_ckv[:, l], cache_mla_krope[:, l], cache_diff_k[:, l], cache_diff_v[:, l])
        xs, _ = trunk_layer(xs, c, l, pos, ctx, lp)
    y_prompt = rmsnorm(xp, g_final)
    y_sample = rmsnorm(xs, g_final)
    state_mla_ckv = jnp.stack(st_ckv, axis=1)
    state_mla_krope = jnp.stack(st_krope, axis=1)
    state_diff_k = jnp.stack(st_k, axis=1)
    state_diff_v = jnp.stack(st_v, axis=1)
    return (y_prompt, y_sample, state_mla_ckv, state_mla_krope, state_diff_k, state_diff_v)
```

```python
import functools
import math

import jax
import jax.numpy as jnp
from jax import lax
from jax.experimental import pallas as pl
from jax.experimental.pallas import tpu as pltpu

BF = jnp.bfloat16
F32 = jnp.float32

D_MODEL = 2048
BATCH = 16
SEQ = 256
DEPTH = 4
DEC_BATCH = 2
DEC_SEQ = 1024
PAST_LEN = 256
GRID_W = 64
ROPE_BASE = 10000.0
EPS = 1e-6

MLA_HEADS = 8
MLA_NOPE = 128
MLA_ROPE = 64
MLA_V = 128
MLA_Q_RANK = 512
MLA_KV_RANK = 256
MLA_HEAD_PAD = 256

POOL_WINDOWS = (2, 4, 8, 16)
POOL_GROUP = 128
POOL_WIDTH = POOL_GROUP * len(POOL_WINDOWS)

DIFF_HEADS = 4
DIFF_QK = 64
DIFF_V = 2 * DIFF_QK
DIFF_WIDTH = DIFF_HEADS * DIFF_V

PEER_HEADS = 8
PEER_KEYS = 128
PEER_EXPERTS = PEER_KEYS * PEER_KEYS
PEER_TOPK = 16
PEER_QDIM = 256

N_CTX = BATCH * SEQ
N_DEC = DEC_BATCH * DEC_SEQ
N_TOK = N_CTX + N_DEC
KEYS_DEC = PAST_LEN + DEC_SEQ

C_QA = 0
C_CKV = 512
C_KR = 768
C_PZ = 1024
C_DQ = 1536
C_DK = 2048
C_DV = 2560
SMALL_WIDTH = 3072
GATE_WIDTH = 3 * D_MODEL

VMEM_LIMIT = 56 * 1024 * 1024


def _cparams(sem):
    return pltpu.CompilerParams(dimension_semantics=sem, vmem_limit_bytes=VMEM_LIMIT)


def _segment(row0):
    return jnp.where(row0 < N_CTX, 0, 1 + (row0 - N_CTX) // DEC_SEQ)


def _rms(x, g):
    return x * lax.rsqrt(jnp.mean(x * x, axis=-1, keepdims=True) + EPS) * g


def _softmax_rows(s):
    m = jnp.max(s, axis=-1, keepdims=True)
    p = jnp.exp(s - m)
    return p / jnp.sum(p, axis=-1, keepdims=True)


def _dot_nt(a, b):
    return lax.dot_general(a, b, (((1,), (1,)), ((), ())), preferred_element_type=F32)


def _rope_swap(x):
    n = x.shape[-1]
    up = pltpu.roll(x, n - 16, 1)
    dn = pltpu.roll(x, 16, 1)
    lane = lax.broadcasted_iota(jnp.int32, x.shape, 1)
    return jnp.where((lane & 16) == 0, up, dn)


def _mod_kernel(c_ref, w_ref, b_ref, o_ref):
    c = c_ref[...]
    a = (c * jax.nn.sigmoid(c)).astype(BF)
    o_ref[...] = jnp.dot(a, w_ref[...].astype(BF), preferred_element_type=F32) + b_ref[...]


def _modulation(cond8, w_mod, b_mod):
    tn = 1024
    return pl.pallas_call(
        _mod_kernel,
        grid=(DEPTH, 6 * D_MODEL // tn),
        in_specs=[
            pl.BlockSpec((8, D_MODEL), lambda l, j: (0, 0)),
            pl.BlockSpec((None, D_MODEL, tn), lambda l, j: (l, 0, j)),
            pl.BlockSpec((None, 1, tn), lambda l, j: (l, 0, j)),
        ],
        out_specs=pl.BlockSpec((None, 8, tn), lambda l, j: (l, 0, j)),
        out_shape=jax.ShapeDtypeStruct((DEPTH, 8, 6 * D_MODEL), F32),
        compiler_params=_cparams(("arbitrary", "arbitrary")),
        name="modulation",
    )(cond8, w_mod, b_mod.reshape(DEPTH, 1, 6 * D_MODEL))


def _inproj_kernel(x_ref, g_ref, sh_ref, sc_ref, w_ref, o_ref, h_ref, *, tm):
    i = pl.program_id(0)

    @pl.when(pl.program_id(1) == 0)
    def _():
        seg = _segment(i * tm)
        y = _rms(x_ref[...], g_ref[...])
        h_ref[...] = (y * (1.0 + sc_ref[pl.ds(seg, 1), :]) + sh_ref[pl.ds(seg, 1), :]).astype(BF)

    o_ref[...] = jnp.dot(h_ref[...], w_ref[...], preferred_element_type=F32).astype(o_ref.dtype)


def _in_projection(x, g_norm1, mod, w_in_re, l, col0, width, out_dtype):
    tm, tn = 1024, 512
    off = col0 // tn
    return pl.pallas_call(
        functools.partial(_inproj_kernel, tm=tm),
        grid=(N_TOK // tm, width // tn),
        in_specs=[
            pl.BlockSpec((tm, D_MODEL), lambda i, j: (i, 0)),
            pl.BlockSpec((None, 1, D_MODEL), lambda i, j: (l, 0, 0)),
            pl.BlockSpec((None, 8, D_MODEL), lambda i, j: (l, 0, 0)),
            pl.BlockSpec((None, 8, D_MODEL), lambda i, j: (l, 0, 1)),
            pl.BlockSpec((None, D_MODEL, tn), lambda i, j: (l, 0, j + off)),
        ],
        out_specs=pl.BlockSpec((tm, tn), lambda i, j: (i, j)),
        out_shape=jax.ShapeDtypeStruct((N_TOK, width), out_dtype),
        scratch_shapes=[pltpu.VMEM((tm, D_MODEL), BF)],
        compiler_params=_cparams(("arbitrary", "arbitrary")),
        name="in_projection",
    )(x, g_norm1.reshape(DEPTH, 1, D_MODEL), mod, mod, w_in_re)


def _mla_heads(q, kfun, vfun, o_ref, scale):
    for h in range(MLA_HEADS):
        s = _dot_nt(q(h), kfun(h)) * scale
        p = _softmax_rows(s).astype(BF)
        o = jnp.dot(p, vfun(h), preferred_element_type=F32)
        o_ref[:, h * MLA_V:(h + 1) * MLA_V] = o.astype(o_ref.dtype)


def _mla_ctx_kernel(qa_ref, ckv_ref, kr_ref, gq_ref, gkv_ref, wqb_ref, wkvb_ref,
                    o_ref, ckv_out_ref, kr_out_ref):
    scale = (MLA_NOPE + MLA_ROPE) ** -0.5
    qn = _rms(qa_ref[...], gq_ref[...]).astype(BF)
    q = jnp.dot(qn, wqb_ref[...], preferred_element_type=F32).astype(BF)
    ckv = _rms(ckv_ref[...], gkv_ref[...])
    ckv_out_ref[...] = ckv
    kr = kr_ref[...]
    kr_out_ref[...] = kr[:, :MLA_ROPE]
    kv = jnp.dot(ckv.astype(BF), wkvb_ref[...], preferred_element_type=F32).astype(BF)
    krp = kr[:, :128].astype(BF)

    def qh(h):
        return q[:, h * MLA_HEAD_PAD:(h + 1) * MLA_HEAD_PAD]

    def kh(h):
        return jnp.concatenate([kv[:, h * 256:h * 256 + MLA_NOPE], krp], axis=1)

    def vh(h):
        return kv[:, h * 256 + MLA_NOPE:(h + 1) * 256]

    _mla_heads(qh, kh, vh, o_ref, scale)


def _mla_ctx(p_small, g_qnorm, g_kvnorm, w_qb_re, w_kvb, l):
    return pl.pallas_call(
        _mla_ctx_kernel,
        grid=(BATCH,),
        in_specs=[
            pl.BlockSpec((SEQ, MLA_Q_RANK), lambda b: (b, C_QA // MLA_Q_RANK)),
            pl.BlockSpec((SEQ, 256), lambda b: (b, C_CKV // 256)),
            pl.BlockSpec((SEQ, 256), lambda b: (b, C_KR // 256)),
            pl.BlockSpec((None, 1, MLA_Q_RANK), lambda b: (l, 0, 0)),
            pl.BlockSpec((None, 1, MLA_KV_RANK), lambda b: (l, 0, 0)),
            pl.BlockSpec((None, MLA_Q_RANK, MLA_HEADS * MLA_HEAD_PAD), lambda b: (l, 0, 0)),
            pl.BlockSpec((None, MLA_KV_RANK, MLA_HEADS * 256), lambda b: (l, 0, 0)),
        ],
        out_specs=[
            pl.BlockSpec((SEQ, MLA_HEADS * MLA_V), lambda b: (b, 0)),
            pl.BlockSpec((SEQ, MLA_KV_RANK), lambda b: (b, 0)),
            pl.BlockSpec((SEQ, MLA_ROPE), lambda b: (b, 0)),
        ],
        out_shape=[
            jax.ShapeDtypeStruct((N_CTX, MLA_HEADS * MLA_V), BF),
            jax.ShapeDtypeStruct((N_CTX, MLA_KV_RANK), F32),
            jax.ShapeDtypeStruct((N_CTX, MLA_ROPE), F32),
        ],
        compiler_params=_cparams(("arbitrary",)),
        name="mla_ctx",
    )(p_small, p_small, p_small, g_qnorm.reshape(DEPTH, 1, -1), g_kvnorm.reshape(DEPTH, 1, -1),
      w_qb_re, w_kvb)


def _mla_dec_kernel(qa_ref, ckv_ref, kr_ref, cckv_ref, ckr_ref, cq_ref, sq_ref, ck_ref, sk_ref,
                    gq_ref, gkv_ref, wqb_ref, wkvb_ref, o_ref, kf_ref, vf_ref):
    scale = (MLA_NOPE + MLA_ROPE) ** -0.5

    @pl.when(pl.program_id(1) == 0)
    def _():
        ckv = _rms(ckv_ref[...], gkv_ref[...])
        ckv_all = jnp.concatenate([cckv_ref[...], ckv], axis=0).astype(BF)
        kr = kr_ref[:, :128]
        kr_rot = kr * ck_ref[...] + _rope_swap(kr) * sk_ref[...]
        kr_all = jnp.concatenate([ckr_ref[...], kr_rot], axis=0).astype(BF)
        for h in range(MLA_HEADS):
            kvh = jnp.dot(ckv_all, wkvb_ref[:, h * 256:(h + 1) * 256], preferred_element_type=F32)
            kf_ref[h, :, 0:MLA_NOPE] = kvh[:, :MLA_NOPE].astype(BF)
            kf_ref[h, :, MLA_NOPE:MLA_HEAD_PAD] = kr_all
            vf_ref[h] = kvh[:, MLA_NOPE:].astype(BF)

    qn = _rms(qa_ref[...], gq_ref[...]).astype(BF)
    q = jnp.dot(qn, wqb_ref[...], preferred_element_type=F32)
    cq = cq_ref[...]
    sq = sq_ref[...]

    def qh(h):
        x = q[:, h * MLA_HEAD_PAD:(h + 1) * MLA_HEAD_PAD]
        return (x * cq + _rope_swap(x) * sq).astype(BF)

    _mla_heads(qh, lambda h: kf_ref[h], lambda h: vf_ref[h], o_ref, scale)


def _mla_dec(p_small, cache_ckv, cache_kr_pad, tabs, g_qnorm, g_kvnorm, w_qb_re, w_kvb, l):
    tq = 256
    nq = DEC_SEQ // tq
    row_blk = N_CTX // DEC_SEQ
    cq, sq, ck, sk = tabs
    return pl.pallas_call(
        _mla_dec_kernel,
        grid=(DEC_BATCH, nq),
        in_specs=[
            pl.BlockSpec((tq, MLA_Q_RANK), lambda b, i: (N_CTX // tq + b * nq + i, 0)),
            pl.BlockSpec((DEC_SEQ, 256), lambda b, i: (row_blk + b, C_CKV // 256)),
            pl.BlockSpec((DEC_SEQ, 256), lambda b, i: (row_blk + b, C_KR // 256)),
            pl.BlockSpec((None, None, PAST_LEN, MLA_KV_RANK), lambda b, i: (b, l, 0, 0)),
            pl.BlockSpec((None, None, PAST_LEN, 128), lambda b, i: (b, l, 0, 0)),
            pl.BlockSpec((tq, MLA_HEAD_PAD), lambda b, i: (i, 0)),
            pl.BlockSpec((tq, MLA_HEAD_PAD), lambda b, i: (i, 0)),
            pl.BlockSpec((DEC_SEQ, 128), lambda b, i: (0, 0)),
            pl.BlockSpec((DEC_SEQ, 128), lambda b, i: (0, 0)),
            pl.BlockSpec((None, 1, MLA_Q_RANK), lambda b, i: (l, 0, 0)),
            pl.BlockSpec((None, 1, MLA_KV_RANK), lambda b, i: (l, 0, 0)),
            pl.BlockSpec((None, MLA_Q_RANK, MLA_HEADS * MLA_HEAD_PAD), lambda b, i: (l, 0, 0)),
            pl.BlockSpec((None, MLA_KV_RANK, MLA_HEADS * 256), lambda b, i: (l, 0, 0)),
        ],
        out_specs=pl.BlockSpec((tq, MLA_HEADS * MLA_V), lambda b, i: (b * nq + i, 0)),
        out_shape=jax.ShapeDtypeStruct((N_DEC, MLA_HEADS * MLA_V), BF),
        scratch_shapes=[
            pltpu.VMEM((MLA_HEADS, KEYS_DEC, MLA_HEAD_PAD), BF),
            pltpu.VMEM((MLA_HEADS, KEYS_DEC, MLA_V), BF),
        ],
        compiler_params=_cparams(("arbitrary", "arbitrary")),
        name="mla_dec",
    )(p_small, p_small, p_small, cache_ckv, cache_kr_pad, cq, sq, ck, sk,
      g_qnorm.reshape(DEPTH, 1, -1), g_kvnorm.reshape(DEPTH, 1, -1), w_qb_re, w_kvb)


def _diff_lambda(lv):
    t1 = jnp.sum(lv[0:1] * lv[1:2], axis=-1, keepdims=True)
    t2 = jnp.sum(lv[2:3] * lv[3:4], axis=-1, keepdims=True)
    return jnp.exp(t1) - jnp.exp(t2)


def _diff_heads(q, kfun, vfun, lam, g, o_ref, lam_init):
    scale = DIFF_QK ** -0.5
    lane = lax.broadcasted_iota(jnp.int32, (q.shape[0], DIFF_V), 1)
    for h in range(DIFF_HEADS):
        qh = q[:, h * DIFF_V:(h + 1) * DIFF_V]
        q1 = jnp.where(lane < DIFF_QK, qh, 0.0).astype(BF)
        q2 = jnp.where(lane >= DIFF_QK, qh, 0.0).astype(BF)
        k = kfun(h)
        v = vfun(h)
        p1 = _softmax_rows(_dot_nt(q1, k) * scale).astype(BF)
        p2 = _softmax_rows(_dot_nt(q2, k) * scale).astype(BF)
        a1 = jnp.dot(p1, v, preferred_element_type=F32)
        a2 = jnp.dot(p2, v, preferred_element_type=F32)
        o = _rms(a1 - lam * a2, g) * (1.0 - lam_init)
        o_ref[:, h * DIFF_V:(h + 1) * DIFF_V] = o.astype(o_ref.dtype)


def _diff_ctx_kernel(q_ref, k_ref, v_ref, lam_ref, g_ref, o_ref, *, lam_init):
    lam = _diff_lambda(lam_ref[...]) + lam_init
    k = k_ref[...].astype(BF)
    v = v_ref[...].astype(BF)
    _diff_heads(q_ref[...],
                lambda h: k[:, h * DIFF_V:(h + 1) * DIFF_V],
                lambda h: v[:, h * DIFF_V:(h + 1) * DIFF_V],
                lam, g_ref[...], o_ref, lam_init)


def _lam_init(l):
    return 0.8 - 0.6 * math.exp(-0.3 * l)


def _diff_ctx(p_small, diff_lambda, g_diffnorm, l):
    return pl.pallas_call(
        functools.partial(_diff_ctx_kernel, lam_init=_lam_init(l)),
        grid=(BATCH,),
        in_specs=[
            pl.BlockSpec((SEQ, DIFF_WIDTH), lambda b: (b, C_DQ // DIFF_WIDTH)),
            pl.BlockSpec((SEQ, DIFF_WIDTH), lambda b: (b, C_DK // DIFF_WIDTH)),
            pl.BlockSpec((SEQ, DIFF_WIDTH), lambda b: (b, C_DV // DIFF_WIDTH)),
            pl.BlockSpec((None, 4, DIFF_QK), lambda b: (l, 0, 0)),
            pl.BlockSpec((None, 1, DIFF_V), lambda b: (l, 0, 0)),
        ],
        out_specs=pl.BlockSpec((SEQ, DIFF_WIDTH), lambda b: (b, 0)),
        out_shape=jax.ShapeDtypeStruct((N_CTX, DIFF_WIDTH), BF),
        compiler_params=_cparams(("arbitrary",)),
        name="diff_ctx",
    )(p_small, p_small, p_small, diff_lambda, g_diffnorm.reshape(DEPTH, 1, -1))


def _diff_dec_kernel(q_ref, k_ref, v_ref, ck_ref, cv_ref, cq_ref, sq_ref, cfull_ref, sfull_ref,
                     lam_ref, g_ref, o_ref, kf_ref, vf_ref, *, lam_init):
    @pl.when(pl.program_id(1) == 0)
    def _():
        k = k_ref[...]
        k_rot = k * cfull_ref[...] + _rope_swap(k) * sfull_ref[...]
        kf_ref[0:PAST_LEN, :] = ck_ref[...].astype(BF)
        kf_ref[PAST_LEN:KEYS_DEC, :] = k_rot.astype(BF)
        vf_ref[0:PAST_LEN, :] = cv_ref[...].astype(BF)
        vf_ref[PAST_LEN:KEYS_DEC, :] = v_ref[...].astype(BF)

    lam = _diff_lambda(lam_ref[...]) + lam_init
    q = q_ref[...]
    q = q * cq_ref[...] + _rope_swap(q) * sq_ref[...]
    _diff_heads(q,
                lambda h: kf_ref[:, h * DIFF_V:(h + 1) * DIFF_V],
                lambda h: vf_ref[:, h * DIFF_V:(h + 1) * DIFF_V],
                lam, g_ref[...], o_ref, lam_init)


def _diff_dec(p_small, cache_k, cache_v, tabs, diff_lambda, g_diffnorm, l):
    tq = 256
    nq = DEC_SEQ // tq
    row_blk = N_CTX // DEC_SEQ
    c512, s512 = tabs
    return pl.pallas_call(
        functools.partial(_diff_dec_kernel, lam_init=_lam_init(l)),
        grid=(DEC_BATCH, nq),
        in_specs=[
            pl.BlockSpec((tq, DIFF_WIDTH), lambda b, i: (N_CTX // tq + b * nq + i, C_DQ // DIFF_WIDTH)),
            pl.BlockSpec((DEC_SEQ, DIFF_WIDTH), lambda b, i: (row_blk + b, C_DK // DIFF_WIDTH)),
            pl.BlockSpec((DEC_SEQ, DIFF_WIDTH), lambda b, i: (row_blk + b, C_DV // DIFF_WIDTH)),
            pl.BlockSpec((None, None, PAST_LEN, DIFF_WIDTH), lambda b, i: (b, l, 0, 0)),
            pl.BlockSpec((None, None, PAST_LEN, DIFF_WIDTH), lambda b, i: (b, l, 0, 0)),
            pl.BlockSpec((tq, DIFF_WIDTH), lambda b, i: (i, 0)),
            pl.BlockSpec((tq, DIFF_WIDTH), lambda b, i: (i, 0)),
            pl.BlockSpec((DEC_SEQ, DIFF_WIDTH), lambda b, i: (0, 0)),
            pl.BlockSpec((DEC_SEQ, DIFF_WIDTH), lambda b, i: (0, 0)),
            pl.BlockSpec((None, 4, DIFF_QK), lambda b, i: (l, 0, 0)),
            pl.BlockSpec((None, 1, DIFF_V), lambda b, i: (l, 0, 0)),
        ],
        out_specs=pl.BlockSpec((tq, DIFF_WIDTH), lambda b, i: (b * nq + i, 0)),
        out_shape=jax.ShapeDtypeStruct((N_DEC, DIFF_WIDTH), BF),
        scratch_shapes=[
            pltpu.VMEM((KEYS_DEC, DIFF_WIDTH), BF),
            pltpu.VMEM((KEYS_DEC, DIFF_WIDTH), BF),
        ],
        compiler_params=_cparams(("arbitrary", "arbitrary")),
        name="diff_dec",
    )(p_small, p_small, p_small, cache_k, cache_v, c512, s512, c512, s512,
      diff_lambda, g_diffnorm.reshape(DEPTH, 1, -1))


def _pool_kernel(z_ref, w_ref, ps_ref, o_ref, *, tm):
    seq_m1 = jnp.where(pl.program_id(0) * tm < N_CTX, SEQ - 1, DEC_SEQ - 1)
    t = lax.broadcasted_iota(jnp.int32, (tm, POOL_GROUP), 0) & seq_m1
    for gi, w in enumerate(POOL_WINDOWS):
        z = z_ref[:, gi * POOL_GROUP:(gi + 1) * POOL_GROUP]
        acc = jnp.zeros_like(z)
        for k in range(-(w // 2), w // 2):
            zs = z if k == 0 else pltpu.roll(z, (-k) % tm, 0)
            ok = (t + k >= 0) & (t + k <= seq_m1)
            acc = acc + jnp.where(ok, zs, 0.0)
        lo = jnp.maximum(t - w // 2, 0)
        hi = jnp.minimum(t + (w - 1) // 2, seq_m1)
        d = (acc / (hi - lo + 1).astype(F32) - z).astype(BF)
        y = jnp.dot(d, w_ref[gi], preferred_element_type=F32)
        y = y * ps_ref[:, gi * POOL_GROUP:(gi + 1) * POOL_GROUP]
        o_ref[:, gi * POOL_GROUP:(gi + 1) * POOL_GROUP] = y.astype(o_ref.dtype)


def _pool(p_small, w_pool_bf, pool_scale, l):
    tm = DEC_SEQ
    return pl.pallas_call(
        functools.partial(_pool_kernel, tm=tm),
        grid=(N_TOK // tm,),
        in_specs=[
            pl.BlockSpec((tm, POOL_WIDTH), lambda i: (i, C_PZ // POOL_WIDTH)),
            pl.BlockSpec((None, len(POOL_WINDOWS), POOL_GROUP, POOL_GROUP), lambda i: (l, 0, 0, 0)),
            pl.BlockSpec((None, 1, POOL_WIDTH), lambda i: (l, 0, 0)),
        ],
        out_specs=pl.BlockSpec((tm, POOL_WIDTH), lambda i: (i, 0)),
        out_shape=jax.ShapeDtypeStruct((N_TOK, POOL_WIDTH), BF),
        compiler_params=_cparams(("arbitrary",)),
        name="pool",
    )(p_small, w_pool_bf, pool_scale.reshape(DEPTH, 1, POOL_WIDTH))


def _merge_kernel(a_ref, p_ref, d_ref, g0_ref, g1_ref, g2_ref, wa_ref, wp_ref, wd_ref, o_ref):
    def sig(r):
        return jax.nn.sigmoid(r[...].astype(F32))

    m = sig(g0_ref) * jnp.dot(a_ref[...], wa_ref[...], preferred_element_type=F32)
    m = m + sig(g1_ref) * jnp.dot(p_ref[...], wp_ref[...], preferred_element_type=F32)
    m = m + sig(g2_ref) * jnp.dot(d_ref[...], wd_ref[...], preferred_element_type=F32)
    o_ref[...] = m.astype(o_ref.dtype)


def _merge(mla_o, pool_o, diff_o, gl, w_br_mla, w_br_pool, w_br_diff, l):
    tm, tn = 1024, 512
    nj = D_MODEL // tn
    return pl.pallas_call(
        _merge_kernel,
        grid=(N_TOK // tm, nj),
        in_specs=[
            pl.BlockSpec((tm, MLA_HEADS * MLA_V), lambda i, j: (i, 0)),
            pl.BlockSpec((tm, POOL_WIDTH), lambda i, j: (i, 0)),
            pl.BlockSpec((tm, DIFF_WIDTH), lambda i, j: (i, 0)),
            pl.BlockSpec((tm, tn), lambda i, j: (i, j)),
            pl.BlockSpec((tm, tn), lambda i, j: (i, nj + j)),
            pl.BlockSpec((tm, tn), lambda i, j: (i, 2 * nj + j)),
            pl.BlockSpec((None, MLA_HEADS * MLA_V, tn), lambda i, j: (l, 0, j)),
            pl.BlockSpec((None, POOL_WIDTH, tn), lambda i, j: (l, 0, j)),
            pl.BlockSpec((None, DIFF_WIDTH, tn), lambda i, j: (l, 0, j)),
        ],
        out_specs=pl.BlockSpec((tm, tn), lambda i, j: (i, j)),
        out_shape=jax.ShapeDtypeStruct((N_TOK, D_MODEL), BF),
        compiler_params=_cparams(("arbitrary", "arbitrary")),
        name="merge",
    )(mla_o, pool_o, diff_o, gl, gl, gl, w_br_mla, w_br_pool, w_br_diff)


def _outproj_kernel(x_ref, m_ref, w_ref, ga_ref, g2_ref, sh_ref, sc_ref, xo_ref, ht_ref, *, tm):
    seg = _segment(pl.program_id(0) * tm)
    y = jnp.dot(m_ref[...], w_ref[...], preferred_element_type=F32)
    xn = x_ref[...] + ga_ref[pl.ds(seg, 1), :] * y
    xo_ref[...] = xn
    h2 = _rms(xn, g2_ref[...]) * (1.0 + sc_ref[pl.ds(seg, 1), :]) + sh_ref[pl.ds(seg, 1), :]
    ht_ref[...] = h2.T.astype(BF)


def _out_projection(x, merged, w_out, mod, g_norm2, l):
    tm = 256
    return pl.pallas_call(
        functools.partial(_outproj_kernel, tm=tm),
        grid=(N_TOK // tm,),
        in_specs=[
            pl.BlockSpec((tm, D_MODEL), lambda i: (i, 0)),
            pl.BlockSpec((tm, D_MODEL), lambda i: (i, 0)),
            pl.BlockSpec((None, D_MODEL, D_MODEL), lambda i: (l, 0, 0)),
            pl.BlockSpec((None, 8, D_MODEL), lambda i: (l, 0, 2)),
            pl.BlockSpec((None, 1, D_MODEL), lambda i: (l, 0, 0)),
            pl.BlockSpec((None, 8, D_MODEL), lambda i: (l, 0, 3)),
            pl.BlockSpec((None, 8, D_MODEL), lambda i: (l, 0, 4)),
        ],
        out_specs=[
            pl.BlockSpec((tm, D_MODEL), lambda i: (i, 0)),
            pl.BlockSpec((D_MODEL, tm), lambda i: (0, i)),
        ],
        out_shape=[
            jax.ShapeDtypeStruct((N_TOK, D_MODEL), F32),
            jax.ShapeDtypeStruct((D_MODEL, N_TOK), BF),
        ],
        compiler_params=_cparams(("arbitrary",)),
        name="out_projection",
    )(x, merged, w_out, mod, g_norm2.reshape(DEPTH, 1, D_MODEL), mod, mod)


def _top_values(x, n):
    vals = []
    for _ in range(n):
        m = jnp.max(x, axis=0, keepdims=True)
        vals.append(m)
        x = jnp.where(x == m, -jnp.inf, x)
    return vals


def _route_kernel(ht_ref, wq_ref, sk_ref, th_ref, e1_ref, s2_ref, e2_ref):
    qt = jnp.dot(wq_ref[...], ht_ref[...], preferred_element_type=F32)
    s1 = jnp.dot(sk_ref[0], qt[0:PEER_KEYS].astype(BF), preferred_element_type=F32)
    s2 = jnp.dot(sk_ref[1], qt[PEER_KEYS:].astype(BF), preferred_element_type=F32)
    a = _top_values(s1, PEER_TOPK)
    b = _top_values(s2, PEER_TOPK)
    rank = lax.broadcasted_iota(jnp.int32, (PEER_TOPK, s1.shape[1]), 0)
    bm = jnp.zeros((PEER_TOPK, s1.shape[1]), F32)
    for r in range(PEER_TOPK):
        bm = jnp.where(rank == r, b[r], bm)
    cands = [ar + bm for ar in a]
    tau = _top_values(jnp.concatenate(cands, axis=0), PEER_TOPK)[-1]
    top = a[0] + b[0]
    z = jnp.zeros_like(tau)
    th = jnp.full(s1.shape, jnp.inf, F32)
    for r in range(PEER_TOPK):
        sel_r = cands[r] >= tau
        z = z + jnp.sum(jnp.where(sel_r, jnp.exp(cands[r] - top), 0.0), axis=0, keepdims=True)
        th_r = jnp.min(jnp.where(sel_r, bm, jnp.inf), axis=0, keepdims=True)
        th = jnp.where(s1 == a[r], th_r, th)
    th_ref[...] = th
    e1_ref[...] = jnp.exp(s1 - a[0]) / z
    s2_ref[...] = s2
    e2_ref[...] = jnp.exp(s2 - b[0])


def _route(h2t, w_peer_qt, subkeys_bf, l):
    tn = 512
    rows = PEER_HEADS * PEER_KEYS
    spec = pl.BlockSpec((PEER_KEYS, tn), lambda t, h: (h, t))
    shape = jax.ShapeDtypeStruct((rows, N_TOK), F32)
    return pl.pallas_call(
        _route_kernel,
        grid=(N_TOK // tn, PEER_HEADS),
        in_specs=[
            pl.BlockSpec((D_MODEL, tn), lambda t, h: (0, t)),
            pl.BlockSpec((None, PEER_QDIM, D_MODEL), lambda t, h: (l, h, 0)),
            pl.BlockSpec((None, 2, PEER_KEYS, PEER_KEYS), lambda t, h: (l, 0, 0, 0)),
        ],
        out_specs=[spec, spec, spec, spec],
        out_shape=[shape, shape, shape, shape],
        compiler_params=_cparams(("arbitrary", "arbitrary")),
        name="peer_route",
    )(h2t, w_peer_qt, subkeys_bf)


def _peer_kernel(ht_ref, u_ref, vt_ref, th_ref, e1_ref, s2_ref, e2_ref, x_ref, ga_ref,
                 o_ref, acc_ref, wg_ref, *, tm, te):
    e = pl.program_id(1)

    @pl.when(e == 0)
    def _():
        acc_ref[...] = jnp.zeros_like(acc_ref)

    at = jnp.dot(u_ref[...], ht_ref[...], preferred_element_type=F32)
    g = 0.5 * at * (1.0 + lax.erf(at * math.sqrt(0.5)))
    nb = te // PEER_KEYS
    for b in range(nb):
        i = e * nb + b
        w = jnp.zeros((PEER_KEYS, tm), F32)
        for h in range(PEER_HEADS):
            th = th_ref[pl.ds(h * PEER_KEYS + i, 1), :]
            e1 = e1_ref[pl.ds(h * PEER_KEYS + i, 1), :]
            hs = slice(h * PEER_KEYS, (h + 1) * PEER_KEYS)
            w = w + jnp.where(s2_ref[hs, :] >= th, e2_ref[hs, :], 0.0) * e1
        wg_ref[b * PEER_KEYS:(b + 1) * PEER_KEYS, :] = (w * g[b * PEER_KEYS:(b + 1) * PEER_KEYS]).astype(BF)
    acc_ref[...] += jnp.dot(vt_ref[...], wg_ref[...], preferred_element_type=F32)

    @pl.when(e == pl.num_programs(1) - 1)
    def _():
        seg = _segment(pl.program_id(0) * tm)
        o_ref[...] = x_ref[...] + ga_ref[pl.ds(seg, 1), :] * acc_ref[...].T


def _peer(h2t, u_bf, vt_bf, routing, x, mod, l):
    tm, te = 512, 512
    th, e1, s2, e2 = routing
    rows = PEER_HEADS * PEER_KEYS
    rspec = pl.BlockSpec((rows, tm), lambda t, e: (0, t))
    return pl.pallas_call(
        functools.partial(_peer_kernel, tm=tm, te=te),
        grid=(N_TOK // tm, PEER_EXPERTS // te),
        in_specs=[
            pl.BlockSpec((D_MODEL, tm), lambda t, e: (0, t)),
            pl.BlockSpec((None, te, D_MODEL), lambda t, e: (l, e, 0)),
            pl.BlockSpec((None, D_MODEL, te), lambda t, e: (l, 0, e)),
            rspec, rspec, rspec, rspec,
            pl.BlockSpec((tm, D_MODEL), lambda t, e: (t, 0)),
            pl.BlockSpec((None, 8, D_MODEL), lambda t, e: (l, 0, 5)),
        ],
        out_specs=pl.BlockSpec((tm, D_MODEL), lambda t, e: (t, 0)),
        out_shape=jax.ShapeDtypeStruct((N_TOK, D_MODEL), F32),
        scratch_shapes=[
            pltpu.VMEM((D_MODEL, tm), F32),
            pltpu.VMEM((te, tm), BF),
        ],
        compiler_params=_cparams(("arbitrary", "arbitrary")),
        name="peer_dense",
    )(h2t, u_bf, vt_bf, th, e1, s2, e2, x, mod)


def _final_kernel(x_ref, g_ref, o_ref):
    o_ref[...] = _rms(x_ref[...], g_ref[...])


def _final_norm(x, g_final):
    tm = 512
    return pl.pallas_call(
        _final_kernel,
        grid=(N_TOK // tm,),
        in_specs=[pl.BlockSpec((tm, D_MODEL), lambda i: (i, 0)),
                  pl.BlockSpec((1, D_MODEL), lambda i: (0, 0))],
        out_specs=pl.BlockSpec((tm, D_MODEL), lambda i: (i, 0)),
        out_shape=jax.ShapeDtypeStruct((N_TOK, D_MODEL), F32),
        compiler_params=_cparams(("arbitrary",)),
        name="final_norm",
    )(x, g_final.reshape(1, D_MODEL))


def _rope_tables():
    t = jnp.arange(DEC_SEQ)
    half = MLA_ROPE // 4
    inv = ROPE_BASE ** (-jnp.arange(half, dtype=F32) / half)
    ang_r = (t // GRID_W).astype(F32)[:, None] * inv
    ang_c = (t % GRID_W).astype(F32)[:, None] * inv
    cos = jnp.concatenate([jnp.cos(ang_r)] * 2 + [jnp.cos(ang_c)] * 2, axis=1)
    sin = jnp.concatenate([-jnp.sin(ang_r), jnp.sin(ang_r), -jnp.sin(ang_c), jnp.sin(ang_c)], axis=1)
    return cos, sin


def _relayout_w_in(w_in):
    q, ckv, kr, pz, dq, dk, dv, gl = jnp.split(
        w_in, (512, 768, 832, 1344, 1856, 2368, 2880), axis=-1)
    pad = jnp.zeros(w_in.shape[:-1] + (C_PZ - C_KR - MLA_ROPE,), w_in.dtype)
    return jnp.concatenate([q, ckv, kr, pad, pz, dq, dk, dv, gl], axis=-1).astype(BF)


def _relayout_w_qb(w_qb):
    w = w_qb.reshape(DEPTH, MLA_Q_RANK, MLA_HEADS, MLA_NOPE + MLA_ROPE)
    w = jnp.pad(w, ((0, 0), (0, 0), (0, 0), (0, MLA_HEAD_PAD - MLA_NOPE - MLA_ROPE)))
    return w.reshape(DEPTH, MLA_Q_RANK, MLA_HEADS * MLA_HEAD_PAD).astype(BF)


def kernel(x_prompt, x_sample, cache_mla_ckv, cache_mla_krope, cache_diff_k, cache_diff_v, c, c_ctx,
           w_mod, b_mod, g_norm1, w_in, g_qnorm, w_qb, g_kvnorm, w_kvb, w_pool, pool_scale,
           diff_lambda, g_diffnorm, w_br_mla, w_br_pool, w_br_diff, w_out, g_norm2,
           w_peer_q, peer_subkeys, peer_u, peer_v, g_final):
    x = jnp.concatenate([x_prompt.reshape(N_CTX, D_MODEL), x_sample.reshape(N_DEC, D_MODEL)], axis=0)
    cond8 = jnp.concatenate([c_ctx[None, :], c, jnp.zeros((8 - 1 - DEC_BATCH, D_MODEL), F32)], axis=0)

    w_in_re = _relayout_w_in(w_in)
    w_qb_re = _relayout_w_qb(w_qb)
    w_kvb_bf = w_kvb.astype(BF)
    w_pool_bf = w_pool.astype(BF)
    w_br_mla_bf, w_br_pool_bf, w_br_diff_bf = w_br_mla.astype(BF), w_br_pool.astype(BF), w_br_diff.astype(BF)
    w_out_bf = w_out.astype(BF)
    w_peer_qt = jnp.swapaxes(w_peer_q, 1, 2).astype(BF)
    subkeys_bf = peer_subkeys.astype(BF)
    u_bf = peer_u.astype(BF)
    vt_bf = jnp.swapaxes(peer_v, 1, 2).astype(BF)

    cache_kr_pad = jnp.pad(cache_mla_krope, ((0, 0), (0, 0), (0, 0), (0, 128 - MLA_ROPE)))
    cache_k = cache_diff_k.reshape(DEC_BATCH, DEPTH, PAST_LEN, DIFF_WIDTH)
    cache_v = cache_diff_v.reshape(DEC_BATCH, DEPTH, PAST_LEN, DIFF_WIDTH)

    cos64, sin64 = _rope_tables()
    ones, zeros = jnp.ones_like(cos64), jnp.zeros_like(cos64)
    mla_tabs = (
        jnp.concatenate([ones, ones, cos64, ones], axis=1),
        jnp.concatenate([zeros, zeros, sin64, zeros], axis=1),
        jnp.concatenate([cos64, ones], axis=1),
        jnp.concatenate([sin64, zeros], axis=1),
    )
    diff_tabs = (jnp.tile(cos64, (1, DIFF_WIDTH // DIFF_QK)), jnp.tile(sin64, (1, DIFF_WIDTH // DIFF_QK)))

    mod = _modulation(cond8, w_mod, b_mod)

    st_ckv, st_kr, st_k, st_v = [], [], [], []
    for l in range(DEPTH):
        p_small = _in_projection(x, g_norm1, mod, w_in_re, l, 0, SMALL_WIDTH, F32)
        gl = _in_projection(x, g_norm1, mod, w_in_re, l, SMALL_WIDTH, GATE_WIDTH, BF)
        mla_c, ckv_c, kr_c = _mla_ctx(p_small, g_qnorm, g_kvnorm, w_qb_re, w_kvb_bf, l)
        mla_d = _mla_dec(p_small, cache_mla_ckv, cache_kr_pad, mla_tabs, g_qnorm, g_kvnorm, w_qb_re, w_kvb_bf, l)
        diff_c = _diff_ctx(p_small, diff_lambda, g_diffnorm, l)
        diff_d = _diff_dec(p_small, cache_k, cache_v, diff_tabs, diff_lambda, g_diffnorm, l)
        pool_o = _pool(p_small, w_pool_bf, pool_scale, l)
        mla_o = jnp.concatenate([mla_c, mla_d], axis=0)
        diff_o = jnp.concatenate([diff_c, diff_d], axis=0)
        merged = _merge(mla_o, pool_o, diff_o, gl, w_br_mla_bf, w_br_pool_bf, w_br_diff_bf, l)
        x, h2t = _out_projection(x, merged, w_out_bf, mod, g_norm2, l)
        routing = _route(h2t, w_peer_qt, subkeys_bf, l)
        x = _peer(h2t, u_bf, vt_bf, routing, x, mod, l)
        st_ckv.append(ckv_c.reshape(BATCH, SEQ, MLA_KV_RANK))
        st_kr.append(kr_c.reshape(BATCH, SEQ, MLA_ROPE))
        st_k.append(p_small[:N_CTX, C_DK:C_DK + DIFF_WIDTH].reshape(BATCH, SEQ, DIFF_HEADS, 2 * DIFF_QK))
        st_v.append(p_small[:N_CTX, C_DV:C_DV + DIFF_WIDTH].reshape(BATCH, SEQ, DIFF_HEADS, DIFF_V))

    y = _final_norm(x, g_final)
    y_prompt = y[:N_CTX].reshape(BATCH, SEQ, D_MODEL)
    y_sample = y[N_CTX:].reshape(DEC_BATCH, DEC_SEQ, D_MODEL)
    return (y_prompt, y_sample, jnp.stack(st_ckv, axis=1), jnp.stack(st_kr, axis=1),
            jnp.stack(st_k, axis=1), jnp.stack(st_v, axis=1))
```

```python
import functools
import math

import jax
import jax.numpy as jnp
from jax import lax
from jax.experimental import pallas as pl
from jax.experimental.pallas import tpu as pltpu

BF = jnp.bfloat16
F32 = jnp.float32

D_MODEL = 2048
BATCH = 16
SEQ = 256
DEPTH = 4
DEC_BATCH = 2
DEC_SEQ = 1024
PAST_LEN = 256
GRID_W = 64
ROPE_BASE = 10000.0
EPS = 1e-6

MLA_HEADS = 8
MLA_NOPE = 128
MLA_ROPE = 64
MLA_V = 128
MLA_Q_RANK = 512
MLA_KV_RANK = 256
MLA_HEAD_PAD = 256

POOL_WINDOWS = (2, 4, 8, 16)
POOL_GROUP = 128
POOL_WIDTH = POOL_GROUP * len(POOL_WINDOWS)

DIFF_HEADS = 4
DIFF_QK = 64
DIFF_V = 2 * DIFF_QK
DIFF_WIDTH = DIFF_HEADS * DIFF_V

PEER_HEADS = 8
PEER_KEYS = 128
PEER_EXPERTS = PEER_KEYS * PEER_KEYS
PEER_TOPK = 16
PEER_QDIM = 256

N_CTX = BATCH * SEQ
N_DEC = DEC_BATCH * DEC_SEQ
N_TOK = N_CTX + N_DEC
KEYS_DEC = PAST_LEN + DEC_SEQ

C_QA = 0
C_CKV = 512
C_KR = 768
C_PZ = 1024
C_DQ = 1536
C_DK = 2048
C_DV = 2560
SMALL_WIDTH = 3072
GATE_WIDTH = 3 * D_MODEL

VMEM_LIMIT = 56 * 1024 * 1024


def _cparams(sem):
    return pltpu.CompilerParams(dimension_semantics=sem, vmem_limit_bytes=VMEM_LIMIT)


def _segment(row0):
    return jnp.where(row0 < N_CTX, 0, 1 + (row0 - N_CTX) // DEC_SEQ)


def _rms(x, g):
    return x * lax.rsqrt(jnp.mean(x * x, axis=-1, keepdims=True) + EPS) * g


def _softmax_rows(s):
    m = jnp.max(s, axis=-1, keepdims=True)
    p = jnp.exp(s - m)
    return p / jnp.sum(p, axis=-1, keepdims=True)


def _dot_nt(a, b):
    return lax.dot_general(a, b, (((1,), (1,)), ((), ())), preferred_element_type=F32)


def _rope_swap(x):
    n = x.shape[-1]
    up = pltpu.roll(x, n - 16, 1)
    dn = pltpu.roll(x, 16, 1)
    lane = lax.broadcasted_iota(jnp.int32, x.shape, 1)
    return jnp.where((lane & 16) == 0, up, dn)


def _mod_kernel(c_ref, w_ref, b_ref, o_ref):
    c = c_ref[...]
    a = (c * jax.nn.sigmoid(c)).astype(BF)
    o_ref[...] = jnp.dot(a, w_ref[...].astype(BF), preferred_element_type=F32) + b_ref[...]


def _modulation(cond8, w_mod, b_mod):
    tn = 1024
    return pl.pallas_call(
        _mod_kernel,
        grid=(DEPTH, 6 * D_MODEL // tn),
        in_specs=[
            pl.BlockSpec((8, D_MODEL), lambda l, j: (0, 0)),
            pl.BlockSpec((None, D_MODEL, tn), lambda l, j: (l, 0, j)),
            pl.BlockSpec((None, 1, tn), lambda l, j: (l, 0, j)),
        ],
        out_specs=pl.BlockSpec((None, 8, tn), lambda l, j: (l, 0, j)),
        out_shape=jax.ShapeDtypeStruct((DEPTH, 8, 6 * D_MODEL), F32),
        compiler_params=_cparams(("arbitrary", "arbitrary")),
        name="modulation",
    )(cond8, w_mod, b_mod.reshape(DEPTH, 1, 6 * D_MODEL))


def _inproj_kernel(x_ref, g_ref, sh_ref, sc_ref, w_ref, o_ref, h_ref, *, tm):
    i = pl.program_id(0)

    @pl.when(pl.program_id(1) == 0)
    def _():
        seg = _segment(i * tm)
        y = _rms(x_ref[...], g_ref[...])
        h_ref[...] = (y * (1.0 + sc_ref[pl.ds(seg, 1), :]) + sh_ref[pl.ds(seg, 1), :]).astype(BF)

    o_ref[...] = jnp.dot(h_ref[...], w_ref[...], preferred_element_type=F32).astype(o_ref.dtype)


def _in_projection(x, g_norm1, mod, w_in_re, l, col0, width, out_dtype):
    tm, tn = 1024, 512
    off = col0 // tn
    return pl.pallas_call(
        functools.partial(_inproj_kernel, tm=tm),
        grid=(N_TOK // tm, width // tn),
        in_specs=[
            pl.BlockSpec((tm, D_MODEL), lambda i, j: (i, 0)),
            pl.BlockSpec((None, 1, D_MODEL), lambda i, j: (l, 0, 0)),
            pl.BlockSpec((None, 8, D_MODEL), lambda i, j: (l, 0, 0)),
            pl.BlockSpec((None, 8, D_MODEL), lambda i, j: (l, 0, 1)),
            pl.BlockSpec((None, D_MODEL, tn), lambda i, j: (l, 0, j + off)),
        ],
        out_specs=pl.BlockSpec((tm, tn), lambda i, j: (i, j)),
        out_shape=jax.ShapeDtypeStruct((N_TOK, width), out_dtype),
        scratch_shapes=[pltpu.VMEM((tm, D_MODEL), BF)],
        compiler_params=_cparams(("arbitrary", "arbitrary")),
        name="in_projection",
    )(x, g_norm1.reshape(DEPTH, 1, D_MODEL), mod, mod, w_in_re)


def _mla_heads(q, kfun, vfun, o_ref, scale):
    for h in range(MLA_HEADS):
        s = _dot_nt(q(h), kfun(h)) * scale
        p = _softmax_rows(s).astype(BF)
        o = jnp.dot(p, vfun(h), preferred_element_type=F32)
        o_ref[:, h * MLA_V:(h + 1) * MLA_V] = o.astype(o_ref.dtype)


def _mla_ctx_kernel(qa_ref, ckv_ref, kr_ref, gq_ref, gkv_ref, wqb_ref, wkvb_ref,
                    o_ref, ckv_out_ref, kr_out_ref):
    scale = (MLA_NOPE + MLA_ROPE) ** -0.5
    qn = _rms(qa_ref[...], gq_ref[...]).astype(BF)
    q = jnp.dot(qn, wqb_ref[...], preferred_element_type=F32).astype(BF)
    ckv = _rms(ckv_ref[...], gkv_ref[...])
    ckv_out_ref[...] = ckv
    kr = kr_ref[...]
    kr_out_ref[...] = kr[:, :MLA_ROPE]
    kv = jnp.dot(ckv.astype(BF), wkvb_ref[...], preferred_element_type=F32).astype(BF)
    krp = kr[:, :128].astype(BF)

    def qh(h):
        return q[:, h * MLA_HEAD_PAD:(h + 1) * MLA_HEAD_PAD]

    def kh(h):
        return jnp.concatenate([kv[:, h * 256:h * 256 + MLA_NOPE], krp], axis=1)

    def vh(h):
        return kv[:, h * 256 + MLA_NOPE:(h + 1) * 256]

    _mla_heads(qh, kh, vh, o_ref, scale)


def _mla_ctx(p_small, g_qnorm, g_kvnorm, w_qb_re, w_kvb, l):
    return pl.pallas_call(
        _mla_ctx_kernel,
        grid=(BATCH,),
        in_specs=[
            pl.BlockSpec((SEQ, MLA_Q_RANK), lambda b: (b, C_QA // MLA_Q_RANK)),
            pl.BlockSpec((SEQ, 256), lambda b: (b, C_CKV // 256)),
            pl.BlockSpec((SEQ, 256), lambda b: (b, C_KR // 256)),
            pl.BlockSpec((None, 1, MLA_Q_RANK), lambda b: (l, 0, 0)),
            pl.BlockSpec((None, 1, MLA_KV_RANK), lambda b: (l, 0, 0)),
            pl.BlockSpec((None, MLA_Q_RANK, MLA_HEADS * MLA_HEAD_PAD), lambda b: (l, 0, 0)),
            pl.BlockSpec((None, MLA_KV_RANK, MLA_HEADS * 256), lambda b: (l, 0, 0)),
        ],
        out_specs=[
            pl.BlockSpec((SEQ, MLA_HEADS * MLA_V), lambda b: (b, 0)),
            pl.BlockSpec((SEQ, MLA_KV_RANK), lambda b: (b, 0)),
            pl.BlockSpec((SEQ, MLA_ROPE), lambda b: (b, 0)),
        ],
        out_shape=[
            jax.ShapeDtypeStruct((N_CTX, MLA_HEADS * MLA_V), BF),
            jax.ShapeDtypeStruct((N_CTX, MLA_KV_RANK), F32),
            jax.ShapeDtypeStruct((N_CTX, MLA_ROPE), F32),
        ],
        compiler_params=_cparams(("arbitrary",)),
        name="mla_ctx",
    )(p_small, p_small, p_small, g_qnorm.reshape(DEPTH, 1, -1), g_kvnorm.reshape(DEPTH, 1, -1),
      w_qb_re, w_kvb)


def _mla_dec_kernel(qa_ref, ckv_ref, kr_ref, cckv_ref, ckr_ref, cq_ref, sq_ref, ck_ref, sk_ref,
                    gq_ref, gkv_ref, wqb_ref, wkvb_ref, o_ref, kf_ref, vf_ref):
    scale = (MLA_NOPE + MLA_ROPE) ** -0.5

    @pl.when(pl.program_id(1) == 0)
    def _():
        ckv = _rms(ckv_ref[...], gkv_ref[...])
        ckv_all = jnp.concatenate([cckv_ref[...], ckv], axis=0).astype(BF)
        kr = kr_ref[:, :128]
        kr_rot = kr * ck_ref[...] + _rope_swap(kr) * sk_ref[...]
        kr_all = jnp.concatenate([ckr_ref[...], kr_rot], axis=0).astype(BF)
        for h in range(MLA_HEADS):
            kvh = jnp.dot(ckv_all, wkvb_ref[:, h * 256:(h + 1) * 256], preferred_element_type=F32)
            kf_ref[h, :, 0:MLA_NOPE] = kvh[:, :MLA_NOPE].astype(BF)
            kf_ref[h, :, MLA_NOPE:MLA_HEAD_PAD] = kr_all
            vf_ref[h] = kvh[:, MLA_NOPE:].astype(BF)

    qn = _rms(qa_ref[...], gq_ref[...]).astype(BF)
    q = jnp.dot(qn, wqb_ref[...], preferred_element_type=F32)
    cq = cq_ref[...]
    sq = sq_ref[...]

    def qh(h):
        x = q[:, h * MLA_HEAD_PAD:(h + 1) * MLA_HEAD_PAD]
        return (x * cq + _rope_swap(x) * sq).astype(BF)

    _mla_heads(qh, lambda h: kf_ref[h], lambda h: vf_ref[h], o_ref, scale)


def _mla_dec(p_small, cache_ckv, cache_kr_pad, tabs, g_qnorm, g_kvnorm, w_qb_re, w_kvb, l):
    tq = 256
    nq = DEC_SEQ // tq
    row_blk = N_CTX // DEC_SEQ
    cq, sq, ck, sk = tabs
    return pl.pallas_call(
        _mla_dec_kernel,
        grid=(DEC_BATCH, nq),
        in_specs=[
            pl.BlockSpec((tq, MLA_Q_RANK), lambda b, i: (N_CTX // tq + b * nq + i, 0)),
            pl.BlockSpec((DEC_SEQ, 256), lambda b, i: (row_blk + b, C_CKV // 256)),
            pl.BlockSpec((DEC_SEQ, 256), lambda b, i: (row_blk + b, C_KR // 256)),
            pl.BlockSpec((None, None, PAST_LEN, MLA_KV_RANK), lambda b, i: (b, l, 0, 0)),
            pl.BlockSpec((None, None, PAST_LEN, 128), lambda b, i: (b, l, 0, 0)),
            pl.BlockSpec((tq, MLA_HEAD_PAD), lambda b, i: (i, 0)),
            pl.BlockSpec((tq, MLA_HEAD_PAD), lambda b, i: (i, 0)),
            pl.BlockSpec((DEC_SEQ, 128), lambda b, i: (0, 0)),
            pl.BlockSpec((DEC_SEQ, 128), lambda b, i: (0, 0)),
            pl.BlockSpec((None, 1, MLA_Q_RANK), lambda b, i: (l, 0, 0)),
            pl.BlockSpec((None, 1, MLA_KV_RANK), lambda b, i: (l, 0, 0)),
            pl.BlockSpec((None, MLA_Q_RANK, MLA_HEADS * MLA_HEAD_PAD), lambda b, i: (l, 0, 0)),
            pl.BlockSpec((None, MLA_KV_RANK, MLA_HEADS * 256), lambda b, i: (l, 0, 0)),
        ],
        out_specs=pl.BlockSpec((tq, MLA_HEADS * MLA_V), lambda b, i: (b * nq + i, 0)),
        out_shape=jax.ShapeDtypeStruct((N_DEC, MLA_HEADS * MLA_V), BF),
        scratch_shapes=[
            pltpu.VMEM((MLA_HEADS, KEYS_DEC, MLA_HEAD_PAD), BF),
            pltpu.VMEM((MLA_HEADS, KEYS_DEC, MLA_V), BF),
        ],
        compiler_params=_cparams(("arbitrary", "arbitrary")),
        name="mla_dec",
    )(p_small, p_small, p_small, cache_ckv, cache_kr_pad, cq, sq, ck, sk,
      g_qnorm.reshape(DEPTH, 1, -1), g_kvnorm.reshape(DEPTH, 1, -1), w_qb_re, w_kvb)


def _diff_lambda(lv):
    t1 = jnp.sum(lv[0:1] * lv[1:2], axis=-1, keepdims=True)
    t2 = jnp.sum(lv[2:3] * lv[3:4], axis=-1, keepdims=True)
    return jnp.exp(t1) - jnp.exp(t2)


def _diff_heads(q, kfun, vfun, lam, g, o_ref, lam_init):
    scale = DIFF_QK ** -0.5
    lane = lax.broadcasted_iota(jnp.int32, (q.shape[0], DIFF_V), 1)
    for h in range(DIFF_HEADS):
        qh = q[:, h * DIFF_V:(h + 1) * DIFF_V]
        q1 = jnp.where(lane < DIFF_QK, qh, 0.0).astype(BF)
        q2 = jnp.where(lane >= DIFF_QK, qh, 0.0).astype(BF)
        k = kfun(h)
        v = vfun(h)
        p1 = _softmax_rows(_dot_nt(q1, k) * scale).astype(BF)
        p2 = _softmax_rows(_dot_nt(q2, k) * scale).astype(BF)
        a1 = jnp.dot(p1, v, preferred_element_type=F32)
        a2 = jnp.dot(p2, v, preferred_element_type=F32)
        o = _rms(a1 - lam * a2, g) * (1.0 - lam_init)
        o_ref[:, h * DIFF_V:(h + 1) * DIFF_V] = o.astype(o_ref.dtype)


def _diff_ctx_kernel(q_ref, k_ref, v_ref, lam_ref, g_ref, o_ref, *, lam_init):
    lam = _diff_lambda(lam_ref[...]) + lam_init
    k = k_ref[...].astype(BF)
    v = v_ref[...].astype(BF)
    _diff_heads(q_ref[...],
                lambda h: k[:, h * DIFF_V:(h + 1) * DIFF_V],
                lambda h: v[:, h * DIFF_V:(h + 1) * DIFF_V],
                lam, g_ref[...], o_ref, lam_init)


def _lam_init(l):
    return 0.8 - 0.6 * math.exp(-0.3 * l)


def _diff_ctx(p_small, diff_lambda, g_diffnorm, l):
    return pl.pallas_call(
        functools.partial(_diff_ctx_kernel, lam_init=_lam_init(l)),
        grid=(BATCH,),
        in_specs=[
            pl.BlockSpec((SEQ, DIFF_WIDTH), lambda b: (b, C_DQ // DIFF_WIDTH)),
            pl.BlockSpec((SEQ, DIFF_WIDTH), lambda b: (b, C_DK // DIFF_WIDTH)),
            pl.BlockSpec((SEQ, DIFF_WIDTH), lambda b: (b, C_DV // DIFF_WIDTH)),
            pl.BlockSpec((None, 4, DIFF_QK), lambda b: (l, 0, 0)),
            pl.BlockSpec((None, 1, DIFF_V), lambda b: (l, 0, 0)),
        ],
        out_specs=pl.BlockSpec((SEQ, DIFF_WIDTH), lambda b: (b, 0)),
        out_shape=jax.ShapeDtypeStruct((N_CTX, DIFF_WIDTH), BF),
        compiler_params=_cparams(("arbitrary",)),
        name="diff_ctx",
    )(p_small, p_small, p_small, diff_lambda, g_diffnorm.reshape(DEPTH, 1, -1))


def _diff_dec_kernel(q_ref, k_ref, v_ref, ck_ref, cv_ref, cq_ref, sq_ref, cfull_ref, sfull_ref,
                     lam_ref, g_ref, o_ref, kf_ref, vf_ref, *, lam_init):
    @pl.when(pl.program_id(1) == 0)
    def _():
        k = k_ref[...]
        k_rot = k * cfull_ref[...] + _rope_swap(k) * sfull_ref[...]
        kf_ref[0:PAST_LEN, :] = ck_ref[...].astype(BF)
        kf_ref[PAST_LEN:KEYS_DEC, :] = k_rot.astype(BF)
        vf_ref[0:PAST_LEN, :] = cv_ref[...].astype(BF)
        vf_ref[PAST_LEN:KEYS_DEC, :] = v_ref[...].astype(BF)

    lam = _diff_lambda(lam_ref[...]) + lam_init
    q = q_ref[...]
    q = q * cq_ref[...] + _rope_swap(q) * sq_ref[...]
    _diff_heads(q,
                lambda h: kf_ref[:, h * DIFF_V:(h + 1) * DIFF_V],
                lambda h: vf_ref[:, h * DIFF_V:(h + 1) * DIFF_V],
                lam, g_ref[...], o_ref, lam_init)


def _diff_dec(p_small, cache_k, cache_v, tabs, diff_lambda, g_diffnorm, l):
    tq = 256
    nq = DEC_SEQ // tq
    row_blk = N_CTX // DEC_SEQ
    c512, s512 = tabs
    return pl.pallas_call(
        functools.partial(_diff_dec_kernel, lam_init=_lam_init(l)),
        grid=(DEC_BATCH, nq),
        in_specs=[
            pl.BlockSpec((tq, DIFF_WIDTH), lambda b, i: (N_CTX // tq + b * nq + i, C_DQ // DIFF_WIDTH)),
            pl.BlockSpec((DEC_SEQ, DIFF_WIDTH), lambda b, i: (row_blk + b, C_DK // DIFF_WIDTH)),
            pl.BlockSpec((DEC_SEQ, DIFF_WIDTH), lambda b, i: (row_blk + b, C_DV // DIFF_WIDTH)),
            pl.BlockSpec((None, None, PAST_LEN, DIFF_WIDTH), lambda b, i: (b, l, 0, 0)),
            pl.BlockSpec((None, None, PAST_LEN, DIFF_WIDTH), lambda b, i: (b, l, 0, 0)),
            pl.BlockSpec((tq, DIFF_WIDTH), lambda b, i: (i, 0)),
            pl.BlockSpec((tq, DIFF_WIDTH), lambda b, i: (i, 0)),
            pl.BlockSpec((DEC_SEQ, DIFF_WIDTH), lambda b, i: (0, 0)),
            pl.BlockSpec((DEC_SEQ, DIFF_WIDTH), lambda b, i: (0, 0)),
            pl.BlockSpec((None, 4, DIFF_QK), lambda b, i: (l, 0, 0)),
            pl.BlockSpec((None, 1, DIFF_V), lambda b, i: (l, 0, 0)),
        ],
        out_specs=pl.BlockSpec((tq, DIFF_WIDTH), lambda b, i: (b * nq + i, 0)),
        out_shape=jax.ShapeDtypeStruct((N_DEC, DIFF_WIDTH), BF),
        scratch_shapes=[
            pltpu.VMEM((KEYS_DEC, DIFF_WIDTH), BF),
            pltpu.VMEM((KEYS_DEC, DIFF_WIDTH), BF),
        ],
        compiler_params=_cparams(("arbitrary", "arbitrary")),
        name="diff_dec",
    )(p_small, p_small, p_small, cache_k, cache_v, c512, s512, c512, s512,
      diff_lambda, g_diffnorm.reshape(DEPTH, 1, -1))


def _pool_kernel(z_ref, w_ref, ps_ref, o_ref, *, tm):
    seq_m1 = jnp.where(pl.program_id(0) * tm < N_CTX, SEQ - 1, DEC_SEQ - 1)
    t = lax.broadcasted_iota(jnp.int32, (tm, POOL_GROUP), 0) & seq_m1
    for gi, w in enumerate(POOL_WINDOWS):
        z = z_ref[:, gi * POOL_GROUP:(gi + 1) * POOL_GROUP]
        acc = jnp.zeros_like(z)
        for k in range(-(w // 2), w // 2):
            zs = z if k == 0 else pltpu.roll(z, (-k) % tm, 0)
            ok = (t + k >= 0) & (t + k <= seq_m1)
            acc = acc + jnp.where(ok, zs, 0.0)
        lo = jnp.maximum(t - w // 2, 0)
        hi = jnp.minimum(t + (w - 1) // 2, seq_m1)
        d = (acc / (hi - lo + 1).astype(F32) - z).astype(BF)
        y = jnp.dot(d, w_ref[gi], preferred_element_type=F32)
        y = y * ps_ref[:, gi * POOL_GROUP:(gi + 1) * POOL_GROUP]
        o_ref[:, gi * POOL_GROUP:(gi + 1) * POOL_GROUP] = y.astype(o_ref.dtype)


def _pool(p_small, w_pool_bf, pool_scale, l):
    tm = DEC_SEQ
    return pl.pallas_call(
        functools.partial(_pool_kernel, tm=tm),
        grid=(N_TOK // tm,),
        in_specs=[
            pl.BlockSpec((tm, POOL_WIDTH), lambda i: (i, C_PZ // POOL_WIDTH)),
            pl.BlockSpec((None, len(POOL_WINDOWS), POOL_GROUP, POOL_GROUP), lambda i: (l, 0, 0, 0)),
            pl.BlockSpec((None, 1, POOL_WIDTH), lambda i: (l, 0, 0)),
        ],
        out_specs=pl.BlockSpec((tm, POOL_WIDTH), lambda i: (i, 0)),
        out_shape=jax.ShapeDtypeStruct((N_TOK, POOL_WIDTH), BF),
        compiler_params=_cparams(("arbitrary",)),
        name="pool",
    )(p_small, w_pool_bf, pool_scale.reshape(DEPTH, 1, POOL_WIDTH))


def _merge_kernel(a_ref, p_ref, d_ref, g0_ref, g1_ref, g2_ref, wa_ref, wp_ref, wd_ref, o_ref):
    def sig(r):
        return jax.nn.sigmoid(r[...].astype(F32))

    m = sig(g0_ref) * jnp.dot(a_ref[...], wa_ref[...], preferred_element_type=F32)
    m = m + sig(g1_ref) * jnp.dot(p_ref[...], wp_ref[...], preferred_element_type=F32)
    m = m + sig(g2_ref) * jnp.dot(d_ref[...], wd_ref[...], preferred_element_type=F32)
    o_ref[...] = m.astype(o_ref.dtype)


def _merge(mla_o, pool_o, diff_o, gl, w_br_mla, w_br_pool, w_br_diff, l):
    tm, tn = 1024, 512
    nj = D_MODEL // tn
    return pl.pallas_call(
        _merge_kernel,
        grid=(N_TOK // tm, nj),
        in_specs=[
            pl.BlockSpec((tm, MLA_HEADS * MLA_V), lambda i, j: (i, 0)),
            pl.BlockSpec((tm, POOL_WIDTH), lambda i, j: (i, 0)),
            pl.BlockSpec((tm, DIFF_WIDTH), lambda i, j: (i, 0)),
            pl.BlockSpec((tm, tn), lambda i, j: (i, j)),
            pl.BlockSpec((tm, tn), lambda i, j: (i, nj + j)),
            pl.BlockSpec((tm, tn), lambda i, j: (i, 2 * nj + j)),
            pl.BlockSpec((None, MLA_HEADS * MLA_V, tn), lambda i, j: (l, 0, j)),
            pl.BlockSpec((None, POOL_WIDTH, tn), lambda i, j: (l, 0, j)),
            pl.BlockSpec((None, DIFF_WIDTH, tn), lambda i, j: (l, 0, j)),
        ],
        out_specs=pl.BlockSpec((tm, tn), lambda i, j: (i, j)),
        out_shape=jax.ShapeDtypeStruct((N_TOK, D_MODEL), BF),
        compiler_params=_cparams(("arbitrary", "arbitrary")),
        name="merge",
    )(mla_o, pool_o, diff_o, gl, gl, gl, w_br_mla, w_br_pool, w_br_diff)


def _outproj_kernel(x_ref, m_ref, w_ref, ga_ref, g2_ref, sh_ref, sc_ref, xo_ref, ht_ref, *, tm):
    seg = _segment(pl.program_id(0) * tm)
    y = jnp.dot(m_ref[...], w_ref[...], preferred_element_type=F32)
    xn = x_ref[...] + ga_ref[pl.ds(seg, 1), :] * y
    xo_ref[...] = xn
    h2 = _rms(xn, g2_ref[...]) * (1.0 + sc_ref[pl.ds(seg, 1), :]) + sh_ref[pl.ds(seg, 1), :]
    ht_ref[...] = h2.T.astype(BF)


def _out_projection(x, merged, w_out, mod, g_norm2, l):
    tm = 256
    return pl.pallas_call(
        functools.partial(_outproj_kernel, tm=tm),
        grid=(N_TOK // tm,),
        in_specs=[
            pl.BlockSpec((tm, D_MODEL), lambda i: (i, 0)),
            pl.BlockSpec((tm, D_MODEL), lambda i: (i, 0)),
            pl.BlockSpec((None, D_MODEL, D_MODEL), lambda i: (l, 0, 0)),
            pl.BlockSpec((None, 8, D_MODEL), lambda i: (l, 0, 2)),
            pl.BlockSpec((None, 1, D_MODEL), lambda i: (l, 0, 0)),
            pl.BlockSpec((None, 8, D_MODEL), lambda i: (l, 0, 3)),
            pl.BlockSpec((None, 8, D_MODEL), lambda i: (l, 0, 4)),
        ],
        out_specs=[
            pl.BlockSpec((tm, D_MODEL), lambda i: (i, 0)),
            pl.BlockSpec((D_MODEL, tm), lambda i: (0, i)),
        ],
        out_shape=[
            jax.ShapeDtypeStruct((N_TOK, D_MODEL), F32),
            jax.ShapeDtypeStruct((D_MODEL, N_TOK), BF),
        ],
        compiler_params=_cparams(("arbitrary",)),
        name="out_projection",
    )(x, merged, w_out, mod, g_norm2.reshape(DEPTH, 1, D_MODEL), mod, mod)


ROUTE_LANES = 128
NOT_SELECTED = 127


def _extract_top(x, n):
    vals = []
    rank = jnp.full(x.shape, float(NOT_SELECTED), F32)
    for r in range(n):
        m = jnp.max(x, axis=0, keepdims=True)
        hit = x == m
        vals.append(m)
        rank = jnp.where(hit, float(r), rank)
        x = jnp.where(hit, -jnp.inf, x)
    return vals, rank


def _stack_rows(rows):
    idx = lax.broadcasted_iota(jnp.int32, (len(rows), rows[0].shape[1]), 0)
    m = jnp.zeros((len(rows), rows[0].shape[1]), F32)
    for r, row in enumerate(rows):
        m = jnp.where(idx == r, row, m)
    return m


def _route_chunk(s1, s2):
    k = PEER_TOPK
    a, r1 = _extract_top(s1, k)
    b, r2 = _extract_top(s2, k)
    am, bm = _stack_rows(a), _stack_rows(b)
    slabs = [a[0] + bm[0:8], a[0] + bm[8:16]]
    slabs += [a[r] + bm[0:8] for r in range(1, 8)]
    slabs += [am[8:16] + b[0]]
    cand = jnp.concatenate(slabs, axis=0)
    x = cand
    for _ in range(k):
        tau = jnp.max(x, axis=0, keepdims=True)
        x = jnp.where(x == tau, -jnp.inf, x)
    sel = cand >= tau
    top = a[0] + b[0]
    z = jnp.sum(jnp.where(sel, jnp.exp(cand - top), 0.0), axis=0, keepdims=True)
    cnt = jnp.where(sel, 1.0, 0.0)
    n_r = [jnp.sum(cnt[0:16], axis=0, keepdims=True)]
    n_r += [jnp.sum(cnt[8 + 8 * r:16 + 8 * r], axis=0, keepdims=True) for r in range(1, 8)]
    tail = cnt[72:80]
    c1 = jnp.full(s1.shape, -1.0, F32)
    for r in range(k):
        last = n_r[r] - 1.0 if r < 8 else tail[r - 8:r - 7] - 1.0
        c1 = jnp.where(r1 == float(r), last, c1)
    e1 = jnp.exp(s1 - a[0]) / z
    e2 = jnp.exp(s2 - b[0])
    return c1, e1, r2, e2


def _route_kernel(ht_ref, wq_ref, sk_ref, c1_ref, e1_ref, r2_ref, e2_ref):
    qt = jnp.dot(wq_ref[...], ht_ref[...], preferred_element_type=F32)
    s1 = jnp.dot(sk_ref[0], qt[0:PEER_KEYS].astype(BF), preferred_element_type=F32)
    s2 = jnp.dot(sk_ref[1], qt[PEER_KEYS:].astype(BF), preferred_element_type=F32)

    def pack_row_pairs(x):
        return pltpu.bitcast(x.astype(BF), jnp.int32)

    def duplicate_halves(x):
        hi = pltpu.bitcast(x.astype(BF).astype(F32), jnp.int32)
        return hi | lax.shift_right_logical(hi, 16)

    for c in range(s1.shape[1] // ROUTE_LANES):
        cs = slice(c * ROUTE_LANES, (c + 1) * ROUTE_LANES)
        c1, e1, r2, e2 = _route_chunk(s1[:, cs], s2[:, cs])
        c1_ref[:, cs] = duplicate_halves(c1)
        e1_ref[:, cs] = duplicate_halves(e1)
        r2_ref[:, cs] = pack_row_pairs(r2)
        e2_ref[:, cs] = pack_row_pairs(e2)


def _route(h2t, w_peer_qt, subkeys_bf, l):
    tn = 512
    rows = PEER_HEADS * PEER_KEYS
    spec = pl.BlockSpec((PEER_KEYS, tn), lambda t, h: (h, t))
    pair_spec = pl.BlockSpec((PEER_KEYS // 2, tn), lambda t, h: (h, t))
    dup = jax.ShapeDtypeStruct((rows, N_TOK), jnp.int32)
    pairs = jax.ShapeDtypeStruct((rows // 2, N_TOK), jnp.int32)
    return pl.pallas_call(
        _route_kernel,
        grid=(N_TOK // tn, PEER_HEADS),
        in_specs=[
            pl.BlockSpec((D_MODEL, tn), lambda t, h: (0, t)),
            pl.BlockSpec((None, PEER_QDIM, D_MODEL), lambda t, h: (l, h, 0)),
            pl.BlockSpec((None, 2, PEER_KEYS, PEER_KEYS), lambda t, h: (l, 0, 0, 0)),
        ],
        out_specs=[spec, spec, pair_spec, pair_spec],
        out_shape=[dup, dup, pairs, pairs],
        compiler_params=_cparams(("arbitrary", "arbitrary")),
        name="peer_route",
    )(h2t, w_peer_qt, subkeys_bf)


PEER_LANES = 128
PEER_BLK = 256


def _peer_kernel(ht_ref, u_ref, vt_ref, c1_ref, e1_ref, r2_ref, e2_ref, o_ref,
                 at0_ref, at1_ref, wg0_ref, wg1_ref, *, tm, te):
    e = pl.program_id(1)
    nblk = te // PEER_BLK
    half = PEER_KEYS // 2

    @pl.when(e == 0)
    def _():
        o_ref[...] = jnp.zeros_like(o_ref)

    at_bufs = (at0_ref, at1_ref)
    wg_bufs = (wg0_ref, wg1_ref)

    def scores(q):
        rows = slice(q * PEER_BLK, (q + 1) * PEER_BLK)
        at_bufs[q % 2][...] = jnp.dot(u_ref[rows, :], ht_ref[...], preferred_element_type=F32)

    def weights(q):
        for bb in range(PEER_BLK // PEER_KEYS):
            i = (e * nblk + q) * (PEER_BLK // PEER_KEYS) + bb
            bs = slice(bb * PEER_KEYS, (bb + 1) * PEER_KEYS)
            last_rows = [c1_ref[pl.ds(h * PEER_KEYS + i, 1), :] for h in range(PEER_HEADS)]
            e1_rows = [e1_ref[pl.ds(h * PEER_KEYS + i, 1), :] for h in range(PEER_HEADS)]
            for c in range(tm // PEER_LANES):
                cs = slice(c * PEER_LANES, (c + 1) * PEER_LANES)
                w = jnp.zeros((PEER_KEYS, PEER_LANES), BF)
                for h in range(PEER_HEADS):
                    hs = slice(h * half, (h + 1) * half)
                    last = pltpu.bitcast(jnp.broadcast_to(last_rows[h][:, cs], (half, PEER_LANES)), BF)
                    e1 = pltpu.bitcast(jnp.broadcast_to(e1_rows[h][:, cs], (half, PEER_LANES)), BF)
                    r2 = pltpu.bitcast(r2_ref[hs, cs], BF)
                    e2 = pltpu.bitcast(e2_ref[hs, cs], BF)
                    w = w + jnp.where(r2 <= last, e2, jnp.zeros_like(e2)) * e1
                a = at_bufs[q % 2][bs, cs]
                g = (0.5 * a * (1.0 + lax.erf(a * math.sqrt(0.5)))).astype(BF)
                wg_bufs[q % 2][bs, cs] = w * g

    def update(q):
        cols = slice(q * PEER_BLK, (q + 1) * PEER_BLK)
        o_ref[...] += jnp.dot(vt_ref[:, cols], wg_bufs[q % 2][...], preferred_element_type=F32)

    scores(0)
    for q in range(nblk):
        if q + 1 < nblk:
            scores(q + 1)
        weights(q)
        if q >= 1:
            update(q - 1)
    update(nblk - 1)


def _peer(h2t, u_bf, vt_bf, routing, l):
    tm, te = 512, 1024
    c1, e1, r2, e2 = routing
    rows = PEER_HEADS * PEER_KEYS
    rspec = pl.BlockSpec((rows, tm), lambda t, e: (0, t))
    pspec = pl.BlockSpec((rows // 2, tm), lambda t, e: (0, t))
    return pl.pallas_call(
        functools.partial(_peer_kernel, tm=tm, te=te),
        grid=(N_TOK // tm, PEER_EXPERTS // te),
        in_specs=[
            pl.BlockSpec((D_MODEL, tm), lambda t, e: (0, t)),
            pl.BlockSpec((None, te, D_MODEL), lambda t, e: (l, e, 0)),
            pl.BlockSpec((None, D_MODEL, te), lambda t, e: (l, 0, e)),
            rspec, rspec, pspec, pspec,
        ],
        out_specs=pl.BlockSpec((D_MODEL, tm), lambda t, e: (0, t)),
        out_shape=jax.ShapeDtypeStruct((D_MODEL, N_TOK), F32),
        scratch_shapes=[
            pltpu.VMEM((PEER_BLK, tm), F32),
            pltpu.VMEM((PEER_BLK, tm), F32),
            pltpu.VMEM((PEER_BLK, tm), BF),
            pltpu.VMEM((PEER_BLK, tm), BF),
        ],
        compiler_params=_cparams(("arbitrary", "arbitrary")),
        name="peer_dense",
    )(h2t, u_bf, vt_bf, c1, e1, r2, e2)


def _peer_residual_kernel(x_ref, pt_ref, ga_ref, o_ref, *, tm):
    seg = _segment(pl.program_id(0) * tm)
    o_ref[...] = x_ref[...] + ga_ref[pl.ds(seg, 1), :] * pt_ref[...].T


def _peer_residual(x, peer_t, mod, l):
    tm = 512
    return pl.pallas_call(
        functools.partial(_peer_residual_kernel, tm=tm),
        grid=(N_TOK // tm,),
        in_specs=[
            pl.BlockSpec((tm, D_MODEL), lambda i: (i, 0)),
            pl.BlockSpec((D_MODEL, tm), lambda i: (0, i)),
            pl.BlockSpec((None, 8, D_MODEL), lambda i: (l, 0, 5)),
        ],
        out_specs=pl.BlockSpec((tm, D_MODEL), lambda i: (i, 0)),
        out_shape=jax.ShapeDtypeStruct((N_TOK, D_MODEL), F32),
        compiler_params=_cparams(("arbitrary",)),
        name="peer_residual",
    )(x, peer_t, mod)


def _final_kernel(x_ref, g_ref, o_ref):
    o_ref[...] = _rms(x_ref[...], g_ref[...])


def _final_norm(x, g_final):
    tm = 512
    return pl.pallas_call(
        _final_kernel,
        grid=(N_TOK // tm,),
        in_specs=[pl.BlockSpec((tm, D_MODEL), lambda i: (i, 0)),
                  pl.BlockSpec((1, D_MODEL), lambda i: (0, 0))],
        out_specs=pl.BlockSpec((tm, D_MODEL), lambda i: (i, 0)),
        out_shape=jax.ShapeDtypeStruct((N_TOK, D_MODEL), F32),
        compiler_params=_cparams(("arbitrary",)),
        name="final_norm",
    )(x, g_final.reshape(1, D_MODEL))


def _rope_tables():
    t = jnp.arange(DEC_SEQ)
    half = MLA_ROPE // 4
    inv = ROPE_BASE ** (-jnp.arange(half, dtype=F32) / half)
    ang_r = (t // GRID_W).astype(F32)[:, None] * inv
    ang_c = (t % GRID_W).astype(F32)[:, None] * inv
    cos = jnp.concatenate([jnp.cos(ang_r)] * 2 + [jnp.cos(ang_c)] * 2, axis=1)
    sin = jnp.concatenate([-jnp.sin(ang_r), jnp.sin(ang_r), -jnp.sin(ang_c), jnp.sin(ang_c)], axis=1)
    return cos, sin


def _relayout_w_in(w_in):
    q, ckv, kr, pz, dq, dk, dv, gl = jnp.split(
        w_in, (512, 768, 832, 1344, 1856, 2368, 2880), axis=-1)
    pad = jnp.zeros(w_in.shape[:-1] + (C_PZ - C_KR - MLA_ROPE,), w_in.dtype)
    return jnp.concatenate([q, ckv, kr, pad, pz, dq, dk, dv, gl], axis=-1).astype(BF)


def _relayout_w_qb(w_qb):
    w = w_qb.reshape(DEPTH, MLA_Q_RANK, MLA_HEADS, MLA_NOPE + MLA_ROPE)
    w = jnp.pad(w, ((0, 0), (0, 0), (0, 0), (0, MLA_HEAD_PAD - MLA_NOPE - MLA_ROPE)))
    return w.reshape(DEPTH, MLA_Q_RANK, MLA_HEADS * MLA_HEAD_PAD).astype(BF)


def kernel(x_prompt, x_sample, cache_mla_ckv, cache_mla_krope, cache_diff_k, cache_diff_v, c, c_ctx,
           w_mod, b_mod, g_norm1, w_in, g_qnorm, w_qb, g_kvnorm, w_kvb, w_pool, pool_scale,
           diff_lambda, g_diffnorm, w_br_mla, w_br_pool, w_br_diff, w_out, g_norm2,
           w_peer_q, peer_subkeys, peer_u, peer_v, g_final):
    x = jnp.concatenate([x_prompt.reshape(N_CTX, D_MODEL), x_sample.reshape(N_DEC, D_MODEL)], axis=0)
    cond8 = jnp.concatenate([c_ctx[None, :], c, jnp.zeros((8 - 1 - DEC_BATCH, D_MODEL), F32)], axis=0)

    w_in_re = _relayout_w_in(w_in)
    w_qb_re = _relayout_w_qb(w_qb)
    w_kvb_bf = w_kvb.astype(BF)
    w_pool_bf = w_pool.astype(BF)
    w_br_mla_bf, w_br_pool_bf, w_br_diff_bf = w_br_mla.astype(BF), w_br_pool.astype(BF), w_br_diff.astype(BF)
    w_out_bf = w_out.astype(BF)
    w_peer_qt = jnp.swapaxes(w_peer_q, 1, 2).astype(BF)
    subkeys_bf = peer_subkeys.astype(BF)
    u_bf = peer_u.astype(BF)
    vt_bf = jnp.swapaxes(peer_v, 1, 2).astype(BF)

    cache_kr_pad = jnp.pad(cache_mla_krope, ((0, 0), (0, 0), (0, 0), (0, 128 - MLA_ROPE)))
    cache_k = cache_diff_k.reshape(DEC_BATCH, DEPTH, PAST_LEN, DIFF_WIDTH)
    cache_v = cache_diff_v.reshape(DEC_BATCH, DEPTH, PAST_LEN, DIFF_WIDTH)

    cos64, sin64 = _rope_tables()
    ones, zeros = jnp.ones_like(cos64), jnp.zeros_like(cos64)
    mla_tabs = (
        jnp.concatenate([ones, ones, cos64, ones], axis=1),
        jnp.concatenate([zeros, zeros, sin64, zeros], axis=1),
        jnp.concatenate([cos64, ones], axis=1),
        jnp.concatenate([sin64, zeros], axis=1),
    )
    diff_tabs = (jnp.tile(cos64, (1, DIFF_WIDTH // DIFF_QK)), jnp.tile(sin64, (1, DIFF_WIDTH // DIFF_QK)))

    mod = _modulation(cond8, w_mod, b_mod)

    st_ckv, st_kr, st_k, st_v = [], [], [], []
    for l in range(DEPTH):
        p_small = _in_projection(x, g_norm1, mod, w_in_re, l, 0, SMALL_WIDTH, F32)
        gl = _in_projection(x, g_norm1, mod, w_in_re, l, SMALL_WIDTH, GATE_WIDTH, BF)
        mla_c, ckv_c, kr_c = _mla_ctx(p_small, g_qnorm, g_kvnorm, w_qb_re, w_kvb_bf, l)
        mla_d = _mla_dec(p_small, cache_mla_ckv, cache_kr_pad, mla_tabs, g_qnorm, g_kvnorm, w_qb_re, w_kvb_bf, l)
        diff_c = _diff_ctx(p_small, diff_lambda, g_diffnorm, l)
        diff_d = _diff_dec(p_small, cache_k, cache_v, diff_tabs, diff_lambda, g_diffnorm, l)
        pool_o = _pool(p_small, w_pool_bf, pool_scale, l)
        mla_o = jnp.concatenate([mla_c, mla_d], axis=0)
        diff_o = jnp.concatenate([diff_c, diff_d], axis=0)
        merged = _merge(mla_o, pool_o, diff_o, gl, w_br_mla_bf, w_br_pool_bf, w_br_diff_bf, l)
        x, h2t = _out_projection(x, merged, w_out_bf, mod, g_norm2, l)
        routing = _route(h2t, w_peer_qt, subkeys_bf, l)
        x = _peer_residual(x, _peer(h2t, u_bf, vt_bf, routing, l), mod, l)
        st_ckv.append(ckv_c.reshape(BATCH, SEQ, MLA_KV_RANK))
        st_kr.append(kr_c.reshape(BATCH, SEQ, MLA_ROPE))
        st_k.append(p_small[:N_CTX, C_DK:C_DK + DIFF_WIDTH].reshape(BATCH, SEQ, DIFF_HEADS, 2 * DIFF_QK))
        st_v.append(p_small[:N_CTX, C_DV:C_DV + DIFF_WIDTH].reshape(BATCH, SEQ, DIFF_HEADS, DIFF_V))

    y = _final_norm(x, g_final)
    y_prompt = y[:N_CTX].reshape(BATCH, SEQ, D_MODEL)
    y_sample = y[N_CTX:].reshape(DEC_BATCH, DEC_SEQ, D_MODEL)
    return (y_prompt, y_sample, jnp.stack(st_ckv, axis=1), jnp.stack(st_kr, axis=1),
            jnp.stack(st_k, axis=1), jnp.stack(st_v, axis=1))
```

```python
import functools
import math

import jax
import jax.numpy as jnp
from jax import lax
from jax.experimental import pallas as pl
from jax.experimental.pallas import tpu as pltpu

BF = jnp.bfloat16
F32 = jnp.float32

D_MODEL = 2048
BATCH = 16
SEQ = 256
DEPTH = 4
DEC_BATCH = 2
DEC_SEQ = 1024
PAST_LEN = 256
GRID_W = 64
ROPE_BASE = 10000.0
EPS = 1e-6

MLA_HEADS = 8
MLA_NOPE = 128
MLA_ROPE = 64
MLA_V = 128
MLA_Q_RANK = 512
MLA_KV_RANK = 256
MLA_HEAD_PAD = 256

POOL_WINDOWS = (2, 4, 8, 16)
POOL_GROUP = 128
POOL_WIDTH = POOL_GROUP * len(POOL_WINDOWS)

DIFF_HEADS = 4
DIFF_QK = 64
DIFF_V = 2 * DIFF_QK
DIFF_WIDTH = DIFF_HEADS * DIFF_V

PEER_HEADS = 8
PEER_KEYS = 128
PEER_EXPERTS = PEER_KEYS * PEER_KEYS
PEER_TOPK = 16
PEER_QDIM = 256

N_CTX = BATCH * SEQ
N_DEC = DEC_BATCH * DEC_SEQ
N_TOK = N_CTX + N_DEC
KEYS_DEC = PAST_LEN + DEC_SEQ

C_QA = 0
C_CKV = 512
C_KR = 768
C_PZ = 1024
C_DQ = 1536
C_DK = 2048
C_DV = 2560
SMALL_WIDTH = 3072
GATE_WIDTH = 3 * D_MODEL

VMEM_LIMIT = 56 * 1024 * 1024


def _cparams(sem):
    return pltpu.CompilerParams(dimension_semantics=sem, vmem_limit_bytes=VMEM_LIMIT)


def _segment(row0):
    return jnp.where(row0 < N_CTX, 0, 1 + (row0 - N_CTX) // DEC_SEQ)


def _rms(x, g):
    return x * lax.rsqrt(jnp.mean(x * x, axis=-1, keepdims=True) + EPS) * g


def _softmax_rows(s):
    m = jnp.max(s, axis=-1, keepdims=True)
    p = jnp.exp(s - m)
    return p / jnp.sum(p, axis=-1, keepdims=True)


def _dot_nt(a, b):
    return lax.dot_general(a, b, (((1,), (1,)), ((), ())), preferred_element_type=F32)


def _rope_swap(x):
    n = x.shape[-1]
    up = pltpu.roll(x, n - 16, 1)
    dn = pltpu.roll(x, 16, 1)
    lane = lax.broadcasted_iota(jnp.int32, x.shape, 1)
    return jnp.where((lane & 16) == 0, up, dn)


def _mod_kernel(c_ref, w_ref, b_ref, o_ref):
    c = c_ref[...]
    a = (c * jax.nn.sigmoid(c)).astype(BF)
    o_ref[...] = jnp.dot(a, w_ref[...].astype(BF), preferred_element_type=F32) + b_ref[...]


def _modulation(cond8, w_mod, b_mod):
    tn = 1024
    return pl.pallas_call(
        _mod_kernel,
        grid=(DEPTH, 6 * D_MODEL // tn),
        in_specs=[
            pl.BlockSpec((8, D_MODEL), lambda l, j: (0, 0)),
            pl.BlockSpec((None, D_MODEL, tn), lambda l, j: (l, 0, j)),
            pl.BlockSpec((None, 1, tn), lambda l, j: (l, 0, j)),
        ],
        out_specs=pl.BlockSpec((None, 8, tn), lambda l, j: (l, 0, j)),
        out_shape=jax.ShapeDtypeStruct((DEPTH, 8, 6 * D_MODEL), F32),
        compiler_params=_cparams(("arbitrary", "arbitrary")),
        name="modulation",
    )(cond8, w_mod, b_mod.reshape(DEPTH, 1, 6 * D_MODEL))


def _inproj_kernel(x_ref, g_ref, sh_ref, sc_ref, w_ref, o_ref, h_ref, *, tm):
    i = pl.program_id(0)

    @pl.when(pl.program_id(1) == 0)
    def _():
        seg = _segment(i * tm)
        y = _rms(x_ref[...], g_ref[...])
        h_ref[...] = (y * (1.0 + sc_ref[pl.ds(seg, 1), :]) + sh_ref[pl.ds(seg, 1), :]).astype(BF)

    o_ref[...] = jnp.dot(h_ref[...], w_ref[...], preferred_element_type=F32).astype(o_ref.dtype)


def _in_projection(x, g_norm1, mod, w_in_re, l, col0, width, out_dtype, tm=1024, tn=512):
    off = col0 // tn
    return pl.pallas_call(
        functools.partial(_inproj_kernel, tm=tm),
        grid=(N_TOK // tm, width // tn),
        in_specs=[
            pl.BlockSpec((tm, D_MODEL), lambda i, j: (i, 0)),
            pl.BlockSpec((None, 1, D_MODEL), lambda i, j: (l, 0, 0)),
            pl.BlockSpec((None, 8, D_MODEL), lambda i, j: (l, 0, 0)),
            pl.BlockSpec((None, 8, D_MODEL), lambda i, j: (l, 0, 1)),
            pl.BlockSpec((None, D_MODEL, tn), lambda i, j: (l, 0, j + off)),
        ],
        out_specs=pl.BlockSpec((tm, tn), lambda i, j: (i, j)),
        out_shape=jax.ShapeDtypeStruct((N_TOK, width), out_dtype),
        scratch_shapes=[pltpu.VMEM((tm, D_MODEL), BF)],
        compiler_params=_cparams(("arbitrary", "arbitrary")),
        name="in_projection",
    )(x, g_norm1.reshape(DEPTH, 1, D_MODEL), mod, mod, w_in_re)


def _mla_heads(q, kfun, vfun, o_ref, scale):
    for h in range(MLA_HEADS):
        s = _dot_nt(q(h), kfun(h)) * scale
        p = _softmax_rows(s).astype(BF)
        o = jnp.dot(p, vfun(h), preferred_element_type=F32)
        o_ref[:, h * MLA_V:(h + 1) * MLA_V] = o.astype(o_ref.dtype)


def _mla_ctx_kernel(qa_ref, ckv_ref, kr_ref, gq_ref, gkv_ref, wqb_ref, wkvb_ref, st_ckv_hbm, st_kr_hbm,
                    o_ref, ckv_out_ref, kr_out_ref):
    del st_ckv_hbm, st_kr_hbm
    scale = (MLA_NOPE + MLA_ROPE) ** -0.5
    qn = _rms(qa_ref[...], gq_ref[...]).astype(BF)
    q = jnp.dot(qn, wqb_ref[...], preferred_element_type=F32).astype(BF)
    ckv = _rms(ckv_ref[...], gkv_ref[...])
    ckv_out_ref[...] = ckv
    kr = kr_ref[...]
    kr_out_ref[...] = kr[:, :MLA_ROPE]
    kv = jnp.dot(ckv.astype(BF), wkvb_ref[...], preferred_element_type=F32).astype(BF)
    krp = kr[:, :128].astype(BF)

    def qh(h):
        return q[:, h * MLA_HEAD_PAD:(h + 1) * MLA_HEAD_PAD]

    def kh(h):
        return jnp.concatenate([kv[:, h * 256:h * 256 + MLA_NOPE], krp], axis=1)

    def vh(h):
        return kv[:, h * 256 + MLA_NOPE:(h + 1) * 256]

    _mla_heads(qh, kh, vh, o_ref, scale)


def _mla_ctx(p_small, g_qnorm, g_kvnorm, w_qb_re, w_kvb, st_ckv, st_kr, l):
    return pl.pallas_call(
        _mla_ctx_kernel,
        grid=(BATCH,),
        in_specs=[
            pl.BlockSpec((SEQ, MLA_Q_RANK), lambda b: (b, C_QA // MLA_Q_RANK)),
            pl.BlockSpec((SEQ, 256), lambda b: (b, C_CKV // 256)),
            pl.BlockSpec((SEQ, 256), lambda b: (b, C_KR // 256)),
            pl.BlockSpec((None, 1, MLA_Q_RANK), lambda b: (l, 0, 0)),
            pl.BlockSpec((None, 1, MLA_KV_RANK), lambda b: (l, 0, 0)),
            pl.BlockSpec((None, MLA_Q_RANK, MLA_HEADS * MLA_HEAD_PAD), lambda b: (l, 0, 0)),
            pl.BlockSpec((None, MLA_KV_RANK, MLA_HEADS * 256), lambda b: (l, 0, 0)),
            pl.BlockSpec(memory_space=pl.ANY),
            pl.BlockSpec(memory_space=pl.ANY),
        ],
        out_specs=[
            pl.BlockSpec((SEQ, MLA_HEADS * MLA_V), lambda b: (b, 0)),
            pl.BlockSpec((None, None, SEQ, MLA_KV_RANK), lambda b: (b, l, 0, 0)),
            pl.BlockSpec((None, None, SEQ, MLA_ROPE), lambda b: (b, l, 0, 0)),
        ],
        out_shape=[
            jax.ShapeDtypeStruct((N_CTX, MLA_HEADS * MLA_V), BF),
            jax.ShapeDtypeStruct(st_ckv.shape, F32),
            jax.ShapeDtypeStruct(st_kr.shape, F32),
        ],
        input_output_aliases={7: 1, 8: 2},
        compiler_params=_cparams(("arbitrary",)),
        name="mla_ctx",
    )(p_small, p_small, p_small, g_qnorm.reshape(DEPTH, 1, -1), g_kvnorm.reshape(DEPTH, 1, -1),
      w_qb_re, w_kvb, st_ckv, st_kr)


def _mla_dec_kernel(qa_ref, ckv_ref, kr_ref, cckv_ref, ckr_ref, cq_ref, sq_ref, ck_ref, sk_ref,
                    gq_ref, gkv_ref, wqb_ref, wkvb_ref, o_ref, kf_ref, vf_ref):
    scale = (MLA_NOPE + MLA_ROPE) ** -0.5

    @pl.when(pl.program_id(1) == 0)
    def _():
        ckv = _rms(ckv_ref[...], gkv_ref[...])
        ckv_all = jnp.concatenate([cckv_ref[...], ckv], axis=0).astype(BF)
        kr = kr_ref[:, :128]
        kr_rot = kr * ck_ref[...] + _rope_swap(kr) * sk_ref[...]
        kr_all = jnp.concatenate([ckr_ref[...], kr_rot], axis=0).astype(BF)
        for h in range(MLA_HEADS):
            kvh = jnp.dot(ckv_all, wkvb_ref[:, h * 256:(h + 1) * 256], preferred_element_type=F32)
            kf_ref[h, :, 0:MLA_NOPE] = kvh[:, :MLA_NOPE].astype(BF)
            kf_ref[h, :, MLA_NOPE:MLA_HEAD_PAD] = kr_all
            vf_ref[h] = kvh[:, MLA_NOPE:].astype(BF)

    qn = _rms(qa_ref[...], gq_ref[...]).astype(BF)
    q = jnp.dot(qn, wqb_ref[...], preferred_element_type=F32)
    cq = cq_ref[...]
    sq = sq_ref[...]

    def qh(h):
        x = q[:, h * MLA_HEAD_PAD:(h + 1) * MLA_HEAD_PAD]
        return (x * cq + _rope_swap(x) * sq).astype(BF)

    _mla_heads(qh, lambda h: kf_ref[h], lambda h: vf_ref[h], o_ref, scale)


def _mla_dec(p_small, cache_ckv, cache_kr_pad, tabs, g_qnorm, g_kvnorm, w_qb_re, w_kvb, l):
    tq = 256
    nq = DEC_SEQ // tq
    row_blk = N_CTX // DEC_SEQ
    cq, sq, ck, sk = tabs
    return pl.pallas_call(
        _mla_dec_kernel,
        grid=(DEC_BATCH, nq),
        in_specs=[
            pl.BlockSpec((tq, MLA_Q_RANK), lambda b, i: (N_CTX // tq + b * nq + i, 0)),
            pl.BlockSpec((DEC_SEQ, 256), lambda b, i: (row_blk + b, C_CKV // 256)),
            pl.BlockSpec((DEC_SEQ, 256), lambda b, i: (row_blk + b, C_KR // 256)),
            pl.BlockSpec((None, None, PAST_LEN, MLA_KV_RANK), lambda b, i: (b, l, 0, 0)),
            pl.BlockSpec((None, None, PAST_LEN, 128), lambda b, i: (b, l, 0, 0)),
            pl.BlockSpec((tq, MLA_HEAD_PAD), lambda b, i: (i, 0)),
            pl.BlockSpec((tq, MLA_HEAD_PAD), lambda b, i: (i, 0)),
            pl.BlockSpec((DEC_SEQ, 128), lambda b, i: (0, 0)),
            pl.BlockSpec((DEC_SEQ, 128), lambda b, i: (0, 0)),
            pl.BlockSpec((None, 1, MLA_Q_RANK), lambda b, i: (l, 0, 0)),
            pl.BlockSpec((None, 1, MLA_KV_RANK), lambda b, i: (l, 0, 0)),
            pl.BlockSpec((None, MLA_Q_RANK, MLA_HEADS * MLA_HEAD_PAD), lambda b, i: (l, 0, 0)),
            pl.BlockSpec((None, MLA_KV_RANK, MLA_HEADS * 256), lambda b, i: (l, 0, 0)),
        ],
        out_specs=pl.BlockSpec((tq, MLA_HEADS * MLA_V), lambda b, i: (b * nq + i, 0)),
        out_shape=jax.ShapeDtypeStruct((N_DEC, MLA_HEADS * MLA_V), BF),
        scratch_shapes=[
            pltpu.VMEM((MLA_HEADS, KEYS_DEC, MLA_HEAD_PAD), BF),
            pltpu.VMEM((MLA_HEADS, KEYS_DEC, MLA_V), BF),
        ],
        compiler_params=_cparams(("arbitrary", "arbitrary")),
        name="mla_dec",
    )(p_small, p_small, p_small, cache_ckv, cache_kr_pad, cq, sq, ck, sk,
      g_qnorm.reshape(DEPTH, 1, -1), g_kvnorm.reshape(DEPTH, 1, -1), w_qb_re, w_kvb)


def _diff_lambda(lv):
    t1 = jnp.sum(lv[0:1] * lv[1:2], axis=-1, keepdims=True)
    t2 = jnp.sum(lv[2:3] * lv[3:4], axis=-1, keepdims=True)
    return jnp.exp(t1) - jnp.exp(t2)


def _diff_heads(q, kfun, vfun, lam, g, o_ref, lam_init):
    scale = DIFF_QK ** -0.5
    lane = lax.broadcasted_iota(jnp.int32, (q.shape[0], DIFF_V), 1)
    for h in range(DIFF_HEADS):
        qh = q[:, h * DIFF_V:(h + 1) * DIFF_V]
        q1 = jnp.where(lane < DIFF_QK, qh, 0.0).astype(BF)
        q2 = jnp.where(lane >= DIFF_QK, qh, 0.0).astype(BF)
        k = kfun(h)
        v = vfun(h)
        p1 = _softmax_rows(_dot_nt(q1, k) * scale).astype(BF)
        p2 = _softmax_rows(_dot_nt(q2, k) * scale).astype(BF)
        a1 = jnp.dot(p1, v, preferred_element_type=F32)
        a2 = jnp.dot(p2, v, preferred_element_type=F32)
        o = _rms(a1 - lam * a2, g) * (1.0 - lam_init)
        o_ref[:, h * DIFF_V:(h + 1) * DIFF_V] = o.astype(o_ref.dtype)


def _diff_ctx_kernel(q_ref, k_ref, v_ref, lam_ref, g_ref, st_k_hbm, st_v_hbm,
                     o_ref, k_out_ref, v_out_ref, *, lam_init):
    del st_k_hbm, st_v_hbm
    lam = _diff_lambda(lam_ref[...]) + lam_init
    k_out_ref[...] = k_ref[...]
    v_out_ref[...] = v_ref[...]
    k = k_ref[...].astype(BF)
    v = v_ref[...].astype(BF)
    _diff_heads(q_ref[...],
                lambda h: k[:, h * DIFF_V:(h + 1) * DIFF_V],
                lambda h: v[:, h * DIFF_V:(h + 1) * DIFF_V],
                lam, g_ref[...], o_ref, lam_init)


def _lam_init(l):
    return 0.8 - 0.6 * math.exp(-0.3 * l)


def _diff_ctx(p_small, diff_lambda, g_diffnorm, st_k, st_v, l):
    return pl.pallas_call(
        functools.partial(_diff_ctx_kernel, lam_init=_lam_init(l)),
        grid=(BATCH,),
        in_specs=[
            pl.BlockSpec((SEQ, DIFF_WIDTH), lambda b: (b, C_DQ // DIFF_WIDTH)),
            pl.BlockSpec((SEQ, DIFF_WIDTH), lambda b: (b, C_DK // DIFF_WIDTH)),
            pl.BlockSpec((SEQ, DIFF_WIDTH), lambda b: (b, C_DV // DIFF_WIDTH)),
            pl.BlockSpec((None, 4, DIFF_QK), lambda b: (l, 0, 0)),
            pl.BlockSpec((None, 1, DIFF_V), lambda b: (l, 0, 0)),
            pl.BlockSpec(memory_space=pl.ANY),
            pl.BlockSpec(memory_space=pl.ANY),
        ],
        out_specs=[
            pl.BlockSpec((SEQ, DIFF_WIDTH), lambda b: (b, 0)),
            pl.BlockSpec((None, None, SEQ, DIFF_WIDTH), lambda b: (b, l, 0, 0)),
            pl.BlockSpec((None, None, SEQ, DIFF_WIDTH), lambda b: (b, l, 0, 0)),
        ],
        out_shape=[
            jax.ShapeDtypeStruct((N_CTX, DIFF_WIDTH), BF),
            jax.ShapeDtypeStruct(st_k.shape, F32),
            jax.ShapeDtypeStruct(st_v.shape, F32),
        ],
        input_output_aliases={5: 1, 6: 2},
        compiler_params=_cparams(("arbitrary",)),
        name="diff_ctx",
    )(p_small, p_small, p_small, diff_lambda, g_diffnorm.reshape(DEPTH, 1, -1), st_k, st_v)


def _diff_dec_kernel(q_ref, k_ref, v_ref, ck_ref, cv_ref, cq_ref, sq_ref, cfull_ref, sfull_ref,
                     lam_ref, g_ref, o_ref, kf_ref, vf_ref, *, lam_init):
    @pl.when(pl.program_id(1) == 0)
    def _():
        k = k_ref[...]
        k_rot = k * cfull_ref[...] + _rope_swap(k) * sfull_ref[...]
        kf_ref[0:PAST_LEN, :] = ck_ref[...].astype(BF)
        kf_ref[PAST_LEN:KEYS_DEC, :] = k_rot.astype(BF)
        vf_ref[0:PAST_LEN, :] = cv_ref[...].astype(BF)
        vf_ref[PAST_LEN:KEYS_DEC, :] = v_ref[...].astype(BF)

    lam = _diff_lambda(lam_ref[...]) + lam_init
    q = q_ref[...]
    q = q * cq_ref[...] + _rope_swap(q) * sq_ref[...]
    _diff_heads(q,
                lambda h: kf_ref[:, h * DIFF_V:(h + 1) * DIFF_V],
                lambda h: vf_ref[:, h * DIFF_V:(h + 1) * DIFF_V],
                lam, g_ref[...], o_ref, lam_init)


def _diff_dec(p_small, cache_k, cache_v, tabs, diff_lambda, g_diffnorm, l):
    tq = 256
    nq = DEC_SEQ // tq
    row_blk = N_CTX // DEC_SEQ
    c512, s512 = tabs
    return pl.pallas_call(
        functools.partial(_diff_dec_kernel, lam_init=_lam_init(l)),
        grid=(DEC_BATCH, nq),
        in_specs=[
            pl.BlockSpec((tq, DIFF_WIDTH), lambda b, i: (N_CTX // tq + b * nq + i, C_DQ // DIFF_WIDTH)),
            pl.BlockSpec((DEC_SEQ, DIFF_WIDTH), lambda b, i: (row_blk + b, C_DK // DIFF_WIDTH)),
            pl.BlockSpec((DEC_SEQ, DIFF_WIDTH), lambda b, i: (row_blk + b, C_DV // DIFF_WIDTH)),
            pl.BlockSpec((None, None, PAST_LEN, DIFF_WIDTH), lambda b, i: (b, l, 0, 0)),
            pl.BlockSpec((None, None, PAST_LEN, DIFF_WIDTH), lambda b, i: (b, l, 0, 0)),
            pl.BlockSpec((tq, DIFF_WIDTH), lambda b, i: (i, 0)),
            pl.BlockSpec((tq, DIFF_WIDTH), lambda b, i: (i, 0)),
            pl.BlockSpec((DEC_SEQ, DIFF_WIDTH), lambda b, i: (0, 0)),
            pl.BlockSpec((DEC_SEQ, DIFF_WIDTH), lambda b, i: (0, 0)),
            pl.BlockSpec((None, 4, DIFF_QK), lambda b, i: (l, 0, 0)),
            pl.BlockSpec((None, 1, DIFF_V), lambda b, i: (l, 0, 0)),
        ],
        out_specs=pl.BlockSpec((tq, DIFF_WIDTH), lambda b, i: (b * nq + i, 0)),
        out_shape=jax.ShapeDtypeStruct((N_DEC, DIFF_WIDTH), BF),
        scratch_shapes=[
            pltpu.VMEM((KEYS_DEC, DIFF_WIDTH), BF),
            pltpu.VMEM((KEYS_DEC, DIFF_WIDTH), BF),
        ],
        compiler_params=_cparams(("arbitrary", "arbitrary")),
        name="diff_dec",
    )(p_small, p_small, p_small, cache_k, cache_v, c512, s512, c512, s512,
      diff_lambda, g_diffnorm.reshape(DEPTH, 1, -1))


def _pool_kernel(z_ref, w_ref, ps_ref, o_ref, *, tm):
    seq_m1 = jnp.where(pl.program_id(0) * tm < N_CTX, SEQ - 1, DEC_SEQ - 1)
    t = lax.broadcasted_iota(jnp.int32, (tm, POOL_GROUP), 0) & seq_m1
    for gi, w in enumerate(POOL_WINDOWS):
        z = z_ref[:, gi * POOL_GROUP:(gi + 1) * POOL_GROUP]
        acc = jnp.zeros_like(z)
        for k in range(-(w // 2), w // 2):
            zs = z if k == 0 else pltpu.roll(z, (-k) % tm, 0)
            ok = (t + k >= 0) & (t + k <= seq_m1)
            acc = acc + jnp.where(ok, zs, 0.0)
        lo = jnp.maximum(t - w // 2, 0)
        hi = jnp.minimum(t + (w - 1) // 2, seq_m1)
        d = (acc / (hi - lo + 1).astype(F32) - z).astype(BF)
        y = jnp.dot(d, w_ref[gi], preferred_element_type=F32)
        y = y * ps_ref[:, gi * POOL_GROUP:(gi + 1) * POOL_GROUP]
        o_ref[:, gi * POOL_GROUP:(gi + 1) * POOL_GROUP] = y.astype(o_ref.dtype)


def _pool(p_small, w_pool_bf, pool_scale, l):
    tm = DEC_SEQ
    return pl.pallas_call(
        functools.partial(_pool_kernel, tm=tm),
        grid=(N_TOK // tm,),
        in_specs=[
            pl.BlockSpec((tm, POOL_WIDTH), lambda i: (i, C_PZ // POOL_WIDTH)),
            pl.BlockSpec((None, len(POOL_WINDOWS), POOL_GROUP, POOL_GROUP), lambda i: (l, 0, 0, 0)),
            pl.BlockSpec((None, 1, POOL_WIDTH), lambda i: (l, 0, 0)),
        ],
        out_specs=pl.BlockSpec((tm, POOL_WIDTH), lambda i: (i, 0)),
        out_shape=jax.ShapeDtypeStruct((N_TOK, POOL_WIDTH), BF),
        compiler_params=_cparams(("arbitrary",)),
        name="pool",
    )(p_small, w_pool_bf, pool_scale.reshape(DEPTH, 1, POOL_WIDTH))


def _merge_kernel(a_ref, p_ref, d_ref, g0_ref, g1_ref, g2_ref, wa_ref, wp_ref, wd_ref, o_ref):
    def sig(r):
        return jax.nn.sigmoid(r[...].astype(F32))

    m = sig(g0_ref) * jnp.dot(a_ref[...], wa_ref[...], preferred_element_type=F32)
    m = m + sig(g1_ref) * jnp.dot(p_ref[...], wp_ref[...], preferred_element_type=F32)
    m = m + sig(g2_ref) * jnp.dot(d_ref[...], wd_ref[...], preferred_element_type=F32)
    o_ref[...] = m.astype(o_ref.dtype)


def _merge(mla_o, pool_o, diff_o, gl, w_br_mla, w_br_pool, w_br_diff, l):
    tm, tn = 1024, 512
    nj = D_MODEL // tn
    return pl.pallas_call(
        _merge_kernel,
        grid=(N_TOK // tm, nj),
        in_specs=[
            pl.BlockSpec((tm, MLA_HEADS * MLA_V), lambda i, j: (i, 0)),
            pl.BlockSpec((tm, POOL_WIDTH), lambda i, j: (i, 0)),
            pl.BlockSpec((tm, DIFF_WIDTH), lambda i, j: (i, 0)),
            pl.BlockSpec((tm, tn), lambda i, j: (i, j)),
            pl.BlockSpec((tm, tn), lambda i, j: (i, nj + j)),
            pl.BlockSpec((tm, tn), lambda i, j: (i, 2 * nj + j)),
            pl.BlockSpec((None, MLA_HEADS * MLA_V, tn), lambda i, j: (l, 0, j)),
            pl.BlockSpec((None, POOL_WIDTH, tn), lambda i, j: (l, 0, j)),
            pl.BlockSpec((None, DIFF_WIDTH, tn), lambda i, j: (l, 0, j)),
        ],
        out_specs=pl.BlockSpec((tm, tn), lambda i, j: (i, j)),
        out_shape=jax.ShapeDtypeStruct((N_TOK, D_MODEL), BF),
        compiler_params=_cparams(("arbitrary", "arbitrary")),
        name="merge",
    )(mla_o, pool_o, diff_o, gl, gl, gl, w_br_mla, w_br_pool, w_br_diff)


def _outproj_kernel(x_ref, m_ref, w_ref, ga_ref, g2_ref, sh_ref, sc_ref, xo_ref, ht_ref, *, tm):
    seg = _segment(pl.program_id(0) * tm)
    y = jnp.dot(m_ref[...], w_ref[...], preferred_element_type=F32)
    xn = x_ref[...] + ga_ref[pl.ds(seg, 1), :] * y
    xo_ref[...] = xn
    h2 = _rms(xn, g2_ref[...]) * (1.0 + sc_ref[pl.ds(seg, 1), :]) + sh_ref[pl.ds(seg, 1), :]
    ht_ref[...] = h2.T.astype(BF)


def _out_projection(x, merged, w_out, mod, g_norm2, l):
    tm = 256
    return pl.pallas_call(
        functools.partial(_outproj_kernel, tm=tm),
        grid=(N_TOK // tm,),
        in_specs=[
            pl.BlockSpec((tm, D_MODEL), lambda i: (i, 0)),
            pl.BlockSpec((tm, D_MODEL), lambda i: (i, 0)),
            pl.BlockSpec((None, D_MODEL, D_MODEL), lambda i: (l, 0, 0)),
            pl.BlockSpec((None, 8, D_MODEL), lambda i: (l, 0, 2)),
            pl.BlockSpec((None, 1, D_MODEL), lambda i: (l, 0, 0)),
            pl.BlockSpec((None, 8, D_MODEL), lambda i: (l, 0, 3)),
            pl.BlockSpec((None, 8, D_MODEL), lambda i: (l, 0, 4)),
        ],
        out_specs=[
            pl.BlockSpec((tm, D_MODEL), lambda i: (i, 0)),
            pl.BlockSpec((D_MODEL, tm), lambda i: (0, i)),
        ],
        out_shape=[
            jax.ShapeDtypeStruct((N_TOK, D_MODEL), F32),
            jax.ShapeDtypeStruct((D_MODEL, N_TOK), BF),
        ],
        compiler_params=_cparams(("arbitrary",)),
        name="out_projection",
    )(x, merged, w_out, mod, g_norm2.reshape(DEPTH, 1, D_MODEL), mod, mod)


ROUTE_LANES = 128
NOT_SELECTED = 127


SLAB = 8
NET_WIDTH = 16


def _sorting_network(n):
    def merge(lo, hi, r):
        step = r * 2
        if step < hi - lo:
            yield from merge(lo, hi, step)
            yield from merge(lo + r, hi, step)
            yield from ((i, i + r) for i in range(lo + r, hi - r, step))
        else:
            yield (lo, lo + r)

    def sort(lo, hi):
        if hi - lo >= 1:
            mid = lo + (hi - lo) // 2
            yield from sort(lo, mid)
            yield from sort(mid + 1, hi)
            yield from merge(lo, hi, 1)

    return tuple(sort(0, n - 1))


def _extract_top(x, n):
    slabs = [x[SLAB * v:SLAB * (v + 1)] for v in range(x.shape[0] // SLAB)]
    slabs += [None] * (NET_WIDTH - len(slabs))
    for i, j in _sorting_network(NET_WIDTH):
        hi, lo = slabs[i], slabs[j]
        if lo is None:
            continue
        if hi is None:
            slabs[i], slabs[j] = lo, None
        else:
            slabs[i], slabs[j] = jnp.maximum(hi, lo), jnp.minimum(hi, lo)
    stack = [s for s in slabs if s is not None]
    sub = lax.broadcasted_iota(jnp.int32, stack[0].shape, 0)
    vals = []
    for r in range(n):
        m = jnp.max(stack[0], axis=0, keepdims=True)
        vals.append(m)
        hit = stack[0] == m
        popped = sub == jnp.min(jnp.where(hit, sub, SLAB), axis=0, keepdims=True)
        for d in range(min(n - r - 1, len(stack))):
            below = stack[d + 1] if d + 1 < len(stack) else -jnp.inf
            stack[d] = jnp.where(popped, below, stack[d])
    return vals


def _stack_rows(rows):
    idx = lax.broadcasted_iota(jnp.int32, (len(rows), rows[0].shape[1]), 0)
    m = jnp.zeros((len(rows), rows[0].shape[1]), F32)
    for r, row in enumerate(rows):
        m = jnp.where(idx == r, row, m)
    return m


def _route_chunk(s1, s2):
    k = PEER_TOPK
    a = _extract_top(s1, k)
    b = _extract_top(s2, k)
    am, bm = _stack_rows(a), _stack_rows(b)
    slabs = [a[0] + bm[0:8], a[0] + bm[8:16]]
    slabs += [a[r] + bm[0:8] for r in range(1, 8)]
    slabs += [am[8:16] + b[0]]
    cand = jnp.concatenate(slabs, axis=0)
    tau = _extract_top(cand, k)[-1]
    sel = cand >= tau
    top = a[0] + b[0]
    cnt = jnp.where(sel, 1.0, 0.0)
    low = jnp.where(sel, cand, jnp.inf)

    def row_rows(r):
        return slice(0, 16) if r == 0 else slice(8 + 8 * r, 16 + 8 * r) if r < 8 else slice(64 + r, 65 + r)

    n_r = [jnp.sum(cnt[row_rows(r)], axis=0, keepdims=True) for r in range(k)]
    low_r = [jnp.min(low[row_rows(r)], axis=0, keepdims=True) for r in range(k)]
    total = n_r[0]
    for r in range(1, k):
        total = total + n_r[r]
    excess = total - float(k)
    for r in reversed(range(k)):
        drop = jnp.where(excess > 0.0, jnp.where(low_r[r] == tau, 1.0, 0.0), 0.0)
        n_r[r] = n_r[r] - drop
        excess = excess - drop
    dropped = total - float(k) - excess
    z = jnp.sum(jnp.where(sel, jnp.exp(cand - top), 0.0), axis=0, keepdims=True) - dropped * jnp.exp(tau - top)

    c1 = jnp.full(s1.shape, -1.0, F32)
    r2 = jnp.full(s2.shape, float(NOT_SELECTED), F32)
    for r in range(k - 1):
        c1 = jnp.where(s1 == a[r], n_r[r] - 1.0, c1)
        r2 = jnp.where(s2 == b[r], float(r), r2)
    key = lax.broadcasted_iota(jnp.int32, s1.shape, 0)

    def last_key_limit(s, vals):
        match = s == vals[k - 1]
        first = jnp.min(jnp.where(match, key, PEER_KEYS), axis=0, keepdims=True)
        return match, jnp.where(vals[k - 2] != vals[k - 1], first, PEER_KEYS)

    match1, limit1 = last_key_limit(s1, a)
    c1 = jnp.where(match1, jnp.where(key <= limit1, n_r[k - 1] - 1.0, c1), c1)
    match2, limit2 = last_key_limit(s2, b)
    r2 = jnp.where(match2, jnp.where(key <= limit2, float(k - 1), r2), r2)
    e1 = jnp.exp(s1 - a[0]) / z
    e2 = jnp.exp(s2 - b[0])
    return c1, e1, r2, e2


def _route_kernel(ht_ref, wq_ref, sk_ref, c1_ref, e1_ref, r2_ref, e2_ref):
    qt = jnp.dot(wq_ref[...], ht_ref[...], preferred_element_type=F32)
    s1 = jnp.dot(sk_ref[0], qt[0:PEER_KEYS].astype(BF), preferred_element_type=F32)
    s2 = jnp.dot(sk_ref[1], qt[PEER_KEYS:].astype(BF), preferred_element_type=F32)

    def pack_row_pairs(x):
        return pltpu.bitcast(x.astype(BF), jnp.int32)

    def duplicate_halves(x):
        hi = pltpu.bitcast(x.astype(BF).astype(F32), jnp.int32)
        return hi | lax.shift_right_logical(hi, 16)

    for c in range(s1.shape[1] // ROUTE_LANES):
        cs = slice(c * ROUTE_LANES, (c + 1) * ROUTE_LANES)
        c1, e1, r2, e2 = _route_chunk(s1[:, cs], s2[:, cs])
        c1_ref[:, cs] = duplicate_halves(c1)
        e1_ref[:, cs] = duplicate_halves(e1)
        r2_ref[:, cs] = pack_row_pairs(r2)
        e2_ref[:, cs] = pack_row_pairs(e2)


def _route(h2t, w_peer_qt, subkeys_bf, l):
    tn = 512
    rows = PEER_HEADS * PEER_KEYS
    spec = pl.BlockSpec((PEER_KEYS, tn), lambda t, h: (h, t))
    pair_spec = pl.BlockSpec((PEER_KEYS // 2, tn), lambda t, h: (h, t))
    dup = jax.ShapeDtypeStruct((rows, N_TOK), jnp.int32)
    pairs = jax.ShapeDtypeStruct((rows // 2, N_TOK), jnp.int32)
    return pl.pallas_call(
        _route_kernel,
        grid=(N_TOK // tn, PEER_HEADS),
        in_specs=[
            pl.BlockSpec((D_MODEL, tn), lambda t, h: (0, t)),
            pl.BlockSpec((None, PEER_QDIM, D_MODEL), lambda t, h: (l, h, 0)),
            pl.BlockSpec((None, 2, PEER_KEYS, PEER_KEYS), lambda t, h: (l, 0, 0, 0)),
        ],
        out_specs=[spec, spec, pair_spec, pair_spec],
        out_shape=[dup, dup, pairs, pairs],
        compiler_params=_cparams(("arbitrary", "arbitrary")),
        name="peer_route",
    )(h2t, w_peer_qt, subkeys_bf)


PEER_LANES = 128
PEER_BLK = 256


def _peer_kernel(ht_ref, u_ref, v_ref, c1_ref, e1_ref, r2_ref, e2_ref, o_ref,
                 at0_ref, at1_ref, wg0_ref, wg1_ref, *, tm, te):
    e = pl.program_id(1)
    nblk = te // PEER_BLK
    half = PEER_KEYS // 2

    @pl.when(e == 0)
    def _():
        o_ref[...] = jnp.zeros_like(o_ref)

    at_bufs = (at0_ref, at1_ref)
    wg_bufs = (wg0_ref, wg1_ref)

    def scores(q):
        rows = slice(q * PEER_BLK, (q + 1) * PEER_BLK)
        at_bufs[q % 2][...] = jnp.dot(u_ref[rows, :], ht_ref[...], preferred_element_type=F32)

    def weights(q):
        for bb in range(PEER_BLK // PEER_KEYS):
            i = (e * nblk + q) * (PEER_BLK // PEER_KEYS) + bb
            bs = slice(bb * PEER_KEYS, (bb + 1) * PEER_KEYS)
            last_rows = [c1_ref[pl.ds(h * PEER_KEYS + i, 1), :] for h in range(PEER_HEADS)]
            e1_rows = [e1_ref[pl.ds(h * PEER_KEYS + i, 1), :] for h in range(PEER_HEADS)]
            for c in range(tm // PEER_LANES):
                cs = slice(c * PEER_LANES, (c + 1) * PEER_LANES)
                w = jnp.zeros((PEER_KEYS, PEER_LANES), BF)
                for h in range(PEER_HEADS):
                    hs = slice(h * half, (h + 1) * half)
                    last = pltpu.bitcast(jnp.broadcast_to(last_rows[h][:, cs], (half, PEER_LANES)), BF)
                    e1 = pltpu.bitcast(jnp.broadcast_to(e1_rows[h][:, cs], (half, PEER_LANES)), BF)
                    r2 = pltpu.bitcast(r2_ref[hs, cs], BF)
                    e2 = pltpu.bitcast(e2_ref[hs, cs], BF)
                    w = w + jnp.where(r2 <= last, e2, jnp.zeros_like(e2)) * e1
                a = at_bufs[q % 2][bs, cs]
                g = (0.5 * a * (1.0 + lax.erf(a * math.sqrt(0.5)))).astype(BF)
                wg_bufs[q % 2][bs, cs] = w * g

    def update(q):
        rows = slice(q * PEER_BLK, (q + 1) * PEER_BLK)
        o_ref[...] += lax.dot_general(v_ref[rows, :], wg_bufs[q % 2][...], (((0,), (0,)), ((), ())),
                                      preferred_element_type=F32)

    scores(0)
    for q in range(nblk):
        if q + 1 < nblk:
            scores(q + 1)
        weights(q)
        if q >= 1:
            update(q - 1)
    update(nblk - 1)


def _peer(h2t, u_bf, v_bf, routing, l):
    tm, te = 1024, 512
    c1, e1, r2, e2 = routing
    rows = PEER_HEADS * PEER_KEYS
    once = pl.Buffered(1)
    rspec = pl.BlockSpec((rows, tm), lambda t, e: (0, t), pipeline_mode=once)
    pspec = pl.BlockSpec((rows // 2, tm), lambda t, e: (0, t), pipeline_mode=once)
    return pl.pallas_call(
        functools.partial(_peer_kernel, tm=tm, te=te),
        grid=(N_TOK // tm, PEER_EXPERTS // te),
        in_specs=[
            pl.BlockSpec((D_MODEL, tm), lambda t, e: (0, t), pipeline_mode=once),
            pl.BlockSpec((None, te, D_MODEL), lambda t, e: (l, e, 0)),
            pl.BlockSpec((None, te, D_MODEL), lambda t, e: (l, e, 0)),
            rspec, rspec, pspec, pspec,
        ],
        out_specs=pl.BlockSpec((D_MODEL, tm), lambda t, e: (0, t)),
        out_shape=jax.ShapeDtypeStruct((D_MODEL, N_TOK), F32),
        scratch_shapes=[
            pltpu.VMEM((PEER_BLK, tm), F32),
            pltpu.VMEM((PEER_BLK, tm), F32),
            pltpu.VMEM((PEER_BLK, tm), BF),
            pltpu.VMEM((PEER_BLK, tm), BF),
        ],
        compiler_params=_cparams(("arbitrary", "arbitrary")),
        name="peer_dense",
    )(h2t, u_bf, v_bf, c1, e1, r2, e2)


def _peer_residual_kernel(x_ref, pt_ref, ga_ref, o_ref, *, tm):
    seg = _segment(pl.program_id(0) * tm)
    o_ref[...] = x_ref[...] + ga_ref[pl.ds(seg, 1), :] * pt_ref[...].T


def _peer_residual(x, peer_t, mod, l):
    tm = 512
    return pl.pallas_call(
        functools.partial(_peer_residual_kernel, tm=tm),
        grid=(N_TOK // tm,),
        in_specs=[
            pl.BlockSpec((tm, D_MODEL), lambda i: (i, 0)),
            pl.BlockSpec((D_MODEL, tm), lambda i: (0, i)),
            pl.BlockSpec((None, 8, D_MODEL), lambda i: (l, 0, 5)),
        ],
        out_specs=pl.BlockSpec((tm, D_MODEL), lambda i: (i, 0)),
        out_shape=jax.ShapeDtypeStruct((N_TOK, D_MODEL), F32),
        compiler_params=_cparams(("arbitrary",)),
        name="peer_residual",
    )(x, peer_t, mod)


def _final_kernel(x_ref, g_ref, ctx_ref, dec_ref, *, n_ctx_tiles):
    y = _rms(x_ref[...], g_ref[...])
    is_ctx = pl.program_id(0) < n_ctx_tiles

    @pl.when(is_ctx)
    def _():
        ctx_ref[...] = y

    @pl.when(jnp.logical_not(is_ctx))
    def _():
        dec_ref[...] = y


def _final_norm(x, g_final):
    tm = 512
    n_ctx_tiles = N_CTX // tm
    return pl.pallas_call(
        functools.partial(_final_kernel, n_ctx_tiles=n_ctx_tiles),
        grid=(N_TOK // tm,),
        in_specs=[pl.BlockSpec((tm, D_MODEL), lambda i: (i, 0)),
                  pl.BlockSpec((1, D_MODEL), lambda i: (0, 0))],
        out_specs=[
            pl.BlockSpec((tm, D_MODEL), lambda i: (jnp.minimum(i, n_ctx_tiles - 1), 0)),
            pl.BlockSpec((tm, D_MODEL), lambda i: (jnp.maximum(i - n_ctx_tiles, 0), 0)),
        ],
        out_shape=[jax.ShapeDtypeStruct((N_CTX, D_MODEL), F32), jax.ShapeDtypeStruct((N_DEC, D_MODEL), F32)],
        compiler_params=_cparams(("arbitrary",)),
        name="final_norm",
    )(x, g_final.reshape(1, D_MODEL))


def _rope_tables():
    t = jnp.arange(DEC_SEQ)
    half = MLA_ROPE // 4
    inv = ROPE_BASE ** (-jnp.arange(half, dtype=F32) / half)
    ang_r = (t // GRID_W).astype(F32)[:, None] * inv
    ang_c = (t % GRID_W).astype(F32)[:, None] * inv
    cos = jnp.concatenate([jnp.cos(ang_r)] * 2 + [jnp.cos(ang_c)] * 2, axis=1)
    sin = jnp.concatenate([-jnp.sin(ang_r), jnp.sin(ang_r), -jnp.sin(ang_c), jnp.sin(ang_c)], axis=1)
    return cos, sin


def _relayout_w_in(w_in):
    q, ckv, kr, pz, dq, dk, dv, gl = jnp.split(
        w_in, (512, 768, 832, 1344, 1856, 2368, 2880), axis=-1)
    pad = jnp.zeros(w_in.shape[:-1] + (C_PZ - C_KR - MLA_ROPE,), w_in.dtype)
    return jnp.concatenate([q, ckv, kr, pad, pz, dq, dk, dv, gl], axis=-1).astype(BF)


def _relayout_w_qb(w_qb):
    w = w_qb.reshape(DEPTH, MLA_Q_RANK, MLA_HEADS, MLA_NOPE + MLA_ROPE)
    w = jnp.pad(w, ((0, 0), (0, 0), (0, 0), (0, MLA_HEAD_PAD - MLA_NOPE - MLA_ROPE)))
    return w.reshape(DEPTH, MLA_Q_RANK, MLA_HEADS * MLA_HEAD_PAD).astype(BF)


def kernel(x_prompt, x_sample, cache_mla_ckv, cache_mla_krope, cache_diff_k, cache_diff_v, c, c_ctx,
           w_mod, b_mod, g_norm1, w_in, g_qnorm, w_qb, g_kvnorm, w_kvb, w_pool, pool_scale,
           diff_lambda, g_diffnorm, w_br_mla, w_br_pool, w_br_diff, w_out, g_norm2,
           w_peer_q, peer_subkeys, peer_u, peer_v, g_final):
    x = jnp.concatenate([x_prompt.reshape(N_CTX, D_MODEL), x_sample.reshape(N_DEC, D_MODEL)], axis=0)
    cond8 = jnp.concatenate([c_ctx[None, :], c, jnp.zeros((8 - 1 - DEC_BATCH, D_MODEL), F32)], axis=0)

    w_in_re = _relayout_w_in(w_in)
    w_qb_re = _relayout_w_qb(w_qb)
    w_kvb_bf = w_kvb.astype(BF)
    w_pool_bf = w_pool.astype(BF)
    w_br_mla_bf, w_br_pool_bf, w_br_diff_bf = w_br_mla.astype(BF), w_br_pool.astype(BF), w_br_diff.astype(BF)
    w_out_bf = w_out.astype(BF)
    w_peer_qt = jnp.swapaxes(w_peer_q, 1, 2).astype(BF)
    subkeys_bf = peer_subkeys.astype(BF)
    u_bf = peer_u.astype(BF)
    v_bf = peer_v.astype(BF)

    cache_kr_pad = jnp.pad(cache_mla_krope, ((0, 0), (0, 0), (0, 0), (0, 128 - MLA_ROPE)))
    cache_k = cache_diff_k.reshape(DEC_BATCH, DEPTH, PAST_LEN, DIFF_WIDTH)
    cache_v = cache_diff_v.reshape(DEC_BATCH, DEPTH, PAST_LEN, DIFF_WIDTH)

    cos64, sin64 = _rope_tables()
    ones, zeros = jnp.ones_like(cos64), jnp.zeros_like(cos64)
    mla_tabs = (
        jnp.concatenate([ones, ones, cos64, ones], axis=1),
        jnp.concatenate([zeros, zeros, sin64, zeros], axis=1),
        jnp.concatenate([cos64, ones], axis=1),
        jnp.concatenate([sin64, zeros], axis=1),
    )
    diff_tabs = (jnp.tile(cos64, (1, DIFF_WIDTH // DIFF_QK)), jnp.tile(sin64, (1, DIFF_WIDTH // DIFF_QK)))

    mod = _modulation(cond8, w_mod, b_mod)

    st_ckv = jnp.zeros((BATCH, DEPTH, SEQ, MLA_KV_RANK), F32)
    st_kr = jnp.zeros((BATCH, DEPTH, SEQ, MLA_ROPE), F32)
    st_k = jnp.zeros((BATCH, DEPTH, SEQ, DIFF_WIDTH), F32)
    st_v = jnp.zeros((BATCH, DEPTH, SEQ, DIFF_WIDTH), F32)
    for l in range(DEPTH):
        p_small = _in_projection(x, g_norm1, mod, w_in_re, l, 0, SMALL_WIDTH, F32)
        gl = _in_projection(x, g_norm1, mod, w_in_re, l, SMALL_WIDTH, GATE_WIDTH, BF)
        mla_c, st_ckv, st_kr = _mla_ctx(p_small, g_qnorm, g_kvnorm, w_qb_re, w_kvb_bf, st_ckv, st_kr, l)
        mla_d = _mla_dec(p_small, cache_mla_ckv, cache_kr_pad, mla_tabs, g_qnorm, g_kvnorm, w_qb_re, w_kvb_bf, l)
        diff_c, st_k, st_v = _diff_ctx(p_small, diff_lambda, g_diffnorm, st_k, st_v, l)
        diff_d = _diff_dec(p_small, cache_k, cache_v, diff_tabs, diff_lambda, g_diffnorm, l)
        pool_o = _pool(p_small, w_pool_bf, pool_scale, l)
        mla_o = jnp.concatenate([mla_c, mla_d], axis=0)
        diff_o = jnp.concatenate([diff_c, diff_d], axis=0)
        merged = _merge(mla_o, pool_o, diff_o, gl, w_br_mla_bf, w_br_pool_bf, w_br_diff_bf, l)
        x, h2t = _out_projection(x, merged, w_out_bf, mod, g_norm2, l)
        routing = _route(h2t, w_peer_qt, subkeys_bf, l)
        x = _peer_residual(x, _peer(h2t, u_bf, v_bf, routing, l), mod, l)

    y_ctx, y_dec = _final_norm(x, g_final)
    return (y_ctx.reshape(BATCH, SEQ, D_MODEL), y_dec.reshape(DEC_BATCH, DEC_SEQ, D_MODEL), st_ckv, st_kr,
            st_k.reshape(BATCH, DEPTH, SEQ, DIFF_HEADS, 2 * DIFF_QK),
            st_v.reshape(BATCH, DEPTH, SEQ, DIFF_HEADS, DIFF_V))
```

```python
import functools
import math

import jax
import jax.numpy as jnp
from jax import lax
from jax.experimental import pallas as pl
from jax.experimental.pallas import tpu as pltpu

BF = jnp.bfloat16
F32 = jnp.float32

D_MODEL = 2048
BATCH = 16
SEQ = 256
DEPTH = 4
DEC_BATCH = 2
DEC_SEQ = 1024
PAST_LEN = 256
GRID_W = 64
ROPE_BASE = 10000.0
EPS = 1e-6

MLA_HEADS = 8
MLA_NOPE = 128
MLA_ROPE = 64
MLA_V = 128
MLA_Q_RANK = 512
MLA_KV_RANK = 256
MLA_HEAD_PAD = 256

POOL_WINDOWS = (2, 4, 8, 16)
POOL_GROUP = 128
POOL_WIDTH = POOL_GROUP * len(POOL_WINDOWS)

DIFF_HEADS = 4
DIFF_QK = 64
DIFF_V = 2 * DIFF_QK
DIFF_WIDTH = DIFF_HEADS * DIFF_V

PEER_HEADS = 8
PEER_KEYS = 128
PEER_EXPERTS = PEER_KEYS * PEER_KEYS
PEER_TOPK = 16
PEER_QDIM = 256

N_CTX = BATCH * SEQ
N_DEC = DEC_BATCH * DEC_SEQ
N_TOK = N_CTX + N_DEC
KEYS_DEC = PAST_LEN + DEC_SEQ

C_QA = 0
C_CKV = 512
C_KR = 768
C_PZ = 1024
C_DQ = 1536
C_DK = 2048
C_DV = 2560
SMALL_WIDTH = 3072
GATE_WIDTH = 3 * D_MODEL

VMEM_LIMIT = 56 * 1024 * 1024


def _cparams(sem):
    return pltpu.CompilerParams(dimension_semantics=sem, vmem_limit_bytes=VMEM_LIMIT)


def _segment(row0):
    return jnp.where(row0 < N_CTX, 0, 1 + (row0 - N_CTX) // DEC_SEQ)


def _rms(x, g):
    return x * lax.rsqrt(jnp.mean(x * x, axis=-1, keepdims=True) + EPS) * g


def _softmax_rows(s):
    m = jnp.max(s, axis=-1, keepdims=True)
    p = jnp.exp(s - m)
    return p / jnp.sum(p, axis=-1, keepdims=True)


def _dot_nt(a, b):
    return lax.dot_general(a, b, (((1,), (1,)), ((), ())), preferred_element_type=F32)


def _rope_swap(x):
    n = x.shape[-1]
    up = pltpu.roll(x, n - 16, 1)
    dn = pltpu.roll(x, 16, 1)
    lane = lax.broadcasted_iota(jnp.int32, x.shape, 1)
    return jnp.where((lane & 16) == 0, up, dn)


def _mod_kernel(c_ref, w_ref, b_ref, o_ref):
    c = c_ref[...]
    a = (c * jax.nn.sigmoid(c)).astype(BF)
    o_ref[...] = jnp.dot(a, w_ref[...].astype(BF), preferred_element_type=F32) + b_ref[...]


def _modulation(cond8, w_mod, b_mod):
    tn = 1024
    return pl.pallas_call(
        _mod_kernel,
        grid=(DEPTH, 6 * D_MODEL // tn),
        in_specs=[
            pl.BlockSpec((8, D_MODEL), lambda l, j: (0, 0)),
            pl.BlockSpec((None, D_MODEL, tn), lambda l, j: (l, 0, j)),
            pl.BlockSpec((None, 1, tn), lambda l, j: (l, 0, j)),
        ],
        out_specs=pl.BlockSpec((None, 8, tn), lambda l, j: (l, 0, j)),
        out_shape=jax.ShapeDtypeStruct((DEPTH, 8, 6 * D_MODEL), F32),
        compiler_params=_cparams(("arbitrary", "arbitrary")),
        name="modulation",
    )(cond8, w_mod, b_mod.reshape(DEPTH, 1, 6 * D_MODEL))


def _inproj_kernel(x_ref, g_ref, sh_ref, sc_ref, w_ref, o_ref, h_ref, *, tm):
    i = pl.program_id(0)

    @pl.when(pl.program_id(1) == 0)
    def _():
        seg = _segment(i * tm)
        y = _rms(x_ref[...], g_ref[...])
        h_ref[...] = (y * (1.0 + sc_ref[pl.ds(seg, 1), :]) + sh_ref[pl.ds(seg, 1), :]).astype(BF)

    o_ref[...] = jnp.dot(h_ref[...], w_ref[...], preferred_element_type=F32).astype(o_ref.dtype)


def _in_projection(x, g_norm1, mod, w_in_re, l, col0, width, out_dtype, tm=1024, tn=512):
    off = col0 // tn
    return pl.pallas_call(
        functools.partial(_inproj_kernel, tm=tm),
        grid=(N_TOK // tm, width // tn),
        in_specs=[
            pl.BlockSpec((tm, D_MODEL), lambda i, j: (i, 0)),
            pl.BlockSpec((None, 1, D_MODEL), lambda i, j: (l, 0, 0)),
            pl.BlockSpec((None, 8, D_MODEL), lambda i, j: (l, 0, 0)),
            pl.BlockSpec((None, 8, D_MODEL), lambda i, j: (l, 0, 1)),
            pl.BlockSpec((None, D_MODEL, tn), lambda i, j: (l, 0, j + off)),
        ],
        out_specs=pl.BlockSpec((tm, tn), lambda i, j: (i, j)),
        out_shape=jax.ShapeDtypeStruct((N_TOK, width), out_dtype),
        scratch_shapes=[pltpu.VMEM((tm, D_MODEL), BF)],
        compiler_params=_cparams(("arbitrary", "arbitrary")),
        name="in_projection",
    )(x, g_norm1.reshape(DEPTH, 1, D_MODEL), mod, mod, w_in_re)


def _mla_heads(q, kfun, vfun, o_ref, scale):
    for h in range(MLA_HEADS):
        s = _dot_nt(q(h), kfun(h)) * scale
        p = _softmax_rows(s).astype(BF)
        o = jnp.dot(p, vfun(h), preferred_element_type=F32)
        o_ref[:, h * MLA_V:(h + 1) * MLA_V] = o.astype(o_ref.dtype)


def _mla_ctx_kernel(qa_ref, ckv_ref, kr_ref, gq_ref, gkv_ref, wqb_ref, wkvb_ref, st_ckv_hbm, st_kr_hbm,
                    o_ref, ckv_out_ref, kr_out_ref):
    del st_ckv_hbm, st_kr_hbm
    scale = (MLA_NOPE + MLA_ROPE) ** -0.5
    qn = _rms(qa_ref[...], gq_ref[...]).astype(BF)
    q = jnp.dot(qn, wqb_ref[...], preferred_element_type=F32).astype(BF)
    ckv = _rms(ckv_ref[...], gkv_ref[...])
    ckv_out_ref[...] = ckv
    kr = kr_ref[...]
    kr_out_ref[...] = kr[:, :MLA_ROPE]
    kv = jnp.dot(ckv.astype(BF), wkvb_ref[...], preferred_element_type=F32).astype(BF)
    krp = kr[:, :128].astype(BF)

    def qh(h):
        return q[:, h * MLA_HEAD_PAD:(h + 1) * MLA_HEAD_PAD]

    def kh(h):
        return jnp.concatenate([kv[:, h * 256:h * 256 + MLA_NOPE], krp], axis=1)

    def vh(h):
        return kv[:, h * 256 + MLA_NOPE:(h + 1) * 256]

    _mla_heads(qh, kh, vh, o_ref, scale)


def _mla_ctx(p_small, g_qnorm, g_kvnorm, w_qb_re, w_kvb, st_ckv, st_kr, l):
    return pl.pallas_call(
        _mla_ctx_kernel,
        grid=(BATCH,),
        in_specs=[
            pl.BlockSpec((SEQ, MLA_Q_RANK), lambda b: (b, C_QA // MLA_Q_RANK)),
            pl.BlockSpec((SEQ, 256), lambda b: (b, C_CKV // 256)),
            pl.BlockSpec((SEQ, 256), lambda b: (b, C_KR // 256)),
            pl.BlockSpec((None, 1, MLA_Q_RANK), lambda b: (l, 0, 0)),
            pl.BlockSpec((None, 1, MLA_KV_RANK), lambda b: (l, 0, 0)),
            pl.BlockSpec((None, MLA_Q_RANK, MLA_HEADS * MLA_HEAD_PAD), lambda b: (l, 0, 0)),
            pl.BlockSpec((None, MLA_KV_RANK, MLA_HEADS * 256), lambda b: (l, 0, 0)),
            pl.BlockSpec(memory_space=pl.ANY),
            pl.BlockSpec(memory_space=pl.ANY),
        ],
        out_specs=[
            pl.BlockSpec((SEQ, MLA_HEADS * MLA_V), lambda b: (b, 0)),
            pl.BlockSpec((None, None, SEQ, MLA_KV_RANK), lambda b: (b, l, 0, 0)),
            pl.BlockSpec((None, None, SEQ, MLA_ROPE), lambda b: (b, l, 0, 0)),
        ],
        out_shape=[
            jax.ShapeDtypeStruct((N_TOK, MLA_HEADS * MLA_V), BF),
            jax.ShapeDtypeStruct(st_ckv.shape, F32),
            jax.ShapeDtypeStruct(st_kr.shape, F32),
        ],
        input_output_aliases={7: 1, 8: 2},
        compiler_params=_cparams(("arbitrary",)),
        name="mla_ctx",
    )(p_small, p_small, p_small, g_qnorm.reshape(DEPTH, 1, -1), g_kvnorm.reshape(DEPTH, 1, -1),
      w_qb_re, w_kvb, st_ckv, st_kr)


def _mla_dec_kernel(qa_ref, ckv_ref, kr_ref, cckv_ref, ckr_ref, cq_ref, sq_ref, ck_ref, sk_ref,
                    gq_ref, gkv_ref, wqb_ref, wkvb_ref, mix_hbm, o_ref, kf_ref, vf_ref):
    del mix_hbm
    scale = (MLA_NOPE + MLA_ROPE) ** -0.5

    @pl.when(pl.program_id(1) == 0)
    def _():
        ckv = _rms(ckv_ref[...], gkv_ref[...])
        ckv_all = jnp.concatenate([cckv_ref[...], ckv], axis=0).astype(BF)
        kr = kr_ref[:, :128]
        kr_rot = kr * ck_ref[...] + _rope_swap(kr) * sk_ref[...]
        kr_all = jnp.concatenate([ckr_ref[...], kr_rot], axis=0).astype(BF)
        for h in range(MLA_HEADS):
            kvh = jnp.dot(ckv_all, wkvb_ref[:, h * 256:(h + 1) * 256], preferred_element_type=F32)
            kf_ref[h, :, 0:MLA_NOPE] = kvh[:, :MLA_NOPE].astype(BF)
            kf_ref[h, :, MLA_NOPE:MLA_HEAD_PAD] = kr_all
            vf_ref[h] = kvh[:, MLA_NOPE:].astype(BF)

    qn = _rms(qa_ref[...], gq_ref[...]).astype(BF)
    q = jnp.dot(qn, wqb_ref[...], preferred_element_type=F32)
    cq = cq_ref[...]
    sq = sq_ref[...]

    def qh(h):
        x = q[:, h * MLA_HEAD_PAD:(h + 1) * MLA_HEAD_PAD]
        return (x * cq + _rope_swap(x) * sq).astype(BF)

    _mla_heads(qh, lambda h: kf_ref[h], lambda h: vf_ref[h], o_ref, scale)


def _mla_dec(p_small, cache_ckv, cache_kr_pad, tabs, g_qnorm, g_kvnorm, w_qb_re, w_kvb, mix, l):
    tq = 256
    nq = DEC_SEQ // tq
    row_blk = N_CTX // DEC_SEQ
    cq, sq, ck, sk = tabs
    return pl.pallas_call(
        _mla_dec_kernel,
        grid=(DEC_BATCH, nq),
        in_specs=[
            pl.BlockSpec((tq, MLA_Q_RANK), lambda b, i: (N_CTX // tq + b * nq + i, 0)),
            pl.BlockSpec((DEC_SEQ, 256), lambda b, i: (row_blk + b, C_CKV // 256)),
            pl.BlockSpec((DEC_SEQ, 256), lambda b, i: (row_blk + b, C_KR // 256)),
            pl.BlockSpec((None, None, PAST_LEN, MLA_KV_RANK), lambda b, i: (b, l, 0, 0)),
            pl.BlockSpec((None, None, PAST_LEN, 128), lambda b, i: (b, l, 0, 0)),
            pl.BlockSpec((tq, MLA_HEAD_PAD), lambda b, i: (i, 0)),
            pl.BlockSpec((tq, MLA_HEAD_PAD), lambda b, i: (i, 0)),
            pl.BlockSpec((DEC_SEQ, 128), lambda b, i: (0, 0)),
            pl.BlockSpec((DEC_SEQ, 128), lambda b, i: (0, 0)),
            pl.BlockSpec((None, 1, MLA_Q_RANK), lambda b, i: (l, 0, 0)),
            pl.BlockSpec((None, 1, MLA_KV_RANK), lambda b, i: (l, 0, 0)),
            pl.BlockSpec((None, MLA_Q_RANK, MLA_HEADS * MLA_HEAD_PAD), lambda b, i: (l, 0, 0)),
            pl.BlockSpec((None, MLA_KV_RANK, MLA_HEADS * 256), lambda b, i: (l, 0, 0)),
            pl.BlockSpec(memory_space=pl.ANY),
        ],
        out_specs=pl.BlockSpec((tq, MLA_HEADS * MLA_V), lambda b, i: (N_CTX // tq + b * nq + i, 0)),
        out_shape=jax.ShapeDtypeStruct((N_TOK, MLA_HEADS * MLA_V), BF),
        input_output_aliases={13: 0},
        scratch_shapes=[
            pltpu.VMEM((MLA_HEADS, KEYS_DEC, MLA_HEAD_PAD), BF),
            pltpu.VMEM((MLA_HEADS, KEYS_DEC, MLA_V), BF),
        ],
        compiler_params=_cparams(("arbitrary", "arbitrary")),
        name="mla_dec",
    )(p_small, p_small, p_small, cache_ckv, cache_kr_pad, cq, sq, ck, sk,
      g_qnorm.reshape(DEPTH, 1, -1), g_kvnorm.reshape(DEPTH, 1, -1), w_qb_re, w_kvb, mix)


def _diff_lambda(lv):
    t1 = jnp.sum(lv[0:1] * lv[1:2], axis=-1, keepdims=True)
    t2 = jnp.sum(lv[2:3] * lv[3:4], axis=-1, keepdims=True)
    return jnp.exp(t1) - jnp.exp(t2)


def _diff_heads(q, kfun, vfun, lam, g, o_ref, lam_init):
    scale = DIFF_QK ** -0.5
    lane = lax.broadcasted_iota(jnp.int32, (q.shape[0], DIFF_V), 1)
    for h in range(DIFF_HEADS):
        qh = q[:, h * DIFF_V:(h + 1) * DIFF_V]
        q1 = jnp.where(lane < DIFF_QK, qh, 0.0).astype(BF)
        q2 = jnp.where(lane >= DIFF_QK, qh, 0.0).astype(BF)
        k = kfun(h)
        v = vfun(h)
        p1 = _softmax_rows(_dot_nt(q1, k) * scale).astype(BF)
        p2 = _softmax_rows(_dot_nt(q2, k) * scale).astype(BF)
        a1 = jnp.dot(p1, v, preferred_element_type=F32)
        a2 = jnp.dot(p2, v, preferred_element_type=F32)
        o = _rms(a1 - lam * a2, g) * (1.0 - lam_init)
        o_ref[:, h * DIFF_V:(h + 1) * DIFF_V] = o.astype(o_ref.dtype)


def _diff_ctx_kernel(q_ref, k_ref, v_ref, lam_ref, g_ref, st_k_hbm, st_v_hbm,
                     o_ref, k_out_ref, v_out_ref, *, lam_init):
    del st_k_hbm, st_v_hbm
    lam = _diff_lambda(lam_ref[...]) + lam_init
    k_out_ref[...] = k_ref[...]
    v_out_ref[...] = v_ref[...]
    k = k_ref[...].astype(BF)
    v = v_ref[...].astype(BF)
    _diff_heads(q_ref[...],
                lambda h: k[:, h * DIFF_V:(h + 1) * DIFF_V],
                lambda h: v[:, h * DIFF_V:(h + 1) * DIFF_V],
                lam, g_ref[...], o_ref, lam_init)


def _lam_init(l):
    return 0.8 - 0.6 * math.exp(-0.3 * l)


def _diff_ctx(p_small, diff_lambda, g_diffnorm, st_k, st_v, l):
    return pl.pallas_call(
        functools.partial(_diff_ctx_kernel, lam_init=_lam_init(l)),
        grid=(BATCH,),
        in_specs=[
            pl.BlockSpec((SEQ, DIFF_WIDTH), lambda b: (b, C_DQ // DIFF_WIDTH)),
            pl.BlockSpec((SEQ, DIFF_WIDTH), lambda b: (b, C_DK // DIFF_WIDTH)),
            pl.BlockSpec((SEQ, DIFF_WIDTH), lambda b: (b, C_DV // DIFF_WIDTH)),
            pl.BlockSpec((None, 4, DIFF_QK), lambda b: (l, 0, 0)),
            pl.BlockSpec((None, 1, DIFF_V), lambda b: (l, 0, 0)),
            pl.BlockSpec(memory_space=pl.ANY),
            pl.BlockSpec(memory_space=pl.ANY),
        ],
        out_specs=[
            pl.BlockSpec((SEQ, DIFF_WIDTH), lambda b: (b, 0)),
            pl.BlockSpec((None, None, SEQ, DIFF_WIDTH), lambda b: (b, l, 0, 0)),
            pl.BlockSpec((None, None, SEQ, DIFF_WIDTH), lambda b: (b, l, 0, 0)),
        ],
        out_shape=[
            jax.ShapeDtypeStruct((N_TOK, DIFF_WIDTH), BF),
            jax.ShapeDtypeStruct(st_k.shape, F32),
            jax.ShapeDtypeStruct(st_v.shape, F32),
        ],
        input_output_aliases={5: 1, 6: 2},
        compiler_params=_cparams(("arbitrary",)),
        name="diff_ctx",
    )(p_small, p_small, p_small, diff_lambda, g_diffnorm.reshape(DEPTH, 1, -1), st_k, st_v)


def _diff_dec_kernel(q_ref, k_ref, v_ref, ck_ref, cv_ref, cq_ref, sq_ref, cfull_ref, sfull_ref,
                     lam_ref, g_ref, mix_hbm, o_ref, kf_ref, vf_ref, *, lam_init):
    del mix_hbm

    @pl.when(pl.program_id(1) == 0)
    def _():
        k = k_ref[...]
        k_rot = k * cfull_ref[...] + _rope_swap(k) * sfull_ref[...]
        kf_ref[0:PAST_LEN, :] = ck_ref[...].astype(BF)
        kf_ref[PAST_LEN:KEYS_DEC, :] = k_rot.astype(BF)
        vf_ref[0:PAST_LEN, :] = cv_ref[...].astype(BF)
        vf_ref[PAST_LEN:KEYS_DEC, :] = v_ref[...].astype(BF)

    lam = _diff_lambda(lam_ref[...]) + lam_init
    q = q_ref[...]
    q = q * cq_ref[...] + _rope_swap(q) * sq_ref[...]
    _diff_heads(q,
                lambda h: kf_ref[:, h * DIFF_V:(h + 1) * DIFF_V],
                lambda h: vf_ref[:, h * DIFF_V:(h + 1) * DIFF_V],
                lam, g_ref[...], o_ref, lam_init)


def _diff_dec(p_small, cache_k, cache_v, tabs, diff_lambda, g_diffnorm, mix, l):
    tq = 256
    nq = DEC_SEQ // tq
    row_blk = N_CTX // DEC_SEQ
    c512, s512 = tabs
    return pl.pallas_call(
        functools.partial(_diff_dec_kernel, lam_init=_lam_init(l)),
        grid=(DEC_BATCH, nq),
        in_specs=[
            pl.BlockSpec((tq, DIFF_WIDTH), lambda b, i: (N_CTX // tq + b * nq + i, C_DQ // DIFF_WIDTH)),
            pl.BlockSpec((DEC_SEQ, DIFF_WIDTH), lambda b, i: (row_blk + b, C_DK // DIFF_WIDTH)),
            pl.BlockSpec((DEC_SEQ, DIFF_WIDTH), lambda b, i: (row_blk + b, C_DV // DIFF_WIDTH)),
            pl.BlockSpec((None, None, PAST_LEN, DIFF_WIDTH), lambda b, i: (b, l, 0, 0)),
            pl.BlockSpec((None, None, PAST_LEN, DIFF_WIDTH), lambda b, i: (b, l, 0, 0)),
            pl.BlockSpec((tq, DIFF_WIDTH), lambda b, i: (i, 0)),
            pl.BlockSpec((tq, DIFF_WIDTH), lambda b, i: (i, 0)),
            pl.BlockSpec((DEC_SEQ, DIFF_WIDTH), lambda b, i: (0, 0)),
            pl.BlockSpec((DEC_SEQ, DIFF_WIDTH), lambda b, i: (0, 0)),
            pl.BlockSpec((None, 4, DIFF_QK), lambda b, i: (l, 0, 0)),
            pl.BlockSpec((None, 1, DIFF_V), lambda b, i: (l, 0, 0)),
            pl.BlockSpec(memory_space=pl.ANY),
        ],
        out_specs=pl.BlockSpec((tq, DIFF_WIDTH), lambda b, i: (N_CTX // tq + b * nq + i, 0)),
        out_shape=jax.ShapeDtypeStruct((N_TOK, DIFF_WIDTH), BF),
        input_output_aliases={11: 0},
        scratch_shapes=[
            pltpu.VMEM((KEYS_DEC, DIFF_WIDTH), BF),
            pltpu.VMEM((KEYS_DEC, DIFF_WIDTH), BF),
        ],
        compiler_params=_cparams(("arbitrary", "arbitrary")),
        name="diff_dec",
    )(p_small, p_small, p_small, cache_k, cache_v, c512, s512, c512, s512,
      diff_lambda, g_diffnorm.reshape(DEPTH, 1, -1), mix)


def _pool_kernel(z_ref, w_ref, ps_ref, o_ref, *, tm):
    seq_m1 = jnp.where(pl.program_id(0) * tm < N_CTX, SEQ - 1, DEC_SEQ - 1)
    t = lax.broadcasted_iota(jnp.int32, (tm, POOL_GROUP), 0) & seq_m1
    for gi, w in enumerate(POOL_WINDOWS):
        z = z_ref[:, gi * POOL_GROUP:(gi + 1) * POOL_GROUP]
        acc = jnp.zeros_like(z)
        for k in range(-(w // 2), w // 2):
            zs = z if k == 0 else pltpu.roll(z, (-k) % tm, 0)
            ok = (t + k >= 0) & (t + k <= seq_m1)
            acc = acc + jnp.where(ok, zs, 0.0)
        lo = jnp.maximum(t - w // 2, 0)
        hi = jnp.minimum(t + (w - 1) // 2, seq_m1)
        d = (acc / (hi - lo + 1).astype(F32) - z).astype(BF)
        y = jnp.dot(d, w_ref[gi], preferred_element_type=F32)
        y = y * ps_ref[:, gi * POOL_GROUP:(gi + 1) * POOL_GROUP]
        o_ref[:, gi * POOL_GROUP:(gi + 1) * POOL_GROUP] = y.astype(o_ref.dtype)


def _pool(p_small, w_pool_bf, pool_scale, l):
    tm = DEC_SEQ
    return pl.pallas_call(
        functools.partial(_pool_kernel, tm=tm),
        grid=(N_TOK // tm,),
        in_specs=[
            pl.BlockSpec((tm, POOL_WIDTH), lambda i: (i, C_PZ // POOL_WIDTH)),
            pl.BlockSpec((None, len(POOL_WINDOWS), POOL_GROUP, POOL_GROUP), lambda i: (l, 0, 0, 0)),
            pl.BlockSpec((None, 1, POOL_WIDTH), lambda i: (l, 0, 0)),
        ],
        out_specs=pl.BlockSpec((tm, POOL_WIDTH), lambda i: (i, 0)),
        out_shape=jax.ShapeDtypeStruct((N_TOK, POOL_WIDTH), BF),
        compiler_params=_cparams(("arbitrary",)),
        name="pool",
    )(p_small, w_pool_bf, pool_scale.reshape(DEPTH, 1, POOL_WIDTH))


def _merge_kernel(a_ref, p_ref, d_ref, g0_ref, g1_ref, g2_ref, wa_ref, wp_ref, wd_ref, o_ref):
    def sig(r):
        return jax.nn.sigmoid(r[...].astype(F32))

    m = sig(g0_ref) * jnp.dot(a_ref[...], wa_ref[...], preferred_element_type=F32)
    m = m + sig(g1_ref) * jnp.dot(p_ref[...], wp_ref[...], preferred_element_type=F32)
    m = m + sig(g2_ref) * jnp.dot(d_ref[...], wd_ref[...], preferred_element_type=F32)
    o_ref[...] = m.astype(o_ref.dtype)


def _merge(mla_o, pool_o, diff_o, gl, w_br_mla, w_br_pool, w_br_diff, l):
    tm, tn = 1024, 512
    nj = D_MODEL // tn
    return pl.pallas_call(
        _merge_kernel,
        grid=(N_TOK // tm, nj),
        in_specs=[
            pl.BlockSpec((tm, MLA_HEADS * MLA_V), lambda i, j: (i, 0)),
            pl.BlockSpec((tm, POOL_WIDTH), lambda i, j: (i, 0)),
            pl.BlockSpec((tm, DIFF_WIDTH), lambda i, j: (i, 0)),
            pl.BlockSpec((tm, tn), lambda i, j: (i, j)),
            pl.BlockSpec((tm, tn), lambda i, j: (i, nj + j)),
            pl.BlockSpec((tm, tn), lambda i, j: (i, 2 * nj + j)),
            pl.BlockSpec((None, MLA_HEADS * MLA_V, tn), lambda i, j: (l, 0, j)),
            pl.BlockSpec((None, POOL_WIDTH, tn), lambda i, j: (l, 0, j)),
            pl.BlockSpec((None, DIFF_WIDTH, tn), lambda i, j: (l, 0, j)),
        ],
        out_specs=pl.BlockSpec((tm, tn), lambda i, j: (i, j)),
        out_shape=jax.ShapeDtypeStruct((N_TOK, D_MODEL), BF),
        compiler_params=_cparams(("arbitrary", "arbitrary")),
        name="merge",
    )(mla_o, pool_o, diff_o, gl, gl, gl, w_br_mla, w_br_pool, w_br_diff)


def _outproj_kernel(x_ref, m_ref, w_ref, ga_ref, g2_ref, sh_ref, sc_ref, xo_ref, ht_ref, *, tm):
    seg = _segment(pl.program_id(0) * tm)
    y = jnp.dot(m_ref[...], w_ref[...], preferred_element_type=F32)
    xn = x_ref[...] + ga_ref[pl.ds(seg, 1), :] * y
    xo_ref[...] = xn
    h2 = _rms(xn, g2_ref[...]) * (1.0 + sc_ref[pl.ds(seg, 1), :]) + sh_ref[pl.ds(seg, 1), :]
    ht_ref[...] = h2.T.astype(BF)


def _out_projection(x, merged, w_out, mod, g_norm2, l):
    tm = 256
    return pl.pallas_call(
        functools.partial(_outproj_kernel, tm=tm),
        grid=(N_TOK // tm,),
        in_specs=[
            pl.BlockSpec((tm, D_MODEL), lambda i: (i, 0)),
            pl.BlockSpec((tm, D_MODEL), lambda i: (i, 0)),
            pl.BlockSpec((None, D_MODEL, D_MODEL), lambda i: (l, 0, 0)),
            pl.BlockSpec((None, 8, D_MODEL), lambda i: (l, 0, 2)),
            pl.BlockSpec((None, 1, D_MODEL), lambda i: (l, 0, 0)),
            pl.BlockSpec((None, 8, D_MODEL), lambda i: (l, 0, 3)),
            pl.BlockSpec((None, 8, D_MODEL), lambda i: (l, 0, 4)),
        ],
        out_specs=[
            pl.BlockSpec((tm, D_MODEL), lambda i: (i, 0)),
            pl.BlockSpec((D_MODEL, tm), lambda i: (0, i)),
        ],
        out_shape=[
            jax.ShapeDtypeStruct((N_TOK, D_MODEL), F32),
            jax.ShapeDtypeStruct((D_MODEL, N_TOK), BF),
        ],
        compiler_params=_cparams(("arbitrary",)),
        name="out_projection",
    )(x, merged, w_out, mod, g_norm2.reshape(DEPTH, 1, D_MODEL), mod, mod)


ROUTE_LANES = 128
NOT_SELECTED = 127


SLAB = 8
NET_WIDTH = 16


def _sorting_network(n):
    def merge(lo, hi, r):
        step = r * 2
        if step < hi - lo:
            yield from merge(lo, hi, step)
            yield from merge(lo + r, hi, step)
            yield from ((i, i + r) for i in range(lo + r, hi - r, step))
        else:
            yield (lo, lo + r)

    def sort(lo, hi):
        if hi - lo >= 1:
            mid = lo + (hi - lo) // 2
            yield from sort(lo, mid)
            yield from sort(mid + 1, hi)
            yield from merge(lo, hi, 1)

    return tuple(sort(0, n - 1))


def _extract_top(x, n):
    slabs = [x[SLAB * v:SLAB * (v + 1)] for v in range(x.shape[0] // SLAB)]
    slabs += [None] * (NET_WIDTH - len(slabs))
    for i, j in _sorting_network(NET_WIDTH):
        hi, lo = slabs[i], slabs[j]
        if lo is None:
            continue
        if hi is None:
            slabs[i], slabs[j] = lo, None
        else:
            slabs[i], slabs[j] = jnp.maximum(hi, lo), jnp.minimum(hi, lo)
    stack = [s for s in slabs if s is not None]
    sub = lax.broadcasted_iota(jnp.int32, stack[0].shape, 0)
    vals = []
    for r in range(n):
        m = jnp.max(stack[0], axis=0, keepdims=True)
        vals.append(m)
        hit = stack[0] == m
        popped = sub == jnp.min(jnp.where(hit, sub, SLAB), axis=0, keepdims=True)
        for d in range(min(n - r - 1, len(stack))):
            below = stack[d + 1] if d + 1 < len(stack) else -jnp.inf
            stack[d] = jnp.where(popped, below, stack[d])
    return vals


def _stack_rows(rows):
    idx = lax.broadcasted_iota(jnp.int32, (len(rows), rows[0].shape[1]), 0)
    m = jnp.zeros((len(rows), rows[0].shape[1]), F32)
    for r, row in enumerate(rows):
        m = jnp.where(idx == r, row, m)
    return m


def _route_chunk(s1, s2):
    k = PEER_TOPK
    a = _extract_top(s1, k)
    b = _extract_top(s2, k)
    am, bm = _stack_rows(a), _stack_rows(b)
    slabs = [a[0] + bm[0:8], a[0] + bm[8:16]]
    slabs += [a[r] + bm[0:8] for r in range(1, 8)]
    slabs += [am[8:16] + b[0]]
    cand = jnp.concatenate(slabs, axis=0)
    tau = _extract_top(cand, k)[-1]
    sel = cand >= tau
    top = a[0] + b[0]
    cnt = jnp.where(sel, 1.0, 0.0)
    low = jnp.where(sel, cand, jnp.inf)

    def row_rows(r):
        return slice(0, 16) if r == 0 else slice(8 + 8 * r, 16 + 8 * r) if r < 8 else slice(64 + r, 65 + r)

    n_r = [jnp.sum(cnt[row_rows(r)], axis=0, keepdims=True) for r in range(k)]
    low_r = [jnp.min(low[row_rows(r)], axis=0, keepdims=True) for r in range(k)]
    total = n_r[0]
    for r in range(1, k):
        total = total + n_r[r]
    excess = total - float(k)
    for r in reversed(range(k)):
        drop = jnp.where(excess > 0.0, jnp.where(low_r[r] == tau, 1.0, 0.0), 0.0)
        n_r[r] = n_r[r] - drop
        excess = excess - drop
    dropped = total - float(k) - excess
    z = jnp.sum(jnp.where(sel, jnp.exp(cand - top), 0.0), axis=0, keepdims=True) - dropped * jnp.exp(tau - top)

    c1 = jnp.full(s1.shape, -1.0, F32)
    r2 = jnp.full(s2.shape, float(NOT_SELECTED), F32)
    for r in range(k - 1):
        c1 = jnp.where(s1 == a[r], n_r[r] - 1.0, c1)
        r2 = jnp.where(s2 == b[r], float(r), r2)
    key = lax.broadcasted_iota(jnp.int32, s1.shape, 0)

    def last_key_limit(s, vals):
        match = s == vals[k - 1]
        first = jnp.min(jnp.where(match, key, PEER_KEYS), axis=0, keepdims=True)
        return match, jnp.where(vals[k - 2] != vals[k - 1], first, PEER_KEYS)

    match1, limit1 = last_key_limit(s1, a)
    c1 = jnp.where(match1, jnp.where(key <= limit1, n_r[k - 1] - 1.0, c1), c1)
    match2, limit2 = last_key_limit(s2, b)
    r2 = jnp.where(match2, jnp.where(key <= limit2, float(k - 1), r2), r2)
    e1 = jnp.exp(s1 - a[0]) / z
    e2 = jnp.exp(s2 - b[0])
    return c1, e1, r2, e2


def _route_kernel(ht_ref, wq_ref, sk_ref, c1_ref, e1_ref, r2_ref, e2_ref):
    qt = jnp.dot(wq_ref[...], ht_ref[...], preferred_element_type=F32)
    s1 = jnp.dot(sk_ref[0], qt[0:PEER_KEYS].astype(BF), preferred_element_type=F32)
    s2 = jnp.dot(sk_ref[1], qt[PEER_KEYS:].astype(BF), preferred_element_type=F32)

    def pack_row_pairs(x):
        return pltpu.bitcast(x.astype(BF), jnp.int32)

    def duplicate_halves(x):
        hi = pltpu.bitcast(x.astype(BF).astype(F32), jnp.int32)
        return hi | lax.shift_right_logical(hi, 16)

    for c in range(s1.shape[1] // ROUTE_LANES):
        cs = slice(c * ROUTE_LANES, (c + 1) * ROUTE_LANES)
        c1, e1, r2, e2 = _route_chunk(s1[:, cs], s2[:, cs])
        c1_ref[:, cs] = duplicate_halves(c1)
        e1_ref[:, cs] = duplicate_halves(e1)
        r2_ref[:, cs] = pack_row_pairs(r2)
        e2_ref[:, cs] = pack_row_pairs(e2)


def _route(h2t, w_peer_qt, subkeys_bf, l):
    tn = 512
    rows = PEER_HEADS * PEER_KEYS
    spec = pl.BlockSpec((PEER_KEYS, tn), lambda t, h: (h, t))
    pair_spec = pl.BlockSpec((PEER_KEYS // 2, tn), lambda t, h: (h, t))
    dup = jax.ShapeDtypeStruct((rows, N_TOK), jnp.int32)
    pairs = jax.ShapeDtypeStruct((rows // 2, N_TOK), jnp.int32)
    return pl.pallas_call(
        _route_kernel,
        grid=(N_TOK // tn, PEER_HEADS),
        in_specs=[
            pl.BlockSpec((D_MODEL, tn), lambda t, h: (0, t)),
            pl.BlockSpec((None, PEER_QDIM, D_MODEL), lambda t, h: (l, h, 0)),
            pl.BlockSpec((None, 2, PEER_KEYS, PEER_KEYS), lambda t, h: (l, 0, 0, 0)),
        ],
        out_specs=[spec, spec, pair_spec, pair_spec],
        out_shape=[dup, dup, pairs, pairs],
        compiler_params=_cparams(("arbitrary", "arbitrary")),
        name="peer_route",
    )(h2t, w_peer_qt, subkeys_bf)


PEER_LANES = 128
PEER_BLK = 256


def _peer_kernel(ht_ref, u_ref, vt_ref, c1_ref, e1_ref, r2_ref, e2_ref, o_ref,
                 at0_ref, at1_ref, wg0_ref, wg1_ref, *, tm, te):
    e = pl.program_id(1)
    nblk = te // PEER_BLK
    half = PEER_KEYS // 2

    @pl.when(e == 0)
    def _():
        o_ref[...] = jnp.zeros_like(o_ref)

    at_bufs = (at0_ref, at1_ref)
    wg_bufs = (wg0_ref, wg1_ref)

    def scores(q):
        rows = slice(q * PEER_BLK, (q + 1) * PEER_BLK)
        at_bufs[q % 2][...] = jnp.dot(u_ref[rows, :], ht_ref[...], preferred_element_type=F32)

    def weights(q):
        for bb in range(PEER_BLK // PEER_KEYS):
            i = (e * nblk + q) * (PEER_BLK // PEER_KEYS) + bb
            bs = slice(bb * PEER_KEYS, (bb + 1) * PEER_KEYS)
            last_rows = [c1_ref[pl.ds(h * PEER_KEYS + i, 1), :] for h in range(PEER_HEADS)]
            e1_rows = [e1_ref[pl.ds(h * PEER_KEYS + i, 1), :] for h in range(PEER_HEADS)]
            for c in range(tm // PEER_LANES):
                cs = slice(c * PEER_LANES, (c + 1) * PEER_LANES)
                w = jnp.zeros((PEER_KEYS, PEER_LANES), BF)
                for h in range(PEER_HEADS):
                    hs = slice(h * half, (h + 1) * half)
                    last = pltpu.bitcast(jnp.broadcast_to(last_rows[h][:, cs], (half, PEER_LANES)), BF)
                    e1 = pltpu.bitcast(jnp.broadcast_to(e1_rows[h][:, cs], (half, PEER_LANES)), BF)
                    r2 = pltpu.bitcast(r2_ref[hs, cs], BF)
                    e2 = pltpu.bitcast(e2_ref[hs, cs], BF)
                    w = w + jnp.where(r2 <= last, e2, jnp.zeros_like(e2)) * e1
                a = at_bufs[q % 2][bs, cs]
                g = (0.5 * a * (1.0 + lax.erf(a * math.sqrt(0.5)))).astype(BF)
                wg_bufs[q % 2][bs, cs] = w * g

    def update(q):
        cols = slice(q * PEER_BLK, (q + 1) * PEER_BLK)
        o_ref[...] += jnp.dot(vt_ref[:, cols], wg_bufs[q % 2][...], preferred_element_type=F32)

    scores(0)
    for q in range(nblk):
        if q + 1 < nblk:
            scores(q + 1)
        weights(q)
        if q >= 1:
            update(q - 1)
    update(nblk - 1)


def _peer(h2t, u_bf, vt_bf, routing, l):
    tm, te = 512, 1024
    c1, e1, r2, e2 = routing
    rows = PEER_HEADS * PEER_KEYS
    rspec = pl.BlockSpec((rows, tm), lambda t, e: (0, t))
    pspec = pl.BlockSpec((rows // 2, tm), lambda t, e: (0, t))
    return pl.pallas_call(
        functools.partial(_peer_kernel, tm=tm, te=te),
        grid=(N_TOK // tm, PEER_EXPERTS // te),
        in_specs=[
            pl.BlockSpec((D_MODEL, tm), lambda t, e: (0, t)),
            pl.BlockSpec((None, te, D_MODEL), lambda t, e: (l, e, 0)),
            pl.BlockSpec((None, D_MODEL, te), lambda t, e: (l, 0, e)),
            rspec, rspec, pspec, pspec,
        ],
        out_specs=pl.BlockSpec((D_MODEL, tm), lambda t, e: (0, t)),
        out_shape=jax.ShapeDtypeStruct((D_MODEL, N_TOK), F32),
        scratch_shapes=[
            pltpu.VMEM((PEER_BLK, tm), F32),
            pltpu.VMEM((PEER_BLK, tm), F32),
            pltpu.VMEM((PEER_BLK, tm), BF),
            pltpu.VMEM((PEER_BLK, tm), BF),
        ],
        compiler_params=_cparams(("arbitrary", "arbitrary")),
        name="peer_dense",
    )(h2t, u_bf, vt_bf, c1, e1, r2, e2)


def _peer_residual_kernel(x_ref, pt_ref, ga_ref, o_ref, *, tm):
    seg = _segment(pl.program_id(0) * tm)
    o_ref[...] = x_ref[...] + ga_ref[pl.ds(seg, 1), :] * pt_ref[...].T


def _peer_residual(x, peer_t, mod, l):
    tm = 512
    return pl.pallas_call(
        functools.partial(_peer_residual_kernel, tm=tm),
        grid=(N_TOK // tm,),
        in_specs=[
            pl.BlockSpec((tm, D_MODEL), lambda i: (i, 0)),
            pl.BlockSpec((D_MODEL, tm), lambda i: (0, i)),
            pl.BlockSpec((None, 8, D_MODEL), lambda i: (l, 0, 5)),
        ],
        out_specs=pl.BlockSpec((tm, D_MODEL), lambda i: (i, 0)),
        out_shape=jax.ShapeDtypeStruct((N_TOK, D_MODEL), F32),
        compiler_params=_cparams(("arbitrary",)),
        name="peer_residual",
    )(x, peer_t, mod)


def _final_kernel(x_ref, g_ref, ctx_ref, dec_ref, *, n_ctx_tiles):
    y = _rms(x_ref[...], g_ref[...])
    is_ctx = pl.program_id(0) < n_ctx_tiles

    @pl.when(is_ctx)
    def _():
        ctx_ref[...] = y

    @pl.when(jnp.logical_not(is_ctx))
    def _():
        dec_ref[...] = y


def _final_norm(x, g_final):
    tm = 512
    n_ctx_tiles = N_CTX // tm
    return pl.pallas_call(
        functools.partial(_final_kernel, n_ctx_tiles=n_ctx_tiles),
        grid=(N_TOK // tm,),
        in_specs=[pl.BlockSpec((tm, D_MODEL), lambda i: (i, 0)),
                  pl.BlockSpec((1, D_MODEL), lambda i: (0, 0))],
        out_specs=[
            pl.BlockSpec((tm, D_MODEL), lambda i: (jnp.minimum(i, n_ctx_tiles - 1), 0)),
            pl.BlockSpec((tm, D_MODEL), lambda i: (jnp.maximum(i - n_ctx_tiles, 0), 0)),
        ],
        out_shape=[jax.ShapeDtypeStruct((N_CTX, D_MODEL), F32), jax.ShapeDtypeStruct((N_DEC, D_MODEL), F32)],
        compiler_params=_cparams(("arbitrary",)),
        name="final_norm",
    )(x, g_final.reshape(1, D_MODEL))


def _rope_tables():
    t = jnp.arange(DEC_SEQ)
    half = MLA_ROPE // 4
    inv = ROPE_BASE ** (-jnp.arange(half, dtype=F32) / half)
    ang_r = (t // GRID_W).astype(F32)[:, None] * inv
    ang_c = (t % GRID_W).astype(F32)[:, None] * inv
    cos = jnp.concatenate([jnp.cos(ang_r)] * 2 + [jnp.cos(ang_c)] * 2, axis=1)
    sin = jnp.concatenate([-jnp.sin(ang_r), jnp.sin(ang_r), -jnp.sin(ang_c), jnp.sin(ang_c)], axis=1)
    return cos, sin


def _relayout_w_in(w_in):
    q, ckv, kr, pz, dq, dk, dv, gl = jnp.split(
        w_in, (512, 768, 832, 1344, 1856, 2368, 2880), axis=-1)
    pad = jnp.zeros(w_in.shape[:-1] + (C_PZ - C_KR - MLA_ROPE,), w_in.dtype)
    return jnp.concatenate([q, ckv, kr, pad, pz, dq, dk, dv, gl], axis=-1).astype(BF)


def _relayout_w_qb(w_qb):
    w = w_qb.reshape(DEPTH, MLA_Q_RANK, MLA_HEADS, MLA_NOPE + MLA_ROPE)
    w = jnp.pad(w, ((0, 0), (0, 0), (0, 0), (0, MLA_HEAD_PAD - MLA_NOPE - MLA_ROPE)))
    return w.reshape(DEPTH, MLA_Q_RANK, MLA_HEADS * MLA_HEAD_PAD).astype(BF)


def kernel(x_prompt, x_sample, cache_mla_ckv, cache_mla_krope, cache_diff_k, cache_diff_v, c, c_ctx,
           w_mod, b_mod, g_norm1, w_in, g_qnorm, w_qb, g_kvnorm, w_kvb, w_pool, pool_scale,
           diff_lambda, g_diffnorm, w_br_mla, w_br_pool, w_br_diff, w_out, g_norm2,
           w_peer_q, peer_subkeys, peer_u, peer_v, g_final):
    x = jnp.concatenate([x_prompt.reshape(N_CTX, D_MODEL), x_sample.reshape(N_DEC, D_MODEL)], axis=0)
    cond8 = jnp.concatenate([c_ctx[None, :], c, jnp.zeros((8 - 1 - DEC_BATCH, D_MODEL), F32)], axis=0)

    w_in_re = _relayout_w_in(w_in)
    w_qb_re = _relayout_w_qb(w_qb)
    w_kvb_bf = w_kvb.astype(BF)
    w_pool_bf = w_pool.astype(BF)
    w_br_mla_bf, w_br_pool_bf, w_br_diff_bf = w_br_mla.astype(BF), w_br_pool.astype(BF), w_br_diff.astype(BF)
    w_out_bf = w_out.astype(BF)
    w_peer_qt = jnp.swapaxes(w_peer_q, 1, 2).astype(BF)
    subkeys_bf = peer_subkeys.astype(BF)
    u_bf = peer_u.astype(BF)
    vt_bf = jnp.swapaxes(peer_v, 1, 2).astype(BF)

    cache_kr_pad = jnp.pad(cache_mla_krope, ((0, 0), (0, 0), (0, 0), (0, 128 - MLA_ROPE)))
    cache_k = cache_diff_k.reshape(DEC_BATCH, DEPTH, PAST_LEN, DIFF_WIDTH)
    cache_v = cache_diff_v.reshape(DEC_BATCH, DEPTH, PAST_LEN, DIFF_WIDTH)

    cos64, sin64 = _rope_tables()
    ones, zeros = jnp.ones_like(cos64), jnp.zeros_like(cos64)
    mla_tabs = (
        jnp.concatenate([ones, ones, cos64, ones], axis=1),
        jnp.concatenate([zeros, zeros, sin64, zeros], axis=1),
        jnp.concatenate([cos64, ones], axis=1),
        jnp.concatenate([sin64, zeros], axis=1),
    )
    diff_tabs = (jnp.tile(cos64, (1, DIFF_WIDTH // DIFF_QK)), jnp.tile(sin64, (1, DIFF_WIDTH // DIFF_QK)))

    mod = _modulation(cond8, w_mod, b_mod)

    st_ckv = jnp.zeros((BATCH, DEPTH, SEQ, MLA_KV_RANK), F32)
    st_kr = jnp.zeros((BATCH, DEPTH, SEQ, MLA_ROPE), F32)
    st_k = jnp.zeros((BATCH, DEPTH, SEQ, DIFF_WIDTH), F32)
    st_v = jnp.zeros((BATCH, DEPTH, SEQ, DIFF_WIDTH), F32)
    for l in range(DEPTH):
        p_small = _in_projection(x, g_norm1, mod, w_in_re, l, 0, SMALL_WIDTH, F32)
        gl = _in_projection(x, g_norm1, mod, w_in_re, l, SMALL_WIDTH, GATE_WIDTH, BF)
        mla_c, st_ckv, st_kr = _mla_ctx(p_small, g_qnorm, g_kvnorm, w_qb_re, w_kvb_bf, st_ckv, st_kr, l)
        mla_o = _mla_dec(p_small, cache_mla_ckv, cache_kr_pad, mla_tabs, g_qnorm, g_kvnorm, w_qb_re, w_kvb_bf,
                         mla_c, l)
        diff_c, st_k, st_v = _diff_ctx(p_small, diff_lambda, g_diffnorm, st_k, st_v, l)
        diff_o = _diff_dec(p_small, cache_k, cache_v, diff_tabs, diff_lambda, g_diffnorm, diff_c, l)
        pool_o = _pool(p_small, w_pool_bf, pool_scale, l)
        merged = _merge(mla_o, pool_o, diff_o, gl, w_br_mla_bf, w_br_pool_bf, w_br_diff_bf, l)
        x, h2t = _out_projection(x, merged, w_out_bf, mod, g_norm2, l)
        routing = _route(h2t, w_peer_qt, subkeys_bf, l)
        x = _peer_residual(x, _peer(h2t, u_bf, vt_bf, routing, l), mod, l)

    y_ctx, y_dec = _final_norm(x, g_final)
    return (y_ctx.reshape(BATCH, SEQ, D_MODEL), y_dec.reshape(DEC_BATCH, DEC_SEQ, D_MODEL), st_ckv, st_kr,
            st_k.reshape(BATCH, DEPTH, SEQ, DIFF_HEADS, 2 * DIFF_QK),
            st_v.reshape(BATCH, DEPTH, SEQ, DIFF_HEADS, DIFF_V))
```

```python
import functools
import math

import jax
import jax.numpy as jnp
from jax import lax
from jax.experimental import pallas as pl
from jax.experimental.pallas import tpu as pltpu

BF = jnp.bfloat16
F32 = jnp.float32

D_MODEL = 2048
BATCH = 16
SEQ = 256
DEPTH = 4
DEC_BATCH = 2
DEC_SEQ = 1024
PAST_LEN = 256
GRID_W = 64
ROPE_BASE = 10000.0
EPS = 1e-6

MLA_HEADS = 8
MLA_NOPE = 128
MLA_ROPE = 64
MLA_V = 128
MLA_Q_RANK = 512
MLA_KV_RANK = 256
MLA_HEAD_PAD = 256

POOL_WINDOWS = (2, 4, 8, 16)
POOL_GROUP = 128
POOL_WIDTH = POOL_GROUP * len(POOL_WINDOWS)

DIFF_HEADS = 4
DIFF_QK = 64
DIFF_V = 2 * DIFF_QK
DIFF_WIDTH = DIFF_HEADS * DIFF_V

PEER_HEADS = 8
PEER_KEYS = 128
PEER_EXPERTS = PEER_KEYS * PEER_KEYS
PEER_TOPK = 16
PEER_QDIM = 256

N_CTX = BATCH * SEQ
N_DEC = DEC_BATCH * DEC_SEQ
N_TOK = N_CTX + N_DEC
KEYS_DEC = PAST_LEN + DEC_SEQ

C_QA = 0
C_CKV = 512
C_KR = 768
C_PZ = 1024
C_DQ = 1536
C_DK = 2048
C_DV = 2560
SMALL_WIDTH = 3072
GATE_WIDTH = 3 * D_MODEL

VMEM_LIMIT = 56 * 1024 * 1024


def _cparams(sem):
    return pltpu.CompilerParams(dimension_semantics=sem, vmem_limit_bytes=VMEM_LIMIT)


def _segment(row0):
    return jnp.where(row0 < N_CTX, 0, 1 + (row0 - N_CTX) // DEC_SEQ)


def _rms(x, g):
    return x * lax.rsqrt(jnp.mean(x * x, axis=-1, keepdims=True) + EPS) * g


def _softmax_rows(s):
    m = jnp.max(s, axis=-1, keepdims=True)
    p = jnp.exp(s - m)
    return p / jnp.sum(p, axis=-1, keepdims=True)


def _dot_nt(a, b):
    return lax.dot_general(a, b, (((1,), (1,)), ((), ())), preferred_element_type=F32)


def _rope_swap(x):
    n = x.shape[-1]
    up = pltpu.roll(x, n - 16, 1)
    dn = pltpu.roll(x, 16, 1)
    lane = lax.broadcasted_iota(jnp.int32, x.shape, 1)
    return jnp.where((lane & 16) == 0, up, dn)


def _mod_kernel(c_ref, w_ref, b_ref, o_ref):
    c = c_ref[...]
    a = (c * jax.nn.sigmoid(c)).astype(BF)
    o_ref[...] = jnp.dot(a, w_ref[...].astype(BF), preferred_element_type=F32) + b_ref[...]


def _modulation(cond8, w_mod, b_mod):
    tn = 1024
    return pl.pallas_call(
        _mod_kernel,
        grid=(DEPTH, 6 * D_MODEL // tn),
        in_specs=[
            pl.BlockSpec((8, D_MODEL), lambda l, j: (0, 0)),
            pl.BlockSpec((None, D_MODEL, tn), lambda l, j: (l, 0, j)),
            pl.BlockSpec((None, 1, tn), lambda l, j: (l, 0, j)),
        ],
        out_specs=pl.BlockSpec((None, 8, tn), lambda l, j: (l, 0, j)),
        out_shape=jax.ShapeDtypeStruct((DEPTH, 8, 6 * D_MODEL), F32),
        compiler_params=_cparams(("arbitrary", "arbitrary")),
        name="modulation",
    )(cond8, w_mod, b_mod.reshape(DEPTH, 1, 6 * D_MODEL))


def _inproj_kernel(x_ref, g_ref, sh_ref, sc_ref, w_ref, o_ref, h_ref, *, tm):
    i = pl.program_id(0)

    @pl.when(pl.program_id(1) == 0)
    def _():
        seg = _segment(i * tm)
        y = _rms(x_ref[...], g_ref[...])
        h_ref[...] = (y * (1.0 + sc_ref[pl.ds(seg, 1), :]) + sh_ref[pl.ds(seg, 1), :]).astype(BF)

    o_ref[...] = jnp.dot(h_ref[...], w_ref[...], preferred_element_type=F32).astype(o_ref.dtype)


def _in_projection(x, g_norm1, mod, w_in_re, l, col0, width, out_dtype, tm=1024, tn=512):
    off = col0 // tn
    return pl.pallas_call(
        functools.partial(_inproj_kernel, tm=tm),
        grid=(N_TOK // tm, width // tn),
        in_specs=[
            pl.BlockSpec((tm, D_MODEL), lambda i, j: (i, 0)),
            pl.BlockSpec((None, 1, D_MODEL), lambda i, j: (l, 0, 0)),
            pl.BlockSpec((None, 8, D_MODEL), lambda i, j: (l, 0, 0)),
            pl.BlockSpec((None, 8, D_MODEL), lambda i, j: (l, 0, 1)),
            pl.BlockSpec((None, D_MODEL, tn), lambda i, j: (l, 0, j + off)),
        ],
        out_specs=pl.BlockSpec((tm, tn), lambda i, j: (i, j)),
        out_shape=jax.ShapeDtypeStruct((N_TOK, width), out_dtype),
        scratch_shapes=[pltpu.VMEM((tm, D_MODEL), BF)],
        compiler_params=_cparams(("arbitrary", "arbitrary")),
        name="in_projection",
    )(x, g_norm1.reshape(DEPTH, 1, D_MODEL), mod, mod, w_in_re)


def _inproj_res_kernel(x_ref, pt_ref, ga_ref, g_ref, sh_ref, sc_ref, w_ref, o_ref, xo_ref, h_ref, *, tm):
    i = pl.program_id(0)

    @pl.when(pl.program_id(1) == 0)
    def _():
        seg = _segment(i * tm)
        x = x_ref[...] + ga_ref[pl.ds(seg, 1), :] * pt_ref[...].T
        xo_ref[...] = x
        y = _rms(x, g_ref[...])
        h_ref[...] = (y * (1.0 + sc_ref[pl.ds(seg, 1), :]) + sh_ref[pl.ds(seg, 1), :]).astype(BF)

    o_ref[...] = jnp.dot(h_ref[...], w_ref[...], preferred_element_type=F32).astype(o_ref.dtype)


def _in_projection_res(x, peer_t, g_norm1, mod, w_in_re, l, width):
    tm, tn = 512, 512
    return pl.pallas_call(
        functools.partial(_inproj_res_kernel, tm=tm),
        grid=(N_TOK // tm, width // tn),
        in_specs=[
            pl.BlockSpec((tm, D_MODEL), lambda i, j: (i, 0)),
            pl.BlockSpec((D_MODEL, tm), lambda i, j: (0, i)),
            pl.BlockSpec((None, 8, D_MODEL), lambda i, j: (l - 1, 0, 5)),
            pl.BlockSpec((None, 1, D_MODEL), lambda i, j: (l, 0, 0)),
            pl.BlockSpec((None, 8, D_MODEL), lambda i, j: (l, 0, 0)),
            pl.BlockSpec((None, 8, D_MODEL), lambda i, j: (l, 0, 1)),
            pl.BlockSpec((None, D_MODEL, tn), lambda i, j: (l, 0, j)),
        ],
        out_specs=[
            pl.BlockSpec((tm, tn), lambda i, j: (i, j)),
            pl.BlockSpec((tm, D_MODEL), lambda i, j: (i, 0)),
        ],
        out_shape=[jax.ShapeDtypeStruct((N_TOK, width), F32), jax.ShapeDtypeStruct((N_TOK, D_MODEL), F32)],
        scratch_shapes=[pltpu.VMEM((tm, D_MODEL), BF)],
        compiler_params=_cparams(("arbitrary", "arbitrary")),
        name="in_projection_res",
    )(x, peer_t, mod, g_norm1.reshape(DEPTH, 1, D_MODEL), mod, mod, w_in_re)


def _mla_heads(q, kfun, vfun, o_ref, scale):
    for h in range(MLA_HEADS):
        s = _dot_nt(q(h), kfun(h)) * scale
        p = _softmax_rows(s).astype(BF)
        o = jnp.dot(p, vfun(h), preferred_element_type=F32)
        o_ref[:, h * MLA_V:(h + 1) * MLA_V] = o.astype(o_ref.dtype)


def _mla_ctx_kernel(qa_ref, ckv_ref, kr_ref, gq_ref, gkv_ref, wqb_ref, wkvb_ref, st_ckv_hbm, st_kr_hbm,
                    o_ref, ckv_out_ref, kr_out_ref):
    del st_ckv_hbm, st_kr_hbm
    scale = (MLA_NOPE + MLA_ROPE) ** -0.5
    qn = _rms(qa_ref[...], gq_ref[...]).astype(BF)
    q = jnp.dot(qn, wqb_ref[...], preferred_element_type=F32).astype(BF)
    ckv = _rms(ckv_ref[...], gkv_ref[...])
    ckv_out_ref[...] = ckv
    kr = kr_ref[...]
    kr_out_ref[...] = kr[:, :MLA_ROPE]
    kv = jnp.dot(ckv.astype(BF), wkvb_ref[...], preferred_element_type=F32).astype(BF)
    krp = kr[:, :128].astype(BF)

    def qh(h):
        return q[:, h * MLA_HEAD_PAD:(h + 1) * MLA_HEAD_PAD]

    def kh(h):
        return jnp.concatenate([kv[:, h * 256:h * 256 + MLA_NOPE], krp], axis=1)

    def vh(h):
        return kv[:, h * 256 + MLA_NOPE:(h + 1) * 256]

    _mla_heads(qh, kh, vh, o_ref, scale)


def _mla_ctx(p_small, g_qnorm, g_kvnorm, w_qb_re, w_kvb, st_ckv, st_kr, l):
    return pl.pallas_call(
        _mla_ctx_kernel,
        grid=(BATCH,),
        in_specs=[
            pl.BlockSpec((SEQ, MLA_Q_RANK), lambda b: (b, C_QA // MLA_Q_RANK)),
            pl.BlockSpec((SEQ, 256), lambda b: (b, C_CKV // 256)),
            pl.BlockSpec((SEQ, 256), lambda b: (b, C_KR // 256)),
            pl.BlockSpec((None, 1, MLA_Q_RANK), lambda b: (l, 0, 0)),
            pl.BlockSpec((None, 1, MLA_KV_RANK), lambda b: (l, 0, 0)),
            pl.BlockSpec((None, MLA_Q_RANK, MLA_HEADS * MLA_HEAD_PAD), lambda b: (l, 0, 0)),
            pl.BlockSpec((None, MLA_KV_RANK, MLA_HEADS * 256), lambda b: (l, 0, 0)),
            pl.BlockSpec(memory_space=pl.ANY),
            pl.BlockSpec(memory_space=pl.ANY),
        ],
        out_specs=[
            pl.BlockSpec((SEQ, MLA_HEADS * MLA_V), lambda b: (b, 0)),
            pl.BlockSpec((None, None, SEQ, MLA_KV_RANK), lambda b: (b, l, 0, 0)),
            pl.BlockSpec((None, None, SEQ, MLA_ROPE), lambda b: (b, l, 0, 0)),
        ],
        out_shape=[
            jax.ShapeDtypeStruct((N_TOK, MLA_HEADS * MLA_V), BF),
            jax.ShapeDtypeStruct(st_ckv.shape, F32),
            jax.ShapeDtypeStruct(st_kr.shape, F32),
        ],
        input_output_aliases={7: 1, 8: 2},
        compiler_params=_cparams(("arbitrary",)),
        name="mla_ctx",
    )(p_small, p_small, p_small, g_qnorm.reshape(DEPTH, 1, -1), g_kvnorm.reshape(DEPTH, 1, -1),
      w_qb_re, w_kvb, st_ckv, st_kr)


def _mla_dec_kernel(qa_ref, ckv_ref, kr_ref, cckv_ref, ckr_ref, cq_ref, sq_ref, ck_ref, sk_ref,
                    gq_ref, gkv_ref, wqb_ref, wkvb_ref, mix_hbm, o_ref, kf_ref, vf_ref):
    del mix_hbm
    scale = (MLA_NOPE + MLA_ROPE) ** -0.5

    @pl.when(pl.program_id(1) == 0)
    def _():
        ckv = _rms(ckv_ref[...], gkv_ref[...])
        ckv_all = jnp.concatenate([cckv_ref[...], ckv], axis=0).astype(BF)
        kr = kr_ref[:, :128]
        kr_rot = kr * ck_ref[...] + _rope_swap(kr) * sk_ref[...]
        kr_all = jnp.concatenate([ckr_ref[...], kr_rot], axis=0).astype(BF)
        for h in range(MLA_HEADS):
            kvh = jnp.dot(ckv_all, wkvb_ref[:, h * 256:(h + 1) * 256], preferred_element_type=F32)
            kf_ref[h, :, 0:MLA_NOPE] = kvh[:, :MLA_NOPE].astype(BF)
            kf_ref[h, :, MLA_NOPE:MLA_HEAD_PAD] = kr_all
            vf_ref[h] = kvh[:, MLA_NOPE:].astype(BF)

    qn = _rms(qa_ref[...], gq_ref[...]).astype(BF)
    q = jnp.dot(qn, wqb_ref[...], preferred_element_type=F32)
    cq = cq_ref[...]
    sq = sq_ref[...]

    def qh(h):
        x = q[:, h * MLA_HEAD_PAD:(h + 1) * MLA_HEAD_PAD]
        return (x * cq + _rope_swap(x) * sq).astype(BF)

    _mla_heads(qh, lambda h: kf_ref[h], lambda h: vf_ref[h], o_ref, scale)


def _mla_dec(p_small, cache_ckv, cache_kr_pad, tabs, g_qnorm, g_kvnorm, w_qb_re, w_kvb, mix, l):
    tq = 256
    nq = DEC_SEQ // tq
    row_blk = N_CTX // DEC_SEQ
    cq, sq, ck, sk = tabs
    return pl.pallas_call(
        _mla_dec_kernel,
        grid=(DEC_BATCH, nq),
        in_specs=[
            pl.BlockSpec((tq, MLA_Q_RANK), lambda b, i: (N_CTX // tq + b * nq + i, 0)),
            pl.BlockSpec((DEC_SEQ, 256), lambda b, i: (row_blk + b, C_CKV // 256)),
            pl.BlockSpec((DEC_SEQ, 256), lambda b, i: (row_blk + b, C_KR // 256)),
            pl.BlockSpec((None, None, PAST_LEN, MLA_KV_RANK), lambda b, i: (b, l, 0, 0)),
            pl.BlockSpec((None, None, PAST_LEN, 128), lambda b, i: (b, l, 0, 0)),
            pl.BlockSpec((tq, MLA_HEAD_PAD), lambda b, i: (i, 0)),
            pl.BlockSpec((tq, MLA_HEAD_PAD), lambda b, i: (i, 0)),
            pl.BlockSpec((DEC_SEQ, 128), lambda b, i: (0, 0)),
            pl.BlockSpec((DEC_SEQ, 128), lambda b, i: (0, 0)),
            pl.BlockSpec((None, 1, MLA_Q_RANK), lambda b, i: (l, 0, 0)),
            pl.BlockSpec((None, 1, MLA_KV_RANK), lambda b, i: (l, 0, 0)),
            pl.BlockSpec((None, MLA_Q_RANK, MLA_HEADS * MLA_HEAD_PAD), lambda b, i: (l, 0, 0)),
            pl.BlockSpec((None, MLA_KV_RANK, MLA_HEADS * 256), lambda b, i: (l, 0, 0)),
            pl.BlockSpec(memory_space=pl.ANY),
        ],
        out_specs=pl.BlockSpec((tq, MLA_HEADS * MLA_V), lambda b, i: (N_CTX // tq + b * nq + i, 0)),
        out_shape=jax.ShapeDtypeStruct((N_TOK, MLA_HEADS * MLA_V), BF),
        input_output_aliases={13: 0},
        scratch_shapes=[
            pltpu.VMEM((MLA_HEADS, KEYS_DEC, MLA_HEAD_PAD), BF),
            pltpu.VMEM((MLA_HEADS, KEYS_DEC, MLA_V), BF),
        ],
        compiler_params=_cparams(("arbitrary", "arbitrary")),
        name="mla_dec",
    )(p_small, p_small, p_small, cache_ckv, cache_kr_pad, cq, sq, ck, sk,
      g_qnorm.reshape(DEPTH, 1, -1), g_kvnorm.reshape(DEPTH, 1, -1), w_qb_re, w_kvb, mix)


def _diff_lambda(lv):
    t1 = jnp.sum(lv[0:1] * lv[1:2], axis=-1, keepdims=True)
    t2 = jnp.sum(lv[2:3] * lv[3:4], axis=-1, keepdims=True)
    return jnp.exp(t1) - jnp.exp(t2)


def _diff_heads(q, kfun, vfun, lam, g, o_ref, lam_init):
    scale = DIFF_QK ** -0.5
    lane = lax.broadcasted_iota(jnp.int32, (q.shape[0], DIFF_V), 1)
    for h in range(DIFF_HEADS):
        qh = q[:, h * DIFF_V:(h + 1) * DIFF_V]
        q1 = jnp.where(lane < DIFF_QK, qh, 0.0).astype(BF)
        q2 = jnp.where(lane >= DIFF_QK, qh, 0.0).astype(BF)
        k = kfun(h)
        v = vfun(h)
        p1 = _softmax_rows(_dot_nt(q1, k) * scale).astype(BF)
        p2 = _softmax_rows(_dot_nt(q2, k) * scale).astype(BF)
        a1 = jnp.dot(p1, v, preferred_element_type=F32)
        a2 = jnp.dot(p2, v, preferred_element_type=F32)
        o = _rms(a1 - lam * a2, g) * (1.0 - lam_init)
        o_ref[:, h * DIFF_V:(h + 1) * DIFF_V] = o.astype(o_ref.dtype)


def _diff_ctx_kernel(q_ref, k_ref, v_ref, lam_ref, g_ref, st_k_hbm, st_v_hbm,
                     o_ref, k_out_ref, v_out_ref, *, lam_init):
    del st_k_hbm, st_v_hbm
    lam = _diff_lambda(lam_ref[...]) + lam_init
    k_out_ref[...] = k_ref[...]
    v_out_ref[...] = v_ref[...]
    k = k_ref[...].astype(BF)
    v = v_ref[...].astype(BF)
    _diff_heads(q_ref[...],
                lambda h: k[:, h * DIFF_V:(h + 1) * DIFF_V],
                lambda h: v[:, h * DIFF_V:(h + 1) * DIFF_V],
                lam, g_ref[...], o_ref, lam_init)


def _lam_init(l):
    return 0.8 - 0.6 * math.exp(-0.3 * l)


def _diff_ctx(p_small, diff_lambda, g_diffnorm, st_k, st_v, l):
    return pl.pallas_call(
        functools.partial(_diff_ctx_kernel, lam_init=_lam_init(l)),
        grid=(BATCH,),
        in_specs=[
            pl.BlockSpec((SEQ, DIFF_WIDTH), lambda b: (b, C_DQ // DIFF_WIDTH)),
            pl.BlockSpec((SEQ, DIFF_WIDTH), lambda b: (b, C_DK // DIFF_WIDTH)),
            pl.BlockSpec((SEQ, DIFF_WIDTH), lambda b: (b, C_DV // DIFF_WIDTH)),
            pl.BlockSpec((None, 4, DIFF_QK), lambda b: (l, 0, 0)),
            pl.BlockSpec((None, 1, DIFF_V), lambda b: (l, 0, 0)),
            pl.BlockSpec(memory_space=pl.ANY),
            pl.BlockSpec(memory_space=pl.ANY),
        ],
        out_specs=[
            pl.BlockSpec((SEQ, DIFF_WIDTH), lambda b: (b, 0)),
            pl.BlockSpec((None, None, SEQ, DIFF_WIDTH), lambda b: (b, l, 0, 0)),
            pl.BlockSpec((None, None, SEQ, DIFF_WIDTH), lambda b: (b, l, 0, 0)),
        ],
        out_shape=[
            jax.ShapeDtypeStruct((N_TOK, DIFF_WIDTH), BF),
            jax.ShapeDtypeStruct(st_k.shape, F32),
            jax.ShapeDtypeStruct(st_v.shape, F32),
        ],
        input_output_aliases={5: 1, 6: 2},
        compiler_params=_cparams(("arbitrary",)),
        name="diff_ctx",
    )(p_small, p_small, p_small, diff_lambda, g_diffnorm.reshape(DEPTH, 1, -1), st_k, st_v)


def _diff_dec_kernel(q_ref, k_ref, v_ref, ck_ref, cv_ref, cq_ref, sq_ref, cfull_ref, sfull_ref,
                     lam_ref, g_ref, mix_hbm, o_ref, kf_ref, vf_ref, *, lam_init):
    del mix_hbm

    @pl.when(pl.program_id(1) == 0)
    def _():
        k = k_ref[...]
        k_rot = k * cfull_ref[...] + _rope_swap(k) * sfull_ref[...]
        kf_ref[0:PAST_LEN, :] = ck_ref[...].astype(BF)
        kf_ref[PAST_LEN:KEYS_DEC, :] = k_rot.astype(BF)
        vf_ref[0:PAST_LEN, :] = cv_ref[...].astype(BF)
        vf_ref[PAST_LEN:KEYS_DEC, :] = v_ref[...].astype(BF)

    lam = _diff_lambda(lam_ref[...]) + lam_init
    q = q_ref[...]
    q = q * cq_ref[...] + _rope_swap(q) * sq_ref[...]
    _diff_heads(q,
                lambda h: kf_ref[:, h * DIFF_V:(h + 1) * DIFF_V],
                lambda h: vf_ref[:, h * DIFF_V:(h + 1) * DIFF_V],
                lam, g_ref[...], o_ref, lam_init)


def _diff_dec(p_small, cache_k, cache_v, tabs, diff_lambda, g_diffnorm, mix, l):
    tq = 256
    nq = DEC_SEQ // tq
    row_blk = N_CTX // DEC_SEQ
    c512, s512 = tabs
    return pl.pallas_call(
        functools.partial(_diff_dec_kernel, lam_init=_lam_init(l)),
        grid=(DEC_BATCH, nq),
        in_specs=[
            pl.BlockSpec((tq, DIFF_WIDTH), lambda b, i: (N_CTX // tq + b * nq + i, C_DQ // DIFF_WIDTH)),
            pl.BlockSpec((DEC_SEQ, DIFF_WIDTH), lambda b, i: (row_blk + b, C_DK // DIFF_WIDTH)),
            pl.BlockSpec((DEC_SEQ, DIFF_WIDTH), lambda b, i: (row_blk + b, C_DV // DIFF_WIDTH)),
            pl.BlockSpec((None, None, PAST_LEN, DIFF_WIDTH), lambda b, i: (b, l, 0, 0)),
            pl.BlockSpec((None, None, PAST_LEN, DIFF_WIDTH), lambda b, i: (b, l, 0, 0)),
            pl.BlockSpec((tq, DIFF_WIDTH), lambda b, i: (i, 0)),
            pl.BlockSpec((tq, DIFF_WIDTH), lambda b, i: (i, 0)),
            pl.BlockSpec((DEC_SEQ, DIFF_WIDTH), lambda b, i: (0, 0)),
            pl.BlockSpec((DEC_SEQ, DIFF_WIDTH), lambda b, i: (0, 0)),
            pl.BlockSpec((None, 4, DIFF_QK), lambda b, i: (l, 0, 0)),
            pl.BlockSpec((None, 1, DIFF_V), lambda b, i: (l, 0, 0)),
            pl.BlockSpec(memory_space=pl.ANY),
        ],
        out_specs=pl.BlockSpec((tq, DIFF_WIDTH), lambda b, i: (N_CTX // tq + b * nq + i, 0)),
        out_shape=jax.ShapeDtypeStruct((N_TOK, DIFF_WIDTH), BF),
        input_output_aliases={11: 0},
        scratch_shapes=[
            pltpu.VMEM((KEYS_DEC, DIFF_WIDTH), BF),
            pltpu.VMEM((KEYS_DEC, DIFF_WIDTH), BF),
        ],
        compiler_params=_cparams(("arbitrary", "arbitrary")),
        name="diff_dec",
    )(p_small, p_small, p_small, cache_k, cache_v, c512, s512, c512, s512,
      diff_lambda, g_diffnorm.reshape(DEPTH, 1, -1), mix)


def _pool_kernel(z_ref, w_ref, ps_ref, o_ref, *, tm):
    seq_m1 = jnp.where(pl.program_id(0) * tm < N_CTX, SEQ - 1, DEC_SEQ - 1)
    t = lax.broadcasted_iota(jnp.int32, (tm, POOL_GROUP), 0) & seq_m1
    for gi, w in enumerate(POOL_WINDOWS):
        z = z_ref[:, gi * POOL_GROUP:(gi + 1) * POOL_GROUP]
        acc = jnp.zeros_like(z)
        for k in range(-(w // 2), w // 2):
            zs = z if k == 0 else pltpu.roll(z, (-k) % tm, 0)
            ok = (t + k >= 0) & (t + k <= seq_m1)
            acc = acc + jnp.where(ok, zs, 0.0)
        lo = jnp.maximum(t - w // 2, 0)
        hi = jnp.minimum(t + (w - 1) // 2, seq_m1)
        d = (acc / (hi - lo + 1).astype(F32) - z).astype(BF)
        y = jnp.dot(d, w_ref[gi], preferred_element_type=F32)
        y = y * ps_ref[:, gi * POOL_GROUP:(gi + 1) * POOL_GROUP]
        o_ref[:, gi * POOL_GROUP:(gi + 1) * POOL_GROUP] = y.astype(o_ref.dtype)


def _pool(p_small, w_pool_bf, pool_scale, l):
    tm = DEC_SEQ
    return pl.pallas_call(
        functools.partial(_pool_kernel, tm=tm),
        grid=(N_TOK // tm,),
        in_specs=[
            pl.BlockSpec((tm, POOL_WIDTH), lambda i: (i, C_PZ // POOL_WIDTH)),
            pl.BlockSpec((None, len(POOL_WINDOWS), POOL_GROUP, POOL_GROUP), lambda i: (l, 0, 0, 0)),
            pl.BlockSpec((None, 1, POOL_WIDTH), lambda i: (l, 0, 0)),
        ],
        out_specs=pl.BlockSpec((tm, POOL_WIDTH), lambda i: (i, 0)),
        out_shape=jax.ShapeDtypeStruct((N_TOK, POOL_WIDTH), BF),
        compiler_params=_cparams(("arbitrary",)),
        name="pool",
    )(p_small, w_pool_bf, pool_scale.reshape(DEPTH, 1, POOL_WIDTH))


def _merge_kernel(a_ref, p_ref, d_ref, g0_ref, g1_ref, g2_ref, wa_ref, wp_ref, wd_ref, o_ref):
    def sig(r):
        return jax.nn.sigmoid(r[...].astype(F32))

    m = sig(g0_ref) * jnp.dot(a_ref[...], wa_ref[...], preferred_element_type=F32)
    m = m + sig(g1_ref) * jnp.dot(p_ref[...], wp_ref[...], preferred_element_type=F32)
    m = m + sig(g2_ref) * jnp.dot(d_ref[...], wd_ref[...], preferred_element_type=F32)
    o_ref[...] = m.astype(o_ref.dtype)


def _merge(mla_o, pool_o, diff_o, gl, w_br_mla, w_br_pool, w_br_diff, l):
    tm, tn = 1024, 512
    nj = D_MODEL // tn
    return pl.pallas_call(
        _merge_kernel,
        grid=(N_TOK // tm, nj),
        in_specs=[
            pl.BlockSpec((tm, MLA_HEADS * MLA_V), lambda i, j: (i, 0)),
            pl.BlockSpec((tm, POOL_WIDTH), lambda i, j: (i, 0)),
            pl.BlockSpec((tm, DIFF_WIDTH), lambda i, j: (i, 0)),
            pl.BlockSpec((tm, tn), lambda i, j: (i, j)),
            pl.BlockSpec((tm, tn), lambda i, j: (i, nj + j)),
            pl.BlockSpec((tm, tn), lambda i, j: (i, 2 * nj + j)),
            pl.BlockSpec((None, MLA_HEADS * MLA_V, tn), lambda i, j: (l, 0, j)),
            pl.BlockSpec((None, POOL_WIDTH, tn), lambda i, j: (l, 0, j)),
            pl.BlockSpec((None, DIFF_WIDTH, tn), lambda i, j: (l, 0, j)),
        ],
        out_specs=pl.BlockSpec((tm, tn), lambda i, j: (i, j)),
        out_shape=jax.ShapeDtypeStruct((N_TOK, D_MODEL), BF),
        compiler_params=_cparams(("arbitrary", "arbitrary")),
        name="merge",
    )(mla_o, pool_o, diff_o, gl, gl, gl, w_br_mla, w_br_pool, w_br_diff)


def _outproj_kernel(x_ref, m_ref, w_ref, ga_ref, g2_ref, sh_ref, sc_ref, xo_ref, ht_ref, *, tm):
    seg = _segment(pl.program_id(0) * tm)
    y = jnp.dot(m_ref[...], w_ref[...], preferred_element_type=F32)
    xn = x_ref[...] + ga_ref[pl.ds(seg, 1), :] * y
    xo_ref[...] = xn
    h2 = _rms(xn, g2_ref[...]) * (1.0 + sc_ref[pl.ds(seg, 1), :]) + sh_ref[pl.ds(seg, 1), :]
    ht_ref[...] = h2.T.astype(BF)


def _out_projection(x, merged, w_out, mod, g_norm2, l):
    tm = 256
    return pl.pallas_call(
        functools.partial(_outproj_kernel, tm=tm),
        grid=(N_TOK // tm,),
        in_specs=[
            pl.BlockSpec((tm, D_MODEL), lambda i: (i, 0)),
            pl.BlockSpec((tm, D_MODEL), lambda i: (i, 0)),
            pl.BlockSpec((None, D_MODEL, D_MODEL), lambda i: (l, 0, 0)),
            pl.BlockSpec((None, 8, D_MODEL), lambda i: (l, 0, 2)),
            pl.BlockSpec((None, 1, D_MODEL), lambda i: (l, 0, 0)),
            pl.BlockSpec((None, 8, D_MODEL), lambda i: (l, 0, 3)),
            pl.BlockSpec((None, 8, D_MODEL), lambda i: (l, 0, 4)),
        ],
        out_specs=[
            pl.BlockSpec((tm, D_MODEL), lambda i: (i, 0)),
            pl.BlockSpec((D_MODEL, tm), lambda i: (0, i)),
        ],
        out_shape=[
            jax.ShapeDtypeStruct((N_TOK, D_MODEL), F32),
            jax.ShapeDtypeStruct((D_MODEL, N_TOK), BF),
        ],
        compiler_params=_cparams(("arbitrary",)),
        name="out_projection",
    )(x, merged, w_out, mod, g_norm2.reshape(DEPTH, 1, D_MODEL), mod, mod)


ROUTE_LANES = 128
NOT_SELECTED = 127


SLAB = 8
NET_WIDTH = 16


def _sorting_network(n):
    def merge(lo, hi, r):
        step = r * 2
        if step < hi - lo:
            yield from merge(lo, hi, step)
            yield from merge(lo + r, hi, step)
            yield from ((i, i + r) for i in range(lo + r, hi - r, step))
        else:
            yield (lo, lo + r)

    def sort(lo, hi):
        if hi - lo >= 1:
            mid = lo + (hi - lo) // 2
            yield from sort(lo, mid)
            yield from sort(mid + 1, hi)
            yield from merge(lo, hi, 1)

    return tuple(sort(0, n - 1))


def _extract_top(x, n):
    slabs = [x[SLAB * v:SLAB * (v + 1)] for v in range(x.shape[0] // SLAB)]
    slabs += [None] * (NET_WIDTH - len(slabs))
    for i, j in _sorting_network(NET_WIDTH):
        hi, lo = slabs[i], slabs[j]
        if lo is None:
            continue
        if hi is None:
            slabs[i], slabs[j] = lo, None
        else:
            slabs[i], slabs[j] = jnp.maximum(hi, lo), jnp.minimum(hi, lo)
    stack = [s for s in slabs if s is not None]
    sub = lax.broadcasted_iota(jnp.int32, stack[0].shape, 0)
    vals = []
    for r in range(n):
        m = jnp.max(stack[0], axis=0, keepdims=True)
        vals.append(m)
        hit = stack[0] == m
        popped = sub == jnp.min(jnp.where(hit, sub, SLAB), axis=0, keepdims=True)
        for d in range(min(n - r - 1, len(stack))):
            below = stack[d + 1] if d + 1 < len(stack) else -jnp.inf
            stack[d] = jnp.where(popped, below, stack[d])
    return vals


def _stack_rows(rows):
    idx = lax.broadcasted_iota(jnp.int32, (len(rows), rows[0].shape[1]), 0)
    m = jnp.zeros((len(rows), rows[0].shape[1]), F32)
    for r, row in enumerate(rows):
        m = jnp.where(idx == r, row, m)
    return m


def _route_chunk(s1, s2):
    k = PEER_TOPK
    a = _extract_top(s1, k)
    b = _extract_top(s2, k)
    am, bm = _stack_rows(a), _stack_rows(b)
    slabs = [a[0] + bm[0:8], a[0] + bm[8:16]]
    slabs += [a[r] + bm[0:8] for r in range(1, 8)]
    slabs += [am[8:16] + b[0]]
    cand = jnp.concatenate(slabs, axis=0)
    tau = _extract_top(cand, k)[-1]
    sel = cand >= tau
    top = a[0] + b[0]
    cnt = jnp.where(sel, 1.0, 0.0)
    low = jnp.where(sel, cand, jnp.inf)

    def row_rows(r):
        return slice(0, 16) if r == 0 else slice(8 + 8 * r, 16 + 8 * r) if r < 8 else slice(64 + r, 65 + r)

    n_r = [jnp.sum(cnt[row_rows(r)], axis=0, keepdims=True) for r in range(k)]
    low_r = [jnp.min(low[row_rows(r)], axis=0, keepdims=True) for r in range(k)]
    total = n_r[0]
    for r in range(1, k):
        total = total + n_r[r]
    excess = total - float(k)
    for r in reversed(range(k)):
        drop = jnp.where(excess > 0.0, jnp.where(low_r[r] == tau, 1.0, 0.0), 0.0)
        n_r[r] = n_r[r] - drop
        excess = excess - drop
    dropped = total - float(k) - excess
    z = jnp.sum(jnp.where(sel, jnp.exp(cand - top), 0.0), axis=0, keepdims=True) - dropped * jnp.exp(tau - top)

    c1 = jnp.full(s1.shape, -1.0, F32)
    r2 = jnp.full(s2.shape, float(NOT_SELECTED), F32)
    for r in range(k - 1):
        c1 = jnp.where(s1 == a[r], n_r[r] - 1.0, c1)
        r2 = jnp.where(s2 == b[r], float(r), r2)
    key = lax.broadcasted_iota(jnp.int32, s1.shape, 0)

    def last_key_limit(s, vals):
        match = s == vals[k - 1]
        first = jnp.min(jnp.where(match, key, PEER_KEYS), axis=0, keepdims=True)
        return match, jnp.where(vals[k - 2] != vals[k - 1], first, PEER_KEYS)

    match1, limit1 = last_key_limit(s1, a)
    c1 = jnp.where(match1, jnp.where(key <= limit1, n_r[k - 1] - 1.0, c1), c1)
    match2, limit2 = last_key_limit(s2, b)
    r2 = jnp.where(match2, jnp.where(key <= limit2, float(k - 1), r2), r2)
    e1 = jnp.exp(s1 - a[0]) / z
    e2 = jnp.exp(s2 - b[0])
    return c1, e1, r2, e2


def _route_kernel(ht_ref, wq_ref, sk_ref, c1_ref, e1_ref, r2_ref, e2_ref):
    qt = jnp.dot(wq_ref[...], ht_ref[...], preferred_element_type=F32)
    s1 = jnp.dot(sk_ref[0], qt[0:PEER_KEYS].astype(BF), preferred_element_type=F32)
    s2 = jnp.dot(sk_ref[1], qt[PEER_KEYS:].astype(BF), preferred_element_type=F32)

    def pack_row_pairs(x):
        return pltpu.bitcast(x.astype(BF), jnp.int32)

    def duplicate_halves(x):
        hi = pltpu.bitcast(x.astype(BF).astype(F32), jnp.int32)
        return hi | lax.shift_right_logical(hi, 16)

    for c in range(s1.shape[1] // ROUTE_LANES):
        cs = slice(c * ROUTE_LANES, (c + 1) * ROUTE_LANES)
        c1, e1, r2, e2 = _route_chunk(s1[:, cs], s2[:, cs])
        c1_ref[:, cs] = duplicate_halves(c1)
        e1_ref[:, cs] = duplicate_halves(e1)
        r2_ref[:, cs] = pack_row_pairs(r2)
        e2_ref[:, cs] = pack_row_pairs(e2)


def _route(h2t, w_peer_qt, subkeys_bf, l):
    tn = 512
    rows = PEER_HEADS * PEER_KEYS
    spec = pl.BlockSpec((PEER_KEYS, tn), lambda t, h: (h, t))
    pair_spec = pl.BlockSpec((PEER_KEYS // 2, tn), lambda t, h: (h, t))
    dup = jax.ShapeDtypeStruct((rows, N_TOK), jnp.int32)
    pairs = jax.ShapeDtypeStruct((rows // 2, N_TOK), jnp.int32)
    return pl.pallas_call(
        _route_kernel,
        grid=(N_TOK // tn, PEER_HEADS),
        in_specs=[
            pl.BlockSpec((D_MODEL, tn), lambda t, h: (0, t)),
            pl.BlockSpec((None, PEER_QDIM, D_MODEL), lambda t, h: (l, h, 0)),
            pl.BlockSpec((None, 2, PEER_KEYS, PEER_KEYS), lambda t, h: (l, 0, 0, 0)),
        ],
        out_specs=[spec, spec, pair_spec, pair_spec],
        out_shape=[dup, dup, pairs, pairs],
        compiler_params=_cparams(("arbitrary", "arbitrary")),
        name="peer_route",
    )(h2t, w_peer_qt, subkeys_bf)


PEER_LANES = 128
PEER_BLK = 256


def _peer_kernel(*refs, tm, te, convert):
    if convert:
        (ht_ref, u_ref, v_ref, c1_ref, e1_ref, r2_ref, e2_ref, o_ref, ub_ref, vtb_ref,
         at0_ref, at1_ref, wg0_ref, wg1_ref) = refs
    else:
        (ht_ref, u_ref, vt_ref, c1_ref, e1_ref, r2_ref, e2_ref, prev_hbm, o_ref,
         at0_ref, at1_ref, wg0_ref, wg1_ref) = refs
        del prev_hbm
    e = pl.program_id(1)
    nblk = te // PEER_BLK
    half = PEER_KEYS // 2

    @pl.when(e == 0)
    def _():
        o_ref[...] = jnp.zeros_like(o_ref)

    at_bufs = (at0_ref, at1_ref)
    wg_bufs = (wg0_ref, wg1_ref)

    def scores(q):
        rows = slice(q * PEER_BLK, (q + 1) * PEER_BLK)
        u = u_ref[rows, :]
        if convert:
            u = u.astype(BF)
            ub_ref[rows, :] = u
        at_bufs[q % 2][...] = jnp.dot(u, ht_ref[...], preferred_element_type=F32)

    def weights(q):
        for bb in range(PEER_BLK // PEER_KEYS):
            i = (e * nblk + q) * (PEER_BLK // PEER_KEYS) + bb
            bs = slice(bb * PEER_KEYS, (bb + 1) * PEER_KEYS)
            last_rows = [c1_ref[pl.ds(h * PEER_KEYS + i, 1), :] for h in range(PEER_HEADS)]
            e1_rows = [e1_ref[pl.ds(h * PEER_KEYS + i, 1), :] for h in range(PEER_HEADS)]
            for c in range(tm // PEER_LANES):
                cs = slice(c * PEER_LANES, (c + 1) * PEER_LANES)
                w = jnp.zeros((PEER_KEYS, PEER_LANES), BF)
                for h in range(PEER_HEADS):
                    hs = slice(h * half, (h + 1) * half)
                    last = pltpu.bitcast(jnp.broadcast_to(last_rows[h][:, cs], (half, PEER_LANES)), BF)
                    e1 = pltpu.bitcast(jnp.broadcast_to(e1_rows[h][:, cs], (half, PEER_LANES)), BF)
                    r2 = pltpu.bitcast(r2_ref[hs, cs], BF)
                    e2 = pltpu.bitcast(e2_ref[hs, cs], BF)
                    w = w + jnp.where(r2 <= last, e2, jnp.zeros_like(e2)) * e1
                a = at_bufs[q % 2][bs, cs]
                g = (0.5 * a * (1.0 + lax.erf(a * math.sqrt(0.5)))).astype(BF)
                wg_bufs[q % 2][bs, cs] = w * g

    def update(q):
        cols = slice(q * PEER_BLK, (q + 1) * PEER_BLK)
        if convert:
            vt = v_ref[cols, :].T.astype(BF)
            vtb_ref[:, cols] = vt
        else:
            vt = vt_ref[:, cols]
        o_ref[...] += jnp.dot(vt, wg_bufs[q % 2][...], preferred_element_type=F32)

    scores(0)
    for q in range(nblk):
        if q + 1 < nblk:
            scores(q + 1)
        weights(q)
        if q >= 1:
            update(q - 1)
    update(nblk - 1)


def _peer_scratch(tm):
    return [pltpu.VMEM((PEER_BLK, tm), F32), pltpu.VMEM((PEER_BLK, tm), F32),
            pltpu.VMEM((PEER_BLK, tm), BF), pltpu.VMEM((PEER_BLK, tm), BF)]


def _peer(h2t, peer_u, peer_v, routing, l):
    tm = 512
    c1, e1, r2, e2 = routing
    rows = PEER_HEADS * PEER_KEYS
    out_sds = jax.ShapeDtypeStruct((D_MODEL, N_TOK), F32)

    te = 512
    rspec = pl.BlockSpec((rows, tm), lambda t, e: (0, 0))
    pspec = pl.BlockSpec((rows // 2, tm), lambda t, e: (0, 0))
    first, u_bf, vt_bf = pl.pallas_call(
        functools.partial(_peer_kernel, tm=tm, te=te, convert=True),
        grid=(1, PEER_EXPERTS // te),
        in_specs=[
            pl.BlockSpec((D_MODEL, tm), lambda t, e: (0, 0)),
            pl.BlockSpec((None, te, D_MODEL), lambda t, e: (l, e, 0)),
            pl.BlockSpec((None, te, D_MODEL), lambda t, e: (l, e, 0)),
            rspec, rspec, pspec, pspec,
        ],
        out_specs=[
            pl.BlockSpec((D_MODEL, tm), lambda t, e: (0, 0)),
            pl.BlockSpec((te, D_MODEL), lambda t, e: (e, 0)),
            pl.BlockSpec((D_MODEL, te), lambda t, e: (0, e)),
        ],
        out_shape=[out_sds, jax.ShapeDtypeStruct((PEER_EXPERTS, D_MODEL), BF),
                   jax.ShapeDtypeStruct((D_MODEL, PEER_EXPERTS), BF)],
        scratch_shapes=_peer_scratch(tm),
        compiler_params=_cparams(("arbitrary", "arbitrary")),
        name="peer_dense_first",
    )(h2t, peer_u, peer_v, c1, e1, r2, e2)

    te = 1024
    rspec = pl.BlockSpec((rows, tm), lambda t, e: (0, t + 1))
    pspec = pl.BlockSpec((rows // 2, tm), lambda t, e: (0, t + 1))
    return pl.pallas_call(
        functools.partial(_peer_kernel, tm=tm, te=te, convert=False),
        grid=(N_TOK // tm - 1, PEER_EXPERTS // te),
        in_specs=[
            pl.BlockSpec((D_MODEL, tm), lambda t, e: (0, t + 1)),
            pl.BlockSpec((te, D_MODEL), lambda t, e: (e, 0)),
            pl.BlockSpec((D_MODEL, te), lambda t, e: (0, e)),
            rspec, rspec, pspec, pspec,
            pl.BlockSpec(memory_space=pl.ANY),
        ],
        out_specs=pl.BlockSpec((D_MODEL, tm), lambda t, e: (0, t + 1)),
        out_shape=out_sds,
        input_output_aliases={7: 0},
        scratch_shapes=_peer_scratch(tm),
        compiler_params=_cparams(("arbitrary", "arbitrary")),
        name="peer_dense",
    )(h2t, u_bf, vt_bf, c1, e1, r2, e2, first)


def _final_kernel(x_ref, pt_ref, ga_ref, g_ref, ctx_ref, dec_ref, *, tm, n_ctx_tiles):
    seg = _segment(pl.program_id(0) * tm)
    x = x_ref[...] + ga_ref[pl.ds(seg, 1), :] * pt_ref[...].T
    y = _rms(x, g_ref[...])
    is_ctx = pl.program_id(0) < n_ctx_tiles

    @pl.when(is_ctx)
    def _():
        ctx_ref[...] = y

    @pl.when(jnp.logical_not(is_ctx))
    def _():
        dec_ref[...] = y


def _final_norm(x, peer_t, mod, g_final):
    tm = 512
    n_ctx_tiles = N_CTX // tm
    return pl.pallas_call(
        functools.partial(_final_kernel, tm=tm, n_ctx_tiles=n_ctx_tiles),
        grid=(N_TOK // tm,),
        in_specs=[pl.BlockSpec((tm, D_MODEL), lambda i: (i, 0)),
                  pl.BlockSpec((D_MODEL, tm), lambda i: (0, i)),
                  pl.BlockSpec((None, 8, D_MODEL), lambda i: (DEPTH - 1, 0, 5)),
                  pl.BlockSpec((1, D_MODEL), lambda i: (0, 0))],
        out_specs=[
            pl.BlockSpec((tm, D_MODEL), lambda i: (jnp.minimum(i, n_ctx_tiles - 1), 0)),
            pl.BlockSpec((tm, D_MODEL), lambda i: (jnp.maximum(i - n_ctx_tiles, 0), 0)),
        ],
        out_shape=[jax.ShapeDtypeStruct((N_CTX, D_MODEL), F32), jax.ShapeDtypeStruct((N_DEC, D_MODEL), F32)],
        compiler_params=_cparams(("arbitrary",)),
        name="final_norm",
    )(x, peer_t, mod, g_final.reshape(1, D_MODEL))


def _rope_tables():
    t = jnp.arange(DEC_SEQ)
    half = MLA_ROPE // 4
    inv = ROPE_BASE ** (-jnp.arange(half, dtype=F32) / half)
    ang_r = (t // GRID_W).astype(F32)[:, None] * inv
    ang_c = (t % GRID_W).astype(F32)[:, None] * inv
    cos = jnp.concatenate([jnp.cos(ang_r)] * 2 + [jnp.cos(ang_c)] * 2, axis=1)
    sin = jnp.concatenate([-jnp.sin(ang_r), jnp.sin(ang_r), -jnp.sin(ang_c), jnp.sin(ang_c)], axis=1)
    return cos, sin


def _relayout_w_in(w_in):
    q, ckv, kr, pz, dq, dk, dv, gl = jnp.split(
        w_in, (512, 768, 832, 1344, 1856, 2368, 2880), axis=-1)
    pad = jnp.zeros(w_in.shape[:-1] + (C_PZ - C_KR - MLA_ROPE,), w_in.dtype)
    return jnp.concatenate([q, ckv, kr, pad, pz, dq, dk, dv, gl], axis=-1).astype(BF)


def _relayout_w_qb(w_qb):
    w = w_qb.reshape(DEPTH, MLA_Q_RANK, MLA_HEADS, MLA_NOPE + MLA_ROPE)
    w = jnp.pad(w, ((0, 0), (0, 0), (0, 0), (0, MLA_HEAD_PAD - MLA_NOPE - MLA_ROPE)))
    return w.reshape(DEPTH, MLA_Q_RANK, MLA_HEADS * MLA_HEAD_PAD).astype(BF)


def kernel(x_prompt, x_sample, cache_mla_ckv, cache_mla_krope, cache_diff_k, cache_diff_v, c, c_ctx,
           w_mod, b_mod, g_norm1, w_in, g_qnorm, w_qb, g_kvnorm, w_kvb, w_pool, pool_scale,
           diff_lambda, g_diffnorm, w_br_mla, w_br_pool, w_br_diff, w_out, g_norm2,
           w_peer_q, peer_subkeys, peer_u, peer_v, g_final):
    x = jnp.concatenate([x_prompt.reshape(N_CTX, D_MODEL), x_sample.reshape(N_DEC, D_MODEL)], axis=0)
    cond8 = jnp.concatenate([c_ctx[None, :], c, jnp.zeros((8 - 1 - DEC_BATCH, D_MODEL), F32)], axis=0)

    w_in_re = _relayout_w_in(w_in)
    w_qb_re = _relayout_w_qb(w_qb)
    w_kvb_bf = w_kvb.astype(BF)
    w_pool_bf = w_pool.astype(BF)
    w_br_mla_bf, w_br_pool_bf, w_br_diff_bf = w_br_mla.astype(BF), w_br_pool.astype(BF), w_br_diff.astype(BF)
    w_out_bf = w_out.astype(BF)
    w_peer_qt = jnp.swapaxes(w_peer_q, 1, 2).astype(BF)
    subkeys_bf = peer_subkeys.astype(BF)

    cache_kr_pad = jnp.pad(cache_mla_krope, ((0, 0), (0, 0), (0, 0), (0, 128 - MLA_ROPE)))
    cache_k = cache_diff_k.reshape(DEC_BATCH, DEPTH, PAST_LEN, DIFF_WIDTH)
    cache_v = cache_diff_v.reshape(DEC_BATCH, DEPTH, PAST_LEN, DIFF_WIDTH)

    cos64, sin64 = _rope_tables()
    ones, zeros = jnp.ones_like(cos64), jnp.zeros_like(cos64)
    mla_tabs = (
        jnp.concatenate([ones, ones, cos64, ones], axis=1),
        jnp.concatenate([zeros, zeros, sin64, zeros], axis=1),
        jnp.concatenate([cos64, ones], axis=1),
        jnp.concatenate([sin64, zeros], axis=1),
    )
    diff_tabs = (jnp.tile(cos64, (1, DIFF_WIDTH // DIFF_QK)), jnp.tile(sin64, (1, DIFF_WIDTH // DIFF_QK)))

    mod = _modulation(cond8, w_mod, b_mod)

    st_ckv = jnp.zeros((BATCH, DEPTH, SEQ, MLA_KV_RANK), F32)
    st_kr = jnp.zeros((BATCH, DEPTH, SEQ, MLA_ROPE), F32)
    st_k = jnp.zeros((BATCH, DEPTH, SEQ, DIFF_WIDTH), F32)
    st_v = jnp.zeros((BATCH, DEPTH, SEQ, DIFF_WIDTH), F32)
    peer_t = None
    for l in range(DEPTH):
        if l == 0:
            p_small = _in_projection(x, g_norm1, mod, w_in_re, l, 0, SMALL_WIDTH, F32)
        else:
            p_small, x = _in_projection_res(x, peer_t, g_norm1, mod, w_in_re, l, SMALL_WIDTH)
        gl = _in_projection(x, g_norm1, mod, w_in_re, l, SMALL_WIDTH, GATE_WIDTH, BF)
        mla_c, st_ckv, st_kr = _mla_ctx(p_small, g_qnorm, g_kvnorm, w_qb_re, w_kvb_bf, st_ckv, st_kr, l)
        mla_o = _mla_dec(p_small, cache_mla_ckv, cache_kr_pad, mla_tabs, g_qnorm, g_kvnorm, w_qb_re, w_kvb_bf,
                         mla_c, l)
        diff_c, st_k, st_v = _diff_ctx(p_small, diff_lambda, g_diffnorm, st_k, st_v, l)
        diff_o = _diff_dec(p_small, cache_k, cache_v, diff_tabs, diff_lambda, g_diffnorm, diff_c, l)
        pool_o = _pool(p_small, w_pool_bf, pool_scale, l)
        merged = _merge(mla_o, pool_o, diff_o, gl, w_br_mla_bf, w_br_pool_bf, w_br_diff_bf, l)
        x, h2t = _out_projection(x, merged, w_out_bf, mod, g_norm2, l)
        routing = _route(h2t, w_peer_qt, subkeys_bf, l)
        peer_t = _peer(h2t, peer_u, peer_v, routing, l)

    y_ctx, y_dec = _final_norm(x, peer_t, mod, g_final)
    return (y_ctx.reshape(BATCH, SEQ, D_MODEL), y_dec.reshape(DEC_BATCH, DEC_SEQ, D_MODEL), st_ckv, st_kr,
            st_k.reshape(BATCH, DEPTH, SEQ, DIFF_HEADS, 2 * DIFF_QK),
            st_v.reshape(BATCH, DEPTH, SEQ, DIFF_HEADS, DIFF_V))
```

```python
import functools
import math

import jax
import jax.numpy as jnp
from jax import lax
from jax.experimental import pallas as pl
from jax.experimental.pallas import tpu as pltpu

BF = jnp.bfloat16
F32 = jnp.float32

D_MODEL = 2048
BATCH = 16
SEQ = 256
DEPTH = 4
DEC_BATCH = 2
DEC_SEQ = 1024
PAST_LEN = 256
GRID_W = 64
ROPE_BASE = 10000.0
EPS = 1e-6

MLA_HEADS = 8
MLA_NOPE = 128
MLA_ROPE = 64
MLA_V = 128
MLA_Q_RANK = 512
MLA_KV_RANK = 256
MLA_HEAD_PAD = 256

POOL_WINDOWS = (2, 4, 8, 16)
POOL_GROUP = 128
POOL_WIDTH = POOL_GROUP * len(POOL_WINDOWS)

DIFF_HEADS = 4
DIFF_QK = 64
DIFF_V = 2 * DIFF_QK
DIFF_WIDTH = DIFF_HEADS * DIFF_V

PEER_HEADS = 8
PEER_KEYS = 128
PEER_EXPERTS = PEER_KEYS * PEER_KEYS
PEER_TOPK = 16
PEER_QDIM = 256

N_CTX = BATCH * SEQ
N_DEC = DEC_BATCH * DEC_SEQ
N_TOK = N_CTX + N_DEC
KEYS_DEC = PAST_LEN + DEC_SEQ

C_QA = 0
C_CKV = 512
C_KR = 768
C_PZ = 1024
C_DQ = 1536
C_DK = 2048
C_DV = 2560
SMALL_WIDTH = 3072
GATE_WIDTH = 3 * D_MODEL

VMEM_LIMIT = 56 * 1024 * 1024


def _cparams(sem):
    return pltpu.CompilerParams(dimension_semantics=sem, vmem_limit_bytes=VMEM_LIMIT)


def _segment(row0):
    return jnp.where(row0 < N_CTX, 0, 1 + (row0 - N_CTX) // DEC_SEQ)


def _rms(x, g):
    return x * lax.rsqrt(jnp.mean(x * x, axis=-1, keepdims=True) + EPS) * g


def _softmax_rows(s):
    m = jnp.max(s, axis=-1, keepdims=True)
    p = jnp.exp(s - m)
    return p / jnp.sum(p, axis=-1, keepdims=True)


def _dot_nt(a, b):
    return lax.dot_general(a, b, (((1,), (1,)), ((), ())), preferred_element_type=F32)


def _rope_swap(x):
    n = x.shape[-1]
    up = pltpu.roll(x, n - 16, 1)
    dn = pltpu.roll(x, 16, 1)
    lane = lax.broadcasted_iota(jnp.int32, x.shape, 1)
    return jnp.where((lane & 16) == 0, up, dn)


def _mod_kernel(c_ref, w_ref, b_ref, o_ref):
    c = c_ref[...]
    a = (c * jax.nn.sigmoid(c)).astype(BF)
    o_ref[...] = jnp.dot(a, w_ref[...].astype(BF), preferred_element_type=F32) + b_ref[...]


def _modulation(cond8, w_mod, b_mod):
    tn = 1024
    return pl.pallas_call(
        _mod_kernel,
        grid=(DEPTH, 6 * D_MODEL // tn),
        in_specs=[
            pl.BlockSpec((8, D_MODEL), lambda l, j: (0, 0)),
            pl.BlockSpec((None, D_MODEL, tn), lambda l, j: (l, 0, j)),
            pl.BlockSpec((None, 1, tn), lambda l, j: (l, 0, j)),
        ],
        out_specs=pl.BlockSpec((None, 8, tn), lambda l, j: (l, 0, j)),
        out_shape=jax.ShapeDtypeStruct((DEPTH, 8, 6 * D_MODEL), F32),
        compiler_params=_cparams(("arbitrary", "arbitrary")),
        name="modulation",
    )(cond8, w_mod, b_mod.reshape(DEPTH, 1, 6 * D_MODEL))


def _inproj_kernel(x_ref, g_ref, sh_ref, sc_ref, w_ref, o_ref, h_ref, *, tm):
    i = pl.program_id(0)

    @pl.when(pl.program_id(1) == 0)
    def _():
        seg = _segment(i * tm)
        y = _rms(x_ref[...], g_ref[...])
        h_ref[...] = (y * (1.0 + sc_ref[pl.ds(seg, 1), :]) + sh_ref[pl.ds(seg, 1), :]).astype(BF)

    o_ref[...] = jnp.dot(h_ref[...], w_ref[...], preferred_element_type=F32).astype(o_ref.dtype)


def _in_projection(x, g_norm1, mod, w_in_re, l, col0, width, out_dtype, tm=1024, tn=512):
    off = col0 // tn
    return pl.pallas_call(
        functools.partial(_inproj_kernel, tm=tm),
        grid=(N_TOK // tm, width // tn),
        in_specs=[
            pl.BlockSpec((tm, D_MODEL), lambda i, j: (i, 0)),
            pl.BlockSpec((None, 1, D_MODEL), lambda i, j: (l, 0, 0)),
            pl.BlockSpec((None, 8, D_MODEL), lambda i, j: (l, 0, 0)),
            pl.BlockSpec((None, 8, D_MODEL), lambda i, j: (l, 0, 1)),
            pl.BlockSpec((None, D_MODEL, tn), lambda i, j: (l, 0, j + off)),
        ],
        out_specs=pl.BlockSpec((tm, tn), lambda i, j: (i, j)),
        out_shape=jax.ShapeDtypeStruct((N_TOK, width), out_dtype),
        scratch_shapes=[pltpu.VMEM((tm, D_MODEL), BF)],
        compiler_params=_cparams(("arbitrary", "arbitrary")),
        name="in_projection",
    )(x, g_norm1.reshape(DEPTH, 1, D_MODEL), mod, mod, w_in_re)


def _inproj_res_kernel(x_ref, pt_ref, ga_ref, g_ref, sh_ref, sc_ref, w_ref, o_ref, xo_ref, h_ref, *, tm):
    i = pl.program_id(0)

    @pl.when(pl.program_id(1) == 0)
    def _():
        seg = _segment(i * tm)
        x = x_ref[...] + ga_ref[pl.ds(seg, 1), :] * pt_ref[...].T
        xo_ref[...] = x
        y = _rms(x, g_ref[...])
        h_ref[...] = (y * (1.0 + sc_ref[pl.ds(seg, 1), :]) + sh_ref[pl.ds(seg, 1), :]).astype(BF)

    o_ref[...] = jnp.dot(h_ref[...], w_ref[...], preferred_element_type=F32).astype(o_ref.dtype)


def _in_projection_res(x, peer_t, g_norm1, mod, w_in_re, l, width):
    tm, tn = 1024, 512
    once = pl.Buffered(1)
    return pl.pallas_call(
        functools.partial(_inproj_res_kernel, tm=tm),
        grid=(N_TOK // tm, width // tn),
        in_specs=[
            pl.BlockSpec((tm, D_MODEL), lambda i, j: (i, 0), pipeline_mode=once),
            pl.BlockSpec((D_MODEL, tm), lambda i, j: (0, i), pipeline_mode=once),
            pl.BlockSpec((None, 8, D_MODEL), lambda i, j: (l - 1, 0, 5)),
            pl.BlockSpec((None, 1, D_MODEL), lambda i, j: (l, 0, 0)),
            pl.BlockSpec((None, 8, D_MODEL), lambda i, j: (l, 0, 0)),
            pl.BlockSpec((None, 8, D_MODEL), lambda i, j: (l, 0, 1)),
            pl.BlockSpec((None, D_MODEL, tn), lambda i, j: (l, 0, j)),
        ],
        out_specs=[
            pl.BlockSpec((tm, tn), lambda i, j: (i, j)),
            pl.BlockSpec((tm, D_MODEL), lambda i, j: (i, 0)),
        ],
        out_shape=[jax.ShapeDtypeStruct((N_TOK, width), F32), jax.ShapeDtypeStruct((N_TOK, D_MODEL), F32)],
        scratch_shapes=[pltpu.VMEM((tm, D_MODEL), BF)],
        compiler_params=_cparams(("arbitrary", "arbitrary")),
        name="in_projection_res",
    )(x, peer_t, mod, g_norm1.reshape(DEPTH, 1, D_MODEL), mod, mod, w_in_re)


def _mla_heads(q, kfun, vfun, o_ref, scale):
    for h in range(MLA_HEADS):
        s = _dot_nt(q(h), kfun(h)) * scale
        p = _softmax_rows(s).astype(BF)
        o = jnp.dot(p, vfun(h), preferred_element_type=F32)
        o_ref[:, h * MLA_V:(h + 1) * MLA_V] = o.astype(o_ref.dtype)


def _mla_ctx_kernel(qa_ref, ckv_ref, kr_ref, gq_ref, gkv_ref, wqb_ref, wkvb_ref, st_ckv_hbm, st_kr_hbm,
                    o_ref, ckv_out_ref, kr_out_ref):
    del st_ckv_hbm, st_kr_hbm
    scale = (MLA_NOPE + MLA_ROPE) ** -0.5
    qn = _rms(qa_ref[...], gq_ref[...]).astype(BF)
    q = jnp.dot(qn, wqb_ref[...], preferred_element_type=F32).astype(BF)
    ckv = _rms(ckv_ref[...], gkv_ref[...])
    ckv_out_ref[...] = ckv
    kr = kr_ref[...]
    kr_out_ref[...] = kr[:, :MLA_ROPE]
    kv = jnp.dot(ckv.astype(BF), wkvb_ref[...], preferred_element_type=F32).astype(BF)
    krp = kr[:, :128].astype(BF)

    def qh(h):
        return q[:, h * MLA_HEAD_PAD:(h + 1) * MLA_HEAD_PAD]

    def kh(h):
        return jnp.concatenate([kv[:, h * 256:h * 256 + MLA_NOPE], krp], axis=1)

    def vh(h):
        return kv[:, h * 256 + MLA_NOPE:(h + 1) * 256]

    _mla_heads(qh, kh, vh, o_ref, scale)


def _mla_ctx(p_small, g_qnorm, g_kvnorm, w_qb_re, w_kvb, st_ckv, st_kr, l):
    return pl.pallas_call(
        _mla_ctx_kernel,
        grid=(BATCH,),
        in_specs=[
            pl.BlockSpec((SEQ, MLA_Q_RANK), lambda b: (b, C_QA // MLA_Q_RANK)),
            pl.BlockSpec((SEQ, 256), lambda b: (b, C_CKV // 256)),
            pl.BlockSpec((SEQ, 256), lambda b: (b, C_KR // 256)),
            pl.BlockSpec((None, 1, MLA_Q_RANK), lambda b: (l, 0, 0)),
            pl.BlockSpec((None, 1, MLA_KV_RANK), lambda b: (l, 0, 0)),
            pl.BlockSpec((None, MLA_Q_RANK, MLA_HEADS * MLA_HEAD_PAD), lambda b: (l, 0, 0)),
            pl.BlockSpec((None, MLA_KV_RANK, MLA_HEADS * 256), lambda b: (l, 0, 0)),
            pl.BlockSpec(memory_space=pl.ANY),
            pl.BlockSpec(memory_space=pl.ANY),
        ],
        out_specs=[
            pl.BlockSpec((SEQ, MLA_HEADS * MLA_V), lambda b: (b, 0)),
            pl.BlockSpec((None, None, SEQ, MLA_KV_RANK), lambda b: (b, l, 0, 0)),
            pl.BlockSpec((None, None, SEQ, MLA_ROPE), lambda b: (b, l, 0, 0)),
        ],
        out_shape=[
            jax.ShapeDtypeStruct((N_TOK, MLA_HEADS * MLA_V), BF),
            jax.ShapeDtypeStruct(st_ckv.shape, F32),
            jax.ShapeDtypeStruct(st_kr.shape, F32),
        ],
        input_output_aliases={7: 1, 8: 2},
        compiler_params=_cparams(("arbitrary",)),
        name="mla_ctx",
    )(p_small, p_small, p_small, g_qnorm.reshape(DEPTH, 1, -1), g_kvnorm.reshape(DEPTH, 1, -1),
      w_qb_re, w_kvb, st_ckv, st_kr)


def _mla_dec_kernel(qa_ref, ckv_ref, kr_ref, cckv_ref, ckr_ref, cq_ref, sq_ref, ck_ref, sk_ref,
                    gq_ref, gkv_ref, wqb_ref, wkvb_ref, mix_hbm, o_ref, kf_ref, vf_ref):
    del mix_hbm
    scale = (MLA_NOPE + MLA_ROPE) ** -0.5

    @pl.when(pl.program_id(1) == 0)
    def _():
        ckv = _rms(ckv_ref[...], gkv_ref[...])
        ckv_all = jnp.concatenate([cckv_ref[...], ckv], axis=0).astype(BF)
        kr = kr_ref[:, :128]
        kr_rot = kr * ck_ref[...] + _rope_swap(kr) * sk_ref[...]
        kr_all = jnp.concatenate([ckr_ref[...], kr_rot], axis=0).astype(BF)
        for h in range(MLA_HEADS):
            kvh = jnp.dot(ckv_all, wkvb_ref[:, h * 256:(h + 1) * 256], preferred_element_type=F32)
            kf_ref[h, :, 0:MLA_NOPE] = kvh[:, :MLA_NOPE].astype(BF)
            kf_ref[h, :, MLA_NOPE:MLA_HEAD_PAD] = kr_all
            vf_ref[h] = kvh[:, MLA_NOPE:].astype(BF)

    qn = _rms(qa_ref[...], gq_ref[...]).astype(BF)
    q = jnp.dot(qn, wqb_ref[...], preferred_element_type=F32)
    cq = cq_ref[...]
    sq = sq_ref[...]

    def qh(h):
        x = q[:, h * MLA_HEAD_PAD:(h + 1) * MLA_HEAD_PAD]
        return (x * cq + _rope_swap(x) * sq).astype(BF)

    _mla_heads(qh, lambda h: kf_ref[h], lambda h: vf_ref[h], o_ref, scale)


def _mla_dec(p_small, cache_ckv, cache_kr_pad, tabs, g_qnorm, g_kvnorm, w_qb_re, w_kvb, mix, l):
    tq = 256
    nq = DEC_SEQ // tq
    row_blk = N_CTX // DEC_SEQ
    cq, sq, ck, sk = tabs
    return pl.pallas_call(
        _mla_dec_kernel,
        grid=(DEC_BATCH, nq),
        in_specs=[
            pl.BlockSpec((tq, MLA_Q_RANK), lambda b, i: (N_CTX // tq + b * nq + i, 0)),
            pl.BlockSpec((DEC_SEQ, 256), lambda b, i: (row_blk + b, C_CKV // 256)),
            pl.BlockSpec((DEC_SEQ, 256), lambda b, i: (row_blk + b, C_KR // 256)),
            pl.BlockSpec((None, None, PAST_LEN, MLA_KV_RANK), lambda b, i: (b, l, 0, 0)),
            pl.BlockSpec((None, None, PAST_LEN, 128), lambda b, i: (b, l, 0, 0)),
            pl.BlockSpec((tq, MLA_HEAD_PAD), lambda b, i: (i, 0)),
            pl.BlockSpec((tq, MLA_HEAD_PAD), lambda b, i: (i, 0)),
            pl.BlockSpec((DEC_SEQ, 128), lambda b, i: (0, 0)),
            pl.BlockSpec((DEC_SEQ, 128), lambda b, i: (0, 0)),
            pl.BlockSpec((None, 1, MLA_Q_RANK), lambda b, i: (l, 0, 0)),
            pl.BlockSpec((None, 1, MLA_KV_RANK), lambda b, i: (l, 0, 0)),
            pl.BlockSpec((None, MLA_Q_RANK, MLA_HEADS * MLA_HEAD_PAD), lambda b, i: (l, 0, 0)),
            pl.BlockSpec((None, MLA_KV_RANK, MLA_HEADS * 256), lambda b, i: (l, 0, 0)),
            pl.BlockSpec(memory_space=pl.ANY),
        ],
        out_specs=pl.BlockSpec((tq, MLA_HEADS * MLA_V), lambda b, i: (N_CTX // tq + b * nq + i, 0)),
        out_shape=jax.ShapeDtypeStruct((N_TOK, MLA_HEADS * MLA_V), BF),
        input_output_aliases={13: 0},
        scratch_shapes=[
            pltpu.VMEM((MLA_HEADS, KEYS_DEC, MLA_HEAD_PAD), BF),
            pltpu.VMEM((MLA_HEADS, KEYS_DEC, MLA_V), BF),
        ],
        compiler_params=_cparams(("arbitrary", "arbitrary")),
        name="mla_dec",
    )(p_small, p_small, p_small, cache_ckv, cache_kr_pad, cq, sq, ck, sk,
      g_qnorm.reshape(DEPTH, 1, -1), g_kvnorm.reshape(DEPTH, 1, -1), w_qb_re, w_kvb, mix)


def _diff_lambda(lv):
    t1 = jnp.sum(lv[0:1] * lv[1:2], axis=-1, keepdims=True)
    t2 = jnp.sum(lv[2:3] * lv[3:4], axis=-1, keepdims=True)
    return jnp.exp(t1) - jnp.exp(t2)


def _diff_heads(q, kfun, vfun, lam, g, o_ref, lam_init):
    scale = DIFF_QK ** -0.5
    lane = lax.broadcasted_iota(jnp.int32, (q.shape[0], DIFF_V), 1)
    for h in range(DIFF_HEADS):
        qh = q[:, h * DIFF_V:(h + 1) * DIFF_V]
        q1 = jnp.where(lane < DIFF_QK, qh, 0.0).astype(BF)
        q2 = jnp.where(lane >= DIFF_QK, qh, 0.0).astype(BF)
        k = kfun(h)
        v = vfun(h)
        p1 = _softmax_rows(_dot_nt(q1, k) * scale).astype(BF)
        p2 = _softmax_rows(_dot_nt(q2, k) * scale).astype(BF)
        a1 = jnp.dot(p1, v, preferred_element_type=F32)
        a2 = jnp.dot(p2, v, preferred_element_type=F32)
        o = _rms(a1 - lam * a2, g) * (1.0 - lam_init)
        o_ref[:, h * DIFF_V:(h + 1) * DIFF_V] = o.astype(o_ref.dtype)


def _diff_ctx_kernel(q_ref, k_ref, v_ref, lam_ref, g_ref, st_k_hbm, st_v_hbm,
                     o_ref, k_out_ref, v_out_ref, *, lam_init):
    del st_k_hbm, st_v_hbm
    lam = _diff_lambda(lam_ref[...]) + lam_init
    k_out_ref[...] = k_ref[...]
    v_out_ref[...] = v_ref[...]
    k = k_ref[...].astype(BF)
    v = v_ref[...].astype(BF)
    _diff_heads(q_ref[...],
                lambda h: k[:, h * DIFF_V:(h + 1) * DIFF_V],
                lambda h: v[:, h * DIFF_V:(h + 1) * DIFF_V],
                lam, g_ref[...], o_ref, lam_init)


def _lam_init(l):
    return 0.8 - 0.6 * math.exp(-0.3 * l)


def _diff_ctx(p_small, diff_lambda, g_diffnorm, st_k, st_v, l):
    return pl.pallas_call(
        functools.partial(_diff_ctx_kernel, lam_init=_lam_init(l)),
        grid=(BATCH,),
        in_specs=[
            pl.BlockSpec((SEQ, DIFF_WIDTH), lambda b: (b, C_DQ // DIFF_WIDTH)),
            pl.BlockSpec((SEQ, DIFF_WIDTH), lambda b: (b, C_DK // DIFF_WIDTH)),
            pl.BlockSpec((SEQ, DIFF_WIDTH), lambda b: (b, C_DV // DIFF_WIDTH)),
            pl.BlockSpec((None, 4, DIFF_QK), lambda b: (l, 0, 0)),
            pl.BlockSpec((None, 1, DIFF_V), lambda b: (l, 0, 0)),
            pl.BlockSpec(memory_space=pl.ANY),
            pl.BlockSpec(memory_space=pl.ANY),
        ],
        out_specs=[
            pl.BlockSpec((SEQ, DIFF_WIDTH), lambda b: (b, 0)),
            pl.BlockSpec((None, None, SEQ, DIFF_WIDTH), lambda b: (b, l, 0, 0)),
            pl.BlockSpec((None, None, SEQ, DIFF_WIDTH), lambda b: (b, l, 0, 0)),
        ],
        out_shape=[
            jax.ShapeDtypeStruct((N_TOK, DIFF_WIDTH), BF),
            jax.ShapeDtypeStruct(st_k.shape, F32),
            jax.ShapeDtypeStruct(st_v.shape, F32),
        ],
        input_output_aliases={5: 1, 6: 2},
        compiler_params=_cparams(("arbitrary",)),
        name="diff_ctx",
    )(p_small, p_small, p_small, diff_lambda, g_diffnorm.reshape(DEPTH, 1, -1), st_k, st_v)


def _diff_dec_kernel(q_ref, k_ref, v_ref, ck_ref, cv_ref, cq_ref, sq_ref, cfull_ref, sfull_ref,
                     lam_ref, g_ref, mix_hbm, o_ref, kf_ref, vf_ref, *, lam_init):
    del mix_hbm

    @pl.when(pl.program_id(1) == 0)
    def _():
        k = k_ref[...]
        k_rot = k * cfull_ref[...] + _rope_swap(k) * sfull_ref[...]
        kf_ref[0:PAST_LEN, :] = ck_ref[...].astype(BF)
        kf_ref[PAST_LEN:KEYS_DEC, :] = k_rot.astype(BF)
        vf_ref[0:PAST_LEN, :] = cv_ref[...].astype(BF)
        vf_ref[PAST_LEN:KEYS_DEC, :] = v_ref[...].astype(BF)

    lam = _diff_lambda(lam_ref[...]) + lam_init
    q = q_ref[...]
    q = q * cq_ref[...] + _rope_swap(q) * sq_ref[...]
    _diff_heads(q,
                lambda h: kf_ref[:, h * DIFF_V:(h + 1) * DIFF_V],
                lambda h: vf_ref[:, h * DIFF_V:(h + 1) * DIFF_V],
                lam, g_ref[...], o_ref, lam_init)


def _diff_dec(p_small, cache_k, cache_v, tabs, diff_lambda, g_diffnorm, mix, l):
    tq = 256
    nq = DEC_SEQ // tq
    row_blk = N_CTX // DEC_SEQ
    c512, s512 = tabs
    return pl.pallas_call(
        functools.partial(_diff_dec_kernel, lam_init=_lam_init(l)),
        grid=(DEC_BATCH, nq),
        in_specs=[
            pl.BlockSpec((tq, DIFF_WIDTH), lambda b, i: (N_CTX // tq + b * nq + i, C_DQ // DIFF_WIDTH)),
            pl.BlockSpec((DEC_SEQ, DIFF_WIDTH), lambda b, i: (row_blk + b, C_DK // DIFF_WIDTH)),
            pl.BlockSpec((DEC_SEQ, DIFF_WIDTH), lambda b, i: (row_blk + b, C_DV // DIFF_WIDTH)),
            pl.BlockSpec((None, None, PAST_LEN, DIFF_WIDTH), lambda b, i: (b, l, 0, 0)),
            pl.BlockSpec((None, None, PAST_LEN, DIFF_WIDTH), lambda b, i: (b, l, 0, 0)),
            pl.BlockSpec((tq, DIFF_WIDTH), lambda b, i: (i, 0)),
            pl.BlockSpec((tq, DIFF_WIDTH), lambda b, i: (i, 0)),
            pl.BlockSpec((DEC_SEQ, DIFF_WIDTH), lambda b, i: (0, 0)),
            pl.BlockSpec((DEC_SEQ, DIFF_WIDTH), lambda b, i: (0, 0)),
            pl.BlockSpec((None, 4, DIFF_QK), lambda b, i: (l, 0, 0)),
            pl.BlockSpec((None, 1, DIFF_V), lambda b, i: (l, 0, 0)),
            pl.BlockSpec(memory_space=pl.ANY),
        ],
        out_specs=pl.BlockSpec((tq, DIFF_WIDTH), lambda b, i: (N_CTX // tq + b * nq + i, 0)),
        out_shape=jax.ShapeDtypeStruct((N_TOK, DIFF_WIDTH), BF),
        input_output_aliases={11: 0},
        scratch_shapes=[
            pltpu.VMEM((KEYS_DEC, DIFF_WIDTH), BF),
            pltpu.VMEM((KEYS_DEC, DIFF_WIDTH), BF),
        ],
        compiler_params=_cparams(("arbitrary", "arbitrary")),
        name="diff_dec",
    )(p_small, p_small, p_small, cache_k, cache_v, c512, s512, c512, s512,
      diff_lambda, g_diffnorm.reshape(DEPTH, 1, -1), mix)


def _pool_kernel(z_ref, w_ref, ps_ref, o_ref, *, tm):
    seq_m1 = jnp.where(pl.program_id(0) * tm < N_CTX, SEQ - 1, DEC_SEQ - 1)
    t = lax.broadcasted_iota(jnp.int32, (tm, POOL_GROUP), 0) & seq_m1
    for gi, w in enumerate(POOL_WINDOWS):
        z = z_ref[:, gi * POOL_GROUP:(gi + 1) * POOL_GROUP]
        acc = jnp.zeros_like(z)
        for k in range(-(w // 2), w // 2):
            zs = z if k == 0 else pltpu.roll(z, (-k) % tm, 0)
            ok = (t + k >= 0) & (t + k <= seq_m1)
            acc = acc + jnp.where(ok, zs, 0.0)
        lo = jnp.maximum(t - w // 2, 0)
        hi = jnp.minimum(t + (w - 1) // 2, seq_m1)
        d = (acc / (hi - lo + 1).astype(F32) - z).astype(BF)
        y = jnp.dot(d, w_ref[gi], preferred_element_type=F32)
        y = y * ps_ref[:, gi * POOL_GROUP:(gi + 1) * POOL_GROUP]
        o_ref[:, gi * POOL_GROUP:(gi + 1) * POOL_GROUP] = y.astype(o_ref.dtype)


def _pool(p_small, w_pool_bf, pool_scale, l):
    tm = DEC_SEQ
    return pl.pallas_call(
        functools.partial(_pool_kernel, tm=tm),
        grid=(N_TOK // tm,),
        in_specs=[
            pl.BlockSpec((tm, POOL_WIDTH), lambda i: (i, C_PZ // POOL_WIDTH)),
            pl.BlockSpec((None, len(POOL_WINDOWS), POOL_GROUP, POOL_GROUP), lambda i: (l, 0, 0, 0)),
            pl.BlockSpec((None, 1, POOL_WIDTH), lambda i: (l, 0, 0)),
        ],
        out_specs=pl.BlockSpec((tm, POOL_WIDTH), lambda i: (i, 0)),
        out_shape=jax.ShapeDtypeStruct((N_TOK, POOL_WIDTH), BF),
        compiler_params=_cparams(("arbitrary",)),
        name="pool",
    )(p_small, w_pool_bf, pool_scale.reshape(DEPTH, 1, POOL_WIDTH))


def _merge_kernel(a_ref, p_ref, d_ref, g0_ref, g1_ref, g2_ref, wa_ref, wp_ref, wd_ref, o_ref):
    def sig(r):
        return jax.nn.sigmoid(r[...].astype(F32))

    m = sig(g0_ref) * jnp.dot(a_ref[...], wa_ref[...], preferred_element_type=F32)
    m = m + sig(g1_ref) * jnp.dot(p_ref[...], wp_ref[...], preferred_element_type=F32)
    m = m + sig(g2_ref) * jnp.dot(d_ref[...], wd_ref[...], preferred_element_type=F32)
    o_ref[...] = m.astype(o_ref.dtype)


def _merge(mla_o, pool_o, diff_o, gl, w_br_mla, w_br_pool, w_br_diff, l):
    tm, tn = 1024, 512
    nj = D_MODEL // tn
    return pl.pallas_call(
        _merge_kernel,
        grid=(N_TOK // tm, nj),
        in_specs=[
            pl.BlockSpec((tm, MLA_HEADS * MLA_V), lambda i, j: (i, 0)),
            pl.BlockSpec((tm, POOL_WIDTH), lambda i, j: (i, 0)),
            pl.BlockSpec((tm, DIFF_WIDTH), lambda i, j: (i, 0)),
            pl.BlockSpec((tm, tn), lambda i, j: (i, j)),
            pl.BlockSpec((tm, tn), lambda i, j: (i, nj + j)),
            pl.BlockSpec((tm, tn), lambda i, j: (i, 2 * nj + j)),
            pl.BlockSpec((None, MLA_HEADS * MLA_V, tn), lambda i, j: (l, 0, j)),
            pl.BlockSpec((None, POOL_WIDTH, tn), lambda i, j: (l, 0, j)),
            pl.BlockSpec((None, DIFF_WIDTH, tn), lambda i, j: (l, 0, j)),
        ],
        out_specs=pl.BlockSpec((tm, tn), lambda i, j: (i, j)),
        out_shape=jax.ShapeDtypeStruct((N_TOK, D_MODEL), BF),
        compiler_params=_cparams(("arbitrary", "arbitrary")),
        name="merge",
    )(mla_o, pool_o, diff_o, gl, gl, gl, w_br_mla, w_br_pool, w_br_diff)


def _outproj_kernel(x_ref, m_ref, w_ref, ga_ref, g2_ref, sh_ref, sc_ref, xo_ref, ht_ref, *, tm):
    seg = _segment(pl.program_id(0) * tm)
    y = jnp.dot(m_ref[...], w_ref[...], preferred_element_type=F32)
    xn = x_ref[...] + ga_ref[pl.ds(seg, 1), :] * y
    xo_ref[...] = xn
    h2 = _rms(xn, g2_ref[...]) * (1.0 + sc_ref[pl.ds(seg, 1), :]) + sh_ref[pl.ds(seg, 1), :]
    ht_ref[...] = h2.T.astype(BF)


def _out_projection(x, merged, w_out, mod, g_norm2, l):
    tm = 256
    return pl.pallas_call(
        functools.partial(_outproj_kernel, tm=tm),
        grid=(N_TOK // tm,),
        in_specs=[
            pl.BlockSpec((tm, D_MODEL), lambda i: (i, 0)),
            pl.BlockSpec((tm, D_MODEL), lambda i: (i, 0)),
            pl.BlockSpec((None, D_MODEL, D_MODEL), lambda i: (l, 0, 0)),
            pl.BlockSpec((None, 8, D_MODEL), lambda i: (l, 0, 2)),
            pl.BlockSpec((None, 1, D_MODEL), lambda i: (l, 0, 0)),
            pl.BlockSpec((None, 8, D_MODEL), lambda i: (l, 0, 3)),
            pl.BlockSpec((None, 8, D_MODEL), lambda i: (l, 0, 4)),
        ],
        out_specs=[
            pl.BlockSpec((tm, D_MODEL), lambda i: (i, 0)),
            pl.BlockSpec((D_MODEL, tm), lambda i: (0, i)),
        ],
        out_shape=[
            jax.ShapeDtypeStruct((N_TOK, D_MODEL), F32),
            jax.ShapeDtypeStruct((D_MODEL, N_TOK), BF),
        ],
        compiler_params=_cparams(("arbitrary",)),
        name="out_projection",
    )(x, merged, w_out, mod, g_norm2.reshape(DEPTH, 1, D_MODEL), mod, mod)


ROUTE_LANES = 128
NOT_SELECTED = 127


SLAB = 8
NET_WIDTH = 16


def _sorting_network(n):
    def merge(lo, hi, r):
        step = r * 2
        if step < hi - lo:
            yield from merge(lo, hi, step)
            yield from merge(lo + r, hi, step)
            yield from ((i, i + r) for i in range(lo + r, hi - r, step))
        else:
            yield (lo, lo + r)

    def sort(lo, hi):
        if hi - lo >= 1:
            mid = lo + (hi - lo) // 2
            yield from sort(lo, mid)
            yield from sort(mid + 1, hi)
            yield from merge(lo, hi, 1)

    return tuple(sort(0, n - 1))


def _extract_top(x, n):
    slabs = [x[SLAB * v:SLAB * (v + 1)] for v in range(x.shape[0] // SLAB)]
    slabs += [None] * (NET_WIDTH - len(slabs))
    for i, j in _sorting_network(NET_WIDTH):
        hi, lo = slabs[i], slabs[j]
        if lo is None:
            continue
        if hi is None:
            slabs[i], slabs[j] = lo, None
        else:
            slabs[i], slabs[j] = jnp.maximum(hi, lo), jnp.minimum(hi, lo)
    stack = [s for s in slabs if s is not None]
    sub = lax.broadcasted_iota(jnp.int32, stack[0].shape, 0)
    vals = []
    for r in range(n):
        m = jnp.max(stack[0], axis=0, keepdims=True)
        vals.append(m)
        hit = stack[0] == m
        popped = sub == jnp.min(jnp.where(hit, sub, SLAB), axis=0, keepdims=True)
        for d in range(min(n - r - 1, len(stack))):
            below = stack[d + 1] if d + 1 < len(stack) else -jnp.inf
            stack[d] = jnp.where(popped, below, stack[d])
    return vals


def _stack_rows(rows):
    idx = lax.broadcasted_iota(jnp.int32, (len(rows), rows[0].shape[1]), 0)
    m = jnp.zeros((len(rows), rows[0].shape[1]), F32)
    for r, row in enumerate(rows):
        m = jnp.where(idx == r, row, m)
    return m


def _route_chunk(s1, s2):
    k = PEER_TOPK
    a = _extract_top(s1, k)
    b = _extract_top(s2, k)
    am, bm = _stack_rows(a), _stack_rows(b)
    slabs = [a[0] + bm[0:8], a[0] + bm[8:16]]
    slabs += [a[r] + bm[0:8] for r in range(1, 8)]
    slabs += [am[8:16] + b[0]]
    cand = jnp.concatenate(slabs, axis=0)
    tau = _extract_top(cand, k)[-1]
    sel = cand >= tau
    top = a[0] + b[0]
    cnt = jnp.where(sel, 1.0, 0.0)
    low = jnp.where(sel, cand, jnp.inf)

    def row_rows(r):
        return slice(0, 16) if r == 0 else slice(8 + 8 * r, 16 + 8 * r) if r < 8 else slice(64 + r, 65 + r)

    n_r = [jnp.sum(cnt[row_rows(r)], axis=0, keepdims=True) for r in range(k)]
    low_r = [jnp.min(low[row_rows(r)], axis=0, keepdims=True) for r in range(k)]
    total = n_r[0]
    for r in range(1, k):
        total = total + n_r[r]
    excess = total - float(k)
    for r in reversed(range(k)):
        drop = jnp.where(excess > 0.0, jnp.where(low_r[r] == tau, 1.0, 0.0), 0.0)
        n_r[r] = n_r[r] - drop
        excess = excess - drop
    dropped = total - float(k) - excess
    z = jnp.sum(jnp.where(sel, jnp.exp(cand - top), 0.0), axis=0, keepdims=True) - dropped * jnp.exp(tau - top)

    c1 = jnp.full(s1.shape, -1.0, F32)
    r2 = jnp.full(s2.shape, float(NOT_SELECTED), F32)
    for r in range(k - 1):
        c1 = jnp.where(s1 == a[r], n_r[r] - 1.0, c1)
        r2 = jnp.where(s2 == b[r], float(r), r2)
    key = lax.broadcasted_iota(jnp.int32, s1.shape, 0)

    def last_key_limit(s, vals):
        match = s == vals[k - 1]
        first = jnp.min(jnp.where(match, key, PEER_KEYS), axis=0, keepdims=True)
        return match, jnp.where(vals[k - 2] != vals[k - 1], first, PEER_KEYS)

    match1, limit1 = last_key_limit(s1, a)
    c1 = jnp.where(match1, jnp.where(key <= limit1, n_r[k - 1] - 1.0, c1), c1)
    match2, limit2 = last_key_limit(s2, b)
    r2 = jnp.where(match2, jnp.where(key <= limit2, float(k - 1), r2), r2)
    e1 = jnp.exp(s1 - a[0]) / z
    e2 = jnp.exp(s2 - b[0])
    return c1, e1, r2, e2


def _route_kernel(ht_ref, wq_ref, sk_ref, c1_ref, e1_ref, r2_ref, e2_ref):
    qt = jnp.dot(wq_ref[...], ht_ref[...], preferred_element_type=F32)
    s1 = jnp.dot(sk_ref[0], qt[0:PEER_KEYS].astype(BF), preferred_element_type=F32)
    s2 = jnp.dot(sk_ref[1], qt[PEER_KEYS:].astype(BF), preferred_element_type=F32)

    def pack_row_pairs(x):
        return pltpu.bitcast(x.astype(BF), jnp.int32)

    def duplicate_halves(x):
        hi = pltpu.bitcast(x.astype(BF).astype(F32), jnp.int32)
        return hi | lax.shift_right_logical(hi, 16)

    for c in range(s1.shape[1] // ROUTE_LANES):
        cs = slice(c * ROUTE_LANES, (c + 1) * ROUTE_LANES)
        c1, e1, r2, e2 = _route_chunk(s1[:, cs], s2[:, cs])
        c1_ref[:, cs] = duplicate_halves(c1)
        e1_ref[:, cs] = duplicate_halves(e1)
        r2_ref[:, cs] = pack_row_pairs(r2)
        e2_ref[:, cs] = pack_row_pairs(e2)


def _route(h2t, w_peer_qt, subkeys_bf, l):
    tn = 512
    rows = PEER_HEADS * PEER_KEYS
    spec = pl.BlockSpec((PEER_KEYS, tn), lambda t, h: (h, t))
    pair_spec = pl.BlockSpec((PEER_KEYS // 2, tn), lambda t, h: (h, t))
    dup = jax.ShapeDtypeStruct((rows, N_TOK), jnp.int32)
    pairs = jax.ShapeDtypeStruct((rows // 2, N_TOK), jnp.int32)
    return pl.pallas_call(
        _route_kernel,
        grid=(N_TOK // tn, PEER_HEADS),
        in_specs=[
            pl.BlockSpec((D_MODEL, tn), lambda t, h: (0, t)),
            pl.BlockSpec((None, PEER_QDIM, D_MODEL), lambda t, h: (l, h, 0)),
            pl.BlockSpec((None, 2, PEER_KEYS, PEER_KEYS), lambda t, h: (l, 0, 0, 0)),
        ],
        out_specs=[spec, spec, pair_spec, pair_spec],
        out_shape=[dup, dup, pairs, pairs],
        compiler_params=_cparams(("arbitrary", "arbitrary")),
        name="peer_route",
    )(h2t, w_peer_qt, subkeys_bf)


PEER_LANES = 128
PEER_BLK = 256


def _peer_kernel(*refs, tm, te, convert):
    if convert:
        (ht_ref, u_ref, v_ref, c1_ref, e1_ref, r2_ref, e2_ref, o_ref, ub_ref, vtb_ref,
         at0_ref, at1_ref, wg0_ref, wg1_ref) = refs
    else:
        (ht_ref, u_ref, vt_ref, c1_ref, e1_ref, r2_ref, e2_ref, prev_hbm, o_ref,
         at0_ref, at1_ref, wg0_ref, wg1_ref) = refs
        del prev_hbm
    e = pl.program_id(1)
    nblk = te // PEER_BLK
    half = PEER_KEYS // 2

    @pl.when(e == 0)
    def _():
        o_ref[...] = jnp.zeros_like(o_ref)

    at_bufs = (at0_ref, at1_ref)
    wg_bufs = (wg0_ref, wg1_ref)

    def scores(q):
        rows = slice(q * PEER_BLK, (q + 1) * PEER_BLK)
        u = u_ref[rows, :]
        if convert:
            u = u.astype(BF)
            ub_ref[rows, :] = u
        at_bufs[q % 2][...] = jnp.dot(u, ht_ref[...], preferred_element_type=F32)

    def weights(q):
        for bb in range(PEER_BLK // PEER_KEYS):
            i = (e * nblk + q) * (PEER_BLK // PEER_KEYS) + bb
            bs = slice(bb * PEER_KEYS, (bb + 1) * PEER_KEYS)
            last_rows = [c1_ref[pl.ds(h * PEER_KEYS + i, 1), :] for h in range(PEER_HEADS)]
            e1_rows = [e1_ref[pl.ds(h * PEER_KEYS + i, 1), :] for h in range(PEER_HEADS)]
            for c in range(tm // PEER_LANES):
                cs = slice(c * PEER_LANES, (c + 1) * PEER_LANES)
                w = jnp.zeros((PEER_KEYS, PEER_LANES), BF)
                for h in range(PEER_HEADS):
                    hs = slice(h * half, (h + 1) * half)
                    last = pltpu.bitcast(jnp.broadcast_to(last_rows[h][:, cs], (half, PEER_LANES)), BF)
                    e1 = pltpu.bitcast(jnp.broadcast_to(e1_rows[h][:, cs], (half, PEER_LANES)), BF)
                    r2 = pltpu.bitcast(r2_ref[hs, cs], BF)
                    e2 = pltpu.bitcast(e2_ref[hs, cs], BF)
                    w = w + jnp.where(r2 <= last, e2, jnp.zeros_like(e2)) * e1
                a = at_bufs[q % 2][bs, cs]
                g = (0.5 * a * (1.0 + lax.erf(a * math.sqrt(0.5)))).astype(BF)
                wg_bufs[q % 2][bs, cs] = w * g

    def update(q):
        cols = slice(q * PEER_BLK, (q + 1) * PEER_BLK)
        if convert:
            vt = v_ref[cols, :].T.astype(BF)
            vtb_ref[:, cols] = vt
        else:
            vt = vt_ref[:, cols]
        o_ref[...] += jnp.dot(vt, wg_bufs[q % 2][...], preferred_element_type=F32)

    scores(0)
    for q in range(nblk):
        if q + 1 < nblk:
            scores(q + 1)
        weights(q)
        if q >= 1:
            update(q - 1)
    update(nblk - 1)


def _peer_scratch(tm):
    return [pltpu.VMEM((PEER_BLK, tm), F32), pltpu.VMEM((PEER_BLK, tm), F32),
            pltpu.VMEM((PEER_BLK, tm), BF), pltpu.VMEM((PEER_BLK, tm), BF)]


def _peer(h2t, peer_u, peer_v, routing, l):
    tm = 512
    c1, e1, r2, e2 = routing
    rows = PEER_HEADS * PEER_KEYS
    out_sds = jax.ShapeDtypeStruct((D_MODEL, N_TOK), F32)

    te = 512
    rspec = pl.BlockSpec((rows, tm), lambda t, e: (0, 0))
    pspec = pl.BlockSpec((rows // 2, tm), lambda t, e: (0, 0))
    first, u_bf, vt_bf = pl.pallas_call(
        functools.partial(_peer_kernel, tm=tm, te=te, convert=True),
        grid=(1, PEER_EXPERTS // te),
        in_specs=[
            pl.BlockSpec((D_MODEL, tm), lambda t, e: (0, 0)),
            pl.BlockSpec((None, te, D_MODEL), lambda t, e: (l, e, 0)),
            pl.BlockSpec((None, te, D_MODEL), lambda t, e: (l, e, 0)),
            rspec, rspec, pspec, pspec,
        ],
        out_specs=[
            pl.BlockSpec((D_MODEL, tm), lambda t, e: (0, 0)),
            pl.BlockSpec((te, D_MODEL), lambda t, e: (e, 0)),
            pl.BlockSpec((D_MODEL, te), lambda t, e: (0, e)),
        ],
        out_shape=[out_sds, jax.ShapeDtypeStruct((PEER_EXPERTS, D_MODEL), BF),
                   jax.ShapeDtypeStruct((D_MODEL, PEER_EXPERTS), BF)],
        scratch_shapes=_peer_scratch(tm),
        compiler_params=_cparams(("arbitrary", "arbitrary")),
        name="peer_dense_first",
    )(h2t, peer_u, peer_v, c1, e1, r2, e2)

    te = 1024
    rspec = pl.BlockSpec((rows, tm), lambda t, e: (0, t + 1))
    pspec = pl.BlockSpec((rows // 2, tm), lambda t, e: (0, t + 1))
    return pl.pallas_call(
        functools.partial(_peer_kernel, tm=tm, te=te, convert=False),
        grid=(N_TOK // tm - 1, PEER_EXPERTS // te),
        in_specs=[
            pl.BlockSpec((D_MODEL, tm), lambda t, e: (0, t + 1)),
            pl.BlockSpec((te, D_MODEL), lambda t, e: (e, 0)),
            pl.BlockSpec((D_MODEL, te), lambda t, e: (0, e)),
            rspec, rspec, pspec, pspec,
            pl.BlockSpec(memory_space=pl.ANY),
        ],
        out_specs=pl.BlockSpec((D_MODEL, tm), lambda t, e: (0, t + 1)),
        out_shape=out_sds,
        input_output_aliases={7: 0},
        scratch_shapes=_peer_scratch(tm),
        compiler_params=_cparams(("arbitrary", "arbitrary")),
        name="peer_dense",
    )(h2t, u_bf, vt_bf, c1, e1, r2, e2, first)


def _final_kernel(x_ref, pt_ref, ga_ref, g_ref, ctx_ref, dec_ref, *, tm, n_ctx_tiles):
    seg = _segment(pl.program_id(0) * tm)
    x = x_ref[...] + ga_ref[pl.ds(seg, 1), :] * pt_ref[...].T
    y = _rms(x, g_ref[...])
    is_ctx = pl.program_id(0) < n_ctx_tiles

    @pl.when(is_ctx)
    def _():
        ctx_ref[...] = y

    @pl.when(jnp.logical_not(is_ctx))
    def _():
        dec_ref[...] = y


def _final_norm(x, peer_t, mod, g_final):
    tm = 512
    n_ctx_tiles = N_CTX // tm
    return pl.pallas_call(
        functools.partial(_final_kernel, tm=tm, n_ctx_tiles=n_ctx_tiles),
        grid=(N_TOK // tm,),
        in_specs=[pl.BlockSpec((tm, D_MODEL), lambda i: (i, 0)),
                  pl.BlockSpec((D_MODEL, tm), lambda i: (0, i)),
                  pl.BlockSpec((None, 8, D_MODEL), lambda i: (DEPTH - 1, 0, 5)),
                  pl.BlockSpec((1, D_MODEL), lambda i: (0, 0))],
        out_specs=[
            pl.BlockSpec((tm, D_MODEL), lambda i: (jnp.minimum(i, n_ctx_tiles - 1), 0)),
            pl.BlockSpec((tm, D_MODEL), lambda i: (jnp.maximum(i - n_ctx_tiles, 0), 0)),
        ],
        out_shape=[jax.ShapeDtypeStruct((N_CTX, D_MODEL), F32), jax.ShapeDtypeStruct((N_DEC, D_MODEL), F32)],
        compiler_params=_cparams(("arbitrary",)),
        name="final_norm",
    )(x, peer_t, mod, g_final.reshape(1, D_MODEL))


def _rope_tables():
    t = jnp.arange(DEC_SEQ)
    half = MLA_ROPE // 4
    inv = ROPE_BASE ** (-jnp.arange(half, dtype=F32) / half)
    ang_r = (t // GRID_W).astype(F32)[:, None] * inv
    ang_c = (t % GRID_W).astype(F32)[:, None] * inv
    cos = jnp.concatenate([jnp.cos(ang_r)] * 2 + [jnp.cos(ang_c)] * 2, axis=1)
    sin = jnp.concatenate([-jnp.sin(ang_r), jnp.sin(ang_r), -jnp.sin(ang_c), jnp.sin(ang_c)], axis=1)
    return cos, sin


IN_WIDTH = 512 + 320 + POOL_WIDTH + 3 * DIFF_WIDTH + GATE_WIDTH
KV_END = 512 + 320
RELAYOUT_TN = 512
RELAYOUT_SRC_BLOCKS = RELAYOUT_TN // 128 + 1


def _relayout_kernel(*refs):
    src, o_ref = refs[:RELAYOUT_SRC_BLOCKS], refs[RELAYOUT_SRC_BLOCKS]
    j = pl.program_id(1)
    cat = jnp.concatenate([r[...] for r in src], axis=1)
    width = cat.shape[1]
    shift = (C_PZ - KV_END) % 128

    @pl.when(j == 0)
    def _():
        o_ref[...] = cat[:, :RELAYOUT_TN].astype(BF)

    @pl.when(j == 1)
    def _():
        part = cat[:, 256:width]
        lane = lax.broadcasted_iota(jnp.int32, part.shape, 1)
        part = jnp.where(lane < KV_END - RELAYOUT_TN, part, 0.0)
        o_ref[...] = jnp.concatenate([part, jnp.zeros((cat.shape[0], 128), F32)], axis=1).astype(BF)

    @pl.when(j >= 2)
    def _():
        o_ref[...] = pltpu.roll(cat, width - shift, 1)[:, :RELAYOUT_TN].astype(BF)


def _relayout_w_in(w_in):
    width = SMALL_WIDTH + GATE_WIDTH
    nsrc = pl.cdiv(IN_WIDTH, 128)

    def src_spec(k):
        return pl.BlockSpec(
            (None, D_MODEL, 128),
            lambda l, j: (l, 0, jnp.minimum(jnp.maximum(RELAYOUT_TN // 128 * j - 2, 0) + k, nsrc - 1)))

    return pl.pallas_call(
        _relayout_kernel,
        grid=(DEPTH, width // RELAYOUT_TN),
        in_specs=[src_spec(k) for k in range(RELAYOUT_SRC_BLOCKS)],
        out_specs=pl.BlockSpec((None, D_MODEL, RELAYOUT_TN), lambda l, j: (l, 0, j)),
        out_shape=jax.ShapeDtypeStruct((DEPTH, D_MODEL, width), BF),
        compiler_params=_cparams(("arbitrary", "arbitrary")),
        name="relayout_w_in",
    )(*([w_in] * RELAYOUT_SRC_BLOCKS))


def _relayout_w_qb(w_qb):
    w = w_qb.reshape(DEPTH, MLA_Q_RANK, MLA_HEADS, MLA_NOPE + MLA_ROPE)
    w = jnp.pad(w, ((0, 0), (0, 0), (0, 0), (0, MLA_HEAD_PAD - MLA_NOPE - MLA_ROPE)))
    return w.reshape(DEPTH, MLA_Q_RANK, MLA_HEADS * MLA_HEAD_PAD).astype(BF)


def kernel(x_prompt, x_sample, cache_mla_ckv, cache_mla_krope, cache_diff_k, cache_diff_v, c, c_ctx,
           w_mod, b_mod, g_norm1, w_in, g_qnorm, w_qb, g_kvnorm, w_kvb, w_pool, pool_scale,
           diff_lambda, g_diffnorm, w_br_mla, w_br_pool, w_br_diff, w_out, g_norm2,
           w_peer_q, peer_subkeys, peer_u, peer_v, g_final):
    x = jnp.concatenate([x_prompt.reshape(N_CTX, D_MODEL), x_sample.reshape(N_DEC, D_MODEL)], axis=0)
    cond8 = jnp.concatenate([c_ctx[None, :], c, jnp.zeros((8 - 1 - DEC_BATCH, D_MODEL), F32)], axis=0)

    w_in_re = _relayout_w_in(w_in)
    w_qb_re = _relayout_w_qb(w_qb)
    w_kvb_bf = w_kvb.astype(BF)
    w_pool_bf = w_pool.astype(BF)
    w_br_mla_bf, w_br_pool_bf, w_br_diff_bf = w_br_mla.astype(BF), w_br_pool.astype(BF), w_br_diff.astype(BF)
    w_out_bf = w_out.astype(BF)
    w_peer_qt = jnp.swapaxes(w_peer_q, 1, 2).astype(BF)
    subkeys_bf = peer_subkeys.astype(BF)

    cache_kr_pad = jnp.pad(cache_mla_krope, ((0, 0), (0, 0), (0, 0), (0, 128 - MLA_ROPE)))
    cache_k = cache_diff_k.reshape(DEC_BATCH, DEPTH, PAST_LEN, DIFF_WIDTH)
    cache_v = cache_diff_v.reshape(DEC_BATCH, DEPTH, PAST_LEN, DIFF_WIDTH)

    cos64, sin64 = _rope_tables()
    ones, zeros = jnp.ones_like(cos64), jnp.zeros_like(cos64)
    mla_tabs = (
        jnp.concatenate([ones, ones, cos64, ones], axis=1),
        jnp.concatenate([zeros, zeros, sin64, zeros], axis=1),
        jnp.concatenate([cos64, ones], axis=1),
        jnp.concatenate([sin64, zeros], axis=1),
    )
    diff_tabs = (jnp.tile(cos64, (1, DIFF_WIDTH // DIFF_QK)), jnp.tile(sin64, (1, DIFF_WIDTH // DIFF_QK)))

    mod = _modulation(cond8, w_mod, b_mod)

    st_ckv = jnp.zeros((BATCH, DEPTH, SEQ, MLA_KV_RANK), F32)
    st_kr = jnp.zeros((BATCH, DEPTH, SEQ, MLA_ROPE), F32)
    st_k = jnp.zeros((BATCH, DEPTH, SEQ, DIFF_WIDTH), F32)
    st_v = jnp.zeros((BATCH, DEPTH, SEQ, DIFF_WIDTH), F32)
    peer_t = None
    for l in range(DEPTH):
        if l == 0:
            p_small = _in_projection(x, g_norm1, mod, w_in_re, l, 0, SMALL_WIDTH, F32)
        else:
            p_small, x = _in_projection_res(x, peer_t, g_norm1, mod, w_in_re, l, SMALL_WIDTH)
        gl = _in_projection(x, g_norm1, mod, w_in_re, l, SMALL_WIDTH, GATE_WIDTH, BF)
        mla_c, st_ckv, st_kr = _mla_ctx(p_small, g_qnorm, g_kvnorm, w_qb_re, w_kvb_bf, st_ckv, st_kr, l)
        mla_o = _mla_dec(p_small, cache_mla_ckv, cache_kr_pad, mla_tabs, g_qnorm, g_kvnorm, w_qb_re, w_kvb_bf,
                         mla_c, l)
        diff_c, st_k, st_v = _diff_ctx(p_small, diff_lambda, g_diffnorm, st_k, st_v, l)
        diff_o = _diff_dec(p_small, cache_k, cache_v, diff_tabs, diff_lambda, g_diffnorm, diff_c, l)
        pool_o = _pool(p_small, w_pool_bf, pool_scale, l)
        merged = _merge(mla_o, pool_o, diff_o, gl, w_br_mla_bf, w_br_pool_bf, w_br_diff_bf, l)
        x, h2t = _out_projection(x, merged, w_out_bf, mod, g_norm2, l)
        routing = _route(h2t, w_peer_qt, subkeys_bf, l)
        peer_t = _peer(h2t, peer_u, peer_v, routing, l)

    y_ctx, y_dec = _final_norm(x, peer_t, mod, g_final)
    return (y_ctx.reshape(BATCH, SEQ, D_MODEL), y_dec.reshape(DEC_BATCH, DEC_SEQ, D_MODEL), st_ckv, st_kr,
            st_k.reshape(BATCH, DEPTH, SEQ, DIFF_HEADS, 2 * DIFF_QK),
            st_v.reshape(BATCH, DEPTH, SEQ, DIFF_HEADS, DIFF_V))
```

```python
import functools
import math

import jax
import jax.numpy as jnp
from jax import lax
from jax.experimental import pallas as pl
from jax.experimental.pallas import tpu as pltpu

BF = jnp.bfloat16
F32 = jnp.float32

D_MODEL = 2048
BATCH = 16
SEQ = 256
DEPTH = 4
DEC_BATCH = 2
DEC_SEQ = 1024
PAST_LEN = 256
GRID_W = 64
ROPE_BASE = 10000.0
EPS = 1e-6

MLA_HEADS = 8
MLA_NOPE = 128
MLA_ROPE = 64
MLA_V = 128
MLA_Q_RANK = 512
MLA_KV_RANK = 256
MLA_HEAD_PAD = 256

POOL_WINDOWS = (2, 4, 8, 16)
POOL_GROUP = 128
POOL_WIDTH = POOL_GROUP * len(POOL_WINDOWS)

DIFF_HEADS = 4
DIFF_QK = 64
DIFF_V = 2 * DIFF_QK
DIFF_WIDTH = DIFF_HEADS * DIFF_V

PEER_HEADS = 8
PEER_KEYS = 128
PEER_EXPERTS = PEER_KEYS * PEER_KEYS
PEER_TOPK = 16
PEER_QDIM = 256

N_CTX = BATCH * SEQ
N_DEC = DEC_BATCH * DEC_SEQ
N_TOK = N_CTX + N_DEC
KEYS_DEC = PAST_LEN + DEC_SEQ

C_QA = 0
C_CKV = 512
C_KR = 768
C_PZ = 1024
C_DQ = 1536
C_DK = 2048
C_DV = 2560
SMALL_WIDTH = 3072
GATE_WIDTH = 3 * D_MODEL

KV_END = 512 + 320
INPROJ_TN = 512

VMEM_LIMIT = 56 * 1024 * 1024


def _cparams(sem):
    return pltpu.CompilerParams(dimension_semantics=sem, vmem_limit_bytes=VMEM_LIMIT)


def _segment(row0):
    return jnp.where(row0 < N_CTX, 0, 1 + (row0 - N_CTX) // DEC_SEQ)


def _rms(x, g):
    return x * lax.rsqrt(jnp.mean(x * x, axis=-1, keepdims=True) + EPS) * g


def _softmax_rows(s):
    m = jnp.max(s, axis=-1, keepdims=True)
    p = jnp.exp(s - m)
    return p / jnp.sum(p, axis=-1, keepdims=True)


def _dot_nt(a, b):
    return lax.dot_general(a, b, (((1,), (1,)), ((), ())), preferred_element_type=F32)


def _rope_swap(x):
    n = x.shape[-1]
    up = pltpu.roll(x, n - 16, 1)
    dn = pltpu.roll(x, 16, 1)
    lane = lax.broadcasted_iota(jnp.int32, x.shape, 1)
    return jnp.where((lane & 16) == 0, up, dn)


def _mod_kernel(c_ref, w_ref, b_ref, o_ref):
    c = c_ref[...]
    a = (c * jax.nn.sigmoid(c)).astype(BF)
    o_ref[...] = jnp.dot(a, w_ref[...].astype(BF), preferred_element_type=F32) + b_ref[...]


def _modulation(cond8, w_mod, b_mod):
    tn = 1024
    return pl.pallas_call(
        _mod_kernel,
        grid=(DEPTH, 6 * D_MODEL // tn),
        in_specs=[
            pl.BlockSpec((8, D_MODEL), lambda l, j: (0, 0)),
            pl.BlockSpec((None, D_MODEL, tn), lambda l, j: (l, 0, j)),
            pl.BlockSpec((None, 1, tn), lambda l, j: (l, 0, j)),
        ],
        out_specs=pl.BlockSpec((None, 8, tn), lambda l, j: (l, 0, j)),
        out_shape=jax.ShapeDtypeStruct((DEPTH, 8, 6 * D_MODEL), F32),
        compiler_params=_cparams(("arbitrary", "arbitrary")),
        name="modulation",
    )(cond8, w_mod, b_mod.reshape(DEPTH, 1, 6 * D_MODEL))


def _w_in_row(tile):
    per_tile = INPROJ_TN // 64
    return 64 * jnp.where(tile < 2, per_tile * tile, per_tile * tile - (C_PZ - KV_END) // 64)


def _inproj_tile(h_ref, w_ref, tile):
    y = _dot_nt(h_ref[...], w_ref[0].astype(BF))
    lane = lax.broadcasted_iota(jnp.int32, y.shape, 1)
    return jnp.where(lane >= jnp.where(tile == 1, KV_END - INPROJ_TN, INPROJ_TN), 0.0, y)


def _inproj_kernel(x_ref, g_ref, sh_ref, sc_ref, w_ref, o_ref, h_ref, *, tm, tile0):
    i = pl.program_id(0)

    @pl.when(pl.program_id(1) == 0)
    def _():
        seg = _segment(i * tm)
        y = _rms(x_ref[...], g_ref[...])
        h_ref[...] = (y * (1.0 + sc_ref[pl.ds(seg, 1), :]) + sh_ref[pl.ds(seg, 1), :]).astype(BF)

    o_ref[...] = _inproj_tile(h_ref, w_ref, pl.program_id(1) + tile0).astype(o_ref.dtype)


def _w_in_spec(l, tile0):
    return pl.BlockSpec((pl.Element(1), pl.Element(INPROJ_TN), pl.Element(D_MODEL)),
                        lambda i, j: (l, _w_in_row(j + tile0), 0))


def _in_projection(x, g_norm1, mod, w_in_t, l, col0, width, out_dtype, tm=1024):
    tn = INPROJ_TN
    tile0 = col0 // tn
    return pl.pallas_call(
        functools.partial(_inproj_kernel, tm=tm, tile0=tile0),
        grid=(N_TOK // tm, width // tn),
        in_specs=[
            pl.BlockSpec((tm, D_MODEL), lambda i, j: (i, 0)),
            pl.BlockSpec((None, 1, D_MODEL), lambda i, j: (l, 0, 0)),
            pl.BlockSpec((None, 8, D_MODEL), lambda i, j: (l, 0, 0)),
            pl.BlockSpec((None, 8, D_MODEL), lambda i, j: (l, 0, 1)),
            _w_in_spec(l, tile0),
        ],
        out_specs=pl.BlockSpec((tm, tn), lambda i, j: (i, j)),
        out_shape=jax.ShapeDtypeStruct((N_TOK, width), out_dtype),
        scratch_shapes=[pltpu.VMEM((tm, D_MODEL), BF)],
        compiler_params=_cparams(("arbitrary", "arbitrary")),
        name="in_projection",
    )(x, g_norm1.reshape(DEPTH, 1, D_MODEL), mod, mod, w_in_t)


def _inproj_res_kernel(x_ref, pt_ref, ga_ref, g_ref, sh_ref, sc_ref, w_ref, o_ref, xo_ref, h_ref, *, tm):
    i = pl.program_id(0)

    @pl.when(pl.program_id(1) == 0)
    def _():
        seg = _segment(i * tm)
        x = x_ref[...] + ga_ref[pl.ds(seg, 1), :] * pt_ref[...].T
        xo_ref[...] = x
        y = _rms(x, g_ref[...])
        h_ref[...] = (y * (1.0 + sc_ref[pl.ds(seg, 1), :]) + sh_ref[pl.ds(seg, 1), :]).astype(BF)

    o_ref[...] = _inproj_tile(h_ref, w_ref, pl.program_id(1)).astype(o_ref.dtype)


def _in_projection_res(x, peer_t, g_norm1, mod, w_in_t, l, width):
    tm, tn = 1024, INPROJ_TN
    once = pl.Buffered(1)
    return pl.pallas_call(
        functools.partial(_inproj_res_kernel, tm=tm),
        grid=(N_TOK // tm, width // tn),
        in_specs=[
            pl.BlockSpec((tm, D_MODEL), lambda i, j: (i, 0), pipeline_mode=once),
            pl.BlockSpec((D_MODEL, tm), lambda i, j: (0, i), pipeline_mode=once),
            pl.BlockSpec((None, 8, D_MODEL), lambda i, j: (l - 1, 0, 5)),
            pl.BlockSpec((None, 1, D_MODEL), lambda i, j: (l, 0, 0)),
            pl.BlockSpec((None, 8, D_MODEL), lambda i, j: (l, 0, 0)),
            pl.BlockSpec((None, 8, D_MODEL), lambda i, j: (l, 0, 1)),
            _w_in_spec(l, 0),
        ],
        out_specs=[
            pl.BlockSpec((tm, tn), lambda i, j: (i, j)),
            pl.BlockSpec((tm, D_MODEL), lambda i, j: (i, 0)),
        ],
        out_shape=[jax.ShapeDtypeStruct((N_TOK, width), F32), jax.ShapeDtypeStruct((N_TOK, D_MODEL), F32)],
        scratch_shapes=[pltpu.VMEM((tm, D_MODEL), BF)],
        compiler_params=_cparams(("arbitrary", "arbitrary")),
        name="in_projection_res",
    )(x, peer_t, mod, g_norm1.reshape(DEPTH, 1, D_MODEL), mod, mod, w_in_t)


def _mla_heads(q, kfun, vfun, o_ref, scale):
    for h in range(MLA_HEADS):
        s = _dot_nt(q(h), kfun(h)) * scale
        p = _softmax_rows(s).astype(BF)
        o = jnp.dot(p, vfun(h), preferred_element_type=F32)
        o_ref[:, h * MLA_V:(h + 1) * MLA_V] = o.astype(o_ref.dtype)


def _mla_ctx_kernel(qa_ref, ckv_ref, kr_ref, gq_ref, gkv_ref, wqb_ref, wkvb_ref, st_ckv_hbm, st_kr_hbm,
                    o_ref, ckv_out_ref, kr_out_ref):
    del st_ckv_hbm, st_kr_hbm
    scale = (MLA_NOPE + MLA_ROPE) ** -0.5
    qn = _rms(qa_ref[...], gq_ref[...]).astype(BF)
    q = jnp.dot(qn, wqb_ref[...], preferred_element_type=F32).astype(BF)
    ckv = _rms(ckv_ref[...], gkv_ref[...])
    ckv_out_ref[...] = ckv
    kr = kr_ref[...]
    kr_out_ref[...] = kr[:, :MLA_ROPE]
    kv = jnp.dot(ckv.astype(BF), wkvb_ref[...], preferred_element_type=F32).astype(BF)
    krp = kr[:, :128].astype(BF)

    def qh(h):
        return q[:, h * MLA_HEAD_PAD:(h + 1) * MLA_HEAD_PAD]

    def kh(h):
        return jnp.concatenate([kv[:, h * 256:h * 256 + MLA_NOPE], krp], axis=1)

    def vh(h):
        return kv[:, h * 256 + MLA_NOPE:(h + 1) * 256]

    _mla_heads(qh, kh, vh, o_ref, scale)


def _mla_ctx(p_small, g_qnorm, g_kvnorm, w_qb_re, w_kvb, st_ckv, st_kr, l):
    return pl.pallas_call(
        _mla_ctx_kernel,
        grid=(BATCH,),
        in_specs=[
            pl.BlockSpec((SEQ, MLA_Q_RANK), lambda b: (b, C_QA // MLA_Q_RANK)),
            pl.BlockSpec((SEQ, 256), lambda b: (b, C_CKV // 256)),
            pl.BlockSpec((SEQ, 256), lambda b: (b, C_KR // 256)),
            pl.BlockSpec((None, 1, MLA_Q_RANK), lambda b: (l, 0, 0)),
            pl.BlockSpec((None, 1, MLA_KV_RANK), lambda b: (l, 0, 0)),
            pl.BlockSpec((None, MLA_Q_RANK, MLA_HEADS * MLA_HEAD_PAD), lambda b: (l, 0, 0)),
            pl.BlockSpec((None, MLA_KV_RANK, MLA_HEADS * 256), lambda b: (l, 0, 0)),
            pl.BlockSpec(memory_space=pl.ANY),
            pl.BlockSpec(memory_space=pl.ANY),
        ],
        out_specs=[
            pl.BlockSpec((SEQ, MLA_HEADS * MLA_V), lambda b: (b, 0)),
            pl.BlockSpec((None, None, SEQ, MLA_KV_RANK), lambda b: (b, l, 0, 0)),
            pl.BlockSpec((None, None, SEQ, MLA_ROPE), lambda b: (b, l, 0, 0)),
        ],
        out_shape=[
            jax.ShapeDtypeStruct((N_TOK, MLA_HEADS * MLA_V), BF),
            jax.ShapeDtypeStruct(st_ckv.shape, F32),
            jax.ShapeDtypeStruct(st_kr.shape, F32),
        ],
        input_output_aliases={7: 1, 8: 2},
        compiler_params=_cparams(("arbitrary",)),
        name="mla_ctx",
    )(p_small, p_small, p_small, g_qnorm.reshape(DEPTH, 1, -1), g_kvnorm.reshape(DEPTH, 1, -1),
      w_qb_re, w_kvb, st_ckv, st_kr)


def _mla_dec_kernel(qa_ref, ckv_ref, kr_ref, cckv_ref, ckr_ref, cq_ref, sq_ref, ck_ref, sk_ref,
                    gq_ref, gkv_ref, wqb_ref, wkvb_ref, mix_hbm, o_ref, kf_ref, vf_ref):
    del mix_hbm
    scale = (MLA_NOPE + MLA_ROPE) ** -0.5

    @pl.when(pl.program_id(1) == 0)
    def _():
        ckv = _rms(ckv_ref[...], gkv_ref[...])
        ckv_all = jnp.concatenate([cckv_ref[...], ckv], axis=0).astype(BF)
        kr = kr_ref[:, :128]
        kr_rot = kr * ck_ref[...] + _rope_swap(kr) * sk_ref[...]
        kr_all = jnp.concatenate([ckr_ref[...], kr_rot], axis=0).astype(BF)
        for h in range(MLA_HEADS):
            kvh = jnp.dot(ckv_all, wkvb_ref[:, h * 256:(h + 1) * 256], preferred_element_type=F32)
            kf_ref[h, :, 0:MLA_NOPE] = kvh[:, :MLA_NOPE].astype(BF)
            kf_ref[h, :, MLA_NOPE:MLA_HEAD_PAD] = kr_all
            vf_ref[h] = kvh[:, MLA_NOPE:].astype(BF)

    qn = _rms(qa_ref[...], gq_ref[...]).astype(BF)
    q = jnp.dot(qn, wqb_ref[...], preferred_element_type=F32)
    cq = cq_ref[...]
    sq = sq_ref[...]

    def qh(h):
        x = q[:, h * MLA_HEAD_PAD:(h + 1) * MLA_HEAD_PAD]
        return (x * cq + _rope_swap(x) * sq).astype(BF)

    _mla_heads(qh, lambda h: kf_ref[h], lambda h: vf_ref[h], o_ref, scale)


def _mla_dec(p_small, cache_ckv, cache_kr_pad, tabs, g_qnorm, g_kvnorm, w_qb_re, w_kvb, mix, l):
    tq = 256
    nq = DEC_SEQ // tq
    row_blk = N_CTX // DEC_SEQ
    cq, sq, ck, sk = tabs
    return pl.pallas_call(
        _mla_dec_kernel,
        grid=(DEC_BATCH, nq),
        in_specs=[
            pl.BlockSpec((tq, MLA_Q_RANK), lambda b, i: (N_CTX // tq + b * nq + i, 0)),
            pl.BlockSpec((DEC_SEQ, 256), lambda b, i: (row_blk + b, C_CKV // 256)),
            pl.BlockSpec((DEC_SEQ, 256), lambda b, i: (row_blk + b, C_KR // 256)),
            pl.BlockSpec((None, None, PAST_LEN, MLA_KV_RANK), lambda b, i: (b, l, 0, 0)),
            pl.BlockSpec((None, None, PAST_LEN, 128), lambda b, i: (b, l, 0, 0)),
            pl.BlockSpec((tq, MLA_HEAD_PAD), lambda b, i: (i, 0)),
            pl.BlockSpec((tq, MLA_HEAD_PAD), lambda b, i: (i, 0)),
            pl.BlockSpec((DEC_SEQ, 128), lambda b, i: (0, 0)),
            pl.BlockSpec((DEC_SEQ, 128), lambda b, i: (0, 0)),
            pl.BlockSpec((None, 1, MLA_Q_RANK), lambda b, i: (l, 0, 0)),
            pl.BlockSpec((None, 1, MLA_KV_RANK), lambda b, i: (l, 0, 0)),
            pl.BlockSpec((None, MLA_Q_RANK, MLA_HEADS * MLA_HEAD_PAD), lambda b, i: (l, 0, 0)),
            pl.BlockSpec((None, MLA_KV_RANK, MLA_HEADS * 256), lambda b, i: (l, 0, 0)),
            pl.BlockSpec(memory_space=pl.ANY),
        ],
        out_specs=pl.BlockSpec((tq, MLA_HEADS * MLA_V), lambda b, i: (N_CTX // tq + b * nq + i, 0)),
        out_shape=jax.ShapeDtypeStruct((N_TOK, MLA_HEADS * MLA_V), BF),
        input_output_aliases={13: 0},
        scratch_shapes=[
            pltpu.VMEM((MLA_HEADS, KEYS_DEC, MLA_HEAD_PAD), BF),
            pltpu.VMEM((MLA_HEADS, KEYS_DEC, MLA_V), BF),
        ],
        compiler_params=_cparams(("arbitrary", "arbitrary")),
        name="mla_dec",
    )(p_small, p_small, p_small, cache_ckv, cache_kr_pad, cq, sq, ck, sk,
      g_qnorm.reshape(DEPTH, 1, -1), g_kvnorm.reshape(DEPTH, 1, -1), w_qb_re, w_kvb, mix)


def _diff_lambda(lv):
    t1 = jnp.sum(lv[0:1] * lv[1:2], axis=-1, keepdims=True)
    t2 = jnp.sum(lv[2:3] * lv[3:4], axis=-1, keepdims=True)
    return jnp.exp(t1) - jnp.exp(t2)


def _diff_heads(q, kfun, vfun, lam, g, o_ref, lam_init):
    scale = DIFF_QK ** -0.5
    lane = lax.broadcasted_iota(jnp.int32, (q.shape[0], DIFF_V), 1)
    for h in range(DIFF_HEADS):
        qh = q[:, h * DIFF_V:(h + 1) * DIFF_V]
        q1 = jnp.where(lane < DIFF_QK, qh, 0.0).astype(BF)
        q2 = jnp.where(lane >= DIFF_QK, qh, 0.0).astype(BF)
        k = kfun(h)
        v = vfun(h)
        p1 = _softmax_rows(_dot_nt(q1, k) * scale).astype(BF)
        p2 = _softmax_rows(_dot_nt(q2, k) * scale).astype(BF)
        a1 = jnp.dot(p1, v, preferred_element_type=F32)
        a2 = jnp.dot(p2, v, preferred_element_type=F32)
        o = _rms(a1 - lam * a2, g) * (1.0 - lam_init)
        o_ref[:, h * DIFF_V:(h + 1) * DIFF_V] = o.astype(o_ref.dtype)


def _diff_ctx_kernel(q_ref, k_ref, v_ref, lam_ref, g_ref, st_k_hbm, st_v_hbm,
                     o_ref, k_out_ref, v_out_ref, *, lam_init):
    del st_k_hbm, st_v_hbm
    lam = _diff_lambda(lam_ref[...]) + lam_init
    k_out_ref[...] = k_ref[...]
    v_out_ref[...] = v_ref[...]
    k = k_ref[...].astype(BF)
    v = v_ref[...].astype(BF)
    _diff_heads(q_ref[...],
                lambda h: k[:, h * DIFF_V:(h + 1) * DIFF_V],
                lambda h: v[:, h * DIFF_V:(h + 1) * DIFF_V],
                lam, g_ref[...], o_ref, lam_init)


def _lam_init(l):
    return 0.8 - 0.6 * math.exp(-0.3 * l)


def _diff_ctx(p_small, diff_lambda, g_diffnorm, st_k, st_v, l):
    return pl.pallas_call(
        functools.partial(_diff_ctx_kernel, lam_init=_lam_init(l)),
        grid=(BATCH,),
        in_specs=[
            pl.BlockSpec((SEQ, DIFF_WIDTH), lambda b: (b, C_DQ // DIFF_WIDTH)),
            pl.BlockSpec((SEQ, DIFF_WIDTH), lambda b: (b, C_DK // DIFF_WIDTH)),
            pl.BlockSpec((SEQ, DIFF_WIDTH), lambda b: (b, C_DV // DIFF_WIDTH)),
            pl.BlockSpec((None, 4, DIFF_QK), lambda b: (l, 0, 0)),
            pl.BlockSpec((None, 1, DIFF_V), lambda b: (l, 0, 0)),
            pl.BlockSpec(memory_space=pl.ANY),
            pl.BlockSpec(memory_space=pl.ANY),
        ],
        out_specs=[
            pl.BlockSpec((SEQ, DIFF_WIDTH), lambda b: (b, 0)),
            pl.BlockSpec((None, None, SEQ, DIFF_WIDTH), lambda b: (b, l, 0, 0)),
            pl.BlockSpec((None, None, SEQ, DIFF_WIDTH), lambda b: (b, l, 0, 0)),
        ],
        out_shape=[
            jax.ShapeDtypeStruct((N_TOK, DIFF_WIDTH), BF),
            jax.ShapeDtypeStruct(st_k.shape, F32),
            jax.ShapeDtypeStruct(st_v.shape, F32),
        ],
        input_output_aliases={5: 1, 6: 2},
        compiler_params=_cparams(("arbitrary",)),
        name="diff_ctx",
    )(p_small, p_small, p_small, diff_lambda, g_diffnorm.reshape(DEPTH, 1, -1), st_k, st_v)


def _diff_dec_kernel(q_ref, k_ref, v_ref, ck_ref, cv_ref, cq_ref, sq_ref, cfull_ref, sfull_ref,
                     lam_ref, g_ref, mix_hbm, o_ref, kf_ref, vf_ref, *, lam_init):
    del mix_hbm

    @pl.when(pl.program_id(1) == 0)
    def _():
        k = k_ref[...]
        k_rot = k * cfull_ref[...] + _rope_swap(k) * sfull_ref[...]
        kf_ref[0:PAST_LEN, :] = ck_ref[...].astype(BF)
        kf_ref[PAST_LEN:KEYS_DEC, :] = k_rot.astype(BF)
        vf_ref[0:PAST_LEN, :] = cv_ref[...].astype(BF)
        vf_ref[PAST_LEN:KEYS_DEC, :] = v_ref[...].astype(BF)

    lam = _diff_lambda(lam_ref[...]) + lam_init
    q = q_ref[...]
    q = q * cq_ref[...] + _rope_swap(q) * sq_ref[...]
    _diff_heads(q,
                lambda h: kf_ref[:, h * DIFF_V:(h + 1) * DIFF_V],
                lambda h: vf_ref[:, h * DIFF_V:(h + 1) * DIFF_V],
                lam, g_ref[...], o_ref, lam_init)


def _diff_dec(p_small, cache_k, cache_v, tabs, diff_lambda, g_diffnorm, mix, l):
    tq = 256
    nq = DEC_SEQ // tq
    row_blk = N_CTX // DEC_SEQ
    c512, s512 = tabs
    return pl.pallas_call(
        functools.partial(_diff_dec_kernel, lam_init=_lam_init(l)),
        grid=(DEC_BATCH, nq),
        in_specs=[
            pl.BlockSpec((tq, DIFF_WIDTH), lambda b, i: (N_CTX // tq + b * nq + i, C_DQ // DIFF_WIDTH)),
            pl.BlockSpec((DEC_SEQ, DIFF_WIDTH), lambda b, i: (row_blk + b, C_DK // DIFF_WIDTH)),
            pl.BlockSpec((DEC_SEQ, DIFF_WIDTH), lambda b, i: (row_blk + b, C_DV // DIFF_WIDTH)),
            pl.BlockSpec((None, None, PAST_LEN, DIFF_WIDTH), lambda b, i: (b, l, 0, 0)),
            pl.BlockSpec((None, None, PAST_LEN, DIFF_WIDTH), lambda b, i: (b, l, 0, 0)),
            pl.BlockSpec((tq, DIFF_WIDTH), lambda b, i: (i, 0)),
            pl.BlockSpec((tq, DIFF_WIDTH), lambda b, i: (i, 0)),
            pl.BlockSpec((DEC_SEQ, DIFF_WIDTH), lambda b, i: (0, 0)),
            pl.BlockSpec((DEC_SEQ, DIFF_WIDTH), lambda b, i: (0, 0)),
            pl.BlockSpec((None, 4, DIFF_QK), lambda b, i: (l, 0, 0)),
            pl.BlockSpec((None, 1, DIFF_V), lambda b, i: (l, 0, 0)),
            pl.BlockSpec(memory_space=pl.ANY),
        ],
        out_specs=pl.BlockSpec((tq, DIFF_WIDTH), lambda b, i: (N_CTX // tq + b * nq + i, 0)),
        out_shape=jax.ShapeDtypeStruct((N_TOK, DIFF_WIDTH), BF),
        input_output_aliases={11: 0},
        scratch_shapes=[
            pltpu.VMEM((KEYS_DEC, DIFF_WIDTH), BF),
            pltpu.VMEM((KEYS_DEC, DIFF_WIDTH), BF),
        ],
        compiler_params=_cparams(("arbitrary", "arbitrary")),
        name="diff_dec",
    )(p_small, p_small, p_small, cache_k, cache_v, c512, s512, c512, s512,
      diff_lambda, g_diffnorm.reshape(DEPTH, 1, -1), mix)


def _pool_kernel(z_ref, w_ref, ps_ref, o_ref, *, tm):
    seq_m1 = jnp.where(pl.program_id(0) * tm < N_CTX, SEQ - 1, DEC_SEQ - 1)
    t = lax.broadcasted_iota(jnp.int32, (tm, POOL_GROUP), 0) & seq_m1
    for gi, w in enumerate(POOL_WINDOWS):
        z = z_ref[:, gi * POOL_GROUP:(gi + 1) * POOL_GROUP]
        acc = jnp.zeros_like(z)
        for k in range(-(w // 2), w // 2):
            zs = z if k == 0 else pltpu.roll(z, (-k) % tm, 0)
            ok = (t + k >= 0) & (t + k <= seq_m1)
            acc = acc + jnp.where(ok, zs, 0.0)
        lo = jnp.maximum(t - w // 2, 0)
        hi = jnp.minimum(t + (w - 1) // 2, seq_m1)
        d = (acc / (hi - lo + 1).astype(F32) - z).astype(BF)
        y = jnp.dot(d, w_ref[gi], preferred_element_type=F32)
        y = y * ps_ref[:, gi * POOL_GROUP:(gi + 1) * POOL_GROUP]
        o_ref[:, gi * POOL_GROUP:(gi + 1) * POOL_GROUP] = y.astype(o_ref.dtype)


def _pool(p_small, w_pool_bf, pool_scale, l):
    tm = DEC_SEQ
    return pl.pallas_call(
        functools.partial(_pool_kernel, tm=tm),
        grid=(N_TOK // tm,),
        in_specs=[
            pl.BlockSpec((tm, POOL_WIDTH), lambda i: (i, C_PZ // POOL_WIDTH)),
            pl.BlockSpec((None, len(POOL_WINDOWS), POOL_GROUP, POOL_GROUP), lambda i: (l, 0, 0, 0)),
            pl.BlockSpec((None, 1, POOL_WIDTH), lambda i: (l, 0, 0)),
        ],
        out_specs=pl.BlockSpec((tm, POOL_WIDTH), lambda i: (i, 0)),
        out_shape=jax.ShapeDtypeStruct((N_TOK, POOL_WIDTH), BF),
        compiler_params=_cparams(("arbitrary",)),
        name="pool",
    )(p_small, w_pool_bf, pool_scale.reshape(DEPTH, 1, POOL_WIDTH))


def _merge_kernel(a_ref, p_ref, d_ref, g0_ref, g1_ref, g2_ref, wa_ref, wp_ref, wd_ref, o_ref):
    def sig(r):
        return jax.nn.sigmoid(r[...].astype(F32))

    m = sig(g0_ref) * jnp.dot(a_ref[...], wa_ref[...], preferred_element_type=F32)
    m = m + sig(g1_ref) * jnp.dot(p_ref[...], wp_ref[...], preferred_element_type=F32)
    m = m + sig(g2_ref) * jnp.dot(d_ref[...], wd_ref[...], preferred_element_type=F32)
    o_ref[...] = m.astype(o_ref.dtype)


def _merge(mla_o, pool_o, diff_o, gl, w_br_mla, w_br_pool, w_br_diff, l):
    tm, tn = 1024, 512
    nj = D_MODEL // tn
    return pl.pallas_call(
        _merge_kernel,
        grid=(N_TOK // tm, nj),
        in_specs=[
            pl.BlockSpec((tm, MLA_HEADS * MLA_V), lambda i, j: (i, 0)),
            pl.BlockSpec((tm, POOL_WIDTH), lambda i, j: (i, 0)),
            pl.BlockSpec((tm, DIFF_WIDTH), lambda i, j: (i, 0)),
            pl.BlockSpec((tm, tn), lambda i, j: (i, j)),
            pl.BlockSpec((tm, tn), lambda i, j: (i, nj + j)),
            pl.BlockSpec((tm, tn), lambda i, j: (i, 2 * nj + j)),
            pl.BlockSpec((None, MLA_HEADS * MLA_V, tn), lambda i, j: (l, 0, j)),
            pl.BlockSpec((None, POOL_WIDTH, tn), lambda i, j: (l, 0, j)),
            pl.BlockSpec((None, DIFF_WIDTH, tn), lambda i, j: (l, 0, j)),
        ],
        out_specs=pl.BlockSpec((tm, tn), lambda i, j: (i, j)),
        out_shape=jax.ShapeDtypeStruct((N_TOK, D_MODEL), BF),
        compiler_params=_cparams(("arbitrary", "arbitrary")),
        name="merge",
    )(mla_o, pool_o, diff_o, gl, gl, gl, w_br_mla, w_br_pool, w_br_diff)


def _outproj_kernel(x_ref, m_ref, w_ref, ga_ref, g2_ref, sh_ref, sc_ref, xo_ref, ht_ref, *, tm):
    seg = _segment(pl.program_id(0) * tm)
    y = jnp.dot(m_ref[...], w_ref[...], preferred_element_type=F32)
    xn = x_ref[...] + ga_ref[pl.ds(seg, 1), :] * y
    xo_ref[...] = xn
    h2 = _rms(xn, g2_ref[...]) * (1.0 + sc_ref[pl.ds(seg, 1), :]) + sh_ref[pl.ds(seg, 1), :]
    ht_ref[...] = h2.T.astype(BF)


def _out_projection(x, merged, w_out, mod, g_norm2, l):
    tm = 256
    return pl.pallas_call(
        functools.partial(_outproj_kernel, tm=tm),
        grid=(N_TOK // tm,),
        in_specs=[
            pl.BlockSpec((tm, D_MODEL), lambda i: (i, 0)),
            pl.BlockSpec((tm, D_MODEL), lambda i: (i, 0)),
            pl.BlockSpec((None, D_MODEL, D_MODEL), lambda i: (l, 0, 0)),
            pl.BlockSpec((None, 8, D_MODEL), lambda i: (l, 0, 2)),
            pl.BlockSpec((None, 1, D_MODEL), lambda i: (l, 0, 0)),
            pl.BlockSpec((None, 8, D_MODEL), lambda i: (l, 0, 3)),
            pl.BlockSpec((None, 8, D_MODEL), lambda i: (l, 0, 4)),
        ],
        out_specs=[
            pl.BlockSpec((tm, D_MODEL), lambda i: (i, 0)),
            pl.BlockSpec((D_MODEL, tm), lambda i: (0, i)),
        ],
        out_shape=[
            jax.ShapeDtypeStruct((N_TOK, D_MODEL), F32),
            jax.ShapeDtypeStruct((D_MODEL, N_TOK), BF),
        ],
        compiler_params=_cparams(("arbitrary",)),
        name="out_projection",
    )(x, merged, w_out, mod, g_norm2.reshape(DEPTH, 1, D_MODEL), mod, mod)


ROUTE_LANES = 128
NOT_SELECTED = 127


SLAB = 8
NET_WIDTH = 16


def _sorting_network(n):
    def merge(lo, hi, r):
        step = r * 2
        if step < hi - lo:
            yield from merge(lo, hi, step)
            yield from merge(lo + r, hi, step)
            yield from ((i, i + r) for i in range(lo + r, hi - r, step))
        else:
            yield (lo, lo + r)

    def sort(lo, hi):
        if hi - lo >= 1:
            mid = lo + (hi - lo) // 2
            yield from sort(lo, mid)
            yield from sort(mid + 1, hi)
            yield from merge(lo, hi, 1)

    return tuple(sort(0, n - 1))


def _extract_top(x, n):
    slabs = [x[SLAB * v:SLAB * (v + 1)] for v in range(x.shape[0] // SLAB)]
    slabs += [None] * (NET_WIDTH - len(slabs))
    for i, j in _sorting_network(NET_WIDTH):
        hi, lo = slabs[i], slabs[j]
        if lo is None:
            continue
        if hi is None:
            slabs[i], slabs[j] = lo, None
        else:
            slabs[i], slabs[j] = jnp.maximum(hi, lo), jnp.minimum(hi, lo)
    stack = [s for s in slabs if s is not None]
    sub = lax.broadcasted_iota(jnp.int32, stack[0].shape, 0)
    vals = []
    for r in range(n):
        m = jnp.max(stack[0], axis=0, keepdims=True)
        vals.append(m)
        hit = stack[0] == m
        popped = sub == jnp.min(jnp.where(hit, sub, SLAB), axis=0, keepdims=True)
        for d in range(min(n - r - 1, len(stack))):
            below = stack[d + 1] if d + 1 < len(stack) else -jnp.inf
            stack[d] = jnp.where(popped, below, stack[d])
    return vals


def _stack_rows(rows):
    idx = lax.broadcasted_iota(jnp.int32, (len(rows), rows[0].shape[1]), 0)
    m = jnp.zeros((len(rows), rows[0].shape[1]), F32)
    for r, row in enumerate(rows):
        m = jnp.where(idx == r, row, m)
    return m


def _route_chunk(s1, s2):
    k = PEER_TOPK
    a = _extract_top(s1, k)
    b = _extract_top(s2, k)
    am, bm = _stack_rows(a), _stack_rows(b)
    slabs = [a[0] + bm[0:8], a[0] + bm[8:16]]
    slabs += [a[r] + bm[0:8] for r in range(1, 8)]
    slabs += [am[8:16] + b[0]]
    cand = jnp.concatenate(slabs, axis=0)
    tau = _extract_top(cand, k)[-1]
    sel = cand >= tau
    top = a[0] + b[0]
    cnt = jnp.where(sel, 1.0, 0.0)
    low = jnp.where(sel, cand, jnp.inf)

    def row_rows(r):
        return slice(0, 16) if r == 0 else slice(8 + 8 * r, 16 + 8 * r) if r < 8 else slice(64 + r, 65 + r)

    n_r = [jnp.sum(cnt[row_rows(r)], axis=0, keepdims=True) for r in range(k)]
    low_r = [jnp.min(low[row_rows(r)], axis=0, keepdims=True) for r in range(k)]
    total = n_r[0]
    for r in range(1, k):
        total = total + n_r[r]
    excess = total - float(k)
    for r in reversed(range(k)):
        drop = jnp.where(excess > 0.0, jnp.where(low_r[r] == tau, 1.0, 0.0), 0.0)
        n_r[r] = n_r[r] - drop
        excess = excess - drop
    dropped = total - float(k) - excess
    z = jnp.sum(jnp.where(sel, jnp.exp(cand - top), 0.0), axis=0, keepdims=True) - dropped * jnp.exp(tau - top)

    c1 = jnp.full(s1.shape, -1.0, F32)
    r2 = jnp.full(s2.shape, float(NOT_SELECTED), F32)
    for r in range(k - 1):
        c1 = jnp.where(s1 == a[r], n_r[r] - 1.0, c1)
        r2 = jnp.where(s2 == b[r], float(r), r2)
    key = lax.broadcasted_iota(jnp.int32, s1.shape, 0)

    def last_key_limit(s, vals):
        match = s == vals[k - 1]
        first = jnp.min(jnp.where(match, key, PEER_KEYS), axis=0, keepdims=True)
        return match, jnp.where(vals[k - 2] != vals[k - 1], first, PEER_KEYS)

    match1, limit1 = last_key_limit(s1, a)
    c1 = jnp.where(match1, jnp.where(key <= limit1, n_r[k - 1] - 1.0, c1), c1)
    match2, limit2 = last_key_limit(s2, b)
    r2 = jnp.where(match2, jnp.where(key <= limit2, float(k - 1), r2), r2)
    e1 = jnp.exp(s1 - a[0]) / z
    e2 = jnp.exp(s2 - b[0])
    return c1, e1, r2, e2


def _route_kernel(ht_ref, wq_ref, sk_ref, c1_ref, e1_ref, r2_ref, e2_ref):
    qt = jnp.dot(wq_ref[...], ht_ref[...], preferred_element_type=F32)
    s1 = jnp.dot(sk_ref[0], qt[0:PEER_KEYS].astype(BF), preferred_element_type=F32)
    s2 = jnp.dot(sk_ref[1], qt[PEER_KEYS:].astype(BF), preferred_element_type=F32)

    def pack_row_pairs(x):
        return pltpu.bitcast(x.astype(BF), jnp.int32)

    def duplicate_halves(x):
        hi = pltpu.bitcast(x.astype(BF).astype(F32), jnp.int32)
        return hi | lax.shift_right_logical(hi, 16)

    for c in range(s1.shape[1] // ROUTE_LANES):
        cs = slice(c * ROUTE_LANES, (c + 1) * ROUTE_LANES)
        c1, e1, r2, e2 = _route_chunk(s1[:, cs], s2[:, cs])
        c1_ref[:, cs] = duplicate_halves(c1)
        e1_ref[:, cs] = duplicate_halves(e1)
        r2_ref[:, cs] = pack_row_pairs(r2)
        e2_ref[:, cs] = pack_row_pairs(e2)


def _route(h2t, w_peer_qt, subkeys_bf, l):
    tn = 512
    rows = PEER_HEADS * PEER_KEYS
    spec = pl.BlockSpec((PEER_KEYS, tn), lambda t, h: (h, t))
    pair_spec = pl.BlockSpec((PEER_KEYS // 2, tn), lambda t, h: (h, t))
    dup = jax.ShapeDtypeStruct((rows, N_TOK), jnp.int32)
    pairs = jax.ShapeDtypeStruct((rows // 2, N_TOK), jnp.int32)
    return pl.pallas_call(
        _route_kernel,
        grid=(N_TOK // tn, PEER_HEADS),
        in_specs=[
            pl.BlockSpec((D_MODEL, tn), lambda t, h: (0, t)),
            pl.BlockSpec((None, PEER_QDIM, D_MODEL), lambda t, h: (l, h, 0)),
            pl.BlockSpec((None, 2, PEER_KEYS, PEER_KEYS), lambda t, h: (l, 0, 0, 0)),
        ],
        out_specs=[spec, spec, pair_spec, pair_spec],
        out_shape=[dup, dup, pairs, pairs],
        compiler_params=_cparams(("arbitrary", "arbitrary")),
        name="peer_route",
    )(h2t, w_peer_qt, subkeys_bf)


PEER_LANES = 128
PEER_BLK = 256


def _peer_kernel(*refs, tm, te, convert):
    if convert:
        (ht_ref, u_ref, v_ref, c1_ref, e1_ref, r2_ref, e2_ref, o_ref, ub_ref, vtb_ref,
         at0_ref, at1_ref, wg0_ref, wg1_ref) = refs
    else:
        (ht_ref, u_ref, vt_ref, c1_ref, e1_ref, r2_ref, e2_ref, prev_hbm, o_ref,
         at0_ref, at1_ref, wg0_ref, wg1_ref) = refs
        del prev_hbm
    e = pl.program_id(1)
    nblk = te // PEER_BLK
    half = PEER_KEYS // 2

    @pl.when(e == 0)
    def _():
        o_ref[...] = jnp.zeros_like(o_ref)

    at_bufs = (at0_ref, at1_ref)
    wg_bufs = (wg0_ref, wg1_ref)

    def scores(q):
        rows = slice(q * PEER_BLK, (q + 1) * PEER_BLK)
        u = u_ref[rows, :]
        if convert:
            u = u.astype(BF)
            ub_ref[rows, :] = u
        at_bufs[q % 2][...] = jnp.dot(u, ht_ref[...], preferred_element_type=F32)

    def weights(q):
        for bb in range(PEER_BLK // PEER_KEYS):
            i = (e * nblk + q) * (PEER_BLK // PEER_KEYS) + bb
            bs = slice(bb * PEER_KEYS, (bb + 1) * PEER_KEYS)
            last_rows = [c1_ref[pl.ds(h * PEER_KEYS + i, 1), :] for h in range(PEER_HEADS)]
            e1_rows = [e1_ref[pl.ds(h * PEER_KEYS + i, 1), :] for h in range(PEER_HEADS)]
            for c in range(tm // PEER_LANES):
                cs = slice(c * PEER_LANES, (c + 1) * PEER_LANES)
                w = jnp.zeros((PEER_KEYS, PEER_LANES), BF)
                for h in range(PEER_HEADS):
                    hs = slice(h * half, (h + 1) * half)
                    last = pltpu.bitcast(jnp.broadcast_to(last_rows[h][:, cs], (half, PEER_LANES)), BF)
                    e1 = pltpu.bitcast(jnp.broadcast_to(e1_rows[h][:, cs], (half, PEER_LANES)), BF)
                    r2 = pltpu.bitcast(r2_ref[hs, cs], BF)
                    e2 = pltpu.bitcast(e2_ref[hs, cs], BF)
                    w = w + jnp.where(r2 <= last, e2, jnp.zeros_like(e2)) * e1
                a = at_bufs[q % 2][bs, cs]
                g = (0.5 * a * (1.0 + lax.erf(a * math.sqrt(0.5)))).astype(BF)
                wg_bufs[q % 2][bs, cs] = w * g

    def update(q):
        cols = slice(q * PEER_BLK, (q + 1) * PEER_BLK)
        if convert:
            vt = v_ref[cols, :].T.astype(BF)
            vtb_ref[:, cols] = vt
        else:
            vt = vt_ref[:, cols]
        o_ref[...] += jnp.dot(vt, wg_bufs[q % 2][...], preferred_element_type=F32)

    scores(0)
    for q in range(nblk):
        if q + 1 < nblk:
            scores(q + 1)
        weights(q)
        if q >= 1:
            update(q - 1)
    update(nblk - 1)


def _peer_scratch(tm):
    return [pltpu.VMEM((PEER_BLK, tm), F32), pltpu.VMEM((PEER_BLK, tm), F32),
            pltpu.VMEM((PEER_BLK, tm), BF), pltpu.VMEM((PEER_BLK, tm), BF)]


def _peer(h2t, peer_u, peer_v, routing, l):
    tm = 512
    c1, e1, r2, e2 = routing
    rows = PEER_HEADS * PEER_KEYS
    out_sds = jax.ShapeDtypeStruct((D_MODEL, N_TOK), F32)

    te = 512
    rspec = pl.BlockSpec((rows, tm), lambda t, e: (0, 0))
    pspec = pl.BlockSpec((rows // 2, tm), lambda t, e: (0, 0))
    first, u_bf, vt_bf = pl.pallas_call(
        functools.partial(_peer_kernel, tm=tm, te=te, convert=True),
        grid=(1, PEER_EXPERTS // te),
        in_specs=[
            pl.BlockSpec((D_MODEL, tm), lambda t, e: (0, 0)),
            pl.BlockSpec((None, te, D_MODEL), lambda t, e: (l, e, 0)),
            pl.BlockSpec((None, te, D_MODEL), lambda t, e: (l, e, 0)),
            rspec, rspec, pspec, pspec,
        ],
        out_specs=[
            pl.BlockSpec((D_MODEL, tm), lambda t, e: (0, 0)),
            pl.BlockSpec((te, D_MODEL), lambda t, e: (e, 0)),
            pl.BlockSpec((D_MODEL, te), lambda t, e: (0, e)),
        ],
        out_shape=[out_sds, jax.ShapeDtypeStruct((PEER_EXPERTS, D_MODEL), BF),
                   jax.ShapeDtypeStruct((D_MODEL, PEER_EXPERTS), BF)],
        scratch_shapes=_peer_scratch(tm),
        compiler_params=_cparams(("arbitrary", "arbitrary")),
        name="peer_dense_first",
    )(h2t, peer_u, peer_v, c1, e1, r2, e2)

    te = 1024
    rspec = pl.BlockSpec((rows, tm), lambda t, e: (0, t + 1))
    pspec = pl.BlockSpec((rows // 2, tm), lambda t, e: (0, t + 1))
    return pl.pallas_call(
        functools.partial(_peer_kernel, tm=tm, te=te, convert=False),
        grid=(N_TOK // tm - 1, PEER_EXPERTS // te),
        in_specs=[
            pl.BlockSpec((D_MODEL, tm), lambda t, e: (0, t + 1)),
            pl.BlockSpec((te, D_MODEL), lambda t, e: (e, 0)),
            pl.BlockSpec((D_MODEL, te), lambda t, e: (0, e)),
            rspec, rspec, pspec, pspec,
            pl.BlockSpec(memory_space=pl.ANY),
        ],
        out_specs=pl.BlockSpec((D_MODEL, tm), lambda t, e: (0, t + 1)),
        out_shape=out_sds,
        input_output_aliases={7: 0},
        scratch_shapes=_peer_scratch(tm),
        compiler_params=_cparams(("arbitrary", "arbitrary")),
        name="peer_dense",
    )(h2t, u_bf, vt_bf, c1, e1, r2, e2, first)


def _final_kernel(x_ref, pt_ref, ga_ref, g_ref, ctx_ref, dec_ref, *, tm, n_ctx_tiles):
    seg = _segment(pl.program_id(0) * tm)
    x = x_ref[...] + ga_ref[pl.ds(seg, 1), :] * pt_ref[...].T
    y = _rms(x, g_ref[...])
    is_ctx = pl.program_id(0) < n_ctx_tiles

    @pl.when(is_ctx)
    def _():
        ctx_ref[...] = y

    @pl.when(jnp.logical_not(is_ctx))
    def _():
        dec_ref[...] = y


def _final_norm(x, peer_t, mod, g_final):
    tm = 512
    n_ctx_tiles = N_CTX // tm
    return pl.pallas_call(
        functools.partial(_final_kernel, tm=tm, n_ctx_tiles=n_ctx_tiles),
        grid=(N_TOK // tm,),
        in_specs=[pl.BlockSpec((tm, D_MODEL), lambda i: (i, 0)),
                  pl.BlockSpec((D_MODEL, tm), lambda i: (0, i)),
                  pl.BlockSpec((None, 8, D_MODEL), lambda i: (DEPTH - 1, 0, 5)),
                  pl.BlockSpec((1, D_MODEL), lambda i: (0, 0))],
        out_specs=[
            pl.BlockSpec((tm, D_MODEL), lambda i: (jnp.minimum(i, n_ctx_tiles - 1), 0)),
            pl.BlockSpec((tm, D_MODEL), lambda i: (jnp.maximum(i - n_ctx_tiles, 0), 0)),
        ],
        out_shape=[jax.ShapeDtypeStruct((N_CTX, D_MODEL), F32), jax.ShapeDtypeStruct((N_DEC, D_MODEL), F32)],
        compiler_params=_cparams(("arbitrary",)),
        name="final_norm",
    )(x, peer_t, mod, g_final.reshape(1, D_MODEL))


def _rope_tables():
    t = jnp.arange(DEC_SEQ)
    half = MLA_ROPE // 4
    inv = ROPE_BASE ** (-jnp.arange(half, dtype=F32) / half)
    ang_r = (t // GRID_W).astype(F32)[:, None] * inv
    ang_c = (t % GRID_W).astype(F32)[:, None] * inv
    cos = jnp.concatenate([jnp.cos(ang_r)] * 2 + [jnp.cos(ang_c)] * 2, axis=1)
    sin = jnp.concatenate([-jnp.sin(ang_r), jnp.sin(ang_r), -jnp.sin(ang_c), jnp.sin(ang_c)], axis=1)
    return cos, sin


def _relayout_w_qb(w_qb):
    w = w_qb.reshape(DEPTH, MLA_Q_RANK, MLA_HEADS, MLA_NOPE + MLA_ROPE)
    w = jnp.pad(w, ((0, 0), (0, 0), (0, 0), (0, MLA_HEAD_PAD - MLA_NOPE - MLA_ROPE)))
    return w.reshape(DEPTH, MLA_Q_RANK, MLA_HEADS * MLA_HEAD_PAD).astype(BF)


def kernel(x_prompt, x_sample, cache_mla_ckv, cache_mla_krope, cache_diff_k, cache_diff_v, c, c_ctx,
           w_mod, b_mod, g_norm1, w_in, g_qnorm, w_qb, g_kvnorm, w_kvb, w_pool, pool_scale,
           diff_lambda, g_diffnorm, w_br_mla, w_br_pool, w_br_diff, w_out, g_norm2,
           w_peer_q, peer_subkeys, peer_u, peer_v, g_final):
    x = jnp.concatenate([x_prompt.reshape(N_CTX, D_MODEL), x_sample.reshape(N_DEC, D_MODEL)], axis=0)
    cond8 = jnp.concatenate([c_ctx[None, :], c, jnp.zeros((8 - 1 - DEC_BATCH, D_MODEL), F32)], axis=0)

    w_in_t = jnp.swapaxes(w_in, 1, 2)
    w_qb_re = _relayout_w_qb(w_qb)
    w_kvb_bf = w_kvb.astype(BF)
    w_pool_bf = w_pool.astype(BF)
    w_br_mla_bf, w_br_pool_bf, w_br_diff_bf = w_br_mla.astype(BF), w_br_pool.astype(BF), w_br_diff.astype(BF)
    w_out_bf = w_out.astype(BF)
    w_peer_qt = jnp.swapaxes(w_peer_q, 1, 2).astype(BF)
    subkeys_bf = peer_subkeys.astype(BF)

    cache_kr_pad = jnp.pad(cache_mla_krope, ((0, 0), (0, 0), (0, 0), (0, 128 - MLA_ROPE)))
    cache_k = cache_diff_k.reshape(DEC_BATCH, DEPTH, PAST_LEN, DIFF_WIDTH)
    cache_v = cache_diff_v.reshape(DEC_BATCH, DEPTH, PAST_LEN, DIFF_WIDTH)

    cos64, sin64 = _rope_tables()
    ones, zeros = jnp.ones_like(cos64), jnp.zeros_like(cos64)
    mla_tabs = (
        jnp.concatenate([ones, ones, cos64, ones], axis=1),
        jnp.concatenate([zeros, zeros, sin64, zeros], axis=1),
        jnp.concatenate([cos64, ones], axis=1),
        jnp.concatenate([sin64, zeros], axis=1),
    )
    diff_tabs = (jnp.tile(cos64, (1, DIFF_WIDTH // DIFF_QK)), jnp.tile(sin64, (1, DIFF_WIDTH // DIFF_QK)))

    mod = _modulation(cond8, w_mod, b_mod)

    st_ckv = jnp.zeros((BATCH, DEPTH, SEQ, MLA_KV_RANK), F32)
    st_kr = jnp.zeros((BATCH, DEPTH, SEQ, MLA_ROPE), F32)
    st_k = jnp.zeros((BATCH, DEPTH, SEQ, DIFF_WIDTH), F32)
    st_v = jnp.zeros((BATCH, DEPTH, SEQ, DIFF_WIDTH), F32)
    peer_t = None
    for l in range(DEPTH):
        if l == 0:
            p_small = _in_projection(x, g_norm1, mod, w_in_t, l, 0, SMALL_WIDTH, F32)
        else:
            p_small, x = _in_projection_res(x, peer_t, g_norm1, mod, w_in_t, l, SMALL_WIDTH)
        gl = _in_projection(x, g_norm1, mod, w_in_t, l, SMALL_WIDTH, GATE_WIDTH, BF)
        mla_c, st_ckv, st_kr = _mla_ctx(p_small, g_qnorm, g_kvnorm, w_qb_re, w_kvb_bf, st_ckv, st_kr, l)
        mla_o = _mla_dec(p_small, cache_mla_ckv, cache_kr_pad, mla_tabs, g_qnorm, g_kvnorm, w_qb_re, w_kvb_bf,
                         mla_c, l)
        diff_c, st_k, st_v = _diff_ctx(p_small, diff_lambda, g_diffnorm, st_k, st_v, l)
        diff_o = _diff_dec(p_small, cache_k, cache_v, diff_tabs, diff_lambda, g_diffnorm, diff_c, l)
        pool_o = _pool(p_small, w_pool_bf, pool_scale, l)
        merged = _merge(mla_o, pool_o, diff_o, gl, w_br_mla_bf, w_br_pool_bf, w_br_diff_bf, l)
        x, h2t = _out_projection(x, merged, w_out_bf, mod, g_norm2, l)
        routing = _route(h2t, w_peer_qt, subkeys_bf, l)
        peer_t = _peer(h2t, peer_u, peer_v, routing, l)

    y_ctx, y_dec = _final_norm(x, peer_t, mod, g_final)
    return (y_ctx.reshape(BATCH, SEQ, D_MODEL), y_dec.reshape(DEC_BATCH, DEC_SEQ, D_MODEL), st_ckv, st_kr,
            st_k.reshape(BATCH, DEPTH, SEQ, DIFF_HEADS, 2 * DIFF_QK),
            st_v.reshape(BATCH, DEPTH, SEQ, DIFF_HEADS, DIFF_V))
```

```python
import functools
import math

import jax
import jax.numpy as jnp
from jax import lax
from jax.experimental import pallas as pl
from jax.experimental.pallas import tpu as pltpu

BF = jnp.bfloat16
F32 = jnp.float32

D_MODEL = 2048
BATCH = 16
SEQ = 256
DEPTH = 4
DEC_BATCH = 2
DEC_SEQ = 1024
PAST_LEN = 256
GRID_W = 64
ROPE_BASE = 10000.0
EPS = 1e-6

MLA_HEADS = 8
MLA_NOPE = 128
MLA_ROPE = 64
MLA_V = 128
MLA_Q_RANK = 512
MLA_KV_RANK = 256
MLA_HEAD_PAD = 256

POOL_WINDOWS = (2, 4, 8, 16)
POOL_GROUP = 128
POOL_WIDTH = POOL_GROUP * len(POOL_WINDOWS)

DIFF_HEADS = 4
DIFF_QK = 64
DIFF_V = 2 * DIFF_QK
DIFF_WIDTH = DIFF_HEADS * DIFF_V

PEER_HEADS = 8
PEER_KEYS = 128
PEER_EXPERTS = PEER_KEYS * PEER_KEYS
PEER_TOPK = 16
PEER_QDIM = 256

N_CTX = BATCH * SEQ
N_DEC = DEC_BATCH * DEC_SEQ
N_TOK = N_CTX + N_DEC
KEYS_DEC = PAST_LEN + DEC_SEQ

C_QA = 0
C_CKV = 512
C_KR = 768
C_PZ = 1024
C_DQ = 1536
C_DK = 2048
C_DV = 2560
SMALL_WIDTH = 3072
GATE_WIDTH = 3 * D_MODEL

KV_END = 512 + 320
INPROJ_TN = 512

VMEM_LIMIT = 56 * 1024 * 1024


def _cparams(sem):
    return pltpu.CompilerParams(dimension_semantics=sem, vmem_limit_bytes=VMEM_LIMIT)


def _segment(row0):
    return jnp.where(row0 < N_CTX, 0, 1 + (row0 - N_CTX) // DEC_SEQ)


def _rms(x, g):
    return x * lax.rsqrt(jnp.mean(x * x, axis=-1, keepdims=True) + EPS) * g


def _softmax_rows(s):
    m = jnp.max(s, axis=-1, keepdims=True)
    p = jnp.exp(s - m)
    return p / jnp.sum(p, axis=-1, keepdims=True)


def _dot_nt(a, b):
    return lax.dot_general(a, b, (((1,), (1,)), ((), ())), preferred_element_type=F32)


def _rope_swap(x):
    n = x.shape[-1]
    up = pltpu.roll(x, n - 16, 1)
    dn = pltpu.roll(x, 16, 1)
    lane = lax.broadcasted_iota(jnp.int32, x.shape, 1)
    return jnp.where((lane & 16) == 0, up, dn)


def _mod_kernel(c_ref, w_ref, b_ref, o_ref):
    c = c_ref[...]
    a = (c * jax.nn.sigmoid(c)).astype(BF)
    o_ref[...] = jnp.dot(a, w_ref[...].astype(BF), preferred_element_type=F32) + b_ref[...]


def _modulation(cond8, w_mod, b_mod):
    tn = 1024
    return pl.pallas_call(
        _mod_kernel,
        grid=(DEPTH, 6 * D_MODEL // tn),
        in_specs=[
            pl.BlockSpec((8, D_MODEL), lambda l, j: (0, 0)),
            pl.BlockSpec((None, D_MODEL, tn), lambda l, j: (l, 0, j)),
            pl.BlockSpec((None, 1, tn), lambda l, j: (l, 0, j)),
        ],
        out_specs=pl.BlockSpec((None, 8, tn), lambda l, j: (l, 0, j)),
        out_shape=jax.ShapeDtypeStruct((DEPTH, 8, 6 * D_MODEL), F32),
        compiler_params=_cparams(("arbitrary", "arbitrary")),
        name="modulation",
    )(cond8, w_mod, b_mod.reshape(DEPTH, 1, 6 * D_MODEL))


def _w_in_row(tile):
    per_tile = INPROJ_TN // 64
    return 64 * jnp.where(tile < 2, per_tile * tile, per_tile * tile - (C_PZ - KV_END) // 64)


def _inproj_tile(h_ref, w_ref, tile):
    y = _dot_nt(h_ref[...], w_ref[0].astype(BF))
    lane = lax.broadcasted_iota(jnp.int32, y.shape, 1)
    return jnp.where(lane >= jnp.where(tile == 1, KV_END - INPROJ_TN, INPROJ_TN), 0.0, y)


def _inproj_kernel(x_ref, g_ref, sh_ref, sc_ref, w_ref, o_ref, h_ref, *, tm, tile0):
    i = pl.program_id(0)

    @pl.when(pl.program_id(1) == 0)
    def _():
        seg = _segment(i * tm)
        y = _rms(x_ref[...], g_ref[...])
        h_ref[...] = (y * (1.0 + sc_ref[pl.ds(seg, 1), :]) + sh_ref[pl.ds(seg, 1), :]).astype(BF)

    o_ref[...] = _inproj_tile(h_ref, w_ref, pl.program_id(1) + tile0).astype(o_ref.dtype)


def _w_in_spec(l, tile0):
    return pl.BlockSpec((pl.Element(1), pl.Element(INPROJ_TN), pl.Element(D_MODEL)),
                        lambda i, j: (l, _w_in_row(j + tile0), 0))


def _in_projection(x, g_norm1, mod, w_in_t, l, col0, width, out_dtype, tm=1024):
    tn = INPROJ_TN
    tile0 = col0 // tn
    return pl.pallas_call(
        functools.partial(_inproj_kernel, tm=tm, tile0=tile0),
        grid=(N_TOK // tm, width // tn),
        in_specs=[
            pl.BlockSpec((tm, D_MODEL), lambda i, j: (i, 0)),
            pl.BlockSpec((None, 1, D_MODEL), lambda i, j: (l, 0, 0)),
            pl.BlockSpec((None, 8, D_MODEL), lambda i, j: (l, 0, 0)),
            pl.BlockSpec((None, 8, D_MODEL), lambda i, j: (l, 0, 1)),
            _w_in_spec(l, tile0),
        ],
        out_specs=pl.BlockSpec((tm, tn), lambda i, j: (i, j)),
        out_shape=jax.ShapeDtypeStruct((N_TOK, width), out_dtype),
        scratch_shapes=[pltpu.VMEM((tm, D_MODEL), BF)],
        compiler_params=_cparams(("arbitrary", "arbitrary")),
        name="in_projection",
    )(x, g_norm1.reshape(DEPTH, 1, D_MODEL), mod, mod, w_in_t)


def _inproj_res_kernel(x_ref, pt_ref, ga_ref, g_ref, sh_ref, sc_ref, w_ref, o_ref, xo_ref, h_ref, *, tm):
    i = pl.program_id(0)

    @pl.when(pl.program_id(1) == 0)
    def _():
        seg = _segment(i * tm)
        x = x_ref[...] + ga_ref[pl.ds(seg, 1), :] * pt_ref[...].T
        xo_ref[...] = x
        y = _rms(x, g_ref[...])
        h_ref[...] = (y * (1.0 + sc_ref[pl.ds(seg, 1), :]) + sh_ref[pl.ds(seg, 1), :]).astype(BF)

    o_ref[...] = _inproj_tile(h_ref, w_ref, pl.program_id(1)).astype(o_ref.dtype)


def _in_projection_res(x, peer_t, g_norm1, mod, w_in_t, l, width):
    tm, tn = 1024, INPROJ_TN
    once = pl.Buffered(1)
    return pl.pallas_call(
        functools.partial(_inproj_res_kernel, tm=tm),
        grid=(N_TOK // tm, width // tn),
        in_specs=[
            pl.BlockSpec((tm, D_MODEL), lambda i, j: (i, 0), pipeline_mode=once),
            pl.BlockSpec((D_MODEL, tm), lambda i, j: (0, i), pipeline_mode=once),
            pl.BlockSpec((None, 8, D_MODEL), lambda i, j: (l - 1, 0, 5)),
            pl.BlockSpec((None, 1, D_MODEL), lambda i, j: (l, 0, 0)),
            pl.BlockSpec((None, 8, D_MODEL), lambda i, j: (l, 0, 0)),
            pl.BlockSpec((None, 8, D_MODEL), lambda i, j: (l, 0, 1)),
            _w_in_spec(l, 0),
        ],
        out_specs=[
            pl.BlockSpec((tm, tn), lambda i, j: (i, j)),
            pl.BlockSpec((tm, D_MODEL), lambda i, j: (i, 0)),
        ],
        out_shape=[jax.ShapeDtypeStruct((N_TOK, width), F32), jax.ShapeDtypeStruct((N_TOK, D_MODEL), F32)],
        scratch_shapes=[pltpu.VMEM((tm, D_MODEL), BF)],
        compiler_params=_cparams(("arbitrary", "arbitrary")),
        name="in_projection_res",
    )(x, peer_t, mod, g_norm1.reshape(DEPTH, 1, D_MODEL), mod, mod, w_in_t)


def _mla_heads(q, kfun, vfun, o_ref, scale):
    for h in range(MLA_HEADS):
        s = _dot_nt(q(h), kfun(h)) * scale
        p = _softmax_rows(s).astype(BF)
        o = jnp.dot(p, vfun(h), preferred_element_type=F32)
        o_ref[:, h * MLA_V:(h + 1) * MLA_V] = o.astype(o_ref.dtype)


def _mla_ctx_kernel(qa_ref, ckv_ref, kr_ref, gq_ref, gkv_ref, wqb_ref, wkvb_ref, st_ckv_hbm, st_kr_hbm,
                    o_ref, ckv_out_ref, kr_out_ref):
    del st_ckv_hbm, st_kr_hbm
    scale = (MLA_NOPE + MLA_ROPE) ** -0.5
    qn = _rms(qa_ref[...], gq_ref[...]).astype(BF)
    q = jnp.dot(qn, wqb_ref[...], preferred_element_type=F32).astype(BF)
    ckv = _rms(ckv_ref[...], gkv_ref[...])
    ckv_out_ref[...] = ckv
    kr = kr_ref[...]
    kr_out_ref[...] = kr[:, :MLA_ROPE]
    kv = jnp.dot(ckv.astype(BF), wkvb_ref[...], preferred_element_type=F32).astype(BF)
    krp = kr[:, :128].astype(BF)

    def qh(h):
        return q[:, h * MLA_HEAD_PAD:(h + 1) * MLA_HEAD_PAD]

    def kh(h):
        return jnp.concatenate([kv[:, h * 256:h * 256 + MLA_NOPE], krp], axis=1)

    def vh(h):
        return kv[:, h * 256 + MLA_NOPE:(h + 1) * 256]

    _mla_heads(qh, kh, vh, o_ref, scale)


def _mla_ctx(p_small, g_qnorm, g_kvnorm, w_qb_re, w_kvb, st_ckv, st_kr, l):
    return pl.pallas_call(
        _mla_ctx_kernel,
        grid=(BATCH,),
        in_specs=[
            pl.BlockSpec((SEQ, MLA_Q_RANK), lambda b: (b, C_QA // MLA_Q_RANK)),
            pl.BlockSpec((SEQ, 256), lambda b: (b, C_CKV // 256)),
            pl.BlockSpec((SEQ, 256), lambda b: (b, C_KR // 256)),
            pl.BlockSpec((None, 1, MLA_Q_RANK), lambda b: (l, 0, 0)),
            pl.BlockSpec((None, 1, MLA_KV_RANK), lambda b: (l, 0, 0)),
            pl.BlockSpec((None, MLA_Q_RANK, MLA_HEADS * MLA_HEAD_PAD), lambda b: (l, 0, 0)),
            pl.BlockSpec((None, MLA_KV_RANK, MLA_HEADS * 256), lambda b: (l, 0, 0)),
            pl.BlockSpec(memory_space=pl.ANY),
            pl.BlockSpec(memory_space=pl.ANY),
        ],
        out_specs=[
            pl.BlockSpec((SEQ, MLA_HEADS * MLA_V), lambda b: (b, 0)),
            pl.BlockSpec((None, None, SEQ, MLA_KV_RANK), lambda b: (b, l, 0, 0)),
            pl.BlockSpec((None, None, SEQ, MLA_ROPE), lambda b: (b, l, 0, 0)),
        ],
        out_shape=[
            jax.ShapeDtypeStruct((N_TOK, MLA_HEADS * MLA_V), BF),
            jax.ShapeDtypeStruct(st_ckv.shape, F32),
            jax.ShapeDtypeStruct(st_kr.shape, F32),
        ],
        input_output_aliases={7: 1, 8: 2},
        compiler_params=_cparams(("arbitrary",)),
        name="mla_ctx",
    )(p_small, p_small, p_small, g_qnorm.reshape(DEPTH, 1, -1), g_kvnorm.reshape(DEPTH, 1, -1),
      w_qb_re, w_kvb, st_ckv, st_kr)


def _mla_dec_kernel(qa_ref, ckv_ref, kr_ref, cckv_ref, ckr_ref, cq_ref, sq_ref, ck_ref, sk_ref,
                    gq_ref, gkv_ref, wqb_ref, wkvb_ref, mix_hbm, o_ref, kf_ref, vf_ref):
    del mix_hbm
    scale = (MLA_NOPE + MLA_ROPE) ** -0.5

    @pl.when(pl.program_id(1) == 0)
    def _():
        ckv = _rms(ckv_ref[...], gkv_ref[...])
        ckv_all = jnp.concatenate([cckv_ref[...], ckv], axis=0).astype(BF)
        kr = kr_ref[:, :128]
        kr_rot = kr * ck_ref[...] + _rope_swap(kr) * sk_ref[...]
        kr_all = jnp.concatenate([ckr_ref[...], kr_rot], axis=0).astype(BF)
        for h in range(MLA_HEADS):
            kvh = jnp.dot(ckv_all, wkvb_ref[:, h * 256:(h + 1) * 256], preferred_element_type=F32)
            kf_ref[h, :, 0:MLA_NOPE] = kvh[:, :MLA_NOPE].astype(BF)
            kf_ref[h, :, MLA_NOPE:MLA_HEAD_PAD] = kr_all
            vf_ref[h] = kvh[:, MLA_NOPE:].astype(BF)

    qn = _rms(qa_ref[...], gq_ref[...]).astype(BF)
    q = jnp.dot(qn, wqb_ref[...], preferred_element_type=F32)
    cq = cq_ref[...]
    sq = sq_ref[...]

    def qh(h):
        x = q[:, h * MLA_HEAD_PAD:(h + 1) * MLA_HEAD_PAD]
        return (x * cq + _rope_swap(x) * sq).astype(BF)

    _mla_heads(qh, lambda h: kf_ref[h], lambda h: vf_ref[h], o_ref, scale)


def _mla_dec(p_small, cache_ckv, cache_kr_pad, tabs, g_qnorm, g_kvnorm, w_qb_re, w_kvb, mix, l):
    tq = 256
    nq = DEC_SEQ // tq
    row_blk = N_CTX // DEC_SEQ
    cq, sq, ck, sk = tabs
    return pl.pallas_call(
        _mla_dec_kernel,
        grid=(DEC_BATCH, nq),
        in_specs=[
            pl.BlockSpec((tq, MLA_Q_RANK), lambda b, i: (N_CTX // tq + b * nq + i, 0)),
            pl.BlockSpec((DEC_SEQ, 256), lambda b, i: (row_blk + b, C_CKV // 256)),
            pl.BlockSpec((DEC_SEQ, 256), lambda b, i: (row_blk + b, C_KR // 256)),
            pl.BlockSpec((None, None, PAST_LEN, MLA_KV_RANK), lambda b, i: (b, l, 0, 0)),
            pl.BlockSpec((None, None, PAST_LEN, 128), lambda b, i: (b, l, 0, 0)),
            pl.BlockSpec((tq, MLA_HEAD_PAD), lambda b, i: (i, 0)),
            pl.BlockSpec((tq, MLA_HEAD_PAD), lambda b, i: (i, 0)),
            pl.BlockSpec((DEC_SEQ, 128), lambda b, i: (0, 0)),
            pl.BlockSpec((DEC_SEQ, 128), lambda b, i: (0, 0)),
            pl.BlockSpec((None, 1, MLA_Q_RANK), lambda b, i: (l, 0, 0)),
            pl.BlockSpec((None, 1, MLA_KV_RANK), lambda b, i: (l, 0, 0)),
            pl.BlockSpec((None, MLA_Q_RANK, MLA_HEADS * MLA_HEAD_PAD), lambda b, i: (l, 0, 0)),
            pl.BlockSpec((None, MLA_KV_RANK, MLA_HEADS * 256), lambda b, i: (l, 0, 0)),
            pl.BlockSpec(memory_space=pl.ANY),
        ],
        out_specs=pl.BlockSpec((tq, MLA_HEADS * MLA_V), lambda b, i: (N_CTX // tq + b * nq + i, 0)),
        out_shape=jax.ShapeDtypeStruct((N_TOK, MLA_HEADS * MLA_V), BF),
        input_output_aliases={13: 0},
        scratch_shapes=[
            pltpu.VMEM((MLA_HEADS, KEYS_DEC, MLA_HEAD_PAD), BF),
            pltpu.VMEM((MLA_HEADS, KEYS_DEC, MLA_V), BF),
        ],
        compiler_params=_cparams(("arbitrary", "arbitrary")),
        name="mla_dec",
    )(p_small, p_small, p_small, cache_ckv, cache_kr_pad, cq, sq, ck, sk,
      g_qnorm.reshape(DEPTH, 1, -1), g_kvnorm.reshape(DEPTH, 1, -1), w_qb_re, w_kvb, mix)


def _diff_lambda(lv):
    t1 = jnp.sum(lv[0:1] * lv[1:2], axis=-1, keepdims=True)
    t2 = jnp.sum(lv[2:3] * lv[3:4], axis=-1, keepdims=True)
    return jnp.exp(t1) - jnp.exp(t2)


def _diff_heads(q, kfun, vfun, lam, g, o_ref, lam_init):
    scale = DIFF_QK ** -0.5
    lane = lax.broadcasted_iota(jnp.int32, (q.shape[0], DIFF_V), 1)
    for h in range(DIFF_HEADS):
        qh = q[:, h * DIFF_V:(h + 1) * DIFF_V]
        q1 = jnp.where(lane < DIFF_QK, qh, 0.0).astype(BF)
        q2 = jnp.where(lane >= DIFF_QK, qh, 0.0).astype(BF)
        k = kfun(h)
        v = vfun(h)
        p1 = _softmax_rows(_dot_nt(q1, k) * scale).astype(BF)
        p2 = _softmax_rows(_dot_nt(q2, k) * scale).astype(BF)
        a1 = jnp.dot(p1, v, preferred_element_type=F32)
        a2 = jnp.dot(p2, v, preferred_element_type=F32)
        o = _rms(a1 - lam * a2, g) * (1.0 - lam_init)
        o_ref[:, h * DIFF_V:(h + 1) * DIFF_V] = o.astype(o_ref.dtype)


def _diff_ctx_kernel(q_ref, k_ref, v_ref, lam_ref, g_ref, st_k_hbm, st_v_hbm,
                     o_ref, k_out_ref, v_out_ref, *, lam_init):
    del st_k_hbm, st_v_hbm
    lam = _diff_lambda(lam_ref[...]) + lam_init
    k_out_ref[...] = k_ref[...]
    v_out_ref[...] = v_ref[...]
    k = k_ref[...].astype(BF)
    v = v_ref[...].astype(BF)
    _diff_heads(q_ref[...],
                lambda h: k[:, h * DIFF_V:(h + 1) * DIFF_V],
                lambda h: v[:, h * DIFF_V:(h + 1) * DIFF_V],
                lam, g_ref[...], o_ref, lam_init)


def _lam_init(l):
    return 0.8 - 0.6 * math.exp(-0.3 * l)


def _diff_ctx(p_small, diff_lambda, g_diffnorm, st_k, st_v, l):
    return pl.pallas_call(
        functools.partial(_diff_ctx_kernel, lam_init=_lam_init(l)),
        grid=(BATCH,),
        in_specs=[
            pl.BlockSpec((SEQ, DIFF_WIDTH), lambda b: (b, C_DQ // DIFF_WIDTH)),
            pl.BlockSpec((SEQ, DIFF_WIDTH), lambda b: (b, C_DK // DIFF_WIDTH)),
            pl.BlockSpec((SEQ, DIFF_WIDTH), lambda b: (b, C_DV // DIFF_WIDTH)),
            pl.BlockSpec((None, 4, DIFF_QK), lambda b: (l, 0, 0)),
            pl.BlockSpec((None, 1, DIFF_V), lambda b: (l, 0, 0)),
            pl.BlockSpec(memory_space=pl.ANY),
            pl.BlockSpec(memory_space=pl.ANY),
        ],
        out_specs=[
            pl.BlockSpec((SEQ, DIFF_WIDTH), lambda b: (b, 0)),
            pl.BlockSpec((None, None, SEQ, DIFF_WIDTH), lambda b: (b, l, 0, 0)),
            pl.BlockSpec((None, None, SEQ, DIFF_WIDTH), lambda b: (b, l, 0, 0)),
        ],
        out_shape=[
            jax.ShapeDtypeStruct((N_TOK, DIFF_WIDTH), BF),
            jax.ShapeDtypeStruct(st_k.shape, F32),
            jax.ShapeDtypeStruct(st_v.shape, F32),
        ],
        input_output_aliases={5: 1, 6: 2},
        compiler_params=_cparams(("arbitrary",)),
        name="diff_ctx",
    )(p_small, p_small, p_small, diff_lambda, g_diffnorm.reshape(DEPTH, 1, -1), st_k, st_v)


def _diff_dec_kernel(q_ref, k_ref, v_ref, ck_ref, cv_ref, cq_ref, sq_ref, cfull_ref, sfull_ref,
                     lam_ref, g_ref, mix_hbm, o_ref, kf_ref, vf_ref, *, lam_init):
    del mix_hbm

    @pl.when(pl.program_id(1) == 0)
    def _():
        k = k_ref[...]
        k_rot = k * cfull_ref[...] + _rope_swap(k) * sfull_ref[...]
        kf_ref[0:PAST_LEN, :] = ck_ref[...].astype(BF)
        kf_ref[PAST_LEN:KEYS_DEC, :] = k_rot.astype(BF)
        vf_ref[0:PAST_LEN, :] = cv_ref[...].astype(BF)
        vf_ref[PAST_LEN:KEYS_DEC, :] = v_ref[...].astype(BF)

    lam = _diff_lambda(lam_ref[...]) + lam_init
    q = q_ref[...]
    q = q * cq_ref[...] + _rope_swap(q) * sq_ref[...]
    _diff_heads(q,
                lambda h: kf_ref[:, h * DIFF_V:(h + 1) * DIFF_V],
                lambda h: vf_ref[:, h * DIFF_V:(h + 1) * DIFF_V],
                lam, g_ref[...], o_ref, lam_init)


def _diff_dec(p_small, cache_k, cache_v, tabs, diff_lambda, g_diffnorm, mix, l):
    tq = 256
    nq = DEC_SEQ // tq
    row_blk = N_CTX // DEC_SEQ
    c512, s512 = tabs
    return pl.pallas_call(
        functools.partial(_diff_dec_kernel, lam_init=_lam_init(l)),
        grid=(DEC_BATCH, nq),
        in_specs=[
            pl.BlockSpec((tq, DIFF_WIDTH), lambda b, i: (N_CTX // tq + b * nq + i, C_DQ // DIFF_WIDTH)),
            pl.BlockSpec((DEC_SEQ, DIFF_WIDTH), lambda b, i: (row_blk + b, C_DK // DIFF_WIDTH)),
            pl.BlockSpec((DEC_SEQ, DIFF_WIDTH), lambda b, i: (row_blk + b, C_DV // DIFF_WIDTH)),
            pl.BlockSpec((None, None, PAST_LEN, DIFF_WIDTH), lambda b, i: (b, l, 0, 0)),
            pl.BlockSpec((None, None, PAST_LEN, DIFF_WIDTH), lambda b, i: (b, l, 0, 0)),
            pl.BlockSpec((tq, DIFF_WIDTH), lambda b, i: (i, 0)),
            pl.BlockSpec((tq, DIFF_WIDTH), lambda b, i: (i, 0)),
            pl.BlockSpec((DEC_SEQ, DIFF_WIDTH), lambda b, i: (0, 0)),
            pl.BlockSpec((DEC_SEQ, DIFF_WIDTH), lambda b, i: (0, 0)),
            pl.BlockSpec((None, 4, DIFF_QK), lambda b, i: (l, 0, 0)),
            pl.BlockSpec((None, 1, DIFF_V), lambda b, i: (l, 0, 0)),
            pl.BlockSpec(memory_space=pl.ANY),
        ],
        out_specs=pl.BlockSpec((tq, DIFF_WIDTH), lambda b, i: (N_CTX // tq + b * nq + i, 0)),
        out_shape=jax.ShapeDtypeStruct((N_TOK, DIFF_WIDTH), BF),
        input_output_aliases={11: 0},
        scratch_shapes=[
            pltpu.VMEM((KEYS_DEC, DIFF_WIDTH), BF),
            pltpu.VMEM((KEYS_DEC, DIFF_WIDTH), BF),
        ],
        compiler_params=_cparams(("arbitrary", "arbitrary")),
        name="diff_dec",
    )(p_small, p_small, p_small, cache_k, cache_v, c512, s512, c512, s512,
      diff_lambda, g_diffnorm.reshape(DEPTH, 1, -1), mix)


def _pool_kernel(z_ref, w_ref, ps_ref, o_ref, *, tm):
    seq_m1 = jnp.where(pl.program_id(0) * tm < N_CTX, SEQ - 1, DEC_SEQ - 1)
    t = lax.broadcasted_iota(jnp.int32, (tm, POOL_GROUP), 0) & seq_m1
    for gi, w in enumerate(POOL_WINDOWS):
        z = z_ref[:, gi * POOL_GROUP:(gi + 1) * POOL_GROUP]
        acc = jnp.zeros_like(z)
        for k in range(-(w // 2), w // 2):
            zs = z if k == 0 else pltpu.roll(z, (-k) % tm, 0)
            ok = (t + k >= 0) & (t + k <= seq_m1)
            acc = acc + jnp.where(ok, zs, 0.0)
        lo = jnp.maximum(t - w // 2, 0)
        hi = jnp.minimum(t + (w - 1) // 2, seq_m1)
        d = (acc / (hi - lo + 1).astype(F32) - z).astype(BF)
        y = jnp.dot(d, w_ref[gi], preferred_element_type=F32)
        y = y * ps_ref[:, gi * POOL_GROUP:(gi + 1) * POOL_GROUP]
        o_ref[:, gi * POOL_GROUP:(gi + 1) * POOL_GROUP] = y.astype(o_ref.dtype)


def _pool(p_small, w_pool_bf, pool_scale, l):
    tm = DEC_SEQ
    return pl.pallas_call(
        functools.partial(_pool_kernel, tm=tm),
        grid=(N_TOK // tm,),
        in_specs=[
            pl.BlockSpec((tm, POOL_WIDTH), lambda i: (i, C_PZ // POOL_WIDTH)),
            pl.BlockSpec((None, len(POOL_WINDOWS), POOL_GROUP, POOL_GROUP), lambda i: (l, 0, 0, 0)),
            pl.BlockSpec((None, 1, POOL_WIDTH), lambda i: (l, 0, 0)),
        ],
        out_specs=pl.BlockSpec((tm, POOL_WIDTH), lambda i: (i, 0)),
        out_shape=jax.ShapeDtypeStruct((N_TOK, POOL_WIDTH), BF),
        compiler_params=_cparams(("arbitrary",)),
        name="pool",
    )(p_small, w_pool_bf, pool_scale.reshape(DEPTH, 1, POOL_WIDTH))


def _merge_kernel(a_ref, p_ref, d_ref, g0_ref, g1_ref, g2_ref, wa_ref, wp_ref, wd_ref, o_ref):
    def sig(r):
        return jax.nn.sigmoid(r[...].astype(F32))

    m = sig(g0_ref) * jnp.dot(a_ref[...], wa_ref[...], preferred_element_type=F32)
    m = m + sig(g1_ref) * jnp.dot(p_ref[...], wp_ref[...], preferred_element_type=F32)
    m = m + sig(g2_ref) * jnp.dot(d_ref[...], wd_ref[...], preferred_element_type=F32)
    o_ref[...] = m.astype(o_ref.dtype)


def _merge(mla_o, pool_o, diff_o, gl, w_br_mla, w_br_pool, w_br_diff, l):
    tm, tn = 1024, 512
    nj = D_MODEL // tn
    return pl.pallas_call(
        _merge_kernel,
        grid=(N_TOK // tm, nj),
        in_specs=[
            pl.BlockSpec((tm, MLA_HEADS * MLA_V), lambda i, j: (i, 0)),
            pl.BlockSpec((tm, POOL_WIDTH), lambda i, j: (i, 0)),
            pl.BlockSpec((tm, DIFF_WIDTH), lambda i, j: (i, 0)),
            pl.BlockSpec((tm, tn), lambda i, j: (i, j)),
            pl.BlockSpec((tm, tn), lambda i, j: (i, nj + j)),
            pl.BlockSpec((tm, tn), lambda i, j: (i, 2 * nj + j)),
            pl.BlockSpec((None, MLA_HEADS * MLA_V, tn), lambda i, j: (l, 0, j)),
            pl.BlockSpec((None, POOL_WIDTH, tn), lambda i, j: (l, 0, j)),
            pl.BlockSpec((None, DIFF_WIDTH, tn), lambda i, j: (l, 0, j)),
        ],
        out_specs=pl.BlockSpec((tm, tn), lambda i, j: (i, j)),
        out_shape=jax.ShapeDtypeStruct((N_TOK, D_MODEL), BF),
        compiler_params=_cparams(("arbitrary", "arbitrary")),
        name="merge",
    )(mla_o, pool_o, diff_o, gl, gl, gl, w_br_mla, w_br_pool, w_br_diff)


def _outproj_kernel(x_ref, m_ref, w_ref, ga_ref, g2_ref, sh_ref, sc_ref, xo_ref, ht_ref, *, tm):
    seg = _segment(pl.program_id(0) * tm)
    y = jnp.dot(m_ref[...], w_ref[...], preferred_element_type=F32)
    xn = x_ref[...] + ga_ref[pl.ds(seg, 1), :] * y
    xo_ref[...] = xn
    h2 = _rms(xn, g2_ref[...]) * (1.0 + sc_ref[pl.ds(seg, 1), :]) + sh_ref[pl.ds(seg, 1), :]
    ht_ref[...] = h2.T.astype(BF)


def _out_projection(x, merged, w_out, mod, g_norm2, l):
    tm = 256
    return pl.pallas_call(
        functools.partial(_outproj_kernel, tm=tm),
        grid=(N_TOK // tm,),
        in_specs=[
            pl.BlockSpec((tm, D_MODEL), lambda i: (i, 0)),
            pl.BlockSpec((tm, D_MODEL), lambda i: (i, 0)),
            pl.BlockSpec((None, D_MODEL, D_MODEL), lambda i: (l, 0, 0)),
            pl.BlockSpec((None, 8, D_MODEL), lambda i: (l, 0, 2)),
            pl.BlockSpec((None, 1, D_MODEL), lambda i: (l, 0, 0)),
            pl.BlockSpec((None, 8, D_MODEL), lambda i: (l, 0, 3)),
            pl.BlockSpec((None, 8, D_MODEL), lambda i: (l, 0, 4)),
        ],
        out_specs=[
            pl.BlockSpec((tm, D_MODEL), lambda i: (i, 0)),
            pl.BlockSpec((D_MODEL, tm), lambda i: (0, i)),
        ],
        out_shape=[
            jax.ShapeDtypeStruct((N_TOK, D_MODEL), F32),
            jax.ShapeDtypeStruct((D_MODEL, N_TOK), BF),
        ],
        compiler_params=_cparams(("arbitrary",)),
        name="out_projection",
    )(x, merged, w_out, mod, g_norm2.reshape(DEPTH, 1, D_MODEL), mod, mod)


ROUTE_LANES = 128
NOT_SELECTED = 127


SLAB = 8
NET_WIDTH = 16


def _sorting_network(n):
    def merge(lo, hi, r):
        step = r * 2
        if step < hi - lo:
            yield from merge(lo, hi, step)
            yield from merge(lo + r, hi, step)
            yield from ((i, i + r) for i in range(lo + r, hi - r, step))
        else:
            yield (lo, lo + r)

    def sort(lo, hi):
        if hi - lo >= 1:
            mid = lo + (hi - lo) // 2
            yield from sort(lo, mid)
            yield from sort(mid + 1, hi)
            yield from merge(lo, hi, 1)

    return tuple(sort(0, n - 1))


def _extract_top(x, n):
    slabs = [x[SLAB * v:SLAB * (v + 1)] for v in range(x.shape[0] // SLAB)]
    slabs += [None] * (NET_WIDTH - len(slabs))
    for i, j in _sorting_network(NET_WIDTH):
        hi, lo = slabs[i], slabs[j]
        if lo is None:
            continue
        if hi is None:
            slabs[i], slabs[j] = lo, None
        else:
            slabs[i], slabs[j] = jnp.maximum(hi, lo), jnp.minimum(hi, lo)
    stack = [s for s in slabs if s is not None]
    sub = lax.broadcasted_iota(jnp.int32, stack[0].shape, 0)
    vals = []
    for r in range(n):
        m = jnp.max(stack[0], axis=0, keepdims=True)
        vals.append(m)
        hit = stack[0] == m
        popped = sub == jnp.min(jnp.where(hit, sub, SLAB), axis=0, keepdims=True)
        for d in range(min(n - r - 1, len(stack))):
            below = stack[d + 1] if d + 1 < len(stack) else -jnp.inf
            stack[d] = jnp.where(popped, below, stack[d])
    return vals


def _stack_rows(rows):
    idx = lax.broadcasted_iota(jnp.int32, (len(rows), rows[0].shape[1]), 0)
    m = jnp.zeros((len(rows), rows[0].shape[1]), F32)
    for r, row in enumerate(rows):
        m = jnp.where(idx == r, row, m)
    return m


def _route_chunk(s1, s2):
    k = PEER_TOPK
    a = _extract_top(s1, k)
    b = _extract_top(s2, k)
    am, bm = _stack_rows(a), _stack_rows(b)
    slabs = [a[0] + bm[0:8], a[0] + bm[8:16]]
    slabs += [a[r] + bm[0:8] for r in range(1, 8)]
    slabs += [am[8:16] + b[0]]
    cand = jnp.concatenate(slabs, axis=0)
    tau = _extract_top(cand, k)[-1]
    sel = cand >= tau
    top = a[0] + b[0]
    cnt = jnp.where(sel, 1.0, 0.0)
    low = jnp.where(sel, cand, jnp.inf)

    def row_rows(r):
        return slice(0, 16) if r == 0 else slice(8 + 8 * r, 16 + 8 * r) if r < 8 else slice(64 + r, 65 + r)

    n_r = [jnp.sum(cnt[row_rows(r)], axis=0, keepdims=True) for r in range(k)]
    low_r = [jnp.min(low[row_rows(r)], axis=0, keepdims=True) for r in range(k)]
    total = n_r[0]
    for r in range(1, k):
        total = total + n_r[r]
    excess = total - float(k)
    for r in reversed(range(k)):
        drop = jnp.where(excess > 0.0, jnp.where(low_r[r] == tau, 1.0, 0.0), 0.0)
        n_r[r] = n_r[r] - drop
        excess = excess - drop
    dropped = total - float(k) - excess
    z = jnp.sum(jnp.where(sel, jnp.exp(cand - top), 0.0), axis=0, keepdims=True) - dropped * jnp.exp(tau - top)

    c1 = jnp.full(s1.shape, -1.0, F32)
    r2 = jnp.full(s2.shape, float(NOT_SELECTED), F32)
    for r in range(k - 1):
        c1 = jnp.where(s1 == a[r], n_r[r] - 1.0, c1)
        r2 = jnp.where(s2 == b[r], float(r), r2)
    key = lax.broadcasted_iota(jnp.int32, s1.shape, 0)

    def last_key_limit(s, vals):
        match = s == vals[k - 1]
        first = jnp.min(jnp.where(match, key, PEER_KEYS), axis=0, keepdims=True)
        return match, jnp.where(vals[k - 2] != vals[k - 1], first, PEER_KEYS)

    match1, limit1 = last_key_limit(s1, a)
    c1 = jnp.where(match1, jnp.where(key <= limit1, n_r[k - 1] - 1.0, c1), c1)
    match2, limit2 = last_key_limit(s2, b)
    r2 = jnp.where(match2, jnp.where(key <= limit2, float(k - 1), r2), r2)
    e1 = jnp.exp(s1 - a[0]) / z
    e2 = jnp.exp(s2 - b[0])
    return c1, e1, r2, e2


def _route_kernel(ht_ref, wq_ref, sk_ref, c1_ref, e1_ref, r2_ref, e2_ref):
    qt = lax.dot_general(wq_ref[...].astype(BF), ht_ref[...], (((0,), (0,)), ((), ())),
                         preferred_element_type=F32)
    s1 = jnp.dot(sk_ref[0], qt[0:PEER_KEYS].astype(BF), preferred_element_type=F32)
    s2 = jnp.dot(sk_ref[1], qt[PEER_KEYS:].astype(BF), preferred_element_type=F32)

    def pack_row_pairs(x):
        return pltpu.bitcast(x.astype(BF), jnp.int32)

    def duplicate_halves(x):
        hi = pltpu.bitcast(x.astype(BF).astype(F32), jnp.int32)
        return hi | lax.shift_right_logical(hi, 16)

    for c in range(s1.shape[1] // ROUTE_LANES):
        cs = slice(c * ROUTE_LANES, (c + 1) * ROUTE_LANES)
        c1, e1, r2, e2 = _route_chunk(s1[:, cs], s2[:, cs])
        c1_ref[:, cs] = duplicate_halves(c1)
        e1_ref[:, cs] = duplicate_halves(e1)
        r2_ref[:, cs] = pack_row_pairs(r2)
        e2_ref[:, cs] = pack_row_pairs(e2)


def _route(h2t, w_peer_q, subkeys_bf, l):
    tn = 512
    rows = PEER_HEADS * PEER_KEYS
    spec = pl.BlockSpec((PEER_KEYS, tn), lambda t, h: (h, t))
    pair_spec = pl.BlockSpec((PEER_KEYS // 2, tn), lambda t, h: (h, t))
    dup = jax.ShapeDtypeStruct((rows, N_TOK), jnp.int32)
    pairs = jax.ShapeDtypeStruct((rows // 2, N_TOK), jnp.int32)
    return pl.pallas_call(
        _route_kernel,
        grid=(N_TOK // tn, PEER_HEADS),
        in_specs=[
            pl.BlockSpec((D_MODEL, tn), lambda t, h: (0, t)),
            pl.BlockSpec((None, D_MODEL, PEER_QDIM), lambda t, h: (l, 0, h)),
            pl.BlockSpec((None, 2, PEER_KEYS, PEER_KEYS), lambda t, h: (l, 0, 0, 0)),
        ],
        out_specs=[spec, spec, pair_spec, pair_spec],
        out_shape=[dup, dup, pairs, pairs],
        compiler_params=_cparams(("arbitrary", "arbitrary")),
        name="peer_route",
    )(h2t, w_peer_q, subkeys_bf)


PEER_LANES = 128
PEER_BLK = 256
PEER_FIRST_TOKENS = 1024


def _peer_kernel(*refs, tm, te, convert):
    if convert:
        (ht_ref, u_ref, v_ref, c1_ref, e1_ref, r2_ref, e2_ref, o_ref, ub_ref, vtb_ref,
         at0_ref, at1_ref, wg0_ref, wg1_ref) = refs
    else:
        (ht_ref, u_ref, vt_ref, c1_ref, e1_ref, r2_ref, e2_ref, prev_hbm, o_ref,
         at0_ref, at1_ref, wg0_ref, wg1_ref) = refs
        del prev_hbm
    e = pl.program_id(1)
    nblk = te // PEER_BLK
    half = PEER_KEYS // 2

    @pl.when(e == 0)
    def _():
        o_ref[...] = jnp.zeros_like(o_ref)

    at_bufs = (at0_ref, at1_ref)
    wg_bufs = (wg0_ref, wg1_ref)

    def scores(q):
        rows = slice(q * PEER_BLK, (q + 1) * PEER_BLK)
        u = u_ref[rows, :]
        if convert:
            u = u.astype(BF)
            ub_ref[rows, :] = u
        at_bufs[q % 2][...] = jnp.dot(u, ht_ref[...], preferred_element_type=F32)

    def weights(q):
        for bb in range(PEER_BLK // PEER_KEYS):
            i = (e * nblk + q) * (PEER_BLK // PEER_KEYS) + bb
            bs = slice(bb * PEER_KEYS, (bb + 1) * PEER_KEYS)
            last_rows = [c1_ref[pl.ds(h * PEER_KEYS + i, 1), :] for h in range(PEER_HEADS)]
            e1_rows = [e1_ref[pl.ds(h * PEER_KEYS + i, 1), :] for h in range(PEER_HEADS)]
            for c in range(tm // PEER_LANES):
                cs = slice(c * PEER_LANES, (c + 1) * PEER_LANES)
                w = jnp.zeros((PEER_KEYS, PEER_LANES), BF)
                for h in range(PEER_HEADS):
                    hs = slice(h * half, (h + 1) * half)
                    last = pltpu.bitcast(jnp.broadcast_to(last_rows[h][:, cs], (half, PEER_LANES)), BF)
                    e1 = pltpu.bitcast(jnp.broadcast_to(e1_rows[h][:, cs], (half, PEER_LANES)), BF)
                    r2 = pltpu.bitcast(r2_ref[hs, cs], BF)
                    e2 = pltpu.bitcast(e2_ref[hs, cs], BF)
                    w = w + jnp.where(r2 <= last, e2, jnp.zeros_like(e2)) * e1
                a = at_bufs[q % 2][bs, cs]
                g = (0.5 * a * (1.0 + lax.erf(a * math.sqrt(0.5)))).astype(BF)
                wg_bufs[q % 2][bs, cs] = w * g

    def update(q):
        cols = slice(q * PEER_BLK, (q + 1) * PEER_BLK)
        if convert:
            vt = v_ref[cols, :].T.astype(BF)
            vtb_ref[:, cols] = vt
        else:
            vt = vt_ref[:, cols]
        o_ref[...] += jnp.dot(vt, wg_bufs[q % 2][...], preferred_element_type=F32)

    scores(0)
    for q in range(nblk):
        if q + 1 < nblk:
            scores(q + 1)
        weights(q)
        if q >= 1:
            update(q - 1)
    update(nblk - 1)


def _peer_scratch(tm):
    return [pltpu.VMEM((PEER_BLK, tm), F32), pltpu.VMEM((PEER_BLK, tm), F32),
            pltpu.VMEM((PEER_BLK, tm), BF), pltpu.VMEM((PEER_BLK, tm), BF)]


def _peer(h2t, peer_u, peer_v, routing, l):
    c1, e1, r2, e2 = routing
    rows = PEER_HEADS * PEER_KEYS
    out_sds = jax.ShapeDtypeStruct((D_MODEL, N_TOK), F32)

    tm, te = PEER_FIRST_TOKENS, 512
    once = pl.Buffered(1)
    rspec = pl.BlockSpec((rows, tm), lambda t, e: (0, 0), pipeline_mode=once)
    pspec = pl.BlockSpec((rows // 2, tm), lambda t, e: (0, 0), pipeline_mode=once)
    first, u_bf, vt_bf = pl.pallas_call(
        functools.partial(_peer_kernel, tm=tm, te=te, convert=True),
        grid=(1, PEER_EXPERTS // te),
        in_specs=[
            pl.BlockSpec((D_MODEL, tm), lambda t, e: (0, 0), pipeline_mode=once),
            pl.BlockSpec((None, te, D_MODEL), lambda t, e: (l, e, 0)),
            pl.BlockSpec((None, te, D_MODEL), lambda t, e: (l, e, 0)),
            rspec, rspec, pspec, pspec,
        ],
        out_specs=[
            pl.BlockSpec((D_MODEL, tm), lambda t, e: (0, 0), pipeline_mode=once),
            pl.BlockSpec((te, D_MODEL), lambda t, e: (e, 0)),
            pl.BlockSpec((D_MODEL, te), lambda t, e: (0, e)),
        ],
        out_shape=[out_sds, jax.ShapeDtypeStruct((PEER_EXPERTS, D_MODEL), BF),
                   jax.ShapeDtypeStruct((D_MODEL, PEER_EXPERTS), BF)],
        scratch_shapes=_peer_scratch(tm),
        compiler_params=_cparams(("arbitrary", "arbitrary")),
        name="peer_dense_first",
    )(h2t, peer_u, peer_v, c1, e1, r2, e2)

    tm, te = 512, 1024
    t0 = PEER_FIRST_TOKENS // tm
    rspec = pl.BlockSpec((rows, tm), lambda t, e: (0, t + t0))
    pspec = pl.BlockSpec((rows // 2, tm), lambda t, e: (0, t + t0))
    return pl.pallas_call(
        functools.partial(_peer_kernel, tm=tm, te=te, convert=False),
        grid=(N_TOK // tm - t0, PEER_EXPERTS // te),
        in_specs=[
            pl.BlockSpec((D_MODEL, tm), lambda t, e: (0, t + t0)),
            pl.BlockSpec((te, D_MODEL), lambda t, e: (e, 0)),
            pl.BlockSpec((D_MODEL, te), lambda t, e: (0, e)),
            rspec, rspec, pspec, pspec,
            pl.BlockSpec(memory_space=pl.ANY),
        ],
        out_specs=pl.BlockSpec((D_MODEL, tm), lambda t, e: (0, t + t0)),
        out_shape=out_sds,
        input_output_aliases={7: 0},
        scratch_shapes=_peer_scratch(tm),
        compiler_params=_cparams(("arbitrary", "arbitrary")),
        name="peer_dense",
    )(h2t, u_bf, vt_bf, c1, e1, r2, e2, first)


def _final_kernel(x_ref, pt_ref, ga_ref, g_ref, ctx_ref, dec_ref, *, tm, n_ctx_tiles):
    seg = _segment(pl.program_id(0) * tm)
    x = x_ref[...] + ga_ref[pl.ds(seg, 1), :] * pt_ref[...].T
    y = _rms(x, g_ref[...])
    is_ctx = pl.program_id(0) < n_ctx_tiles

    @pl.when(is_ctx)
    def _():
        ctx_ref[...] = y

    @pl.when(jnp.logical_not(is_ctx))
    def _():
        dec_ref[...] = y


def _final_norm(x, peer_t, mod, g_final):
    tm = 512
    n_ctx_tiles = N_CTX // tm
    return pl.pallas_call(
        functools.partial(_final_kernel, tm=tm, n_ctx_tiles=n_ctx_tiles),
        grid=(N_TOK // tm,),
        in_specs=[pl.BlockSpec((tm, D_MODEL), lambda i: (i, 0)),
                  pl.BlockSpec((D_MODEL, tm), lambda i: (0, i)),
                  pl.BlockSpec((None, 8, D_MODEL), lambda i: (DEPTH - 1, 0, 5)),
                  pl.BlockSpec((1, D_MODEL), lambda i: (0, 0))],
        out_specs=[
            pl.BlockSpec((tm, D_MODEL), lambda i: (jnp.minimum(i, n_ctx_tiles - 1), 0)),
            pl.BlockSpec((tm, D_MODEL), lambda i: (jnp.maximum(i - n_ctx_tiles, 0), 0)),
        ],
        out_shape=[jax.ShapeDtypeStruct((N_CTX, D_MODEL), F32), jax.ShapeDtypeStruct((N_DEC, D_MODEL), F32)],
        compiler_params=_cparams(("arbitrary",)),
        name="final_norm",
    )(x, peer_t, mod, g_final.reshape(1, D_MODEL))


def _rope_tables():
    t = jnp.arange(DEC_SEQ)
    half = MLA_ROPE // 4
    inv = ROPE_BASE ** (-jnp.arange(half, dtype=F32) / half)
    ang_r = (t // GRID_W).astype(F32)[:, None] * inv
    ang_c = (t % GRID_W).astype(F32)[:, None] * inv
    cos = jnp.concatenate([jnp.cos(ang_r)] * 2 + [jnp.cos(ang_c)] * 2, axis=1)
    sin = jnp.concatenate([-jnp.sin(ang_r), jnp.sin(ang_r), -jnp.sin(ang_c), jnp.sin(ang_c)], axis=1)
    return cos, sin


def _relayout_w_qb(w_qb):
    w = w_qb.reshape(DEPTH, MLA_Q_RANK, MLA_HEADS, MLA_NOPE + MLA_ROPE)
    w = jnp.pad(w, ((0, 0), (0, 0), (0, 0), (0, MLA_HEAD_PAD - MLA_NOPE - MLA_ROPE)))
    return w.reshape(DEPTH, MLA_Q_RANK, MLA_HEADS * MLA_HEAD_PAD).astype(BF)


def kernel(x_prompt, x_sample, cache_mla_ckv, cache_mla_krope, cache_diff_k, cache_diff_v, c, c_ctx,
           w_mod, b_mod, g_norm1, w_in, g_qnorm, w_qb, g_kvnorm, w_kvb, w_pool, pool_scale,
           diff_lambda, g_diffnorm, w_br_mla, w_br_pool, w_br_diff, w_out, g_norm2,
           w_peer_q, peer_subkeys, peer_u, peer_v, g_final):
    x = jnp.concatenate([x_prompt.reshape(N_CTX, D_MODEL), x_sample.reshape(N_DEC, D_MODEL)], axis=0)
    cond8 = jnp.concatenate([c_ctx[None, :], c, jnp.zeros((8 - 1 - DEC_BATCH, D_MODEL), F32)], axis=0)

    w_in_t = jnp.swapaxes(w_in, 1, 2)
    w_qb_re = _relayout_w_qb(w_qb)
    w_kvb_bf = w_kvb.astype(BF)
    w_pool_bf = w_pool.astype(BF)
    w_br_mla_bf, w_br_pool_bf, w_br_diff_bf = w_br_mla.astype(BF), w_br_pool.astype(BF), w_br_diff.astype(BF)
    w_out_bf = w_out.astype(BF)
    subkeys_bf = peer_subkeys.astype(BF)

    cache_kr_pad = jnp.pad(cache_mla_krope, ((0, 0), (0, 0), (0, 0), (0, 128 - MLA_ROPE)))
    cache_k = cache_diff_k.reshape(DEC_BATCH, DEPTH, PAST_LEN, DIFF_WIDTH)
    cache_v = cache_diff_v.reshape(DEC_BATCH, DEPTH, PAST_LEN, DIFF_WIDTH)

    cos64, sin64 = _rope_tables()
    ones, zeros = jnp.ones_like(cos64), jnp.zeros_like(cos64)
    mla_tabs = (
        jnp.concatenate([ones, ones, cos64, ones], axis=1),
        jnp.concatenate([zeros, zeros, sin64, zeros], axis=1),
        jnp.concatenate([cos64, ones], axis=1),
        jnp.concatenate([sin64, zeros], axis=1),
    )
    diff_tabs = (jnp.tile(cos64, (1, DIFF_WIDTH // DIFF_QK)), jnp.tile(sin64, (1, DIFF_WIDTH // DIFF_QK)))

    mod = _modulation(cond8, w_mod, b_mod)

    st_ckv = jnp.zeros((BATCH, DEPTH, SEQ, MLA_KV_RANK), F32)
    st_kr = jnp.zeros((BATCH, DEPTH, SEQ, MLA_ROPE), F32)
    st_k = jnp.zeros((BATCH, DEPTH, SEQ, DIFF_WIDTH), F32)
    st_v = jnp.zeros((BATCH, DEPTH, SEQ, DIFF_WIDTH), F32)
    peer_t = None
    for l in range(DEPTH):
        if l == 0:
            p_small = _in_projection(x, g_norm1, mod, w_in_t, l, 0, SMALL_WIDTH, F32)
        else:
            p_small, x = _in_projection_res(x, peer_t, g_norm1, mod, w_in_t, l, SMALL_WIDTH)
        gl = _in_projection(x, g_norm1, mod, w_in_t, l, SMALL_WIDTH, GATE_WIDTH, BF)
        mla_c, st_ckv, st_kr = _mla_ctx(p_small, g_qnorm, g_kvnorm, w_qb_re, w_kvb_bf, st_ckv, st_kr, l)
        mla_o = _mla_dec(p_small, cache_mla_ckv, cache_kr_pad, mla_tabs, g_qnorm, g_kvnorm, w_qb_re, w_kvb_bf,
                         mla_c, l)
        diff_c, st_k, st_v = _diff_ctx(p_small, diff_lambda, g_diffnorm, st_k, st_v, l)
        diff_o = _diff_dec(p_small, cache_k, cache_v, diff_tabs, diff_lambda, g_diffnorm, diff_c, l)
        pool_o = _pool(p_small, w_pool_bf, pool_scale, l)
        merged = _merge(mla_o, pool_o, diff_o, gl, w_br_mla_bf, w_br_pool_bf, w_br_diff_bf, l)
        x, h2t = _out_projection(x, merged, w_out_bf, mod, g_norm2, l)
        routing = _route(h2t, w_peer_q, subkeys_bf, l)
        peer_t = _peer(h2t, peer_u, peer_v, routing, l)

    y_ctx, y_dec = _final_norm(x, peer_t, mod, g_final)
    return (y_ctx.reshape(BATCH, SEQ, D_MODEL), y_dec.reshape(DEC_BATCH, DEC_SEQ, D_MODEL), st_ckv, st_kr,
            st_k.reshape(BATCH, DEPTH, SEQ, DIFF_HEADS, 2 * DIFF_QK),
            st_v.reshape(BATCH, DEPTH, SEQ, DIFF_HEADS, DIFF_V))
```

```python
import functools
import math

import jax
import jax.numpy as jnp
from jax import lax
from jax.experimental import pallas as pl
from jax.experimental.pallas import tpu as pltpu

BF = jnp.bfloat16
F32 = jnp.float32

D_MODEL = 2048
BATCH = 16
SEQ = 256
DEPTH = 4
DEC_BATCH = 2
DEC_SEQ = 1024
PAST_LEN = 256
GRID_W = 64
ROPE_BASE = 10000.0
EPS = 1e-6

MLA_HEADS = 8
MLA_NOPE = 128
MLA_ROPE = 64
MLA_V = 128
MLA_Q_RANK = 512
MLA_KV_RANK = 256
MLA_HEAD_PAD = 256
MLA_KV_HEAD = MLA_NOPE + MLA_V
LANES = 128
ROPE_HALF = MLA_ROPE // 4

POOL_WINDOWS = (2, 4, 8, 16)
POOL_GROUP = 128
POOL_WIDTH = POOL_GROUP * len(POOL_WINDOWS)

DIFF_HEADS = 4
DIFF_QK = 64
DIFF_V = 2 * DIFF_QK
DIFF_WIDTH = DIFF_HEADS * DIFF_V

PEER_HEADS = 8
PEER_KEYS = 128
PEER_EXPERTS = PEER_KEYS * PEER_KEYS
PEER_TOPK = 16
PEER_QDIM = 256

N_CTX = BATCH * SEQ
N_DEC = DEC_BATCH * DEC_SEQ
N_TOK = N_CTX + N_DEC
KEYS_DEC = PAST_LEN + DEC_SEQ

C_QA = 0
C_CKV = 512
C_KR = 768
KR_BLOCK = 256
C_PZ = 1024
C_DQ = 1536
C_DK = 2048
C_DV = 2560
SMALL_WIDTH = 3072
GATE_WIDTH = 3 * D_MODEL

KV_END = 512 + 320
INPROJ_TN = 512

VMEM_LIMIT = 56 * 1024 * 1024


def _cparams(sem):
    return pltpu.CompilerParams(dimension_semantics=sem, vmem_limit_bytes=VMEM_LIMIT)


def _segment(row0):
    return jnp.where(row0 < N_CTX, 0, 1 + (row0 - N_CTX) // DEC_SEQ)


def _rms(x, g):
    return x * lax.rsqrt(jnp.mean(x * x, axis=-1, keepdims=True) + EPS) * g


def _softmax_rows(s):
    m = jnp.max(s, axis=-1, keepdims=True)
    p = jnp.exp(s - m)
    return p / jnp.sum(p, axis=-1, keepdims=True)


def _dot_nt(a, b):
    return lax.dot_general(a, b, (((1,), (1,)), ((), ())), preferred_element_type=F32)


def _rope_swap(x):
    n = x.shape[-1]
    up = pltpu.roll(x, n - ROPE_HALF, 1)
    dn = pltpu.roll(x, ROPE_HALF, 1)
    lane = lax.broadcasted_iota(jnp.int32, x.shape, 1)
    return jnp.where((lane & ROPE_HALF) == 0, up, dn)


def _mod_kernel(c_ref, w_ref, b_ref, o_ref):
    c = c_ref[...]
    a = (c * jax.nn.sigmoid(c)).astype(BF)
    o_ref[...] = jnp.dot(a, w_ref[...].astype(BF), preferred_element_type=F32) + b_ref[...]


def _modulation(cond8, w_mod, b_mod):
    tn = 1024
    return pl.pallas_call(
        _mod_kernel,
        grid=(DEPTH, 6 * D_MODEL // tn),
        in_specs=[
            pl.BlockSpec((8, D_MODEL), lambda l, j: (0, 0)),
            pl.BlockSpec((None, D_MODEL, tn), lambda l, j: (l, 0, j)),
            pl.BlockSpec((None, 1, tn), lambda l, j: (l, 0, j)),
        ],
        out_specs=pl.BlockSpec((None, 8, tn), lambda l, j: (l, 0, j)),
        out_shape=jax.ShapeDtypeStruct((DEPTH, 8, 6 * D_MODEL), F32),
        compiler_params=_cparams(("arbitrary", "arbitrary")),
        name="modulation",
    )(cond8, w_mod, b_mod.reshape(DEPTH, 1, 6 * D_MODEL))


def _w_in_row(tile):
    per_tile = INPROJ_TN // 64
    return 64 * jnp.where(tile < 2, per_tile * tile, per_tile * tile - (C_PZ - KV_END) // 64)


def _inproj_tile(h_ref, w_ref, tile):
    y = _dot_nt(h_ref[...], w_ref[0].astype(BF))
    lane = lax.broadcasted_iota(jnp.int32, y.shape, 1)
    return jnp.where(lane >= jnp.where(tile == 1, KV_END - INPROJ_TN, INPROJ_TN), 0.0, y)


def _inproj_kernel(x_ref, g_ref, sh_ref, sc_ref, w_ref, o_ref, h_ref, *, tm, tile0):
    i = pl.program_id(0)

    @pl.when(pl.program_id(1) == 0)
    def _():
        seg = _segment(i * tm)
        y = _rms(x_ref[...], g_ref[...])
        h_ref[...] = (y * (1.0 + sc_ref[pl.ds(seg, 1), :]) + sh_ref[pl.ds(seg, 1), :]).astype(BF)

    o_ref[...] = _inproj_tile(h_ref, w_ref, pl.program_id(1) + tile0).astype(o_ref.dtype)


def _w_in_spec(l, tile0):
    return pl.BlockSpec((pl.Element(1), pl.Element(INPROJ_TN), pl.Element(D_MODEL)),
                        lambda i, j: (l, _w_in_row(j + tile0), 0))


def _in_projection(x, g_norm1, mod, w_in_t, l, col0, width, out_dtype, tm=1024):
    tn = INPROJ_TN
    tile0 = col0 // tn
    return pl.pallas_call(
        functools.partial(_inproj_kernel, tm=tm, tile0=tile0),
        grid=(N_TOK // tm, width // tn),
        in_specs=[
            pl.BlockSpec((tm, D_MODEL), lambda i, j: (i, 0)),
            pl.BlockSpec((None, 1, D_MODEL), lambda i, j: (l, 0, 0)),
            pl.BlockSpec((None, 8, D_MODEL), lambda i, j: (l, 0, 0)),
            pl.BlockSpec((None, 8, D_MODEL), lambda i, j: (l, 0, 1)),
            _w_in_spec(l, tile0),
        ],
        out_specs=pl.BlockSpec((tm, tn), lambda i, j: (i, j)),
        out_shape=jax.ShapeDtypeStruct((N_TOK, width), out_dtype),
        scratch_shapes=[pltpu.VMEM((tm, D_MODEL), BF)],
        compiler_params=_cparams(("arbitrary", "arbitrary")),
        name="in_projection",
    )(x, g_norm1.reshape(DEPTH, 1, D_MODEL), mod, mod, w_in_t)


def _inproj_res_kernel(x_ref, pt_ref, ga_ref, g_ref, sh_ref, sc_ref, w_ref, o_ref, xo_ref, h_ref, *, tm):
    i = pl.program_id(0)

    @pl.when(pl.program_id(1) == 0)
    def _():
        seg = _segment(i * tm)
        x = x_ref[...] + ga_ref[pl.ds(seg, 1), :] * pt_ref[...].T
        xo_ref[...] = x
        y = _rms(x, g_ref[...])
        h_ref[...] = (y * (1.0 + sc_ref[pl.ds(seg, 1), :]) + sh_ref[pl.ds(seg, 1), :]).astype(BF)

    o_ref[...] = _inproj_tile(h_ref, w_ref, pl.program_id(1)).astype(o_ref.dtype)


def _in_projection_res(x, peer_t, g_norm1, mod, w_in_t, l, width):
    tm, tn = 1024, INPROJ_TN
    once = pl.Buffered(1)
    return pl.pallas_call(
        functools.partial(_inproj_res_kernel, tm=tm),
        grid=(N_TOK // tm, width // tn),
        in_specs=[
            pl.BlockSpec((tm, D_MODEL), lambda i, j: (i, 0), pipeline_mode=once),
            pl.BlockSpec((D_MODEL, tm), lambda i, j: (0, i), pipeline_mode=once),
            pl.BlockSpec((None, 8, D_MODEL), lambda i, j: (l - 1, 0, 5)),
            pl.BlockSpec((None, 1, D_MODEL), lambda i, j: (l, 0, 0)),
            pl.BlockSpec((None, 8, D_MODEL), lambda i, j: (l, 0, 0)),
            pl.BlockSpec((None, 8, D_MODEL), lambda i, j: (l, 0, 1)),
            _w_in_spec(l, 0),
        ],
        out_specs=[
            pl.BlockSpec((tm, tn), lambda i, j: (i, j)),
            pl.BlockSpec((tm, D_MODEL), lambda i, j: (i, 0)),
        ],
        out_shape=[jax.ShapeDtypeStruct((N_TOK, width), F32), jax.ShapeDtypeStruct((N_TOK, D_MODEL), F32)],
        scratch_shapes=[pltpu.VMEM((tm, D_MODEL), BF)],
        compiler_params=_cparams(("arbitrary", "arbitrary")),
        name="in_projection_res",
    )(x, peer_t, mod, g_norm1.reshape(DEPTH, 1, D_MODEL), mod, mod, w_in_t)


def _mla_heads(q, kfun, vfun, o_ref, scale):
    for h in range(MLA_HEADS):
        s = _dot_nt(q(h), kfun(h)) * scale
        p = _softmax_rows(s).astype(BF)
        o = jnp.dot(p, vfun(h), preferred_element_type=F32)
        o_ref[:, h * MLA_V:(h + 1) * MLA_V] = o.astype(o_ref.dtype)


def _mla_ctx_kernel(qa_ref, ckv_ref, kr_ref, gq_ref, gkv_ref, wqb_ref, wkvb_ref, st_ckv_hbm, st_kr_hbm,
                    o_ref, ckv_out_ref, kr_out_ref):
    del st_ckv_hbm, st_kr_hbm
    scale = (MLA_NOPE + MLA_ROPE) ** -0.5
    qn = _rms(qa_ref[...], gq_ref[...]).astype(BF)
    q = jnp.dot(qn, wqb_ref[...], preferred_element_type=F32).astype(BF)
    ckv = _rms(ckv_ref[...], gkv_ref[...])
    ckv_out_ref[...] = ckv
    kr = kr_ref[...]
    kr_out_ref[...] = kr[:, :MLA_ROPE]
    kv = jnp.dot(ckv.astype(BF), wkvb_ref[...].astype(BF), preferred_element_type=F32).astype(BF)
    krp = kr[:, :LANES].astype(BF)

    def qh(h):
        return q[:, h * MLA_HEAD_PAD:(h + 1) * MLA_HEAD_PAD]

    def kh(h):
        return jnp.concatenate([kv[:, h * MLA_KV_HEAD:h * MLA_KV_HEAD + MLA_NOPE], krp], axis=1)

    def vh(h):
        return kv[:, h * MLA_KV_HEAD + MLA_NOPE:(h + 1) * MLA_KV_HEAD]

    _mla_heads(qh, kh, vh, o_ref, scale)


def _mla_ctx(p_small, g_qnorm, g_kvnorm, w_qb_re, w_kvb, st_ckv, st_kr, l):
    return pl.pallas_call(
        _mla_ctx_kernel,
        grid=(BATCH,),
        in_specs=[
            pl.BlockSpec((SEQ, MLA_Q_RANK), lambda b: (b, C_QA // MLA_Q_RANK)),
            pl.BlockSpec((SEQ, MLA_KV_RANK), lambda b: (b, C_CKV // MLA_KV_RANK)),
            pl.BlockSpec((SEQ, KR_BLOCK), lambda b: (b, C_KR // KR_BLOCK)),
            pl.BlockSpec((None, 1, MLA_Q_RANK), lambda b: (l, 0, 0)),
            pl.BlockSpec((None, 1, MLA_KV_RANK), lambda b: (l, 0, 0)),
            pl.BlockSpec((None, MLA_Q_RANK, MLA_HEADS * MLA_HEAD_PAD), lambda b: (l, 0, 0)),
            pl.BlockSpec((None, MLA_KV_RANK, MLA_HEADS * MLA_KV_HEAD), lambda b: (l, 0, 0)),
            pl.BlockSpec(memory_space=pl.ANY),
            pl.BlockSpec(memory_space=pl.ANY),
        ],
        out_specs=[
            pl.BlockSpec((SEQ, MLA_HEADS * MLA_V), lambda b: (b, 0)),
            pl.BlockSpec((None, None, SEQ, MLA_KV_RANK), lambda b: (b, l, 0, 0)),
            pl.BlockSpec((None, None, SEQ, MLA_ROPE), lambda b: (b, l, 0, 0)),
        ],
        out_shape=[
            jax.ShapeDtypeStruct((N_TOK, MLA_HEADS * MLA_V), BF),
            jax.ShapeDtypeStruct(st_ckv.shape, F32),
            jax.ShapeDtypeStruct(st_kr.shape, F32),
        ],
        input_output_aliases={7: 1, 8: 2},
        compiler_params=_cparams(("arbitrary",)),
        name="mla_ctx",
    )(p_small, p_small, p_small, g_qnorm.reshape(DEPTH, 1, -1), g_kvnorm.reshape(DEPTH, 1, -1),
      w_qb_re, w_kvb, st_ckv, st_kr)


def _mla_dec_kernel(qa_ref, ckv_ref, kr_ref, cckv_ref, ckr_ref, cq_ref, sq_ref, ck_ref, sk_ref,
                    gq_ref, gkv_ref, wqb_ref, wkvb_ref, mix_hbm, o_ref, kf_ref, vf_ref):
    del mix_hbm
    scale = (MLA_NOPE + MLA_ROPE) ** -0.5

    @pl.when(pl.program_id(1) == 0)
    def _():
        ckv = _rms(ckv_ref[...], gkv_ref[...])
        ckv_all = jnp.concatenate([cckv_ref[...], ckv], axis=0).astype(BF)
        kr = kr_ref[:, :LANES]
        kr_rot = kr * ck_ref[...] + _rope_swap(kr) * sk_ref[...]
        kr_all = jnp.concatenate([ckr_ref[...], kr_rot], axis=0).astype(BF)
        for h in range(MLA_HEADS):
            w_h = wkvb_ref[:, h * MLA_KV_HEAD:(h + 1) * MLA_KV_HEAD].astype(BF)
            kvh = jnp.dot(ckv_all, w_h, preferred_element_type=F32)
            kf_ref[h, :, 0:MLA_NOPE] = kvh[:, :MLA_NOPE].astype(BF)
            kf_ref[h, :, MLA_NOPE:MLA_HEAD_PAD] = kr_all
            vf_ref[h] = kvh[:, MLA_NOPE:].astype(BF)

    qn = _rms(qa_ref[...], gq_ref[...]).astype(BF)
    q = jnp.dot(qn, wqb_ref[...], preferred_element_type=F32)
    cq = cq_ref[...]
    sq = sq_ref[...]

    def qh(h):
        x = q[:, h * MLA_HEAD_PAD:(h + 1) * MLA_HEAD_PAD]
        return (x * cq + _rope_swap(x) * sq).astype(BF)

    _mla_heads(qh, lambda h: kf_ref[h], lambda h: vf_ref[h], o_ref, scale)


def _mla_dec(p_small, cache_ckv, cache_kr_pad, tabs, g_qnorm, g_kvnorm, w_qb_re, w_kvb, mix, l):
    tq = 256
    nq = DEC_SEQ // tq
    row_blk = N_CTX // DEC_SEQ
    cq, sq, ck, sk = tabs
    return pl.pallas_call(
        _mla_dec_kernel,
        grid=(DEC_BATCH, nq),
        in_specs=[
            pl.BlockSpec((tq, MLA_Q_RANK), lambda b, i: (N_CTX // tq + b * nq + i, 0)),
            pl.BlockSpec((DEC_SEQ, MLA_KV_RANK), lambda b, i: (row_blk + b, C_CKV // MLA_KV_RANK)),
            pl.BlockSpec((DEC_SEQ, KR_BLOCK), lambda b, i: (row_blk + b, C_KR // KR_BLOCK)),
            pl.BlockSpec((None, None, PAST_LEN, MLA_KV_RANK), lambda b, i: (b, l, 0, 0)),
            pl.BlockSpec((None, None, PAST_LEN, LANES), lambda b, i: (b, l, 0, 0)),
            pl.BlockSpec((tq, MLA_HEAD_PAD), lambda b, i: (i, 0)),
            pl.BlockSpec((tq, MLA_HEAD_PAD), lambda b, i: (i, 0)),
            pl.BlockSpec((DEC_SEQ, LANES), lambda b, i: (0, 0)),
            pl.BlockSpec((DEC_SEQ, LANES), lambda b, i: (0, 0)),
            pl.BlockSpec((None, 1, MLA_Q_RANK), lambda b, i: (l, 0, 0)),
            pl.BlockSpec((None, 1, MLA_KV_RANK), lambda b, i: (l, 0, 0)),
            pl.BlockSpec((None, MLA_Q_RANK, MLA_HEADS * MLA_HEAD_PAD), lambda b, i: (l, 0, 0)),
            pl.BlockSpec((None, MLA_KV_RANK, MLA_HEADS * MLA_KV_HEAD), lambda b, i: (l, 0, 0)),
            pl.BlockSpec(memory_space=pl.ANY),
        ],
        out_specs=pl.BlockSpec((tq, MLA_HEADS * MLA_V), lambda b, i: (N_CTX // tq + b * nq + i, 0)),
        out_shape=jax.ShapeDtypeStruct((N_TOK, MLA_HEADS * MLA_V), BF),
        input_output_aliases={13: 0},
        scratch_shapes=[
            pltpu.VMEM((MLA_HEADS, KEYS_DEC, MLA_HEAD_PAD), BF),
            pltpu.VMEM((MLA_HEADS, KEYS_DEC, MLA_V), BF),
        ],
        compiler_params=_cparams(("arbitrary", "arbitrary")),
        name="mla_dec",
    )(p_small, p_small, p_small, cache_ckv, cache_kr_pad, cq, sq, ck, sk,
      g_qnorm.reshape(DEPTH, 1, -1), g_kvnorm.reshape(DEPTH, 1, -1), w_qb_re, w_kvb, mix)


def _diff_lambda(lv):
    t1 = jnp.sum(lv[0:1] * lv[1:2], axis=-1, keepdims=True)
    t2 = jnp.sum(lv[2:3] * lv[3:4], axis=-1, keepdims=True)
    return jnp.exp(t1) - jnp.exp(t2)


def _diff_heads(q, kfun, vfun, lam, g, o_ref, lam_init):
    scale = DIFF_QK ** -0.5
    lane = lax.broadcasted_iota(jnp.int32, (q.shape[0], DIFF_V), 1)
    for h in range(DIFF_HEADS):
        qh = q[:, h * DIFF_V:(h + 1) * DIFF_V]
        q1 = jnp.where(lane < DIFF_QK, qh, 0.0).astype(BF)
        q2 = jnp.where(lane >= DIFF_QK, qh, 0.0).astype(BF)
        k = kfun(h)
        v = vfun(h)
        p1 = _softmax_rows(_dot_nt(q1, k) * scale).astype(BF)
        p2 = _softmax_rows(_dot_nt(q2, k) * scale).astype(BF)
        a1 = jnp.dot(p1, v, preferred_element_type=F32)
        a2 = jnp.dot(p2, v, preferred_element_type=F32)
        o = _rms(a1 - lam * a2, g) * (1.0 - lam_init)
        o_ref[:, h * DIFF_V:(h + 1) * DIFF_V] = o.astype(o_ref.dtype)


def _diff_ctx_kernel(q_ref, k_ref, v_ref, lam_ref, g_ref, st_k_hbm, st_v_hbm,
                     o_ref, k_out_ref, v_out_ref, *, lam_init):
    del st_k_hbm, st_v_hbm
    lam = _diff_lambda(lam_ref[...]) + lam_init
    k_out_ref[...] = k_ref[...]
    v_out_ref[...] = v_ref[...]
    k = k_ref[...].astype(BF)
    v = v_ref[...].astype(BF)
    _diff_heads(q_ref[...],
                lambda h: k[:, h * DIFF_V:(h + 1) * DIFF_V],
                lambda h: v[:, h * DIFF_V:(h + 1) * DIFF_V],
                lam, g_ref[...], o_ref, lam_init)


def _lam_init(l):
    return 0.8 - 0.6 * math.exp(-0.3 * l)


def _diff_ctx(p_small, diff_lambda, g_diffnorm, st_k, st_v, l):
    return pl.pallas_call(
        functools.partial(_diff_ctx_kernel, lam_init=_lam_init(l)),
        grid=(BATCH,),
        in_specs=[
            pl.BlockSpec((SEQ, DIFF_WIDTH), lambda b: (b, C_DQ // DIFF_WIDTH)),
            pl.BlockSpec((SEQ, DIFF_WIDTH), lambda b: (b, C_DK // DIFF_WIDTH)),
            pl.BlockSpec((SEQ, DIFF_WIDTH), lambda b: (b, C_DV // DIFF_WIDTH)),
            pl.BlockSpec((None, 4, DIFF_QK), lambda b: (l, 0, 0)),
            pl.BlockSpec((None, 1, DIFF_V), lambda b: (l, 0, 0)),
            pl.BlockSpec(memory_space=pl.ANY),
            pl.BlockSpec(memory_space=pl.ANY),
        ],
        out_specs=[
            pl.BlockSpec((SEQ, DIFF_WIDTH), lambda b: (b, 0)),
            pl.BlockSpec((None, None, SEQ, DIFF_WIDTH), lambda b: (b, l, 0, 0)),
            pl.BlockSpec((None, None, SEQ, DIFF_WIDTH), lambda b: (b, l, 0, 0)),
        ],
        out_shape=[
            jax.ShapeDtypeStruct((N_TOK, DIFF_WIDTH), BF),
            jax.ShapeDtypeStruct(st_k.shape, F32),
            jax.ShapeDtypeStruct(st_v.shape, F32),
        ],
        input_output_aliases={5: 1, 6: 2},
        compiler_params=_cparams(("arbitrary",)),
        name="diff_ctx",
    )(p_small, p_small, p_small, diff_lambda, g_diffnorm.reshape(DEPTH, 1, -1), st_k, st_v)


def _diff_dec_kernel(q_ref, k_ref, v_ref, ck_ref, cv_ref, cq_ref, sq_ref, cfull_ref, sfull_ref,
                     lam_ref, g_ref, mix_hbm, o_ref, kf_ref, vf_ref, *, lam_init):
    del mix_hbm

    @pl.when(pl.program_id(1) == 0)
    def _():
        k = k_ref[...]
        k_rot = k * cfull_ref[...] + _rope_swap(k) * sfull_ref[...]
        kf_ref[0:PAST_LEN, :] = ck_ref[...].astype(BF)
        kf_ref[PAST_LEN:KEYS_DEC, :] = k_rot.astype(BF)
        vf_ref[0:PAST_LEN, :] = cv_ref[...].astype(BF)
        vf_ref[PAST_LEN:KEYS_DEC, :] = v_ref[...].astype(BF)

    lam = _diff_lambda(lam_ref[...]) + lam_init
    q = q_ref[...]
    q = q * cq_ref[...] + _rope_swap(q) * sq_ref[...]
    _diff_heads(q,
                lambda h: kf_ref[:, h * DIFF_V:(h + 1) * DIFF_V],
                lambda h: vf_ref[:, h * DIFF_V:(h + 1) * DIFF_V],
                lam, g_ref[...], o_ref, lam_init)


def _diff_dec(p_small, cache_k, cache_v, tabs, diff_lambda, g_diffnorm, mix, l):
    tq = 256
    nq = DEC_SEQ // tq
    row_blk = N_CTX // DEC_SEQ
    c512, s512 = tabs
    return pl.pallas_call(
        functools.partial(_diff_dec_kernel, lam_init=_lam_init(l)),
        grid=(DEC_BATCH, nq),
        in_specs=[
            pl.BlockSpec((tq, DIFF_WIDTH), lambda b, i: (N_CTX // tq + b * nq + i, C_DQ // DIFF_WIDTH)),
            pl.BlockSpec((DEC_SEQ, DIFF_WIDTH), lambda b, i: (row_blk + b, C_DK // DIFF_WIDTH)),
            pl.BlockSpec((DEC_SEQ, DIFF_WIDTH), lambda b, i: (row_blk + b, C_DV // DIFF_WIDTH)),
            pl.BlockSpec((None, None, PAST_LEN, DIFF_WIDTH), lambda b, i: (b, l, 0, 0)),
            pl.BlockSpec((None, None, PAST_LEN, DIFF_WIDTH), lambda b, i: (b, l, 0, 0)),
            pl.BlockSpec((tq, DIFF_WIDTH), lambda b, i: (i, 0)),
            pl.BlockSpec((tq, DIFF_WIDTH), lambda b, i: (i, 0)),
            pl.BlockSpec((DEC_SEQ, DIFF_WIDTH), lambda b, i: (0, 0)),
            pl.BlockSpec((DEC_SEQ, DIFF_WIDTH), lambda b, i: (0, 0)),
            pl.BlockSpec((None, 4, DIFF_QK), lambda b, i: (l, 0, 0)),
            pl.BlockSpec((None, 1, DIFF_V), lambda b, i: (l, 0, 0)),
            pl.BlockSpec(memory_space=pl.ANY),
        ],
        out_specs=pl.BlockSpec((tq, DIFF_WIDTH), lambda b, i: (N_CTX // tq + b * nq + i, 0)),
        out_shape=jax.ShapeDtypeStruct((N_TOK, DIFF_WIDTH), BF),
        input_output_aliases={11: 0},
        scratch_shapes=[
            pltpu.VMEM((KEYS_DEC, DIFF_WIDTH), BF),
            pltpu.VMEM((KEYS_DEC, DIFF_WIDTH), BF),
        ],
        compiler_params=_cparams(("arbitrary", "arbitrary")),
        name="diff_dec",
    )(p_small, p_small, p_small, cache_k, cache_v, c512, s512, c512, s512,
      diff_lambda, g_diffnorm.reshape(DEPTH, 1, -1), mix)


def _pool_kernel(z_ref, w_ref, ps_ref, o_ref, *, tm):
    seq_m1 = jnp.where(pl.program_id(0) * tm < N_CTX, SEQ - 1, DEC_SEQ - 1)
    t = lax.broadcasted_iota(jnp.int32, (tm, POOL_GROUP), 0) & seq_m1
    for gi, w in enumerate(POOL_WINDOWS):
        z = z_ref[:, gi * POOL_GROUP:(gi + 1) * POOL_GROUP]
        acc = jnp.zeros_like(z)
        for k in range(-(w // 2), w // 2):
            zs = z if k == 0 else pltpu.roll(z, (-k) % tm, 0)
            ok = (t + k >= 0) & (t + k <= seq_m1)
            acc = acc + jnp.where(ok, zs, 0.0)
        lo = jnp.maximum(t - w // 2, 0)
        hi = jnp.minimum(t + (w - 1) // 2, seq_m1)
        d = (acc / (hi - lo + 1).astype(F32) - z).astype(BF)
        y = jnp.dot(d, w_ref[gi], preferred_element_type=F32)
        y = y * ps_ref[:, gi * POOL_GROUP:(gi + 1) * POOL_GROUP]
        o_ref[:, gi * POOL_GROUP:(gi + 1) * POOL_GROUP] = y.astype(o_ref.dtype)


def _pool(p_small, w_pool_bf, pool_scale, l):
    tm = DEC_SEQ
    return pl.pallas_call(
        functools.partial(_pool_kernel, tm=tm),
        grid=(N_TOK // tm,),
        in_specs=[
            pl.BlockSpec((tm, POOL_WIDTH), lambda i: (i, C_PZ // POOL_WIDTH)),
            pl.BlockSpec((None, len(POOL_WINDOWS), POOL_GROUP, POOL_GROUP), lambda i: (l, 0, 0, 0)),
            pl.BlockSpec((None, 1, POOL_WIDTH), lambda i: (l, 0, 0)),
        ],
        out_specs=pl.BlockSpec((tm, POOL_WIDTH), lambda i: (i, 0)),
        out_shape=jax.ShapeDtypeStruct((N_TOK, POOL_WIDTH), BF),
        compiler_params=_cparams(("arbitrary",)),
        name="pool",
    )(p_small, w_pool_bf, pool_scale.reshape(DEPTH, 1, POOL_WIDTH))


def _merge_kernel(a_ref, p_ref, d_ref, g0_ref, g1_ref, g2_ref, wa_ref, wp_ref, wd_ref, o_ref):
    def sig(r):
        return jax.nn.sigmoid(r[...].astype(F32))

    def proj(x_ref, w_ref):
        return jnp.dot(x_ref[...], w_ref[...].astype(BF), preferred_element_type=F32)

    m = sig(g0_ref) * proj(a_ref, wa_ref)
    m = m + sig(g1_ref) * proj(p_ref, wp_ref)
    m = m + sig(g2_ref) * proj(d_ref, wd_ref)
    o_ref[...] = m.astype(o_ref.dtype)


def _merge(mla_o, pool_o, diff_o, gl, w_br_mla, w_br_pool, w_br_diff, l):
    tm, tn = 1024, 512
    nj = D_MODEL // tn
    return pl.pallas_call(
        _merge_kernel,
        grid=(N_TOK // tm, nj),
        in_specs=[
            pl.BlockSpec((tm, MLA_HEADS * MLA_V), lambda i, j: (i, 0)),
            pl.BlockSpec((tm, POOL_WIDTH), lambda i, j: (i, 0)),
            pl.BlockSpec((tm, DIFF_WIDTH), lambda i, j: (i, 0)),
            pl.BlockSpec((tm, tn), lambda i, j: (i, j)),
            pl.BlockSpec((tm, tn), lambda i, j: (i, nj + j)),
            pl.BlockSpec((tm, tn), lambda i, j: (i, 2 * nj + j)),
            pl.BlockSpec((None, MLA_HEADS * MLA_V, tn), lambda i, j: (l, 0, j)),
            pl.BlockSpec((None, POOL_WIDTH, tn), lambda i, j: (l, 0, j)),
            pl.BlockSpec((None, DIFF_WIDTH, tn), lambda i, j: (l, 0, j)),
        ],
        out_specs=pl.BlockSpec((tm, tn), lambda i, j: (i, j)),
        out_shape=jax.ShapeDtypeStruct((N_TOK, D_MODEL), BF),
        compiler_params=_cparams(("arbitrary", "arbitrary")),
        name="merge",
    )(mla_o, pool_o, diff_o, gl, gl, gl, w_br_mla, w_br_pool, w_br_diff)


def _outproj_kernel(x_ref, m_ref, w_ref, ga_ref, g2_ref, sh_ref, sc_ref, xo_ref, ht_ref, wb_ref, *, tm):
    @pl.when(pl.program_id(0) == 0)
    def _():
        wb_ref[...] = w_ref[...].astype(BF)

    seg = _segment(pl.program_id(0) * tm)
    y = jnp.dot(m_ref[...], wb_ref[...], preferred_element_type=F32)
    xn = x_ref[...] + ga_ref[pl.ds(seg, 1), :] * y
    xo_ref[...] = xn
    h2 = _rms(xn, g2_ref[...]) * (1.0 + sc_ref[pl.ds(seg, 1), :]) + sh_ref[pl.ds(seg, 1), :]
    ht_ref[...] = h2.T.astype(BF)


def _out_projection(x, merged, w_out, mod, g_norm2, l):
    tm = 256
    return pl.pallas_call(
        functools.partial(_outproj_kernel, tm=tm),
        grid=(N_TOK // tm,),
        in_specs=[
            pl.BlockSpec((tm, D_MODEL), lambda i: (i, 0)),
            pl.BlockSpec((tm, D_MODEL), lambda i: (i, 0)),
            pl.BlockSpec((None, D_MODEL, D_MODEL), lambda i: (l, 0, 0), pipeline_mode=pl.Buffered(1)),
            pl.BlockSpec((None, 8, D_MODEL), lambda i: (l, 0, 2)),
            pl.BlockSpec((None, 1, D_MODEL), lambda i: (l, 0, 0)),
            pl.BlockSpec((None, 8, D_MODEL), lambda i: (l, 0, 3)),
            pl.BlockSpec((None, 8, D_MODEL), lambda i: (l, 0, 4)),
        ],
        out_specs=[
            pl.BlockSpec((tm, D_MODEL), lambda i: (i, 0)),
            pl.BlockSpec((D_MODEL, tm), lambda i: (0, i)),
        ],
        out_shape=[
            jax.ShapeDtypeStruct((N_TOK, D_MODEL), F32),
            jax.ShapeDtypeStruct((D_MODEL, N_TOK), BF),
        ],
        scratch_shapes=[pltpu.VMEM((D_MODEL, D_MODEL), BF)],
        compiler_params=_cparams(("arbitrary",)),
        name="out_projection",
    )(x, merged, w_out, mod, g_norm2.reshape(DEPTH, 1, D_MODEL), mod, mod)


ROUTE_LANES = 128
NOT_SELECTED = 127


SLAB = 8
NET_WIDTH = 16


def _sorting_network(n):
    def merge(lo, hi, r):
        step = r * 2
        if step < hi - lo:
            yield from merge(lo, hi, step)
            yield from merge(lo + r, hi, step)
            yield from ((i, i + r) for i in range(lo + r, hi - r, step))
        else:
            yield (lo, lo + r)

    def sort(lo, hi):
        if hi - lo >= 1:
            mid = lo + (hi - lo) // 2
            yield from sort(lo, mid)
            yield from sort(mid + 1, hi)
            yield from merge(lo, hi, 1)

    return tuple(sort(0, n - 1))


def _extract_top(x, n):
    slabs = [x[SLAB * v:SLAB * (v + 1)] for v in range(x.shape[0] // SLAB)]
    slabs += [None] * (NET_WIDTH - len(slabs))
    for i, j in _sorting_network(NET_WIDTH):
        hi, lo = slabs[i], slabs[j]
        if lo is None:
            continue
        if hi is None:
            slabs[i], slabs[j] = lo, None
        else:
            slabs[i], slabs[j] = jnp.maximum(hi, lo), jnp.minimum(hi, lo)
    stack = [s for s in slabs if s is not None]
    sub = lax.broadcasted_iota(jnp.int32, stack[0].shape, 0)
    vals = []
    for r in range(n):
        m = jnp.max(stack[0], axis=0, keepdims=True)
        vals.append(m)
        hit = stack[0] == m
        popped = sub == jnp.min(jnp.where(hit, sub, SLAB), axis=0, keepdims=True)
        for d in range(min(n - r - 1, len(stack))):
            below = stack[d + 1] if d + 1 < len(stack) else -jnp.inf
            stack[d] = jnp.where(popped, below, stack[d])
    return vals


def _stack_rows(rows):
    idx = lax.broadcasted_iota(jnp.int32, (len(rows), rows[0].shape[1]), 0)
    m = jnp.zeros((len(rows), rows[0].shape[1]), F32)
    for r, row in enumerate(rows):
        m = jnp.where(idx == r, row, m)
    return m


def _route_chunk(s1, s2):
    k = PEER_TOPK
    a = _extract_top(s1, k)
    b = _extract_top(s2, k)
    am, bm = _stack_rows(a), _stack_rows(b)
    slabs = [a[0] + bm[0:8], a[0] + bm[8:16]]
    slabs += [a[r] + bm[0:8] for r in range(1, 8)]
    slabs += [am[8:16] + b[0]]
    cand = jnp.concatenate(slabs, axis=0)
    tau = _extract_top(cand, k)[-1]
    sel = cand >= tau
    top = a[0] + b[0]
    cnt = jnp.where(sel, 1.0, 0.0)
    low = jnp.where(sel, cand, jnp.inf)

    def row_rows(r):
        return slice(0, 16) if r == 0 else slice(8 + 8 * r, 16 + 8 * r) if r < 8 else slice(64 + r, 65 + r)

    n_r = [jnp.sum(cnt[row_rows(r)], axis=0, keepdims=True) for r in range(k)]
    low_r = [jnp.min(low[row_rows(r)], axis=0, keepdims=True) for r in range(k)]
    total = n_r[0]
    for r in range(1, k):
        total = total + n_r[r]
    excess = total - float(k)
    for r in reversed(range(k)):
        drop = jnp.where(excess > 0.0, jnp.where(low_r[r] == tau, 1.0, 0.0), 0.0)
        n_r[r] = n_r[r] - drop
        excess = excess - drop
    dropped = total - float(k) - excess
    z = jnp.sum(jnp.where(sel, jnp.exp(cand - top), 0.0), axis=0, keepdims=True) - dropped * jnp.exp(tau - top)

    c1 = jnp.full(s1.shape, -1.0, F32)
    r2 = jnp.full(s2.shape, float(NOT_SELECTED), F32)
    for r in range(k - 1):
        c1 = jnp.where(s1 == a[r], n_r[r] - 1.0, c1)
        r2 = jnp.where(s2 == b[r], float(r), r2)
    key = lax.broadcasted_iota(jnp.int32, s1.shape, 0)

    def last_key_limit(s, vals):
        match = s == vals[k - 1]
        first = jnp.min(jnp.where(match, key, PEER_KEYS), axis=0, keepdims=True)
        return match, jnp.where(vals[k - 2] != vals[k - 1], first, PEER_KEYS)

    match1, limit1 = last_key_limit(s1, a)
    c1 = jnp.where(match1, jnp.where(key <= limit1, n_r[k - 1] - 1.0, c1), c1)
    match2, limit2 = last_key_limit(s2, b)
    r2 = jnp.where(match2, jnp.where(key <= limit2, float(k - 1), r2), r2)
    e1 = jnp.exp(s1 - a[0]) / z
    e2 = jnp.exp(s2 - b[0])
    return c1, e1, r2, e2


def _route_kernel(ht_ref, wq_ref, sk_ref, c1_ref, e1_ref, r2_ref, e2_ref):
    qt = lax.dot_general(wq_ref[...].astype(BF), ht_ref[...], (((0,), (0,)), ((), ())),
                         preferred_element_type=F32)
    s1 = jnp.dot(sk_ref[0], qt[0:PEER_KEYS].astype(BF), preferred_element_type=F32)
    s2 = jnp.dot(sk_ref[1], qt[PEER_KEYS:].astype(BF), preferred_element_type=F32)

    def pack_row_pairs(x):
        return pltpu.bitcast(x.astype(BF), jnp.int32)

    def duplicate_halves(x):
        hi = pltpu.bitcast(x.astype(BF).astype(F32), jnp.int32)
        return hi | lax.shift_right_logical(hi, 16)

    for c in range(s1.shape[1] // ROUTE_LANES):
        cs = slice(c * ROUTE_LANES, (c + 1) * ROUTE_LANES)
        c1, e1, r2, e2 = _route_chunk(s1[:, cs], s2[:, cs])
        c1_ref[:, cs] = duplicate_halves(c1)
        e1_ref[:, cs] = duplicate_halves(e1)
        r2_ref[:, cs] = pack_row_pairs(r2)
        e2_ref[:, cs] = pack_row_pairs(e2)


def _route(h2t, w_peer_q, subkeys_bf, l):
    tn = 512
    rows = PEER_HEADS * PEER_KEYS
    spec = pl.BlockSpec((PEER_KEYS, tn), lambda t, h: (h, t))
    pair_spec = pl.BlockSpec((PEER_KEYS // 2, tn), lambda t, h: (h, t))
    dup = jax.ShapeDtypeStruct((rows, N_TOK), jnp.int32)
    pairs = jax.ShapeDtypeStruct((rows // 2, N_TOK), jnp.int32)
    return pl.pallas_call(
        _route_kernel,
        grid=(N_TOK // tn, PEER_HEADS),
        in_specs=[
            pl.BlockSpec((D_MODEL, tn), lambda t, h: (0, t)),
            pl.BlockSpec((None, D_MODEL, PEER_QDIM), lambda t, h: (l, 0, h)),
            pl.BlockSpec((None, 2, PEER_KEYS, PEER_KEYS), lambda t, h: (l, 0, 0, 0)),
        ],
        out_specs=[spec, spec, pair_spec, pair_spec],
        out_shape=[dup, dup, pairs, pairs],
        compiler_params=_cparams(("arbitrary", "arbitrary")),
        name="peer_route",
    )(h2t, w_peer_q, subkeys_bf)


PEER_LANES = 128
PEER_BLK = 256
PEER_FIRST_TOKENS = 1024


def _peer_kernel(*refs, tm, te, convert):
    if convert:
        (ht_ref, u_ref, v_ref, c1_ref, e1_ref, r2_ref, e2_ref, o_ref, ub_ref, vtb_ref,
         at0_ref, at1_ref, wg0_ref, wg1_ref) = refs
    else:
        (ht_ref, u_ref, vt_ref, c1_ref, e1_ref, r2_ref, e2_ref, prev_hbm, o_ref,
         at0_ref, at1_ref, wg0_ref, wg1_ref) = refs
        del prev_hbm
    e = pl.program_id(1)
    nblk = te // PEER_BLK
    half = PEER_KEYS // 2

    @pl.when(e == 0)
    def _():
        o_ref[...] = jnp.zeros_like(o_ref)

    at_bufs = (at0_ref, at1_ref)
    wg_bufs = (wg0_ref, wg1_ref)

    def scores(q):
        rows = slice(q * PEER_BLK, (q + 1) * PEER_BLK)
        u = u_ref[rows, :]
        if convert:
            u = u.astype(BF)
            ub_ref[rows, :] = u
        at_bufs[q % 2][...] = jnp.dot(u, ht_ref[...], preferred_element_type=F32)

    def weights(q):
        for bb in range(PEER_BLK // PEER_KEYS):
            i = (e * nblk + q) * (PEER_BLK // PEER_KEYS) + bb
            bs = slice(bb * PEER_KEYS, (bb + 1) * PEER_KEYS)
            last_rows = [c1_ref[pl.ds(h * PEER_KEYS + i, 1), :] for h in range(PEER_HEADS)]
            e1_rows = [e1_ref[pl.ds(h * PEER_KEYS + i, 1), :] for h in range(PEER_HEADS)]
            for c in range(tm // PEER_LANES):
                cs = slice(c * PEER_LANES, (c + 1) * PEER_LANES)
                w = jnp.zeros((PEER_KEYS, PEER_LANES), BF)
                for h in range(PEER_HEADS):
                    hs = slice(h * half, (h + 1) * half)
                    last = pltpu.bitcast(jnp.broadcast_to(last_rows[h][:, cs], (half, PEER_LANES)), BF)
                    e1 = pltpu.bitcast(jnp.broadcast_to(e1_rows[h][:, cs], (half, PEER_LANES)), BF)
                    r2 = pltpu.bitcast(r2_ref[hs, cs], BF)
                    e2 = pltpu.bitcast(e2_ref[hs, cs], BF)
                    w = w + jnp.where(r2 <= last, e2, jnp.zeros_like(e2)) * e1
                a = at_bufs[q % 2][bs, cs]
                g = (0.5 * a * (1.0 + lax.erf(a * math.sqrt(0.5)))).astype(BF)
                wg_bufs[q % 2][bs, cs] = w * g

    def update(q):
        cols = slice(q * PEER_BLK, (q + 1) * PEER_BLK)
        if convert:
            vt = v_ref[cols, :].T.astype(BF)
            vtb_ref[:, cols] = vt
        else:
            vt = vt_ref[:, cols]
        o_ref[...] += jnp.dot(vt, wg_bufs[q % 2][...], preferred_element_type=F32)

    scores(0)
    for q in range(nblk):
        if q + 1 < nblk:
            scores(q + 1)
        weights(q)
        if q >= 1:
            update(q - 1)
    update(nblk - 1)


def _peer_scratch(tm):
    return [pltpu.VMEM((PEER_BLK, tm), F32), pltpu.VMEM((PEER_BLK, tm), F32),
            pltpu.VMEM((PEER_BLK, tm), BF), pltpu.VMEM((PEER_BLK, tm), BF)]


def _peer(h2t, peer_u, peer_v, routing, l):
    c1, e1, r2, e2 = routing
    rows = PEER_HEADS * PEER_KEYS
    out_sds = jax.ShapeDtypeStruct((D_MODEL, N_TOK), F32)

    tm, te = PEER_FIRST_TOKENS, 512
    once = pl.Buffered(1)
    rspec = pl.BlockSpec((rows, tm), lambda t, e: (0, 0), pipeline_mode=once)
    pspec = pl.BlockSpec((rows // 2, tm), lambda t, e: (0, 0), pipeline_mode=once)
    first, u_bf, vt_bf = pl.pallas_call(
        functools.partial(_peer_kernel, tm=tm, te=te, convert=True),
        grid=(1, PEER_EXPERTS // te),
        in_specs=[
            pl.BlockSpec((D_MODEL, tm), lambda t, e: (0, 0), pipeline_mode=once),
            pl.BlockSpec((None, te, D_MODEL), lambda t, e: (l, e, 0)),
            pl.BlockSpec((None, te, D_MODEL), lambda t, e: (l, e, 0)),
            rspec, rspec, pspec, pspec,
        ],
        out_specs=[
            pl.BlockSpec((D_MODEL, tm), lambda t, e: (0, 0), pipeline_mode=once),
            pl.BlockSpec((te, D_MODEL), lambda t, e: (e, 0)),
            pl.BlockSpec((D_MODEL, te), lambda t, e: (0, e)),
        ],
        out_shape=[out_sds, jax.ShapeDtypeStruct((PEER_EXPERTS, D_MODEL), BF),
                   jax.ShapeDtypeStruct((D_MODEL, PEER_EXPERTS), BF)],
        scratch_shapes=_peer_scratch(tm),
        compiler_params=_cparams(("arbitrary", "arbitrary")),
        name="peer_dense_first",
    )(h2t, peer_u, peer_v, c1, e1, r2, e2)

    tm, te = 512, 1024
    t0 = PEER_FIRST_TOKENS // tm
    rspec = pl.BlockSpec((rows, tm), lambda t, e: (0, t + t0))
    pspec = pl.BlockSpec((rows // 2, tm), lambda t, e: (0, t + t0))
    return pl.pallas_call(
        functools.partial(_peer_kernel, tm=tm, te=te, convert=False),
        grid=(N_TOK // tm - t0, PEER_EXPERTS // te),
        in_specs=[
            pl.BlockSpec((D_MODEL, tm), lambda t, e: (0, t + t0)),
            pl.BlockSpec((te, D_MODEL), lambda t, e: (e, 0)),
            pl.BlockSpec((D_MODEL, te), lambda t, e: (0, e)),
            rspec, rspec, pspec, pspec,
            pl.BlockSpec(memory_space=pl.ANY),
        ],
        out_specs=pl.BlockSpec((D_MODEL, tm), lambda t, e: (0, t + t0)),
        out_shape=out_sds,
        input_output_aliases={7: 0},
        scratch_shapes=_peer_scratch(tm),
        compiler_params=_cparams(("arbitrary", "arbitrary")),
        name="peer_dense",
    )(h2t, u_bf, vt_bf, c1, e1, r2, e2, first)


def _final_kernel(x_ref, pt_ref, ga_ref, g_ref, ctx_ref, dec_ref, *, tm, n_ctx_tiles):
    seg = _segment(pl.program_id(0) * tm)
    x = x_ref[...] + ga_ref[pl.ds(seg, 1), :] * pt_ref[...].T
    y = _rms(x, g_ref[...])
    is_ctx = pl.program_id(0) < n_ctx_tiles

    @pl.when(is_ctx)
    def _():
        ctx_ref[...] = y

    @pl.when(jnp.logical_not(is_ctx))
    def _():
        dec_ref[...] = y


def _final_norm(x, peer_t, mod, g_final):
    tm = 512
    n_ctx_tiles = N_CTX // tm
    return pl.pallas_call(
        functools.partial(_final_kernel, tm=tm, n_ctx_tiles=n_ctx_tiles),
        grid=(N_TOK // tm,),
        in_specs=[pl.BlockSpec((tm, D_MODEL), lambda i: (i, 0)),
                  pl.BlockSpec((D_MODEL, tm), lambda i: (0, i)),
                  pl.BlockSpec((None, 8, D_MODEL), lambda i: (DEPTH - 1, 0, 5)),
                  pl.BlockSpec((1, D_MODEL), lambda i: (0, 0))],
        out_specs=[
            pl.BlockSpec((tm, D_MODEL), lambda i: (jnp.minimum(i, n_ctx_tiles - 1), 0)),
            pl.BlockSpec((tm, D_MODEL), lambda i: (jnp.maximum(i - n_ctx_tiles, 0), 0)),
        ],
        out_shape=[jax.ShapeDtypeStruct((N_CTX, D_MODEL), F32), jax.ShapeDtypeStruct((N_DEC, D_MODEL), F32)],
        compiler_params=_cparams(("arbitrary",)),
        name="final_norm",
    )(x, peer_t, mod, g_final.reshape(1, D_MODEL))


def _rope_tables():
    t = jnp.arange(DEC_SEQ)
    half = MLA_ROPE // 4
    inv = ROPE_BASE ** (-jnp.arange(half, dtype=F32) / half)
    ang_r = (t // GRID_W).astype(F32)[:, None] * inv
    ang_c = (t % GRID_W).astype(F32)[:, None] * inv
    cos = jnp.concatenate([jnp.cos(ang_r)] * 2 + [jnp.cos(ang_c)] * 2, axis=1)
    sin = jnp.concatenate([-jnp.sin(ang_r), jnp.sin(ang_r), -jnp.sin(ang_c), jnp.sin(ang_c)], axis=1)
    return cos, sin


def _relayout_w_qb(w_qb):
    w = w_qb.reshape(DEPTH, MLA_Q_RANK, MLA_HEADS, MLA_NOPE + MLA_ROPE)
    w = jnp.pad(w, ((0, 0), (0, 0), (0, 0), (0, MLA_HEAD_PAD - MLA_NOPE - MLA_ROPE)))
    return w.reshape(DEPTH, MLA_Q_RANK, MLA_HEADS * MLA_HEAD_PAD).astype(BF)


def kernel(x_prompt, x_sample, cache_mla_ckv, cache_mla_krope, cache_diff_k, cache_diff_v, c, c_ctx,
           w_mod, b_mod, g_norm1, w_in, g_qnorm, w_qb, g_kvnorm, w_kvb, w_pool, pool_scale,
           diff_lambda, g_diffnorm, w_br_mla, w_br_pool, w_br_diff, w_out, g_norm2,
           w_peer_q, peer_subkeys, peer_u, peer_v, g_final):
    x = jnp.concatenate([x_prompt.reshape(N_CTX, D_MODEL), x_sample.reshape(N_DEC, D_MODEL)], axis=0)
    cond8 = jnp.concatenate([c_ctx[None, :], c, jnp.zeros((8 - 1 - DEC_BATCH, D_MODEL), F32)], axis=0)

    w_in_t = jnp.swapaxes(w_in, 1, 2)
    w_qb_re = _relayout_w_qb(w_qb)
    w_pool_bf = w_pool.astype(BF)
    subkeys_bf = peer_subkeys.astype(BF)

    cache_kr_pad = jnp.pad(cache_mla_krope, ((0, 0), (0, 0), (0, 0), (0, LANES - MLA_ROPE)))
    cache_k = cache_diff_k.reshape(DEC_BATCH, DEPTH, PAST_LEN, DIFF_WIDTH)
    cache_v = cache_diff_v.reshape(DEC_BATCH, DEPTH, PAST_LEN, DIFF_WIDTH)

    cos64, sin64 = _rope_tables()
    ones, zeros = jnp.ones_like(cos64), jnp.zeros_like(cos64)
    mla_tabs = (
        jnp.concatenate([ones, ones, cos64, ones], axis=1),
        jnp.concatenate([zeros, zeros, sin64, zeros], axis=1),
        jnp.concatenate([cos64, ones], axis=1),
        jnp.concatenate([sin64, zeros], axis=1),
    )
    diff_tabs = (jnp.tile(cos64, (1, DIFF_WIDTH // DIFF_QK)), jnp.tile(sin64, (1, DIFF_WIDTH // DIFF_QK)))

    mod = _modulation(cond8, w_mod, b_mod)

    st_ckv = jnp.zeros((BATCH, DEPTH, SEQ, MLA_KV_RANK), F32)
    st_kr = jnp.zeros((BATCH, DEPTH, SEQ, MLA_ROPE), F32)
    st_k = jnp.zeros((BATCH, DEPTH, SEQ, DIFF_WIDTH), F32)
    st_v = jnp.zeros((BATCH, DEPTH, SEQ, DIFF_WIDTH), F32)
    peer_t = None
    for l in range(DEPTH):
        if l == 0:
            p_small = _in_projection(x, g_norm1, mod, w_in_t, l, 0, SMALL_WIDTH, F32)
        else:
            p_small, x = _in_projection_res(x, peer_t, g_norm1, mod, w_in_t, l, SMALL_WIDTH)
        gl = _in_projection(x, g_norm1, mod, w_in_t, l, SMALL_WIDTH, GATE_WIDTH, BF)
        mla_c, st_ckv, st_kr = _mla_ctx(p_small, g_qnorm, g_kvnorm, w_qb_re, w_kvb, st_ckv, st_kr, l)
        mla_o = _mla_dec(p_small, cache_mla_ckv, cache_kr_pad, mla_tabs, g_qnorm, g_kvnorm, w_qb_re, w_kvb,
                         mla_c, l)
        diff_c, st_k, st_v = _diff_ctx(p_small, diff_lambda, g_diffnorm, st_k, st_v, l)
        diff_o = _diff_dec(p_small, cache_k, cache_v, diff_tabs, diff_lambda, g_diffnorm, diff_c, l)
        pool_o = _pool(p_small, w_pool_bf, pool_scale, l)
        merged = _merge(mla_o, pool_o, diff_o, gl, w_br_mla, w_br_pool, w_br_diff, l)
        x, h2t = _out_projection(x, merged, w_out, mod, g_norm2, l)
        routing = _route(h2t, w_peer_q, subkeys_bf, l)
        peer_t = _peer(h2t, peer_u, peer_v, routing, l)

    y_ctx, y_dec = _final_norm(x, peer_t, mod, g_final)
    return (y_ctx.reshape(BATCH, SEQ, D_MODEL), y_dec.reshape(DEC_BATCH, DEC_SEQ, D_MODEL), st_ckv, st_kr,
            st_k.reshape(BATCH, DEPTH, SEQ, DIFF_HEADS, 2 * DIFF_QK),
            st_v.reshape(BATCH, DEPTH, SEQ, DIFF_HEADS, DIFF_V))
```

```python
import functools
import math

import jax
import jax.numpy as jnp
from jax import lax
from jax.experimental import pallas as pl
from jax.experimental.pallas import tpu as pltpu

BF = jnp.bfloat16
F32 = jnp.float32

D_MODEL = 2048
BATCH = 16
SEQ = 256
DEPTH = 4
DEC_BATCH = 2
DEC_SEQ = 1024
PAST_LEN = 256
GRID_W = 64
ROPE_BASE = 10000.0
EPS = 1e-6

MLA_HEADS = 8
MLA_NOPE = 128
MLA_ROPE = 64
MLA_V = 128
MLA_Q_RANK = 512
MLA_KV_RANK = 256
MLA_HEAD_PAD = 256
MLA_KV_HEAD = MLA_NOPE + MLA_V
LANES = 128
ROPE_HALF = MLA_ROPE // 4

POOL_WINDOWS = (2, 4, 8, 16)
POOL_GROUP = 128
POOL_WIDTH = POOL_GROUP * len(POOL_WINDOWS)

DIFF_HEADS = 4
DIFF_QK = 64
DIFF_V = 2 * DIFF_QK
DIFF_WIDTH = DIFF_HEADS * DIFF_V

PEER_HEADS = 8
PEER_KEYS = 128
PEER_EXPERTS = PEER_KEYS * PEER_KEYS
PEER_TOPK = 16
PEER_QDIM = 256

N_CTX = BATCH * SEQ
N_DEC = DEC_BATCH * DEC_SEQ
N_TOK = N_CTX + N_DEC
KEYS_DEC = PAST_LEN + DEC_SEQ

C_QA = 0
C_CKV = 512
C_KR = 768
KR_BLOCK = 256
C_PZ = 1024
C_DQ = 1536
C_DK = 2048
C_DV = 2560
SMALL_WIDTH = 3072
GATE_WIDTH = 3 * D_MODEL

KV_END = 512 + 320
INPROJ_TN = 512

VMEM_LIMIT = 56 * 1024 * 1024


def _cparams(sem):
    return pltpu.CompilerParams(dimension_semantics=sem, vmem_limit_bytes=VMEM_LIMIT)


def _segment(row0):
    return jnp.where(row0 < N_CTX, 0, 1 + (row0 - N_CTX) // DEC_SEQ)


def _rms(x, g):
    return x * lax.rsqrt(jnp.mean(x * x, axis=-1, keepdims=True) + EPS) * g


def _softmax_rows(s):
    m = jnp.max(s, axis=-1, keepdims=True)
    p = jnp.exp(s - m)
    return p / jnp.sum(p, axis=-1, keepdims=True)


def _dot_nt(a, b):
    return lax.dot_general(a, b, (((1,), (1,)), ((), ())), preferred_element_type=F32)


def _rope_swap(x):
    n = x.shape[-1]
    up = pltpu.roll(x, n - ROPE_HALF, 1)
    dn = pltpu.roll(x, ROPE_HALF, 1)
    lane = lax.broadcasted_iota(jnp.int32, x.shape, 1)
    return jnp.where((lane & ROPE_HALF) == 0, up, dn)


def _mod_kernel(c_ref, w_ref, b_ref, o_ref):
    c = c_ref[...]
    a = (c * jax.nn.sigmoid(c)).astype(BF)
    o_ref[...] = jnp.dot(a, w_ref[...].astype(BF), preferred_element_type=F32) + b_ref[...]


def _modulation(cond8, w_mod, b_mod):
    tn = 1024
    return pl.pallas_call(
        _mod_kernel,
        grid=(DEPTH, 6 * D_MODEL // tn),
        in_specs=[
            pl.BlockSpec((8, D_MODEL), lambda l, j: (0, 0)),
            pl.BlockSpec((None, D_MODEL, tn), lambda l, j: (l, 0, j)),
            pl.BlockSpec((None, 1, tn), lambda l, j: (l, 0, j)),
        ],
        out_specs=pl.BlockSpec((None, 8, tn), lambda l, j: (l, 0, j)),
        out_shape=jax.ShapeDtypeStruct((DEPTH, 8, 6 * D_MODEL), F32),
        compiler_params=_cparams(("arbitrary", "arbitrary")),
        name="modulation",
    )(cond8, w_mod, b_mod.reshape(DEPTH, 1, 6 * D_MODEL))


def _w_in_row(tile):
    per_tile = INPROJ_TN // 64
    return 64 * jnp.where(tile < 2, per_tile * tile, per_tile * tile - (C_PZ - KV_END) // 64)


def _inproj_tile(h_ref, w_ref, tile):
    y = _dot_nt(h_ref[...], w_ref[0].astype(BF))
    lane = lax.broadcasted_iota(jnp.int32, y.shape, 1)
    return jnp.where(lane >= jnp.where(tile == 1, KV_END - INPROJ_TN, INPROJ_TN), 0.0, y)


def _inproj_kernel(x_ref, g_ref, sh_ref, sc_ref, w_ref, o_ref, h_ref, *, tm, tile0):
    i = pl.program_id(0)

    @pl.when(pl.program_id(1) == 0)
    def _():
        seg = _segment(i * tm)
        y = _rms(x_ref[...], g_ref[...])
        h_ref[...] = (y * (1.0 + sc_ref[pl.ds(seg, 1), :]) + sh_ref[pl.ds(seg, 1), :]).astype(BF)

    o_ref[...] = _inproj_tile(h_ref, w_ref, pl.program_id(1) + tile0).astype(o_ref.dtype)


def _w_in_spec(l, tile0):
    return pl.BlockSpec((pl.Element(1), pl.Element(INPROJ_TN), pl.Element(D_MODEL)),
                        lambda i, j: (l, _w_in_row(j + tile0), 0))


def _in_projection(x, g_norm1, mod, w_in_t, l, col0, width, out_dtype, tm=1024):
    tn = INPROJ_TN
    tile0 = col0 // tn
    return pl.pallas_call(
        functools.partial(_inproj_kernel, tm=tm, tile0=tile0),
        grid=(N_TOK // tm, width // tn),
        in_specs=[
            pl.BlockSpec((tm, D_MODEL), lambda i, j: (i, 0)),
            pl.BlockSpec((None, 1, D_MODEL), lambda i, j: (l, 0, 0)),
            pl.BlockSpec((None, 8, D_MODEL), lambda i, j: (l, 0, 0)),
            pl.BlockSpec((None, 8, D_MODEL), lambda i, j: (l, 0, 1)),
            _w_in_spec(l, tile0),
        ],
        out_specs=pl.BlockSpec((tm, tn), lambda i, j: (i, j)),
        out_shape=jax.ShapeDtypeStruct((N_TOK, width), out_dtype),
        scratch_shapes=[pltpu.VMEM((tm, D_MODEL), BF)],
        compiler_params=_cparams(("arbitrary", "arbitrary")),
        name="in_projection",
    )(x, g_norm1.reshape(DEPTH, 1, D_MODEL), mod, mod, w_in_t)


def _inproj_res_kernel(x_ref, pt_ref, ga_ref, g_ref, sh_ref, sc_ref, w_ref, o_ref, xo_ref, h_ref, *, tm):
    i = pl.program_id(0)

    @pl.when(pl.program_id(1) == 0)
    def _():
        seg = _segment(i * tm)
        x = x_ref[...] + ga_ref[pl.ds(seg, 1), :] * pt_ref[...].T
        xo_ref[...] = x
        y = _rms(x, g_ref[...])
        h_ref[...] = (y * (1.0 + sc_ref[pl.ds(seg, 1), :]) + sh_ref[pl.ds(seg, 1), :]).astype(BF)

    o_ref[...] = _inproj_tile(h_ref, w_ref, pl.program_id(1)).astype(o_ref.dtype)


def _in_projection_res(x, peer_t, g_norm1, mod, w_in_t, l, width):
    tm, tn = 1024, INPROJ_TN
    once = pl.Buffered(1)
    return pl.pallas_call(
        functools.partial(_inproj_res_kernel, tm=tm),
        grid=(N_TOK // tm, width // tn),
        in_specs=[
            pl.BlockSpec((tm, D_MODEL), lambda i, j: (i, 0), pipeline_mode=once),
            pl.BlockSpec((D_MODEL, tm), lambda i, j: (0, i), pipeline_mode=once),
            pl.BlockSpec((None, 8, D_MODEL), lambda i, j: (l - 1, 0, 5)),
            pl.BlockSpec((None, 1, D_MODEL), lambda i, j: (l, 0, 0)),
            pl.BlockSpec((None, 8, D_MODEL), lambda i, j: (l, 0, 0)),
            pl.BlockSpec((None, 8, D_MODEL), lambda i, j: (l, 0, 1)),
            _w_in_spec(l, 0),
        ],
        out_specs=[
            pl.BlockSpec((tm, tn), lambda i, j: (i, j)),
            pl.BlockSpec((tm, D_MODEL), lambda i, j: (i, 0)),
        ],
        out_shape=[jax.ShapeDtypeStruct((N_TOK, width), F32), jax.ShapeDtypeStruct((N_TOK, D_MODEL), F32)],
        scratch_shapes=[pltpu.VMEM((tm, D_MODEL), BF)],
        compiler_params=_cparams(("arbitrary", "arbitrary")),
        name="in_projection_res",
    )(x, peer_t, mod, g_norm1.reshape(DEPTH, 1, D_MODEL), mod, mod, w_in_t)


def _mla_heads(q, kfun, vfun, o_ref, scale):
    for h in range(MLA_HEADS):
        s = _dot_nt(q(h), kfun(h)) * scale
        p = _softmax_rows(s).astype(BF)
        o = jnp.dot(p, vfun(h), preferred_element_type=F32)
        o_ref[:, h * MLA_V:(h + 1) * MLA_V] = o.astype(o_ref.dtype)


def _mla_ctx_kernel(qa_ref, ckv_ref, kr_ref, gq_ref, gkv_ref, wqb_ref, wkvb_ref, st_ckv_hbm, st_kr_hbm,
                    o_ref, ckv_out_ref, kr_out_ref):
    del st_ckv_hbm, st_kr_hbm
    scale = (MLA_NOPE + MLA_ROPE) ** -0.5
    qn = _rms(qa_ref[...], gq_ref[...]).astype(BF)
    q = jnp.dot(qn, wqb_ref[...], preferred_element_type=F32).astype(BF)
    ckv = _rms(ckv_ref[...], gkv_ref[...])
    ckv_out_ref[...] = ckv
    kr = kr_ref[...]
    kr_out_ref[...] = kr[:, :MLA_ROPE]
    kv = jnp.dot(ckv.astype(BF), wkvb_ref[...].astype(BF), preferred_element_type=F32).astype(BF)
    krp = kr[:, :LANES].astype(BF)

    def qh(h):
        return q[:, h * MLA_HEAD_PAD:(h + 1) * MLA_HEAD_PAD]

    def kh(h):
        return jnp.concatenate([kv[:, h * MLA_KV_HEAD:h * MLA_KV_HEAD + MLA_NOPE], krp], axis=1)

    def vh(h):
        return kv[:, h * MLA_KV_HEAD + MLA_NOPE:(h + 1) * MLA_KV_HEAD]

    _mla_heads(qh, kh, vh, o_ref, scale)


def _mla_ctx(p_small, g_qnorm, g_kvnorm, w_qb_re, w_kvb, st_ckv, st_kr, l):
    return pl.pallas_call(
        _mla_ctx_kernel,
        grid=(BATCH,),
        in_specs=[
            pl.BlockSpec((SEQ, MLA_Q_RANK), lambda b: (b, C_QA // MLA_Q_RANK)),
            pl.BlockSpec((SEQ, MLA_KV_RANK), lambda b: (b, C_CKV // MLA_KV_RANK)),
            pl.BlockSpec((SEQ, KR_BLOCK), lambda b: (b, C_KR // KR_BLOCK)),
            pl.BlockSpec((None, 1, MLA_Q_RANK), lambda b: (l, 0, 0)),
            pl.BlockSpec((None, 1, MLA_KV_RANK), lambda b: (l, 0, 0)),
            pl.BlockSpec((None, MLA_Q_RANK, MLA_HEADS * MLA_HEAD_PAD), lambda b: (l, 0, 0)),
            pl.BlockSpec((None, MLA_KV_RANK, MLA_HEADS * MLA_KV_HEAD), lambda b: (l, 0, 0)),
            pl.BlockSpec(memory_space=pl.ANY),
            pl.BlockSpec(memory_space=pl.ANY),
        ],
        out_specs=[
            pl.BlockSpec((SEQ, MLA_HEADS * MLA_V), lambda b: (b, 0)),
            pl.BlockSpec((None, None, SEQ, MLA_KV_RANK), lambda b: (b, l, 0, 0)),
            pl.BlockSpec((None, None, SEQ, MLA_ROPE), lambda b: (b, l, 0, 0)),
        ],
        out_shape=[
            jax.ShapeDtypeStruct((N_TOK, MLA_HEADS * MLA_V), BF),
            jax.ShapeDtypeStruct(st_ckv.shape, F32),
            jax.ShapeDtypeStruct(st_kr.shape, F32),
        ],
        input_output_aliases={7: 1, 8: 2},
        compiler_params=_cparams(("arbitrary",)),
        name="mla_ctx",
    )(p_small, p_small, p_small, g_qnorm.reshape(DEPTH, 1, -1), g_kvnorm.reshape(DEPTH, 1, -1),
      w_qb_re, w_kvb, st_ckv, st_kr)


def _mla_dec_kernel(qa_ref, ckv_ref, kr_ref, cckv_ref, ckr_ref, cq_ref, sq_ref, ck_ref, sk_ref,
                    gq_ref, gkv_ref, wqb_ref, wkvb_ref, mix_hbm, o_ref, kf_ref, vf_ref):
    del mix_hbm
    scale = (MLA_NOPE + MLA_ROPE) ** -0.5

    @pl.when(pl.program_id(1) == 0)
    def _():
        ckv = _rms(ckv_ref[...], gkv_ref[...])
        ckv_all = jnp.concatenate([cckv_ref[...], ckv], axis=0).astype(BF)
        kr = kr_ref[:, :LANES]
        kr_rot = kr * ck_ref[...] + _rope_swap(kr) * sk_ref[...]
        kr_all = jnp.concatenate([ckr_ref[...], kr_rot], axis=0).astype(BF)
        for h in range(MLA_HEADS):
            w_h = wkvb_ref[:, h * MLA_KV_HEAD:(h + 1) * MLA_KV_HEAD].astype(BF)
            kvh = jnp.dot(ckv_all, w_h, preferred_element_type=F32)
            kf_ref[h, :, 0:MLA_NOPE] = kvh[:, :MLA_NOPE].astype(BF)
            kf_ref[h, :, MLA_NOPE:MLA_HEAD_PAD] = kr_all
            vf_ref[h] = kvh[:, MLA_NOPE:].astype(BF)

    qn = _rms(qa_ref[...], gq_ref[...]).astype(BF)
    q = jnp.dot(qn, wqb_ref[...], preferred_element_type=F32)
    cq = cq_ref[...]
    sq = sq_ref[...]

    def qh(h):
        x = q[:, h * MLA_HEAD_PAD:(h + 1) * MLA_HEAD_PAD]
        return (x * cq + _rope_swap(x) * sq).astype(BF)

    _mla_heads(qh, lambda h: kf_ref[h], lambda h: vf_ref[h], o_ref, scale)


def _mla_dec(p_small, cache_ckv, cache_kr_pad, tabs, g_qnorm, g_kvnorm, w_qb_re, w_kvb, mix, l):
    tq = 256
    nq = DEC_SEQ // tq
    row_blk = N_CTX // DEC_SEQ
    cq, sq, ck, sk = tabs
    return pl.pallas_call(
        _mla_dec_kernel,
        grid=(DEC_BATCH, nq),
        in_specs=[
            pl.BlockSpec((tq, MLA_Q_RANK), lambda b, i: (N_CTX // tq + b * nq + i, 0)),
            pl.BlockSpec((DEC_SEQ, MLA_KV_RANK), lambda b, i: (row_blk + b, C_CKV // MLA_KV_RANK)),
            pl.BlockSpec((DEC_SEQ, KR_BLOCK), lambda b, i: (row_blk + b, C_KR // KR_BLOCK)),
            pl.BlockSpec((None, None, PAST_LEN, MLA_KV_RANK), lambda b, i: (b, l, 0, 0)),
            pl.BlockSpec((None, None, PAST_LEN, LANES), lambda b, i: (b, l, 0, 0)),
            pl.BlockSpec((tq, MLA_HEAD_PAD), lambda b, i: (i, 0)),
            pl.BlockSpec((tq, MLA_HEAD_PAD), lambda b, i: (i, 0)),
            pl.BlockSpec((DEC_SEQ, LANES), lambda b, i: (0, 0)),
            pl.BlockSpec((DEC_SEQ, LANES), lambda b, i: (0, 0)),
            pl.BlockSpec((None, 1, MLA_Q_RANK), lambda b, i: (l, 0, 0)),
            pl.BlockSpec((None, 1, MLA_KV_RANK), lambda b, i: (l, 0, 0)),
            pl.BlockSpec((None, MLA_Q_RANK, MLA_HEADS * MLA_HEAD_PAD), lambda b, i: (l, 0, 0)),
            pl.BlockSpec((None, MLA_KV_RANK, MLA_HEADS * MLA_KV_HEAD), lambda b, i: (l, 0, 0)),
            pl.BlockSpec(memory_space=pl.ANY),
        ],
        out_specs=pl.BlockSpec((tq, MLA_HEADS * MLA_V), lambda b, i: (N_CTX // tq + b * nq + i, 0)),
        out_shape=jax.ShapeDtypeStruct((N_TOK, MLA_HEADS * MLA_V), BF),
        input_output_aliases={13: 0},
        scratch_shapes=[
            pltpu.VMEM((MLA_HEADS, KEYS_DEC, MLA_HEAD_PAD), BF),
            pltpu.VMEM((MLA_HEADS, KEYS_DEC, MLA_V), BF),
        ],
        compiler_params=_cparams(("arbitrary", "arbitrary")),
        name="mla_dec",
    )(p_small, p_small, p_small, cache_ckv, cache_kr_pad, cq, sq, ck, sk,
      g_qnorm.reshape(DEPTH, 1, -1), g_kvnorm.reshape(DEPTH, 1, -1), w_qb_re, w_kvb, mix)


def _diff_lambda(lv):
    t1 = jnp.sum(lv[0:1] * lv[1:2], axis=-1, keepdims=True)
    t2 = jnp.sum(lv[2:3] * lv[3:4], axis=-1, keepdims=True)
    return jnp.exp(t1) - jnp.exp(t2)


def _diff_heads(q, kfun, vfun, lam, g, o_ref, lam_init):
    scale = DIFF_QK ** -0.5
    lane = lax.broadcasted_iota(jnp.int32, (q.shape[0], DIFF_V), 1)
    for h in range(DIFF_HEADS):
        qh = q[:, h * DIFF_V:(h + 1) * DIFF_V]
        q1 = jnp.where(lane < DIFF_QK, qh, 0.0).astype(BF)
        q2 = jnp.where(lane >= DIFF_QK, qh, 0.0).astype(BF)
        k = kfun(h)
        v = vfun(h)
        p1 = _softmax_rows(_dot_nt(q1, k) * scale).astype(BF)
        p2 = _softmax_rows(_dot_nt(q2, k) * scale).astype(BF)
        a1 = jnp.dot(p1, v, preferred_element_type=F32)
        a2 = jnp.dot(p2, v, preferred_element_type=F32)
        o = _rms(a1 - lam * a2, g) * (1.0 - lam_init)
        o_ref[:, h * DIFF_V:(h + 1) * DIFF_V] = o.astype(o_ref.dtype)


def _diff_ctx_kernel(q_ref, k_ref, v_ref, lam_ref, g_ref, st_k_hbm, st_v_hbm,
                     o_ref, k_out_ref, v_out_ref, *, lam_init):
    del st_k_hbm, st_v_hbm
    lam = _diff_lambda(lam_ref[...]) + lam_init
    k_out_ref[...] = k_ref[...]
    v_out_ref[...] = v_ref[...]
    k = k_ref[...].astype(BF)
    v = v_ref[...].astype(BF)
    _diff_heads(q_ref[...],
                lambda h: k[:, h * DIFF_V:(h + 1) * DIFF_V],
                lambda h: v[:, h * DIFF_V:(h + 1) * DIFF_V],
                lam, g_ref[...], o_ref, lam_init)


def _lam_init(l):
    return 0.8 - 0.6 * math.exp(-0.3 * l)


def _diff_ctx(p_small, diff_lambda, g_diffnorm, st_k, st_v, l):
    return pl.pallas_call(
        functools.partial(_diff_ctx_kernel, lam_init=_lam_init(l)),
        grid=(BATCH,),
        in_specs=[
            pl.BlockSpec((SEQ, DIFF_WIDTH), lambda b: (b, C_DQ // DIFF_WIDTH)),
            pl.BlockSpec((SEQ, DIFF_WIDTH), lambda b: (b, C_DK // DIFF_WIDTH)),
            pl.BlockSpec((SEQ, DIFF_WIDTH), lambda b: (b, C_DV // DIFF_WIDTH)),
            pl.BlockSpec((None, 4, DIFF_QK), lambda b: (l, 0, 0)),
            pl.BlockSpec((None, 1, DIFF_V), lambda b: (l, 0, 0)),
            pl.BlockSpec(memory_space=pl.ANY),
            pl.BlockSpec(memory_space=pl.ANY),
        ],
        out_specs=[
            pl.BlockSpec((SEQ, DIFF_WIDTH), lambda b: (b, 0)),
            pl.BlockSpec((None, None, SEQ, DIFF_WIDTH), lambda b: (b, l, 0, 0)),
            pl.BlockSpec((None, None, SEQ, DIFF_WIDTH), lambda b: (b, l, 0, 0)),
        ],
        out_shape=[
            jax.ShapeDtypeStruct((N_TOK, DIFF_WIDTH), BF),
            jax.ShapeDtypeStruct(st_k.shape, F32),
            jax.ShapeDtypeStruct(st_v.shape, F32),
        ],
        input_output_aliases={5: 1, 6: 2},
        compiler_params=_cparams(("arbitrary",)),
        name="diff_ctx",
    )(p_small, p_small, p_small, diff_lambda, g_diffnorm.reshape(DEPTH, 1, -1), st_k, st_v)


def _diff_dec_kernel(q_ref, k_ref, v_ref, ck_ref, cv_ref, cq_ref, sq_ref, cfull_ref, sfull_ref,
                     lam_ref, g_ref, mix_hbm, o_ref, kf_ref, vf_ref, *, lam_init):
    del mix_hbm

    @pl.when(pl.program_id(1) == 0)
    def _():
        k = k_ref[...]
        k_rot = k * cfull_ref[...] + _rope_swap(k) * sfull_ref[...]
        kf_ref[0:PAST_LEN, :] = ck_ref[...].astype(BF)
        kf_ref[PAST_LEN:KEYS_DEC, :] = k_rot.astype(BF)
        vf_ref[0:PAST_LEN, :] = cv_ref[...].astype(BF)
        vf_ref[PAST_LEN:KEYS_DEC, :] = v_ref[...].astype(BF)

    lam = _diff_lambda(lam_ref[...]) + lam_init
    q = q_ref[...]
    q = q * cq_ref[...] + _rope_swap(q) * sq_ref[...]
    _diff_heads(q,
                lambda h: kf_ref[:, h * DIFF_V:(h + 1) * DIFF_V],
                lambda h: vf_ref[:, h * DIFF_V:(h + 1) * DIFF_V],
                lam, g_ref[...], o_ref, lam_init)


def _diff_dec(p_small, cache_k, cache_v, tabs, diff_lambda, g_diffnorm, mix, l):
    tq = 256
    nq = DEC_SEQ // tq
    row_blk = N_CTX // DEC_SEQ
    c512, s512 = tabs
    return pl.pallas_call(
        functools.partial(_diff_dec_kernel, lam_init=_lam_init(l)),
        grid=(DEC_BATCH, nq),
        in_specs=[
            pl.BlockSpec((tq, DIFF_WIDTH), lambda b, i: (N_CTX // tq + b * nq + i, C_DQ // DIFF_WIDTH)),
            pl.BlockSpec((DEC_SEQ, DIFF_WIDTH), lambda b, i: (row_blk + b, C_DK // DIFF_WIDTH)),
            pl.BlockSpec((DEC_SEQ, DIFF_WIDTH), lambda b, i: (row_blk + b, C_DV // DIFF_WIDTH)),
            pl.BlockSpec((None, None, PAST_LEN, DIFF_WIDTH), lambda b, i: (b, l, 0, 0)),
            pl.BlockSpec((None, None, PAST_LEN, DIFF_WIDTH), lambda b, i: (b, l, 0, 0)),
            pl.BlockSpec((tq, DIFF_WIDTH), lambda b, i: (i, 0)),
            pl.BlockSpec((tq, DIFF_WIDTH), lambda b, i: (i, 0)),
            pl.BlockSpec((DEC_SEQ, DIFF_WIDTH), lambda b, i: (0, 0)),
            pl.BlockSpec((DEC_SEQ, DIFF_WIDTH), lambda b, i: (0, 0)),
            pl.BlockSpec((None, 4, DIFF_QK), lambda b, i: (l, 0, 0)),
            pl.BlockSpec((None, 1, DIFF_V), lambda b, i: (l, 0, 0)),
            pl.BlockSpec(memory_space=pl.ANY),
        ],
        out_specs=pl.BlockSpec((tq, DIFF_WIDTH), lambda b, i: (N_CTX // tq + b * nq + i, 0)),
        out_shape=jax.ShapeDtypeStruct((N_TOK, DIFF_WIDTH), BF),
        input_output_aliases={11: 0},
        scratch_shapes=[
            pltpu.VMEM((KEYS_DEC, DIFF_WIDTH), BF),
            pltpu.VMEM((KEYS_DEC, DIFF_WIDTH), BF),
        ],
        compiler_params=_cparams(("arbitrary", "arbitrary")),
        name="diff_dec",
    )(p_small, p_small, p_small, cache_k, cache_v, c512, s512, c512, s512,
      diff_lambda, g_diffnorm.reshape(DEPTH, 1, -1), mix)


def _pool_kernel(z_ref, w_ref, ps_ref, o_ref, *, tm):
    seq_m1 = jnp.where(pl.program_id(0) * tm < N_CTX, SEQ - 1, DEC_SEQ - 1)
    t = lax.broadcasted_iota(jnp.int32, (tm, POOL_GROUP), 0) & seq_m1
    for gi, w in enumerate(POOL_WINDOWS):
        z = z_ref[:, gi * POOL_GROUP:(gi + 1) * POOL_GROUP]
        acc = jnp.zeros_like(z)
        for k in range(-(w // 2), w // 2):
            zs = z if k == 0 else pltpu.roll(z, (-k) % tm, 0)
            ok = (t + k >= 0) & (t + k <= seq_m1)
            acc = acc + jnp.where(ok, zs, 0.0)
        lo = jnp.maximum(t - w // 2, 0)
        hi = jnp.minimum(t + (w - 1) // 2, seq_m1)
        d = (acc / (hi - lo + 1).astype(F32) - z).astype(BF)
        y = jnp.dot(d, w_ref[gi], preferred_element_type=F32)
        y = y * ps_ref[:, gi * POOL_GROUP:(gi + 1) * POOL_GROUP]
        o_ref[:, gi * POOL_GROUP:(gi + 1) * POOL_GROUP] = y.astype(o_ref.dtype)


def _pool(p_small, w_pool_bf, pool_scale, l):
    tm = DEC_SEQ
    return pl.pallas_call(
        functools.partial(_pool_kernel, tm=tm),
        grid=(N_TOK // tm,),
        in_specs=[
            pl.BlockSpec((tm, POOL_WIDTH), lambda i: (i, C_PZ // POOL_WIDTH)),
            pl.BlockSpec((None, len(POOL_WINDOWS), POOL_GROUP, POOL_GROUP), lambda i: (l, 0, 0, 0)),
            pl.BlockSpec((None, 1, POOL_WIDTH), lambda i: (l, 0, 0)),
        ],
        out_specs=pl.BlockSpec((tm, POOL_WIDTH), lambda i: (i, 0)),
        out_shape=jax.ShapeDtypeStruct((N_TOK, POOL_WIDTH), BF),
        compiler_params=_cparams(("arbitrary",)),
        name="pool",
    )(p_small, w_pool_bf, pool_scale.reshape(DEPTH, 1, POOL_WIDTH))


def _merge_kernel(a_ref, p_ref, d_ref, g0_ref, g1_ref, g2_ref, wa_ref, wp_ref, wd_ref, o_ref):
    def sig(r):
        return jax.nn.sigmoid(r[...].astype(F32))

    def proj(x_ref, w_ref):
        return jnp.dot(x_ref[...], w_ref[...].astype(BF), preferred_element_type=F32)

    m = sig(g0_ref) * proj(a_ref, wa_ref)
    m = m + sig(g1_ref) * proj(p_ref, wp_ref)
    m = m + sig(g2_ref) * proj(d_ref, wd_ref)
    o_ref[...] = m.astype(o_ref.dtype)


def _merge(mla_o, pool_o, diff_o, gl, w_br_mla, w_br_pool, w_br_diff, l):
    tm, tn = 1024, 512
    nj = D_MODEL // tn
    return pl.pallas_call(
        _merge_kernel,
        grid=(N_TOK // tm, nj),
        in_specs=[
            pl.BlockSpec((tm, MLA_HEADS * MLA_V), lambda i, j: (i, 0)),
            pl.BlockSpec((tm, POOL_WIDTH), lambda i, j: (i, 0)),
            pl.BlockSpec((tm, DIFF_WIDTH), lambda i, j: (i, 0)),
            pl.BlockSpec((tm, tn), lambda i, j: (i, j)),
            pl.BlockSpec((tm, tn), lambda i, j: (i, nj + j)),
            pl.BlockSpec((tm, tn), lambda i, j: (i, 2 * nj + j)),
            pl.BlockSpec((None, MLA_HEADS * MLA_V, tn), lambda i, j: (l, 0, j)),
            pl.BlockSpec((None, POOL_WIDTH, tn), lambda i, j: (l, 0, j)),
            pl.BlockSpec((None, DIFF_WIDTH, tn), lambda i, j: (l, 0, j)),
        ],
        out_specs=pl.BlockSpec((tm, tn), lambda i, j: (i, j)),
        out_shape=jax.ShapeDtypeStruct((N_TOK, D_MODEL), BF),
        compiler_params=_cparams(("arbitrary", "arbitrary")),
        name="merge",
    )(mla_o, pool_o, diff_o, gl, gl, gl, w_br_mla, w_br_pool, w_br_diff)


def _outproj_kernel(x_ref, m_ref, w_ref, ga_ref, g2_ref, sh_ref, sc_ref, xo_ref, ht_ref, wb_ref, *, tm):
    @pl.when(pl.program_id(0) == 0)
    def _():
        wb_ref[...] = w_ref[...].astype(BF)

    seg = _segment(pl.program_id(0) * tm)
    y = jnp.dot(m_ref[...], wb_ref[...], preferred_element_type=F32)
    xn = x_ref[...] + ga_ref[pl.ds(seg, 1), :] * y
    xo_ref[...] = xn
    h2 = _rms(xn, g2_ref[...]) * (1.0 + sc_ref[pl.ds(seg, 1), :]) + sh_ref[pl.ds(seg, 1), :]
    ht_ref[...] = h2.T.astype(BF)


def _out_projection(x, merged, w_out, mod, g_norm2, l):
    tm = 256
    return pl.pallas_call(
        functools.partial(_outproj_kernel, tm=tm),
        grid=(N_TOK // tm,),
        in_specs=[
            pl.BlockSpec((tm, D_MODEL), lambda i: (i, 0)),
            pl.BlockSpec((tm, D_MODEL), lambda i: (i, 0)),
            pl.BlockSpec((None, D_MODEL, D_MODEL), lambda i: (l, 0, 0), pipeline_mode=pl.Buffered(1)),
            pl.BlockSpec((None, 8, D_MODEL), lambda i: (l, 0, 2)),
            pl.BlockSpec((None, 1, D_MODEL), lambda i: (l, 0, 0)),
            pl.BlockSpec((None, 8, D_MODEL), lambda i: (l, 0, 3)),
            pl.BlockSpec((None, 8, D_MODEL), lambda i: (l, 0, 4)),
        ],
        out_specs=[
            pl.BlockSpec((tm, D_MODEL), lambda i: (i, 0)),
            pl.BlockSpec((D_MODEL, tm), lambda i: (0, i)),
        ],
        out_shape=[
            jax.ShapeDtypeStruct((N_TOK, D_MODEL), F32),
            jax.ShapeDtypeStruct((D_MODEL, N_TOK), BF),
        ],
        scratch_shapes=[pltpu.VMEM((D_MODEL, D_MODEL), BF)],
        compiler_params=_cparams(("arbitrary",)),
        name="out_projection",
    )(x, merged, w_out, mod, g_norm2.reshape(DEPTH, 1, D_MODEL), mod, mod)


ROUTE_LANES = 128
NOT_SELECTED = 127


SLAB = 8
NET_WIDTH = 16


def _sorting_network(n):
    def merge(lo, hi, r):
        step = r * 2
        if step < hi - lo:
            yield from merge(lo, hi, step)
            yield from merge(lo + r, hi, step)
            yield from ((i, i + r) for i in range(lo + r, hi - r, step))
        else:
            yield (lo, lo + r)

    def sort(lo, hi):
        if hi - lo >= 1:
            mid = lo + (hi - lo) // 2
            yield from sort(lo, mid)
            yield from sort(mid + 1, hi)
            yield from merge(lo, hi, 1)

    return tuple(sort(0, n - 1))


def _extract_top(x, n):
    slabs = [x[SLAB * v:SLAB * (v + 1)] for v in range(x.shape[0] // SLAB)]
    slabs += [None] * (NET_WIDTH - len(slabs))
    for i, j in _sorting_network(NET_WIDTH):
        hi, lo = slabs[i], slabs[j]
        if lo is None:
            continue
        if hi is None:
            slabs[i], slabs[j] = lo, None
        else:
            slabs[i], slabs[j] = jnp.maximum(hi, lo), jnp.minimum(hi, lo)
    stack = [s for s in slabs if s is not None]
    sub = lax.broadcasted_iota(jnp.int32, stack[0].shape, 0)
    vals = []
    for r in range(n):
        m = jnp.max(stack[0], axis=0, keepdims=True)
        vals.append(m)
        hit = stack[0] == m
        popped = sub == jnp.min(jnp.where(hit, sub, SLAB), axis=0, keepdims=True)
        for d in range(min(n - r - 1, len(stack))):
            below = stack[d + 1] if d + 1 < len(stack) else -jnp.inf
            stack[d] = jnp.where(popped, below, stack[d])
    return vals


def _stack_rows(rows):
    idx = lax.broadcasted_iota(jnp.int32, (len(rows), rows[0].shape[1]), 0)
    m = jnp.zeros((len(rows), rows[0].shape[1]), F32)
    for r, row in enumerate(rows):
        m = jnp.where(idx == r, row, m)
    return m


def _route_chunk(s1, s2):
    k = PEER_TOPK
    a = _extract_top(s1, k)
    b = _extract_top(s2, k)
    am, bm = _stack_rows(a), _stack_rows(b)
    slabs = [a[0] + bm[0:8], a[0] + bm[8:16]]
    slabs += [a[r] + bm[0:8] for r in range(1, 8)]
    slabs += [am[8:16] + b[0]]
    cand = jnp.concatenate(slabs, axis=0)
    tau = _extract_top(cand, k)[-1]
    sel = cand >= tau
    top = a[0] + b[0]
    cnt = jnp.where(sel, 1.0, 0.0)
    low = jnp.where(sel, cand, jnp.inf)

    def row_rows(r):
        return slice(0, 16) if r == 0 else slice(8 + 8 * r, 16 + 8 * r) if r < 8 else slice(64 + r, 65 + r)

    n_r = [jnp.sum(cnt[row_rows(r)], axis=0, keepdims=True) for r in range(k)]
    low_r = [jnp.min(low[row_rows(r)], axis=0, keepdims=True) for r in range(k)]
    total = n_r[0]
    for r in range(1, k):
        total = total + n_r[r]
    excess = total - float(k)
    for r in reversed(range(k)):
        drop = jnp.where(excess > 0.0, jnp.where(low_r[r] == tau, 1.0, 0.0), 0.0)
        n_r[r] = n_r[r] - drop
        excess = excess - drop
    dropped = total - float(k) - excess
    z = jnp.sum(jnp.where(sel, jnp.exp(cand - top), 0.0), axis=0, keepdims=True) - dropped * jnp.exp(tau - top)

    c1 = jnp.full(s1.shape, -1.0, F32)
    r2 = jnp.full(s2.shape, float(NOT_SELECTED), F32)
    for r in range(k - 1):
        c1 = jnp.where(s1 == a[r], n_r[r] - 1.0, c1)
        r2 = jnp.where(s2 == b[r], float(r), r2)
    key = lax.broadcasted_iota(jnp.int32, s1.shape, 0)

    def last_key_limit(s, vals):
        match = s == vals[k - 1]
        first = jnp.min(jnp.where(match, key, PEER_KEYS), axis=0, keepdims=True)
        return match, jnp.where(vals[k - 2] != vals[k - 1], first, PEER_KEYS)

    match1, limit1 = last_key_limit(s1, a)
    c1 = jnp.where(match1, jnp.where(key <= limit1, n_r[k - 1] - 1.0, c1), c1)
    match2, limit2 = last_key_limit(s2, b)
    r2 = jnp.where(match2, jnp.where(key <= limit2, float(k - 1), r2), r2)
    e1 = jnp.exp(s1 - a[0]) / z
    e2 = jnp.exp(s2 - b[0])
    return c1, e1, r2, e2


ROUTE_HEADS = 4


def _route_kernel(ht_ref, wq_ref, sk_ref, c1_ref, e1_ref, r2_ref, e2_ref):
    def pack_row_pairs(x):
        return pltpu.bitcast(x.astype(BF), jnp.int32)

    def duplicate_halves(x):
        hi = pltpu.bitcast(x.astype(BF).astype(F32), jnp.int32)
        return hi | lax.shift_right_logical(hi, 16)

    half = PEER_KEYS // 2
    for hh in range(ROUTE_HEADS):
        wq = wq_ref[:, hh * PEER_QDIM:(hh + 1) * PEER_QDIM].astype(BF)
        qt = lax.dot_general(wq, ht_ref[...], (((0,), (0,)), ((), ())), preferred_element_type=F32)
        s1 = jnp.dot(sk_ref[0], qt[0:PEER_KEYS].astype(BF), preferred_element_type=F32)
        s2 = jnp.dot(sk_ref[1], qt[PEER_KEYS:].astype(BF), preferred_element_type=F32)
        for c in range(s1.shape[1] // ROUTE_LANES):
            cs = slice(c * ROUTE_LANES, (c + 1) * ROUTE_LANES)
            c1, e1, r2, e2 = _route_chunk(s1[:, cs], s2[:, cs])
            c1_ref[hh * PEER_KEYS:(hh + 1) * PEER_KEYS, cs] = duplicate_halves(c1)
            e1_ref[hh * PEER_KEYS:(hh + 1) * PEER_KEYS, cs] = duplicate_halves(e1)
            r2_ref[hh * half:(hh + 1) * half, cs] = pack_row_pairs(r2)
            e2_ref[hh * half:(hh + 1) * half, cs] = pack_row_pairs(e2)


def _route(h2t, w_peer_q, subkeys_bf, l):
    tn = 512
    rows = PEER_HEADS * PEER_KEYS
    spec = pl.BlockSpec((ROUTE_HEADS * PEER_KEYS, tn), lambda t, h: (h, t))
    pair_spec = pl.BlockSpec((ROUTE_HEADS * PEER_KEYS // 2, tn), lambda t, h: (h, t))
    dup = jax.ShapeDtypeStruct((rows, N_TOK), jnp.int32)
    pairs = jax.ShapeDtypeStruct((rows // 2, N_TOK), jnp.int32)
    return pl.pallas_call(
        _route_kernel,
        grid=(N_TOK // tn, PEER_HEADS // ROUTE_HEADS),
        in_specs=[
            pl.BlockSpec((D_MODEL, tn), lambda t, h: (0, t)),
            pl.BlockSpec((None, D_MODEL, ROUTE_HEADS * PEER_QDIM), lambda t, h: (l, 0, h)),
            pl.BlockSpec((None, 2, PEER_KEYS, PEER_KEYS), lambda t, h: (l, 0, 0, 0)),
        ],
        out_specs=[spec, spec, pair_spec, pair_spec],
        out_shape=[dup, dup, pairs, pairs],
        compiler_params=_cparams(("arbitrary", "arbitrary")),
        name="peer_route",
    )(h2t, w_peer_q, subkeys_bf)


PEER_LANES = 128
PEER_BLK = 256
PEER_FIRST_TOKENS = 1024


def _peer_kernel(*refs, tm, te, convert):
    if convert:
        (ht_ref, u_ref, v_ref, c1_ref, e1_ref, r2_ref, e2_ref, o_ref, ub_ref, vtb_ref,
         at0_ref, at1_ref, wg0_ref, wg1_ref) = refs
    else:
        (ht_ref, u_ref, vt_ref, c1_ref, e1_ref, r2_ref, e2_ref, prev_hbm, o_ref,
         at0_ref, at1_ref, wg0_ref, wg1_ref) = refs
        del prev_hbm
    e = pl.program_id(1)
    nblk = te // PEER_BLK
    half = PEER_KEYS // 2

    @pl.when(e == 0)
    def _():
        o_ref[...] = jnp.zeros_like(o_ref)

    at_bufs = (at0_ref, at1_ref)
    wg_bufs = (wg0_ref, wg1_ref)

    def scores(q):
        rows = slice(q * PEER_BLK, (q + 1) * PEER_BLK)
        u = u_ref[rows, :]
        if convert:
            u = u.astype(BF)
            ub_ref[rows, :] = u
        at_bufs[q % 2][...] = jnp.dot(u, ht_ref[...], preferred_element_type=F32)

    def weights(q):
        for bb in range(PEER_BLK // PEER_KEYS):
            i = (e * nblk + q) * (PEER_BLK // PEER_KEYS) + bb
            bs = slice(bb * PEER_KEYS, (bb + 1) * PEER_KEYS)
            last_rows = [c1_ref[pl.ds(h * PEER_KEYS + i, 1), :] for h in range(PEER_HEADS)]
            e1_rows = [e1_ref[pl.ds(h * PEER_KEYS + i, 1), :] for h in range(PEER_HEADS)]
            for c in range(tm // PEER_LANES):
                cs = slice(c * PEER_LANES, (c + 1) * PEER_LANES)
                w = jnp.zeros((PEER_KEYS, PEER_LANES), BF)
                for h in range(PEER_HEADS):
                    hs = slice(h * half, (h + 1) * half)
                    last = pltpu.bitcast(jnp.broadcast_to(last_rows[h][:, cs], (half, PEER_LANES)), BF)
                    e1 = pltpu.bitcast(jnp.broadcast_to(e1_rows[h][:, cs], (half, PEER_LANES)), BF)
                    r2 = pltpu.bitcast(r2_ref[hs, cs], BF)
                    e2 = pltpu.bitcast(e2_ref[hs, cs], BF)
                    w = w + jnp.where(r2 <= last, e2, jnp.zeros_like(e2)) * e1
                a = at_bufs[q % 2][bs, cs]
                g = (0.5 * a * (1.0 + lax.erf(a * math.sqrt(0.5)))).astype(BF)
                wg_bufs[q % 2][bs, cs] = w * g

    def update(q):
        cols = slice(q * PEER_BLK, (q + 1) * PEER_BLK)
        if convert:
            vt = v_ref[cols, :].T.astype(BF)
            vtb_ref[:, cols] = vt
        else:
            vt = vt_ref[:, cols]
        o_ref[...] += jnp.dot(vt, wg_bufs[q % 2][...], preferred_element_type=F32)

    scores(0)
    for q in range(nblk):
        if q + 1 < nblk:
            scores(q + 1)
        weights(q)
        if q >= 1:
            update(q - 1)
    update(nblk - 1)


def _peer_scratch(tm):
    return [pltpu.VMEM((PEER_BLK, tm), F32), pltpu.VMEM((PEER_BLK, tm), F32),
            pltpu.VMEM((PEER_BLK, tm), BF), pltpu.VMEM((PEER_BLK, tm), BF)]


def _peer(h2t, peer_u, peer_v, routing, l):
    c1, e1, r2, e2 = routing
    rows = PEER_HEADS * PEER_KEYS
    out_sds = jax.ShapeDtypeStruct((D_MODEL, N_TOK), F32)

    tm, te = PEER_FIRST_TOKENS, 512
    once = pl.Buffered(1)
    rspec = pl.BlockSpec((rows, tm), lambda t, e: (0, 0), pipeline_mode=once)
    pspec = pl.BlockSpec((rows // 2, tm), lambda t, e: (0, 0), pipeline_mode=once)
    first, u_bf, vt_bf = pl.pallas_call(
        functools.partial(_peer_kernel, tm=tm, te=te, convert=True),
        grid=(1, PEER_EXPERTS // te),
        in_specs=[
            pl.BlockSpec((D_MODEL, tm), lambda t, e: (0, 0), pipeline_mode=once),
            pl.BlockSpec((None, te, D_MODEL), lambda t, e: (l, e, 0)),
            pl.BlockSpec((None, te, D_MODEL), lambda t, e: (l, e, 0)),
            rspec, rspec, pspec, pspec,
        ],
        out_specs=[
            pl.BlockSpec((D_MODEL, tm), lambda t, e: (0, 0), pipeline_mode=once),
            pl.BlockSpec((te, D_MODEL), lambda t, e: (e, 0)),
            pl.BlockSpec((D_MODEL, te), lambda t, e: (0, e)),
        ],
        out_shape=[out_sds, jax.ShapeDtypeStruct((PEER_EXPERTS, D_MODEL), BF),
                   jax.ShapeDtypeStruct((D_MODEL, PEER_EXPERTS), BF)],
        scratch_shapes=_peer_scratch(tm),
        compiler_params=_cparams(("arbitrary", "arbitrary")),
        name="peer_dense_first",
    )(h2t, peer_u, peer_v, c1, e1, r2, e2)

    tm, te = 512, 1024
    t0 = PEER_FIRST_TOKENS // tm
    rspec = pl.BlockSpec((rows, tm), lambda t, e: (0, t + t0))
    pspec = pl.BlockSpec((rows // 2, tm), lambda t, e: (0, t + t0))
    return pl.pallas_call(
        functools.partial(_peer_kernel, tm=tm, te=te, convert=False),
        grid=(N_TOK // tm - t0, PEER_EXPERTS // te),
        in_specs=[
            pl.BlockSpec((D_MODEL, tm), lambda t, e: (0, t + t0)),
            pl.BlockSpec((te, D_MODEL), lambda t, e: (e, 0)),
            pl.BlockSpec((D_MODEL, te), lambda t, e: (0, e)),
            rspec, rspec, pspec, pspec,
            pl.BlockSpec(memory_space=pl.ANY),
        ],
        out_specs=pl.BlockSpec((D_MODEL, tm), lambda t, e: (0, t + t0)),
        out_shape=out_sds,
        input_output_aliases={7: 0},
        scratch_shapes=_peer_scratch(tm),
        compiler_params=_cparams(("arbitrary", "arbitrary")),
        name="peer_dense",
    )(h2t, u_bf, vt_bf, c1, e1, r2, e2, first)


def _final_kernel(x_ref, pt_ref, ga_ref, g_ref, ctx_ref, dec_ref, *, tm, n_ctx_tiles):
    seg = _segment(pl.program_id(0) * tm)
    x = x_ref[...] + ga_ref[pl.ds(seg, 1), :] * pt_ref[...].T
    y = _rms(x, g_ref[...])
    is_ctx = pl.program_id(0) < n_ctx_tiles

    @pl.when(is_ctx)
    def _():
        ctx_ref[...] = y

    @pl.when(jnp.logical_not(is_ctx))
    def _():
        dec_ref[...] = y


def _final_norm(x, peer_t, mod, g_final):
    tm = 512
    n_ctx_tiles = N_CTX // tm
    return pl.pallas_call(
        functools.partial(_final_kernel, tm=tm, n_ctx_tiles=n_ctx_tiles),
        grid=(N_TOK // tm,),
        in_specs=[pl.BlockSpec((tm, D_MODEL), lambda i: (i, 0)),
                  pl.BlockSpec((D_MODEL, tm), lambda i: (0, i)),
                  pl.BlockSpec((None, 8, D_MODEL), lambda i: (DEPTH - 1, 0, 5)),
                  pl.BlockSpec((1, D_MODEL), lambda i: (0, 0))],
        out_specs=[
            pl.BlockSpec((tm, D_MODEL), lambda i: (jnp.minimum(i, n_ctx_tiles - 1), 0)),
            pl.BlockSpec((tm, D_MODEL), lambda i: (jnp.maximum(i - n_ctx_tiles, 0), 0)),
        ],
        out_shape=[jax.ShapeDtypeStruct((N_CTX, D_MODEL), F32), jax.ShapeDtypeStruct((N_DEC, D_MODEL), F32)],
        compiler_params=_cparams(("arbitrary",)),
        name="final_norm",
    )(x, peer_t, mod, g_final.reshape(1, D_MODEL))


def _rope_tables():
    t = jnp.arange(DEC_SEQ)
    half = MLA_ROPE // 4
    inv = ROPE_BASE ** (-jnp.arange(half, dtype=F32) / half)
    ang_r = (t // GRID_W).astype(F32)[:, None] * inv
    ang_c = (t % GRID_W).astype(F32)[:, None] * inv
    cos = jnp.concatenate([jnp.cos(ang_r)] * 2 + [jnp.cos(ang_c)] * 2, axis=1)
    sin = jnp.concatenate([-jnp.sin(ang_r), jnp.sin(ang_r), -jnp.sin(ang_c), jnp.sin(ang_c)], axis=1)
    return cos, sin


def _relayout_w_qb(w_qb):
    w = w_qb.reshape(DEPTH, MLA_Q_RANK, MLA_HEADS, MLA_NOPE + MLA_ROPE)
    w = jnp.pad(w, ((0, 0), (0, 0), (0, 0), (0, MLA_HEAD_PAD - MLA_NOPE - MLA_ROPE)))
    return w.reshape(DEPTH, MLA_Q_RANK, MLA_HEADS * MLA_HEAD_PAD).astype(BF)


def kernel(x_prompt, x_sample, cache_mla_ckv, cache_mla_krope, cache_diff_k, cache_diff_v, c, c_ctx,
           w_mod, b_mod, g_norm1, w_in, g_qnorm, w_qb, g_kvnorm, w_kvb, w_pool, pool_scale,
           diff_lambda, g_diffnorm, w_br_mla, w_br_pool, w_br_diff, w_out, g_norm2,
           w_peer_q, peer_subkeys, peer_u, peer_v, g_final):
    x = jnp.concatenate([x_prompt.reshape(N_CTX, D_MODEL), x_sample.reshape(N_DEC, D_MODEL)], axis=0)
    cond8 = jnp.concatenate([c_ctx[None, :], c, jnp.zeros((8 - 1 - DEC_BATCH, D_MODEL), F32)], axis=0)

    w_in_t = jnp.swapaxes(w_in, 1, 2)
    w_qb_re = _relayout_w_qb(w_qb)
    w_pool_bf = w_pool.astype(BF)
    subkeys_bf = peer_subkeys.astype(BF)

    cache_kr_pad = jnp.pad(cache_mla_krope, ((0, 0), (0, 0), (0, 0), (0, LANES - MLA_ROPE)))
    cache_k = cache_diff_k.reshape(DEC_BATCH, DEPTH, PAST_LEN, DIFF_WIDTH)
    cache_v = cache_diff_v.reshape(DEC_BATCH, DEPTH, PAST_LEN, DIFF_WIDTH)

    cos64, sin64 = _rope_tables()
    ones, zeros = jnp.ones_like(cos64), jnp.zeros_like(cos64)
    mla_tabs = (
        jnp.concatenate([ones, ones, cos64, ones], axis=1),
        jnp.concatenate([zeros, zeros, sin64, zeros], axis=1),
        jnp.concatenate([cos64, ones], axis=1),
        jnp.concatenate([sin64, zeros], axis=1),
    )
    diff_tabs = (jnp.tile(cos64, (1, DIFF_WIDTH // DIFF_QK)), jnp.tile(sin64, (1, DIFF_WIDTH // DIFF_QK)))

    mod = _modulation(cond8, w_mod, b_mod)

    st_ckv = jnp.zeros((BATCH, DEPTH, SEQ, MLA_KV_RANK), F32)
    st_kr = jnp.zeros((BATCH, DEPTH, SEQ, MLA_ROPE), F32)
    st_k = jnp.zeros((BATCH, DEPTH, SEQ, DIFF_WIDTH), F32)
    st_v = jnp.zeros((BATCH, DEPTH, SEQ, DIFF_WIDTH), F32)
    peer_t = None
    for l in range(DEPTH):
        if l == 0:
            p_small = _in_projection(x, g_norm1, mod, w_in_t, l, 0, SMALL_WIDTH, F32)
        else:
            p_small, x = _in_projection_res(x, peer_t, g_norm1, mod, w_in_t, l, SMALL_WIDTH)
        gl = _in_projection(x, g_norm1, mod, w_in_t, l, SMALL_WIDTH, GATE_WIDTH, BF)
        mla_c, st_ckv, st_kr = _mla_ctx(p_small, g_qnorm, g_kvnorm, w_qb_re, w_kvb, st_ckv, st_kr, l)
        mla_o = _mla_dec(p_small, cache_mla_ckv, cache_kr_pad, mla_tabs, g_qnorm, g_kvnorm, w_qb_re, w_kvb,
                         mla_c, l)
        diff_c, st_k, st_v = _diff_ctx(p_small, diff_lambda, g_diffnorm, st_k, st_v, l)
        diff_o = _diff_dec(p_small, cache_k, cache_v, diff_tabs, diff_lambda, g_diffnorm, diff_c, l)
        pool_o = _pool(p_small, w_pool_bf, pool_scale, l)
        merged = _merge(mla_o, pool_o, diff_o, gl, w_br_mla, w_br_pool, w_br_diff, l)
        x, h2t = _out_projection(x, merged, w_out, mod, g_norm2, l)
        routing = _route(h2t, w_peer_q, subkeys_bf, l)
        peer_t = _peer(h2t, peer_u, peer_v, routing, l)

    y_ctx, y_dec = _final_norm(x, peer_t, mod, g_final)
    return (y_ctx.reshape(BATCH, SEQ, D_MODEL), y_dec.reshape(DEC_BATCH, DEC_SEQ, D_MODEL), st_ckv, st_kr,
            st_k.reshape(BATCH, DEPTH, SEQ, DIFF_HEADS, 2 * DIFF_QK),
            st_v.reshape(BATCH, DEPTH, SEQ, DIFF_HEADS, DIFF_V))
```

```python
import functools
import math

import jax
import jax.numpy as jnp
from jax import lax
from jax.experimental import pallas as pl
from jax.experimental.pallas import tpu as pltpu

BF = jnp.bfloat16
F32 = jnp.float32

D_MODEL = 2048
BATCH = 16
SEQ = 256
DEPTH = 4
DEC_BATCH = 2
DEC_SEQ = 1024
PAST_LEN = 256
GRID_W = 64
ROPE_BASE = 10000.0
EPS = 1e-6

MLA_HEADS = 8
MLA_NOPE = 128
MLA_ROPE = 64
MLA_V = 128
MLA_Q_RANK = 512
MLA_KV_RANK = 256
MLA_HEAD_PAD = 256
MLA_KV_HEAD = MLA_NOPE + MLA_V
LANES = 128
ROPE_HALF = MLA_ROPE // 4

POOL_WINDOWS = (2, 4, 8, 16)
POOL_GROUP = 128
POOL_WIDTH = POOL_GROUP * len(POOL_WINDOWS)

DIFF_HEADS = 4
DIFF_QK = 64
DIFF_V = 2 * DIFF_QK
DIFF_WIDTH = DIFF_HEADS * DIFF_V

PEER_HEADS = 8
PEER_KEYS = 128
PEER_EXPERTS = PEER_KEYS * PEER_KEYS
PEER_TOPK = 16
PEER_QDIM = 256

N_CTX = BATCH * SEQ
N_DEC = DEC_BATCH * DEC_SEQ
N_TOK = N_CTX + N_DEC
KEYS_DEC = PAST_LEN + DEC_SEQ

C_QA = 0
C_CKV = 512
C_KR = 768
KR_BLOCK = 256
C_PZ = 1024
C_DQ = 1536
C_DK = 2048
C_DV = 2560
SMALL_WIDTH = 3072
GATE_WIDTH = 3 * D_MODEL

KV_END = 512 + 320
INPROJ_TN = 512

VMEM_LIMIT = 56 * 1024 * 1024


def _cparams(sem):
    return pltpu.CompilerParams(dimension_semantics=sem, vmem_limit_bytes=VMEM_LIMIT)


def _segment(row0):
    return jnp.where(row0 < N_CTX, 0, 1 + (row0 - N_CTX) // DEC_SEQ)


def _rms(x, g):
    return x * lax.rsqrt(jnp.mean(x * x, axis=-1, keepdims=True) + EPS) * g


def _softmax_rows(s):
    m = jnp.max(s, axis=-1, keepdims=True)
    p = jnp.exp(s - m)
    return p / jnp.sum(p, axis=-1, keepdims=True)


def _dot_nt(a, b):
    return lax.dot_general(a, b, (((1,), (1,)), ((), ())), preferred_element_type=F32)


def _rope_swap(x):
    n = x.shape[-1]
    up = pltpu.roll(x, n - ROPE_HALF, 1)
    dn = pltpu.roll(x, ROPE_HALF, 1)
    lane = lax.broadcasted_iota(jnp.int32, x.shape, 1)
    return jnp.where((lane & ROPE_HALF) == 0, up, dn)


def _mod_kernel(c_ref, w_ref, b_ref, o_ref):
    c = c_ref[...]
    a = (c * jax.nn.sigmoid(c)).astype(BF)
    o_ref[...] = jnp.dot(a, w_ref[...].astype(BF), preferred_element_type=F32) + b_ref[...]


def _modulation(cond8, w_mod, b_mod):
    tn = 1024
    return pl.pallas_call(
        _mod_kernel,
        grid=(DEPTH, 6 * D_MODEL // tn),
        in_specs=[
            pl.BlockSpec((8, D_MODEL), lambda l, j: (0, 0)),
            pl.BlockSpec((None, D_MODEL, tn), lambda l, j: (l, 0, j)),
            pl.BlockSpec((None, 1, tn), lambda l, j: (l, 0, j)),
        ],
        out_specs=pl.BlockSpec((None, 8, tn), lambda l, j: (l, 0, j)),
        out_shape=jax.ShapeDtypeStruct((DEPTH, 8, 6 * D_MODEL), F32),
        compiler_params=_cparams(("arbitrary", "arbitrary")),
        name="modulation",
    )(cond8, w_mod, b_mod.reshape(DEPTH, 1, 6 * D_MODEL))


def _w_in_row(tile):
    per_tile = INPROJ_TN // 64
    return 64 * jnp.where(tile < 2, per_tile * tile, per_tile * tile - (C_PZ - KV_END) // 64)


def _inproj_tile(h_ref, w_ref, tile):
    y = _dot_nt(h_ref[...], w_ref[0].astype(BF))
    lane = lax.broadcasted_iota(jnp.int32, y.shape, 1)
    return jnp.where(lane >= jnp.where(tile == 1, KV_END - INPROJ_TN, INPROJ_TN), 0.0, y)


def _inproj_kernel(x_ref, g_ref, sh_ref, sc_ref, w_ref, o_ref, h_ref, *, tm, tile0):
    i = pl.program_id(0)

    @pl.when(pl.program_id(1) == 0)
    def _():
        seg = _segment(i * tm)
        y = _rms(x_ref[...], g_ref[...])
        h_ref[...] = (y * (1.0 + sc_ref[pl.ds(seg, 1), :]) + sh_ref[pl.ds(seg, 1), :]).astype(BF)

    o_ref[...] = _inproj_tile(h_ref, w_ref, pl.program_id(1) + tile0).astype(o_ref.dtype)


def _w_in_spec(l, tile0):
    return pl.BlockSpec((pl.Element(1), pl.Element(INPROJ_TN), pl.Element(D_MODEL)),
                        lambda i, j: (l, _w_in_row(j + tile0), 0))


def _in_projection(x, g_norm1, mod, w_in_t, l, col0, width, out_dtype, tm=1024):
    tn = INPROJ_TN
    tile0 = col0 // tn
    return pl.pallas_call(
        functools.partial(_inproj_kernel, tm=tm, tile0=tile0),
        grid=(N_TOK // tm, width // tn),
        in_specs=[
            pl.BlockSpec((tm, D_MODEL), lambda i, j: (i, 0)),
            pl.BlockSpec((None, 1, D_MODEL), lambda i, j: (l, 0, 0)),
            pl.BlockSpec((None, 8, D_MODEL), lambda i, j: (l, 0, 0)),
            pl.BlockSpec((None, 8, D_MODEL), lambda i, j: (l, 0, 1)),
            _w_in_spec(l, tile0),
        ],
        out_specs=pl.BlockSpec((tm, tn), lambda i, j: (i, j)),
        out_shape=jax.ShapeDtypeStruct((N_TOK, width), out_dtype),
        scratch_shapes=[pltpu.VMEM((tm, D_MODEL), BF)],
        compiler_params=_cparams(("arbitrary", "arbitrary")),
        name="in_projection",
    )(x, g_norm1.reshape(DEPTH, 1, D_MODEL), mod, mod, w_in_t)


def _inproj_res_kernel(x_ref, pt_ref, ga_ref, g_ref, sh_ref, sc_ref, w_ref, o_ref, xo_ref, h_ref, *, tm):
    i = pl.program_id(0)

    @pl.when(pl.program_id(1) == 0)
    def _():
        seg = _segment(i * tm)
        x = x_ref[...] + ga_ref[pl.ds(seg, 1), :] * pt_ref[...].T
        xo_ref[...] = x
        y = _rms(x, g_ref[...])
        h_ref[...] = (y * (1.0 + sc_ref[pl.ds(seg, 1), :]) + sh_ref[pl.ds(seg, 1), :]).astype(BF)

    o_ref[...] = _inproj_tile(h_ref, w_ref, pl.program_id(1)).astype(o_ref.dtype)


def _in_projection_res(x, peer_t, g_norm1, mod, w_in_t, l, width):
    tm, tn = 1024, INPROJ_TN
    once = pl.Buffered(1)
    return pl.pallas_call(
        functools.partial(_inproj_res_kernel, tm=tm),
        grid=(N_TOK // tm, width // tn),
        in_specs=[
            pl.BlockSpec((tm, D_MODEL), lambda i, j: (i, 0), pipeline_mode=once),
            pl.BlockSpec((D_MODEL, tm), lambda i, j: (0, i), pipeline_mode=once),
            pl.BlockSpec((None, 8, D_MODEL), lambda i, j: (l - 1, 0, 5)),
            pl.BlockSpec((None, 1, D_MODEL), lambda i, j: (l, 0, 0)),
            pl.BlockSpec((None, 8, D_MODEL), lambda i, j: (l, 0, 0)),
            pl.BlockSpec((None, 8, D_MODEL), lambda i, j: (l, 0, 1)),
            _w_in_spec(l, 0),
        ],
        out_specs=[
            pl.BlockSpec((tm, tn), lambda i, j: (i, j)),
            pl.BlockSpec((tm, D_MODEL), lambda i, j: (i, 0)),
        ],
        out_shape=[jax.ShapeDtypeStruct((N_TOK, width), F32), jax.ShapeDtypeStruct((N_TOK, D_MODEL), F32)],
        scratch_shapes=[pltpu.VMEM((tm, D_MODEL), BF)],
        compiler_params=_cparams(("arbitrary", "arbitrary")),
        name="in_projection_res",
    )(x, peer_t, mod, g_norm1.reshape(DEPTH, 1, D_MODEL), mod, mod, w_in_t)


def _mla_heads(q, kfun, vfun, o_ref, scale):
    for h in range(MLA_HEADS):
        s = _dot_nt(q(h), kfun(h)) * scale
        p = _softmax_rows(s).astype(BF)
        o = jnp.dot(p, vfun(h), preferred_element_type=F32)
        o_ref[:, h * MLA_V:(h + 1) * MLA_V] = o.astype(o_ref.dtype)


def _mla_ctx_kernel(qa_ref, ckv_ref, kr_ref, gq_ref, gkv_ref, wqb_ref, wkvb_ref, st_ckv_hbm, st_kr_hbm,
                    o_ref, ckv_out_ref, kr_out_ref):
    del st_ckv_hbm, st_kr_hbm
    scale = (MLA_NOPE + MLA_ROPE) ** -0.5
    qn = _rms(qa_ref[...], gq_ref[...]).astype(BF)
    q = jnp.dot(qn, wqb_ref[...], preferred_element_type=F32).astype(BF)
    ckv = _rms(ckv_ref[...], gkv_ref[...])
    ckv_out_ref[...] = ckv
    kr = kr_ref[...]
    kr_out_ref[...] = kr[:, :MLA_ROPE]
    kv = jnp.dot(ckv.astype(BF), wkvb_ref[...].astype(BF), preferred_element_type=F32).astype(BF)
    krp = kr[:, :LANES].astype(BF)

    def qh(h):
        return q[:, h * MLA_HEAD_PAD:(h + 1) * MLA_HEAD_PAD]

    def kh(h):
        return jnp.concatenate([kv[:, h * MLA_KV_HEAD:h * MLA_KV_HEAD + MLA_NOPE], krp], axis=1)

    def vh(h):
        return kv[:, h * MLA_KV_HEAD + MLA_NOPE:(h + 1) * MLA_KV_HEAD]

    _mla_heads(qh, kh, vh, o_ref, scale)


def _mla_ctx(p_small, g_qnorm, g_kvnorm, w_qb_re, w_kvb, st_ckv, st_kr, l):
    return pl.pallas_call(
        _mla_ctx_kernel,
        grid=(BATCH,),
        in_specs=[
            pl.BlockSpec((SEQ, MLA_Q_RANK), lambda b: (b, C_QA // MLA_Q_RANK)),
            pl.BlockSpec((SEQ, MLA_KV_RANK), lambda b: (b, C_CKV // MLA_KV_RANK)),
            pl.BlockSpec((SEQ, KR_BLOCK), lambda b: (b, C_KR // KR_BLOCK)),
            pl.BlockSpec((None, 1, MLA_Q_RANK), lambda b: (l, 0, 0)),
            pl.BlockSpec((None, 1, MLA_KV_RANK), lambda b: (l, 0, 0)),
            pl.BlockSpec((None, MLA_Q_RANK, MLA_HEADS * MLA_HEAD_PAD), lambda b: (l, 0, 0)),
            pl.BlockSpec((None, MLA_KV_RANK, MLA_HEADS * MLA_KV_HEAD), lambda b: (l, 0, 0)),
            pl.BlockSpec(memory_space=pl.ANY),
            pl.BlockSpec(memory_space=pl.ANY),
        ],
        out_specs=[
            pl.BlockSpec((SEQ, MLA_HEADS * MLA_V), lambda b: (b, 0)),
            pl.BlockSpec((None, None, SEQ, MLA_KV_RANK), lambda b: (b, l, 0, 0)),
            pl.BlockSpec((None, None, SEQ, MLA_ROPE), lambda b: (b, l, 0, 0)),
        ],
        out_shape=[
            jax.ShapeDtypeStruct((N_TOK, MLA_HEADS * MLA_V), BF),
            jax.ShapeDtypeStruct(st_ckv.shape, F32),
            jax.ShapeDtypeStruct(st_kr.shape, F32),
        ],
        input_output_aliases={7: 1, 8: 2},
        compiler_params=_cparams(("arbitrary",)),
        name="mla_ctx",
    )(p_small, p_small, p_small, g_qnorm.reshape(DEPTH, 1, -1), g_kvnorm.reshape(DEPTH, 1, -1),
      w_qb_re, w_kvb, st_ckv, st_kr)


def _mla_dec_kernel(qa_ref, ckv_ref, kr_ref, cckv_ref, ckr_ref, cq_ref, sq_ref, ck_ref, sk_ref,
                    gq_ref, gkv_ref, wqb_ref, wkvb_ref, mix_hbm, o_ref, kf_ref, vf_ref):
    del mix_hbm
    scale = (MLA_NOPE + MLA_ROPE) ** -0.5

    @pl.when(pl.program_id(1) == 0)
    def _():
        ckv = _rms(ckv_ref[...], gkv_ref[...])
        ckv_all = jnp.concatenate([cckv_ref[...], ckv], axis=0).astype(BF)
        kr = kr_ref[:, :LANES]
        kr_rot = kr * ck_ref[...] + _rope_swap(kr) * sk_ref[...]
        kr_all = jnp.concatenate([ckr_ref[...], kr_rot], axis=0).astype(BF)
        for h in range(MLA_HEADS):
            w_h = wkvb_ref[:, h * MLA_KV_HEAD:(h + 1) * MLA_KV_HEAD].astype(BF)
            kvh = jnp.dot(ckv_all, w_h, preferred_element_type=F32)
            kf_ref[h, :, 0:MLA_NOPE] = kvh[:, :MLA_NOPE].astype(BF)
            kf_ref[h, :, MLA_NOPE:MLA_HEAD_PAD] = kr_all
            vf_ref[h] = kvh[:, MLA_NOPE:].astype(BF)

    qn = _rms(qa_ref[...], gq_ref[...]).astype(BF)
    q = jnp.dot(qn, wqb_ref[...], preferred_element_type=F32)
    cq = cq_ref[...]
    sq = sq_ref[...]

    def qh(h):
        x = q[:, h * MLA_HEAD_PAD:(h + 1) * MLA_HEAD_PAD]
        return (x * cq + _rope_swap(x) * sq).astype(BF)

    _mla_heads(qh, lambda h: kf_ref[h], lambda h: vf_ref[h], o_ref, scale)


def _mla_dec(p_small, cache_ckv, cache_kr_pad, tabs, g_qnorm, g_kvnorm, w_qb_re, w_kvb, mix, l):
    tq = 512
    nq = DEC_SEQ // tq
    row_blk = N_CTX // DEC_SEQ
    cq, sq, ck, sk = tabs
    return pl.pallas_call(
        _mla_dec_kernel,
        grid=(DEC_BATCH, nq),
        in_specs=[
            pl.BlockSpec((tq, MLA_Q_RANK), lambda b, i: (N_CTX // tq + b * nq + i, 0)),
            pl.BlockSpec((DEC_SEQ, MLA_KV_RANK), lambda b, i: (row_blk + b, C_CKV // MLA_KV_RANK)),
            pl.BlockSpec((DEC_SEQ, KR_BLOCK), lambda b, i: (row_blk + b, C_KR // KR_BLOCK)),
            pl.BlockSpec((None, None, PAST_LEN, MLA_KV_RANK), lambda b, i: (b, l, 0, 0)),
            pl.BlockSpec((None, None, PAST_LEN, LANES), lambda b, i: (b, l, 0, 0)),
            pl.BlockSpec((tq, MLA_HEAD_PAD), lambda b, i: (i, 0)),
            pl.BlockSpec((tq, MLA_HEAD_PAD), lambda b, i: (i, 0)),
            pl.BlockSpec((DEC_SEQ, LANES), lambda b, i: (0, 0)),
            pl.BlockSpec((DEC_SEQ, LANES), lambda b, i: (0, 0)),
            pl.BlockSpec((None, 1, MLA_Q_RANK), lambda b, i: (l, 0, 0)),
            pl.BlockSpec((None, 1, MLA_KV_RANK), lambda b, i: (l, 0, 0)),
            pl.BlockSpec((None, MLA_Q_RANK, MLA_HEADS * MLA_HEAD_PAD), lambda b, i: (l, 0, 0)),
            pl.BlockSpec((None, MLA_KV_RANK, MLA_HEADS * MLA_KV_HEAD), lambda b, i: (l, 0, 0)),
            pl.BlockSpec(memory_space=pl.ANY),
        ],
        out_specs=pl.BlockSpec((tq, MLA_HEADS * MLA_V), lambda b, i: (N_CTX // tq + b * nq + i, 0)),
        out_shape=jax.ShapeDtypeStruct((N_TOK, MLA_HEADS * MLA_V), BF),
        input_output_aliases={13: 0},
        scratch_shapes=[
            pltpu.VMEM((MLA_HEADS, KEYS_DEC, MLA_HEAD_PAD), BF),
            pltpu.VMEM((MLA_HEADS, KEYS_DEC, MLA_V), BF),
        ],
        compiler_params=_cparams(("arbitrary", "arbitrary")),
        name="mla_dec",
    )(p_small, p_small, p_small, cache_ckv, cache_kr_pad, cq, sq, ck, sk,
      g_qnorm.reshape(DEPTH, 1, -1), g_kvnorm.reshape(DEPTH, 1, -1), w_qb_re, w_kvb, mix)


def _diff_lambda(lv):
    t1 = jnp.sum(lv[0:1] * lv[1:2], axis=-1, keepdims=True)
    t2 = jnp.sum(lv[2:3] * lv[3:4], axis=-1, keepdims=True)
    return jnp.exp(t1) - jnp.exp(t2)


def _diff_heads(q, kfun, vfun, lam, g, o_ref, lam_init):
    scale = DIFF_QK ** -0.5
    lane = lax.broadcasted_iota(jnp.int32, (q.shape[0], DIFF_V), 1)
    for h in range(DIFF_HEADS):
        qh = q[:, h * DIFF_V:(h + 1) * DIFF_V]
        q1 = jnp.where(lane < DIFF_QK, qh, 0.0).astype(BF)
        q2 = jnp.where(lane >= DIFF_QK, qh, 0.0).astype(BF)
        k = kfun(h)
        v = vfun(h)
        p1 = _softmax_rows(_dot_nt(q1, k) * scale).astype(BF)
        p2 = _softmax_rows(_dot_nt(q2, k) * scale).astype(BF)
        a1 = jnp.dot(p1, v, preferred_element_type=F32)
        a2 = jnp.dot(p2, v, preferred_element_type=F32)
        o = _rms(a1 - lam * a2, g) * (1.0 - lam_init)
        o_ref[:, h * DIFF_V:(h + 1) * DIFF_V] = o.astype(o_ref.dtype)


def _diff_ctx_kernel(q_ref, k_ref, v_ref, lam_ref, g_ref, st_k_hbm, st_v_hbm,
                     o_ref, k_out_ref, v_out_ref, *, lam_init):
    del st_k_hbm, st_v_hbm
    lam = _diff_lambda(lam_ref[...]) + lam_init
    k_out_ref[...] = k_ref[...]
    v_out_ref[...] = v_ref[...]
    k = k_ref[...].astype(BF)
    v = v_ref[...].astype(BF)
    _diff_heads(q_ref[...],
                lambda h: k[:, h * DIFF_V:(h + 1) * DIFF_V],
                lambda h: v[:, h * DIFF_V:(h + 1) * DIFF_V],
                lam, g_ref[...], o_ref, lam_init)


def _lam_init(l):
    return 0.8 - 0.6 * math.exp(-0.3 * l)


def _diff_ctx(p_small, diff_lambda, g_diffnorm, st_k, st_v, l):
    return pl.pallas_call(
        functools.partial(_diff_ctx_kernel, lam_init=_lam_init(l)),
        grid=(BATCH,),
        in_specs=[
            pl.BlockSpec((SEQ, DIFF_WIDTH), lambda b: (b, C_DQ // DIFF_WIDTH)),
            pl.BlockSpec((SEQ, DIFF_WIDTH), lambda b: (b, C_DK // DIFF_WIDTH)),
            pl.BlockSpec((SEQ, DIFF_WIDTH), lambda b: (b, C_DV // DIFF_WIDTH)),
            pl.BlockSpec((None, 4, DIFF_QK), lambda b: (l, 0, 0)),
            pl.BlockSpec((None, 1, DIFF_V), lambda b: (l, 0, 0)),
            pl.BlockSpec(memory_space=pl.ANY),
            pl.BlockSpec(memory_space=pl.ANY),
        ],
        out_specs=[
            pl.BlockSpec((SEQ, DIFF_WIDTH), lambda b: (b, 0)),
            pl.BlockSpec((None, None, SEQ, DIFF_WIDTH), lambda b: (b, l, 0, 0)),
            pl.BlockSpec((None, None, SEQ, DIFF_WIDTH), lambda b: (b, l, 0, 0)),
        ],
        out_shape=[
            jax.ShapeDtypeStruct((N_TOK, DIFF_WIDTH), BF),
            jax.ShapeDtypeStruct(st_k.shape, F32),
            jax.ShapeDtypeStruct(st_v.shape, F32),
        ],
        input_output_aliases={5: 1, 6: 2},
        compiler_params=_cparams(("arbitrary",)),
        name="diff_ctx",
    )(p_small, p_small, p_small, diff_lambda, g_diffnorm.reshape(DEPTH, 1, -1), st_k, st_v)


def _diff_dec_kernel(q_ref, k_ref, v_ref, ck_ref, cv_ref, cq_ref, sq_ref, cfull_ref, sfull_ref,
                     lam_ref, g_ref, mix_hbm, o_ref, kf_ref, vf_ref, *, lam_init):
    del mix_hbm

    @pl.when(pl.program_id(1) == 0)
    def _():
        k = k_ref[...]
        k_rot = k * cfull_ref[...] + _rope_swap(k) * sfull_ref[...]
        kf_ref[0:PAST_LEN, :] = ck_ref[...].astype(BF)
        kf_ref[PAST_LEN:KEYS_DEC, :] = k_rot.astype(BF)
        vf_ref[0:PAST_LEN, :] = cv_ref[...].astype(BF)
        vf_ref[PAST_LEN:KEYS_DEC, :] = v_ref[...].astype(BF)

    lam = _diff_lambda(lam_ref[...]) + lam_init
    q = q_ref[...]
    q = q * cq_ref[...] + _rope_swap(q) * sq_ref[...]
    _diff_heads(q,
                lambda h: kf_ref[:, h * DIFF_V:(h + 1) * DIFF_V],
                lambda h: vf_ref[:, h * DIFF_V:(h + 1) * DIFF_V],
                lam, g_ref[...], o_ref, lam_init)


def _diff_dec(p_small, cache_k, cache_v, tabs, diff_lambda, g_diffnorm, mix, l):
    tq = 512
    nq = DEC_SEQ // tq
    row_blk = N_CTX // DEC_SEQ
    c512, s512 = tabs
    return pl.pallas_call(
        functools.partial(_diff_dec_kernel, lam_init=_lam_init(l)),
        grid=(DEC_BATCH, nq),
        in_specs=[
            pl.BlockSpec((tq, DIFF_WIDTH), lambda b, i: (N_CTX // tq + b * nq + i, C_DQ // DIFF_WIDTH)),
            pl.BlockSpec((DEC_SEQ, DIFF_WIDTH), lambda b, i: (row_blk + b, C_DK // DIFF_WIDTH)),
            pl.BlockSpec((DEC_SEQ, DIFF_WIDTH), lambda b, i: (row_blk + b, C_DV // DIFF_WIDTH)),
            pl.BlockSpec((None, None, PAST_LEN, DIFF_WIDTH), lambda b, i: (b, l, 0, 0)),
            pl.BlockSpec((None, None, PAST_LEN, DIFF_WIDTH), lambda b, i: (b, l, 0, 0)),
            pl.BlockSpec((tq, DIFF_WIDTH), lambda b, i: (i, 0)),
            pl.BlockSpec((tq, DIFF_WIDTH), lambda b, i: (i, 0)),
            pl.BlockSpec((DEC_SEQ, DIFF_WIDTH), lambda b, i: (0, 0)),
            pl.BlockSpec((DEC_SEQ, DIFF_WIDTH), lambda b, i: (0, 0)),
            pl.BlockSpec((None, 4, DIFF_QK), lambda b, i: (l, 0, 0)),
            pl.BlockSpec((None, 1, DIFF_V), lambda b, i: (l, 0, 0)),
            pl.BlockSpec(memory_space=pl.ANY),
        ],
        out_specs=pl.BlockSpec((tq, DIFF_WIDTH), lambda b, i: (N_CTX // tq + b * nq + i, 0)),
        out_shape=jax.ShapeDtypeStruct((N_TOK, DIFF_WIDTH), BF),
        input_output_aliases={11: 0},
        scratch_shapes=[
            pltpu.VMEM((KEYS_DEC, DIFF_WIDTH), BF),
            pltpu.VMEM((KEYS_DEC, DIFF_WIDTH), BF),
        ],
        compiler_params=_cparams(("arbitrary", "arbitrary")),
        name="diff_dec",
    )(p_small, p_small, p_small, cache_k, cache_v, c512, s512, c512, s512,
      diff_lambda, g_diffnorm.reshape(DEPTH, 1, -1), mix)


def _pool_kernel(z_ref, w_ref, ps_ref, o_ref, *, tm):
    seq_m1 = jnp.where(pl.program_id(0) * tm < N_CTX, SEQ - 1, DEC_SEQ - 1)
    t = lax.broadcasted_iota(jnp.int32, (tm, POOL_GROUP), 0) & seq_m1
    for gi, w in enumerate(POOL_WINDOWS):
        z = z_ref[:, gi * POOL_GROUP:(gi + 1) * POOL_GROUP]
        acc = jnp.zeros_like(z)
        for k in range(-(w // 2), w // 2):
            zs = z if k == 0 else pltpu.roll(z, (-k) % tm, 0)
            ok = (t + k >= 0) & (t + k <= seq_m1)
            acc = acc + jnp.where(ok, zs, 0.0)
        lo = jnp.maximum(t - w // 2, 0)
        hi = jnp.minimum(t + (w - 1) // 2, seq_m1)
        d = (acc / (hi - lo + 1).astype(F32) - z).astype(BF)
        y = jnp.dot(d, w_ref[gi], preferred_element_type=F32)
        y = y * ps_ref[:, gi * POOL_GROUP:(gi + 1) * POOL_GROUP]
        o_ref[:, gi * POOL_GROUP:(gi + 1) * POOL_GROUP] = y.astype(o_ref.dtype)


def _pool(p_small, w_pool_bf, pool_scale, l):
    tm = DEC_SEQ
    return pl.pallas_call(
        functools.partial(_pool_kernel, tm=tm),
        grid=(N_TOK // tm,),
        in_specs=[
            pl.BlockSpec((tm, POOL_WIDTH), lambda i: (i, C_PZ // POOL_WIDTH)),
            pl.BlockSpec((None, len(POOL_WINDOWS), POOL_GROUP, POOL_GROUP), lambda i: (l, 0, 0, 0)),
            pl.BlockSpec((None, 1, POOL_WIDTH), lambda i: (l, 0, 0)),
        ],
        out_specs=pl.BlockSpec((tm, POOL_WIDTH), lambda i: (i, 0)),
        out_shape=jax.ShapeDtypeStruct((N_TOK, POOL_WIDTH), BF),
        compiler_params=_cparams(("arbitrary",)),
        name="pool",
    )(p_small, w_pool_bf, pool_scale.reshape(DEPTH, 1, POOL_WIDTH))


def _merge_kernel(a_ref, p_ref, d_ref, g0_ref, g1_ref, g2_ref, wa_ref, wp_ref, wd_ref, o_ref):
    def sig(r):
        return jax.nn.sigmoid(r[...].astype(F32))

    def proj(x_ref, w_ref):
        return jnp.dot(x_ref[...], w_ref[...].astype(BF), preferred_element_type=F32)

    m = sig(g0_ref) * proj(a_ref, wa_ref)
    m = m + sig(g1_ref) * proj(p_ref, wp_ref)
    m = m + sig(g2_ref) * proj(d_ref, wd_ref)
    o_ref[...] = m.astype(o_ref.dtype)


def _merge(mla_o, pool_o, diff_o, gl, w_br_mla, w_br_pool, w_br_diff, l):
    tm, tn = 1024, 512
    nj = D_MODEL // tn
    return pl.pallas_call(
        _merge_kernel,
        grid=(N_TOK // tm, nj),
        in_specs=[
            pl.BlockSpec((tm, MLA_HEADS * MLA_V), lambda i, j: (i, 0)),
            pl.BlockSpec((tm, POOL_WIDTH), lambda i, j: (i, 0)),
            pl.BlockSpec((tm, DIFF_WIDTH), lambda i, j: (i, 0)),
            pl.BlockSpec((tm, tn), lambda i, j: (i, j)),
            pl.BlockSpec((tm, tn), lambda i, j: (i, nj + j)),
            pl.BlockSpec((tm, tn), lambda i, j: (i, 2 * nj + j)),
            pl.BlockSpec((None, MLA_HEADS * MLA_V, tn), lambda i, j: (l, 0, j)),
            pl.BlockSpec((None, POOL_WIDTH, tn), lambda i, j: (l, 0, j)),
            pl.BlockSpec((None, DIFF_WIDTH, tn), lambda i, j: (l, 0, j)),
        ],
        out_specs=pl.BlockSpec((tm, tn), lambda i, j: (i, j)),
        out_shape=jax.ShapeDtypeStruct((N_TOK, D_MODEL), BF),
        compiler_params=_cparams(("arbitrary", "arbitrary")),
        name="merge",
    )(mla_o, pool_o, diff_o, gl, gl, gl, w_br_mla, w_br_pool, w_br_diff)


def _outproj_kernel(x_ref, m_ref, w_ref, ga_ref, g2_ref, sh_ref, sc_ref, xo_ref, ht_ref, wb_ref, *, tm):
    @pl.when(pl.program_id(0) == 0)
    def _():
        wb_ref[...] = w_ref[...].astype(BF)

    seg = _segment(pl.program_id(0) * tm)
    y = jnp.dot(m_ref[...], wb_ref[...], preferred_element_type=F32)
    xn = x_ref[...] + ga_ref[pl.ds(seg, 1), :] * y
    xo_ref[...] = xn
    h2 = _rms(xn, g2_ref[...]) * (1.0 + sc_ref[pl.ds(seg, 1), :]) + sh_ref[pl.ds(seg, 1), :]
    ht_ref[...] = h2.T.astype(BF)


def _out_projection(x, merged, w_out, mod, g_norm2, l):
    tm = 256
    return pl.pallas_call(
        functools.partial(_outproj_kernel, tm=tm),
        grid=(N_TOK // tm,),
        in_specs=[
            pl.BlockSpec((tm, D_MODEL), lambda i: (i, 0)),
            pl.BlockSpec((tm, D_MODEL), lambda i: (i, 0)),
            pl.BlockSpec((None, D_MODEL, D_MODEL), lambda i: (l, 0, 0), pipeline_mode=pl.Buffered(1)),
            pl.BlockSpec((None, 8, D_MODEL), lambda i: (l, 0, 2)),
            pl.BlockSpec((None, 1, D_MODEL), lambda i: (l, 0, 0)),
            pl.BlockSpec((None, 8, D_MODEL), lambda i: (l, 0, 3)),
            pl.BlockSpec((None, 8, D_MODEL), lambda i: (l, 0, 4)),
        ],
        out_specs=[
            pl.BlockSpec((tm, D_MODEL), lambda i: (i, 0)),
            pl.BlockSpec((D_MODEL, tm), lambda i: (0, i)),
        ],
        out_shape=[
            jax.ShapeDtypeStruct((N_TOK, D_MODEL), F32),
            jax.ShapeDtypeStruct((D_MODEL, N_TOK), BF),
        ],
        scratch_shapes=[pltpu.VMEM((D_MODEL, D_MODEL), BF)],
        compiler_params=_cparams(("arbitrary",)),
        name="out_projection",
    )(x, merged, w_out, mod, g_norm2.reshape(DEPTH, 1, D_MODEL), mod, mod)


ROUTE_LANES = 128
NOT_SELECTED = 127


SLAB = 8
NET_WIDTH = 16


def _sorting_network(n):
    def merge(lo, hi, r):
        step = r * 2
        if step < hi - lo:
            yield from merge(lo, hi, step)
            yield from merge(lo + r, hi, step)
            yield from ((i, i + r) for i in range(lo + r, hi - r, step))
        else:
            yield (lo, lo + r)

    def sort(lo, hi):
        if hi - lo >= 1:
            mid = lo + (hi - lo) // 2
            yield from sort(lo, mid)
            yield from sort(mid + 1, hi)
            yield from merge(lo, hi, 1)

    return tuple(sort(0, n - 1))


def _extract_top(x, n):
    slabs = [x[SLAB * v:SLAB * (v + 1)] for v in range(x.shape[0] // SLAB)]
    slabs += [None] * (NET_WIDTH - len(slabs))
    for i, j in _sorting_network(NET_WIDTH):
        hi, lo = slabs[i], slabs[j]
        if lo is None:
            continue
        if hi is None:
            slabs[i], slabs[j] = lo, None
        else:
            slabs[i], slabs[j] = jnp.maximum(hi, lo), jnp.minimum(hi, lo)
    stack = [s for s in slabs if s is not None]
    sub = lax.broadcasted_iota(jnp.int32, stack[0].shape, 0)
    vals = []
    for r in range(n):
        m = jnp.max(stack[0], axis=0, keepdims=True)
        vals.append(m)
        hit = stack[0] == m
        popped = sub == jnp.min(jnp.where(hit, sub, SLAB), axis=0, keepdims=True)
        for d in range(min(n - r - 1, len(stack))):
            below = stack[d + 1] if d + 1 < len(stack) else -jnp.inf
            stack[d] = jnp.where(popped, below, stack[d])
    return vals


def _stack_rows(rows):
    idx = lax.broadcasted_iota(jnp.int32, (len(rows), rows[0].shape[1]), 0)
    m = jnp.zeros((len(rows), rows[0].shape[1]), F32)
    for r, row in enumerate(rows):
        m = jnp.where(idx == r, row, m)
    return m


def _route_chunk(s1, s2):
    k = PEER_TOPK
    a = _extract_top(s1, k)
    b = _extract_top(s2, k)
    am, bm = _stack_rows(a), _stack_rows(b)
    slabs = [a[0] + bm[0:8], a[0] + bm[8:16]]
    slabs += [a[r] + bm[0:8] for r in range(1, 8)]
    slabs += [am[8:16] + b[0]]
    cand = jnp.concatenate(slabs, axis=0)
    tau = _extract_top(cand, k)[-1]
    sel = cand >= tau
    top = a[0] + b[0]
    cnt = jnp.where(sel, 1.0, 0.0)
    low = jnp.where(sel, cand, jnp.inf)

    def row_rows(r):
        return slice(0, 16) if r == 0 else slice(8 + 8 * r, 16 + 8 * r) if r < 8 else slice(64 + r, 65 + r)

    n_r = [jnp.sum(cnt[row_rows(r)], axis=0, keepdims=True) for r in range(k)]
    low_r = [jnp.min(low[row_rows(r)], axis=0, keepdims=True) for r in range(k)]
    total = n_r[0]
    for r in range(1, k):
        total = total + n_r[r]
    excess = total - float(k)
    for r in reversed(range(k)):
        drop = jnp.where(excess > 0.0, jnp.where(low_r[r] == tau, 1.0, 0.0), 0.0)
        n_r[r] = n_r[r] - drop
        excess = excess - drop
    dropped = total - float(k) - excess
    z = jnp.sum(jnp.where(sel, jnp.exp(cand - top), 0.0), axis=0, keepdims=True) - dropped * jnp.exp(tau - top)

    c1 = jnp.full(s1.shape, -1.0, F32)
    r2 = jnp.full(s2.shape, float(NOT_SELECTED), F32)
    for r in range(k - 1):
        c1 = jnp.where(s1 == a[r], n_r[r] - 1.0, c1)
        r2 = jnp.where(s2 == b[r], float(r), r2)
    key = lax.broadcasted_iota(jnp.int32, s1.shape, 0)

    def last_key_limit(s, vals):
        match = s == vals[k - 1]
        first = jnp.min(jnp.where(match, key, PEER_KEYS), axis=0, keepdims=True)
        return match, jnp.where(vals[k - 2] != vals[k - 1], first, PEER_KEYS)

    match1, limit1 = last_key_limit(s1, a)
    c1 = jnp.where(match1, jnp.where(key <= limit1, n_r[k - 1] - 1.0, c1), c1)
    match2, limit2 = last_key_limit(s2, b)
    r2 = jnp.where(match2, jnp.where(key <= limit2, float(k - 1), r2), r2)
    e1 = jnp.exp(s1 - a[0]) / z
    e2 = jnp.exp(s2 - b[0])
    return c1, e1, r2, e2


ROUTE_HEADS = 8


def _route_kernel(ht_ref, wq_ref, sk_ref, c1_ref, e1_ref, r2_ref, e2_ref):
    def pack_row_pairs(x):
        return pltpu.bitcast(x.astype(BF), jnp.int32)

    def duplicate_halves(x):
        hi = pltpu.bitcast(x.astype(BF).astype(F32), jnp.int32)
        return hi | lax.shift_right_logical(hi, 16)

    half = PEER_KEYS // 2
    for hh in range(ROUTE_HEADS):
        wq = wq_ref[:, hh * PEER_QDIM:(hh + 1) * PEER_QDIM].astype(BF)
        qt = lax.dot_general(wq, ht_ref[...], (((0,), (0,)), ((), ())), preferred_element_type=F32)
        s1 = jnp.dot(sk_ref[0], qt[0:PEER_KEYS].astype(BF), preferred_element_type=F32)
        s2 = jnp.dot(sk_ref[1], qt[PEER_KEYS:].astype(BF), preferred_element_type=F32)
        for c in range(s1.shape[1] // ROUTE_LANES):
            cs = slice(c * ROUTE_LANES, (c + 1) * ROUTE_LANES)
            c1, e1, r2, e2 = _route_chunk(s1[:, cs], s2[:, cs])
            c1_ref[hh * PEER_KEYS:(hh + 1) * PEER_KEYS, cs] = duplicate_halves(c1)
            e1_ref[hh * PEER_KEYS:(hh + 1) * PEER_KEYS, cs] = duplicate_halves(e1)
            r2_ref[hh * half:(hh + 1) * half, cs] = pack_row_pairs(r2)
            e2_ref[hh * half:(hh + 1) * half, cs] = pack_row_pairs(e2)


def _route(h2t, w_peer_q, subkeys_bf, l):
    tn = 512
    rows = PEER_HEADS * PEER_KEYS
    spec = pl.BlockSpec((ROUTE_HEADS * PEER_KEYS, tn), lambda t, h: (h, t))
    pair_spec = pl.BlockSpec((ROUTE_HEADS * PEER_KEYS // 2, tn), lambda t, h: (h, t))
    dup = jax.ShapeDtypeStruct((rows, N_TOK), jnp.int32)
    pairs = jax.ShapeDtypeStruct((rows // 2, N_TOK), jnp.int32)
    return pl.pallas_call(
        _route_kernel,
        grid=(N_TOK // tn, PEER_HEADS // ROUTE_HEADS),
        in_specs=[
            pl.BlockSpec((D_MODEL, tn), lambda t, h: (0, t)),
            pl.BlockSpec((None, D_MODEL, ROUTE_HEADS * PEER_QDIM), lambda t, h: (l, 0, h)),
            pl.BlockSpec((None, 2, PEER_KEYS, PEER_KEYS), lambda t, h: (l, 0, 0, 0)),
        ],
        out_specs=[spec, spec, pair_spec, pair_spec],
        out_shape=[dup, dup, pairs, pairs],
        compiler_params=_cparams(("arbitrary", "arbitrary")),
        name="peer_route",
    )(h2t, w_peer_q, subkeys_bf)


PEER_LANES = 128
PEER_BLK = 256
PEER_FIRST_TOKENS = 1024


def _peer_kernel(*refs, tm, te, convert):
    if convert:
        (ht_ref, u_ref, v_ref, c1_ref, e1_ref, r2_ref, e2_ref, o_ref, ub_ref, vtb_ref,
         at0_ref, at1_ref, wg0_ref, wg1_ref) = refs
    else:
        (ht_ref, u_ref, vt_ref, c1_ref, e1_ref, r2_ref, e2_ref, prev_hbm, o_ref,
         at0_ref, at1_ref, wg0_ref, wg1_ref) = refs
        del prev_hbm
    e = pl.program_id(1)
    nblk = te // PEER_BLK
    half = PEER_KEYS // 2

    @pl.when(e == 0)
    def _():
        o_ref[...] = jnp.zeros_like(o_ref)

    at_bufs = (at0_ref, at1_ref)
    wg_bufs = (wg0_ref, wg1_ref)

    def scores(q):
        rows = slice(q * PEER_BLK, (q + 1) * PEER_BLK)
        u = u_ref[rows, :]
        if convert:
            u = u.astype(BF)
            ub_ref[rows, :] = u
        at_bufs[q % 2][...] = jnp.dot(u, ht_ref[...], preferred_element_type=F32)

    def weights(q):
        for bb in range(PEER_BLK // PEER_KEYS):
            i = (e * nblk + q) * (PEER_BLK // PEER_KEYS) + bb
            bs = slice(bb * PEER_KEYS, (bb + 1) * PEER_KEYS)
            last_rows = [c1_ref[pl.ds(h * PEER_KEYS + i, 1), :] for h in range(PEER_HEADS)]
            e1_rows = [e1_ref[pl.ds(h * PEER_KEYS + i, 1), :] for h in range(PEER_HEADS)]
            for c in range(tm // PEER_LANES):
                cs = slice(c * PEER_LANES, (c + 1) * PEER_LANES)
                w = jnp.zeros((PEER_KEYS, PEER_LANES), BF)
                for h in range(PEER_HEADS):
                    hs = slice(h * half, (h + 1) * half)
                    last = pltpu.bitcast(jnp.broadcast_to(last_rows[h][:, cs], (half, PEER_LANES)), BF)
                    e1 = pltpu.bitcast(jnp.broadcast_to(e1_rows[h][:, cs], (half, PEER_LANES)), BF)
                    r2 = pltpu.bitcast(r2_ref[hs, cs], BF)
                    e2 = pltpu.bitcast(e2_ref[hs, cs], BF)
                    w = w + jnp.where(r2 <= last, e2, jnp.zeros_like(e2)) * e1
                a = at_bufs[q % 2][bs, cs]
                g = (0.5 * a * (1.0 + lax.erf(a * math.sqrt(0.5)))).astype(BF)
                wg_bufs[q % 2][bs, cs] = w * g

    def update(q):
        cols = slice(q * PEER_BLK, (q + 1) * PEER_BLK)
        if convert:
            vt = v_ref[cols, :].T.astype(BF)
            vtb_ref[:, cols] = vt
        else:
            vt = vt_ref[:, cols]
        o_ref[...] += jnp.dot(vt, wg_bufs[q % 2][...], preferred_element_type=F32)

    scores(0)
    for q in range(nblk):
        if q + 1 < nblk:
            scores(q + 1)
        weights(q)
        if q >= 1:
            update(q - 1)
    update(nblk - 1)


def _peer_scratch(tm):
    return [pltpu.VMEM((PEER_BLK, tm), F32), pltpu.VMEM((PEER_BLK, tm), F32),
            pltpu.VMEM((PEER_BLK, tm), BF), pltpu.VMEM((PEER_BLK, tm), BF)]


def _peer(h2t, peer_u, peer_v, routing, l):
    c1, e1, r2, e2 = routing
    rows = PEER_HEADS * PEER_KEYS
    out_sds = jax.ShapeDtypeStruct((D_MODEL, N_TOK), F32)

    tm, te = PEER_FIRST_TOKENS, 512
    once = pl.Buffered(1)
    rspec = pl.BlockSpec((rows, tm), lambda t, e: (0, 0), pipeline_mode=once)
    pspec = pl.BlockSpec((rows // 2, tm), lambda t, e: (0, 0), pipeline_mode=once)
    first, u_bf, vt_bf = pl.pallas_call(
        functools.partial(_peer_kernel, tm=tm, te=te, convert=True),
        grid=(1, PEER_EXPERTS // te),
        in_specs=[
            pl.BlockSpec((D_MODEL, tm), lambda t, e: (0, 0), pipeline_mode=once),
            pl.BlockSpec((None, te, D_MODEL), lambda t, e: (l, e, 0)),
            pl.BlockSpec((None, te, D_MODEL), lambda t, e: (l, e, 0)),
            rspec, rspec, pspec, pspec,
        ],
        out_specs=[
            pl.BlockSpec((D_MODEL, tm), lambda t, e: (0, 0), pipeline_mode=once),
            pl.BlockSpec((te, D_MODEL), lambda t, e: (e, 0)),
            pl.BlockSpec((D_MODEL, te), lambda t, e: (0, e)),
        ],
        out_shape=[out_sds, jax.ShapeDtypeStruct((PEER_EXPERTS, D_MODEL), BF),
                   jax.ShapeDtypeStruct((D_MODEL, PEER_EXPERTS), BF)],
        scratch_shapes=_peer_scratch(tm),
        compiler_params=_cparams(("arbitrary", "arbitrary")),
        name="peer_dense_first",
    )(h2t, peer_u, peer_v, c1, e1, r2, e2)

    tm, te = 512, 1024
    t0 = PEER_FIRST_TOKENS // tm
    rspec = pl.BlockSpec((rows, tm), lambda t, e: (0, t + t0))
    pspec = pl.BlockSpec((rows // 2, tm), lambda t, e: (0, t + t0))
    return pl.pallas_call(
        functools.partial(_peer_kernel, tm=tm, te=te, convert=False),
        grid=(N_TOK // tm - t0, PEER_EXPERTS // te),
        in_specs=[
            pl.BlockSpec((D_MODEL, tm), lambda t, e: (0, t + t0)),
            pl.BlockSpec((te, D_MODEL), lambda t, e: (e, 0)),
            pl.BlockSpec((D_MODEL, te), lambda t, e: (0, e)),
            rspec, rspec, pspec, pspec,
            pl.BlockSpec(memory_space=pl.ANY),
        ],
        out_specs=pl.BlockSpec((D_MODEL, tm), lambda t, e: (0, t + t0)),
        out_shape=out_sds,
        input_output_aliases={7: 0},
        scratch_shapes=_peer_scratch(tm),
        compiler_params=_cparams(("arbitrary", "arbitrary")),
        name="peer_dense",
    )(h2t, u_bf, vt_bf, c1, e1, r2, e2, first)


def _final_kernel(x_ref, pt_ref, ga_ref, g_ref, ctx_ref, dec_ref, *, tm, n_ctx_tiles):
    seg = _segment(pl.program_id(0) * tm)
    x = x_ref[...] + ga_ref[pl.ds(seg, 1), :] * pt_ref[...].T
    y = _rms(x, g_ref[...])
    is_ctx = pl.program_id(0) < n_ctx_tiles

    @pl.when(is_ctx)
    def _():
        ctx_ref[...] = y

    @pl.when(jnp.logical_not(is_ctx))
    def _():
        dec_ref[...] = y


def _final_norm(x, peer_t, mod, g_final):
    tm = 512
    n_ctx_tiles = N_CTX // tm
    return pl.pallas_call(
        functools.partial(_final_kernel, tm=tm, n_ctx_tiles=n_ctx_tiles),
        grid=(N_TOK // tm,),
        in_specs=[pl.BlockSpec((tm, D_MODEL), lambda i: (i, 0)),
                  pl.BlockSpec((D_MODEL, tm), lambda i: (0, i)),
                  pl.BlockSpec((None, 8, D_MODEL), lambda i: (DEPTH - 1, 0, 5)),
                  pl.BlockSpec((1, D_MODEL), lambda i: (0, 0))],
        out_specs=[
            pl.BlockSpec((tm, D_MODEL), lambda i: (jnp.minimum(i, n_ctx_tiles - 1), 0)),
            pl.BlockSpec((tm, D_MODEL), lambda i: (jnp.maximum(i - n_ctx_tiles, 0), 0)),
        ],
        out_shape=[jax.ShapeDtypeStruct((N_CTX, D_MODEL), F32), jax.ShapeDtypeStruct((N_DEC, D_MODEL), F32)],
        compiler_params=_cparams(("arbitrary",)),
        name="final_norm",
    )(x, peer_t, mod, g_final.reshape(1, D_MODEL))


def _rope_tables():
    t = jnp.arange(DEC_SEQ)
    half = MLA_ROPE // 4
    inv = ROPE_BASE ** (-jnp.arange(half, dtype=F32) / half)
    ang_r = (t // GRID_W).astype(F32)[:, None] * inv
    ang_c = (t % GRID_W).astype(F32)[:, None] * inv
    cos = jnp.concatenate([jnp.cos(ang_r)] * 2 + [jnp.cos(ang_c)] * 2, axis=1)
    sin = jnp.concatenate([-jnp.sin(ang_r), jnp.sin(ang_r), -jnp.sin(ang_c), jnp.sin(ang_c)], axis=1)
    return cos, sin


def _relayout_w_qb(w_qb):
    w = w_qb.reshape(DEPTH, MLA_Q_RANK, MLA_HEADS, MLA_NOPE + MLA_ROPE)
    w = jnp.pad(w, ((0, 0), (0, 0), (0, 0), (0, MLA_HEAD_PAD - MLA_NOPE - MLA_ROPE)))
    return w.reshape(DEPTH, MLA_Q_RANK, MLA_HEADS * MLA_HEAD_PAD).astype(BF)


def kernel(x_prompt, x_sample, cache_mla_ckv, cache_mla_krope, cache_diff_k, cache_diff_v, c, c_ctx,
           w_mod, b_mod, g_norm1, w_in, g_qnorm, w_qb, g_kvnorm, w_kvb, w_pool, pool_scale,
           diff_lambda, g_diffnorm, w_br_mla, w_br_pool, w_br_diff, w_out, g_norm2,
           w_peer_q, peer_subkeys, peer_u, peer_v, g_final):
    x = jnp.concatenate([x_prompt.reshape(N_CTX, D_MODEL), x_sample.reshape(N_DEC, D_MODEL)], axis=0)
    cond8 = jnp.concatenate([c_ctx[None, :], c, jnp.zeros((8 - 1 - DEC_BATCH, D_MODEL), F32)], axis=0)

    w_in_t = jnp.swapaxes(w_in, 1, 2)
    w_qb_re = _relayout_w_qb(w_qb)
    w_pool_bf = w_pool.astype(BF)
    subkeys_bf = peer_subkeys.astype(BF)

    cache_kr_pad = jnp.pad(cache_mla_krope, ((0, 0), (0, 0), (0, 0), (0, LANES - MLA_ROPE)))
    cache_k = cache_diff_k.reshape(DEC_BATCH, DEPTH, PAST_LEN, DIFF_WIDTH)
    cache_v = cache_diff_v.reshape(DEC_BATCH, DEPTH, PAST_LEN, DIFF_WIDTH)

    cos64, sin64 = _rope_tables()
    ones, zeros = jnp.ones_like(cos64), jnp.zeros_like(cos64)
    mla_tabs = (
        jnp.concatenate([ones, ones, cos64, ones], axis=1),
        jnp.concatenate([zeros, zeros, sin64, zeros], axis=1),
        jnp.concatenate([cos64, ones], axis=1),
        jnp.concatenate([sin64, zeros], axis=1),
    )
    diff_tabs = (jnp.tile(cos64, (1, DIFF_WIDTH // DIFF_QK)), jnp.tile(sin64, (1, DIFF_WIDTH // DIFF_QK)))

    mod = _modulation(cond8, w_mod, b_mod)

    st_ckv = jnp.zeros((BATCH, DEPTH, SEQ, MLA_KV_RANK), F32)
    st_kr = jnp.zeros((BATCH, DEPTH, SEQ, MLA_ROPE), F32)
    st_k = jnp.zeros((BATCH, DEPTH, SEQ, DIFF_WIDTH), F32)
    st_v = jnp.zeros((BATCH, DEPTH, SEQ, DIFF_WIDTH), F32)
    peer_t = None
    for l in range(DEPTH):
        if l == 0:
            p_small = _in_projection(x, g_norm1, mod, w_in_t, l, 0, SMALL_WIDTH, F32)
        else:
            p_small, x = _in_projection_res(x, peer_t, g_norm1, mod, w_in_t, l, SMALL_WIDTH)
        gl = _in_projection(x, g_norm1, mod, w_in_t, l, SMALL_WIDTH, GATE_WIDTH, BF)
        mla_c, st_ckv, st_kr = _mla_ctx(p_small, g_qnorm, g_kvnorm, w_qb_re, w_kvb, st_ckv, st_kr, l)
        mla_o = _mla_dec(p_small, cache_mla_ckv, cache_kr_pad, mla_tabs, g_qnorm, g_kvnorm, w_qb_re, w_kvb,
                         mla_c, l)
        diff_c, st_k, st_v = _diff_ctx(p_small, diff_lambda, g_diffnorm, st_k, st_v, l)
        diff_o = _diff_dec(p_small, cache_k, cache_v, diff_tabs, diff_lambda, g_diffnorm, diff_c, l)
        pool_o = _pool(p_small, w_pool_bf, pool_scale, l)
        merged = _merge(mla_o, pool_o, diff_o, gl, w_br_mla, w_br_pool, w_br_diff, l)
        x, h2t = _out_projection(x, merged, w_out, mod, g_norm2, l)
        routing = _route(h2t, w_peer_q, subkeys_bf, l)
        peer_t = _peer(h2t, peer_u, peer_v, routing, l)

    y_ctx, y_dec = _final_norm(x, peer_t, mod, g_final)
    return (y_ctx.reshape(BATCH, SEQ, D_MODEL), y_dec.reshape(DEC_BATCH, DEC_SEQ, D_MODEL), st_ckv, st_kr,
            st_k.reshape(BATCH, DEPTH, SEQ, DIFF_HEADS, 2 * DIFF_QK),
            st_v.reshape(BATCH, DEPTH, SEQ, DIFF_HEADS, DIFF_V))
```

```python
import functools
import math

import jax
import jax.numpy as jnp
from jax import lax
from jax.experimental import pallas as pl
from jax.experimental.pallas import tpu as pltpu

BF = jnp.bfloat16
F32 = jnp.float32

D_MODEL = 2048
BATCH = 16
SEQ = 256
DEPTH = 4
DEC_BATCH = 2
DEC_SEQ = 1024
PAST_LEN = 256
GRID_W = 64
ROPE_BASE = 10000.0
EPS = 1e-6

MLA_HEADS = 8
MLA_NOPE = 128
MLA_ROPE = 64
MLA_V = 128
MLA_Q_RANK = 512
MLA_KV_RANK = 256
MLA_HEAD_PAD = 256
MLA_KV_HEAD = MLA_NOPE + MLA_V
LANES = 128
CTX_SEQS = 4
ROPE_HALF = MLA_ROPE // 4

POOL_WINDOWS = (2, 4, 8, 16)
POOL_GROUP = 128
POOL_WIDTH = POOL_GROUP * len(POOL_WINDOWS)

DIFF_HEADS = 4
DIFF_QK = 64
DIFF_V = 2 * DIFF_QK
DIFF_WIDTH = DIFF_HEADS * DIFF_V

PEER_HEADS = 8
PEER_KEYS = 128
PEER_EXPERTS = PEER_KEYS * PEER_KEYS
PEER_TOPK = 16
PEER_QDIM = 256

N_CTX = BATCH * SEQ
N_DEC = DEC_BATCH * DEC_SEQ
N_TOK = N_CTX + N_DEC
KEYS_DEC = PAST_LEN + DEC_SEQ

C_QA = 0
C_CKV = 512
C_KR = 768
KR_BLOCK = 256
C_PZ = 1024
C_DQ = 1536
C_DK = 2048
C_DV = 2560
SMALL_WIDTH = 3072
GATE_WIDTH = 3 * D_MODEL

KV_END = 512 + 320
INPROJ_TN = 512

VMEM_LIMIT = 56 * 1024 * 1024


def _cparams(sem):
    return pltpu.CompilerParams(dimension_semantics=sem, vmem_limit_bytes=VMEM_LIMIT)


def _segment(row0):
    return jnp.where(row0 < N_CTX, 0, 1 + (row0 - N_CTX) // DEC_SEQ)


def _rms(x, g):
    return x * lax.rsqrt(jnp.mean(x * x, axis=-1, keepdims=True) + EPS) * g


def _softmax_rows(s):
    m = jnp.max(s, axis=-1, keepdims=True)
    p = jnp.exp(s - m)
    return p / jnp.sum(p, axis=-1, keepdims=True)


def _dot_nt(a, b):
    return lax.dot_general(a, b, (((1,), (1,)), ((), ())), preferred_element_type=F32)


def _rope_swap(x):
    n = x.shape[-1]
    up = pltpu.roll(x, n - ROPE_HALF, 1)
    dn = pltpu.roll(x, ROPE_HALF, 1)
    lane = lax.broadcasted_iota(jnp.int32, x.shape, 1)
    return jnp.where((lane & ROPE_HALF) == 0, up, dn)


def _mod_kernel(c_ref, w_ref, b_ref, o_ref):
    c = c_ref[...]
    a = (c * jax.nn.sigmoid(c)).astype(BF)
    o_ref[...] = jnp.dot(a, w_ref[...].astype(BF), preferred_element_type=F32) + b_ref[...]


def _modulation(cond8, w_mod, b_mod):
    tn = 1024
    return pl.pallas_call(
        _mod_kernel,
        grid=(DEPTH, 6 * D_MODEL // tn),
        in_specs=[
            pl.BlockSpec((8, D_MODEL), lambda l, j: (0, 0)),
            pl.BlockSpec((None, D_MODEL, tn), lambda l, j: (l, 0, j)),
            pl.BlockSpec((None, 1, tn), lambda l, j: (l, 0, j)),
        ],
        out_specs=pl.BlockSpec((None, 8, tn), lambda l, j: (l, 0, j)),
        out_shape=jax.ShapeDtypeStruct((DEPTH, 8, 6 * D_MODEL), F32),
        compiler_params=_cparams(("arbitrary", "arbitrary")),
        name="modulation",
    )(cond8, w_mod, b_mod.reshape(DEPTH, 1, 6 * D_MODEL))


def _w_in_row(tile):
    per_tile = INPROJ_TN // 64
    return 64 * jnp.where(tile < 2, per_tile * tile, per_tile * tile - (C_PZ - KV_END) // 64)


def _inproj_tile(h_ref, w_ref, tile):
    y = _dot_nt(h_ref[...], w_ref[0].astype(BF))
    lane = lax.broadcasted_iota(jnp.int32, y.shape, 1)
    return jnp.where(lane >= jnp.where(tile == 1, KV_END - INPROJ_TN, INPROJ_TN), 0.0, y)


def _inproj_kernel(x_ref, g_ref, sh_ref, sc_ref, w_ref, o_ref, h_ref, *, tm, tile0):
    i = pl.program_id(0)

    @pl.when(pl.program_id(1) == 0)
    def _():
        seg = _segment(i * tm)
        y = _rms(x_ref[...], g_ref[...])
        h_ref[...] = (y * (1.0 + sc_ref[pl.ds(seg, 1), :]) + sh_ref[pl.ds(seg, 1), :]).astype(BF)

    o_ref[...] = _inproj_tile(h_ref, w_ref, pl.program_id(1) + tile0).astype(o_ref.dtype)


def _w_in_spec(l, tile0):
    return pl.BlockSpec((pl.Element(1), pl.Element(INPROJ_TN), pl.Element(D_MODEL)),
                        lambda i, j: (l, _w_in_row(j + tile0), 0))


def _in_projection(x, g_norm1, mod, w_in_t, l, col0, width, out_dtype, tm=1024):
    tn = INPROJ_TN
    tile0 = col0 // tn
    return pl.pallas_call(
        functools.partial(_inproj_kernel, tm=tm, tile0=tile0),
        grid=(N_TOK // tm, width // tn),
        in_specs=[
            pl.BlockSpec((tm, D_MODEL), lambda i, j: (i, 0)),
            pl.BlockSpec((None, 1, D_MODEL), lambda i, j: (l, 0, 0)),
            pl.BlockSpec((None, 8, D_MODEL), lambda i, j: (l, 0, 0)),
            pl.BlockSpec((None, 8, D_MODEL), lambda i, j: (l, 0, 1)),
            _w_in_spec(l, tile0),
        ],
        out_specs=pl.BlockSpec((tm, tn), lambda i, j: (i, j)),
        out_shape=jax.ShapeDtypeStruct((N_TOK, width), out_dtype),
        scratch_shapes=[pltpu.VMEM((tm, D_MODEL), BF)],
        compiler_params=_cparams(("arbitrary", "arbitrary")),
        name="in_projection",
    )(x, g_norm1.reshape(DEPTH, 1, D_MODEL), mod, mod, w_in_t)


def _inproj_res_kernel(x_ref, pt_ref, ga_ref, g_ref, sh_ref, sc_ref, w_ref, o_ref, xo_ref, h_ref, *, tm):
    i = pl.program_id(0)

    @pl.when(pl.program_id(1) == 0)
    def _():
        seg = _segment(i * tm)
        x = x_ref[...] + ga_ref[pl.ds(seg, 1), :] * pt_ref[...].T
        xo_ref[...] = x
        y = _rms(x, g_ref[...])
        h_ref[...] = (y * (1.0 + sc_ref[pl.ds(seg, 1), :]) + sh_ref[pl.ds(seg, 1), :]).astype(BF)

    o_ref[...] = _inproj_tile(h_ref, w_ref, pl.program_id(1)).astype(o_ref.dtype)


def _in_projection_res(x, peer_t, g_norm1, mod, w_in_t, l, width):
    tm, tn = 1024, INPROJ_TN
    once = pl.Buffered(1)
    return pl.pallas_call(
        functools.partial(_inproj_res_kernel, tm=tm),
        grid=(N_TOK // tm, width // tn),
        in_specs=[
            pl.BlockSpec((tm, D_MODEL), lambda i, j: (i, 0), pipeline_mode=once),
            pl.BlockSpec((D_MODEL, tm), lambda i, j: (0, i), pipeline_mode=once),
            pl.BlockSpec((None, 8, D_MODEL), lambda i, j: (l - 1, 0, 5)),
            pl.BlockSpec((None, 1, D_MODEL), lambda i, j: (l, 0, 0)),
            pl.BlockSpec((None, 8, D_MODEL), lambda i, j: (l, 0, 0)),
            pl.BlockSpec((None, 8, D_MODEL), lambda i, j: (l, 0, 1)),
            _w_in_spec(l, 0),
        ],
        out_specs=[
            pl.BlockSpec((tm, tn), lambda i, j: (i, j)),
            pl.BlockSpec((tm, D_MODEL), lambda i, j: (i, 0)),
        ],
        out_shape=[jax.ShapeDtypeStruct((N_TOK, width), F32), jax.ShapeDtypeStruct((N_TOK, D_MODEL), F32)],
        scratch_shapes=[pltpu.VMEM((tm, D_MODEL), BF)],
        compiler_params=_cparams(("arbitrary", "arbitrary")),
        name="in_projection_res",
    )(x, peer_t, mod, g_norm1.reshape(DEPTH, 1, D_MODEL), mod, mod, w_in_t)


def _mla_heads(q, kfun, vfun, o_ref, scale, rows=slice(None)):
    for h in range(MLA_HEADS):
        s = _dot_nt(q(h), kfun(h)) * scale
        p = _softmax_rows(s).astype(BF)
        o = jnp.dot(p, vfun(h), preferred_element_type=F32)
        o_ref[rows, h * MLA_V:(h + 1) * MLA_V] = o.astype(o_ref.dtype)


def _mla_ctx_kernel(qa_ref, ckv_ref, kr_ref, gq_ref, gkv_ref, wqb_ref, wkvb_ref, st_ckv_hbm, st_kr_hbm,
                    o_ref, ckv_out_ref, kr_out_ref):
    del st_ckv_hbm, st_kr_hbm
    scale = (MLA_NOPE + MLA_ROPE) ** -0.5
    qn = _rms(qa_ref[...], gq_ref[...]).astype(BF)
    q = jnp.dot(qn, wqb_ref[...], preferred_element_type=F32).astype(BF)
    ckv = _rms(ckv_ref[...], gkv_ref[...])
    kr = kr_ref[...]
    kv = jnp.dot(ckv.astype(BF), wkvb_ref[...].astype(BF), preferred_element_type=F32).astype(BF)
    krp = kr[:, :LANES].astype(BF)
    for s in range(CTX_SEQS):
        rows = slice(s * SEQ, (s + 1) * SEQ)
        ckv_out_ref[s] = ckv[rows]
        kr_out_ref[s] = kr[rows, :MLA_ROPE]

        def qh(h, rows=rows):
            return q[rows, h * MLA_HEAD_PAD:(h + 1) * MLA_HEAD_PAD]

        def kh(h, rows=rows):
            return jnp.concatenate([kv[rows, h * MLA_KV_HEAD:h * MLA_KV_HEAD + MLA_NOPE], krp[rows]], axis=1)

        def vh(h, rows=rows):
            return kv[rows, h * MLA_KV_HEAD + MLA_NOPE:(h + 1) * MLA_KV_HEAD]

        _mla_heads(qh, kh, vh, o_ref, scale, rows)


def _mla_ctx(p_small, g_qnorm, g_kvnorm, w_qb_re, w_kvb, st_ckv, st_kr, l):
    return pl.pallas_call(
        _mla_ctx_kernel,
        grid=(BATCH // CTX_SEQS,),
        in_specs=[
            pl.BlockSpec((CTX_SEQS * SEQ, MLA_Q_RANK), lambda b: (b, C_QA // MLA_Q_RANK)),
            pl.BlockSpec((CTX_SEQS * SEQ, MLA_KV_RANK), lambda b: (b, C_CKV // MLA_KV_RANK)),
            pl.BlockSpec((CTX_SEQS * SEQ, KR_BLOCK), lambda b: (b, C_KR // KR_BLOCK)),
            pl.BlockSpec((None, 1, MLA_Q_RANK), lambda b: (l, 0, 0)),
            pl.BlockSpec((None, 1, MLA_KV_RANK), lambda b: (l, 0, 0)),
            pl.BlockSpec((None, MLA_Q_RANK, MLA_HEADS * MLA_HEAD_PAD), lambda b: (l, 0, 0)),
            pl.BlockSpec((None, MLA_KV_RANK, MLA_HEADS * MLA_KV_HEAD), lambda b: (l, 0, 0)),
            pl.BlockSpec(memory_space=pl.ANY),
            pl.BlockSpec(memory_space=pl.ANY),
        ],
        out_specs=[
            pl.BlockSpec((CTX_SEQS * SEQ, MLA_HEADS * MLA_V), lambda b: (b, 0)),
            pl.BlockSpec((CTX_SEQS, None, SEQ, MLA_KV_RANK), lambda b: (b, l, 0, 0)),
            pl.BlockSpec((CTX_SEQS, None, SEQ, MLA_ROPE), lambda b: (b, l, 0, 0)),
        ],
        out_shape=[
            jax.ShapeDtypeStruct((N_TOK, MLA_HEADS * MLA_V), BF),
            jax.ShapeDtypeStruct(st_ckv.shape, F32),
            jax.ShapeDtypeStruct(st_kr.shape, F32),
        ],
        input_output_aliases={7: 1, 8: 2},
        compiler_params=_cparams(("arbitrary",)),
        name="mla_ctx",
    )(p_small, p_small, p_small, g_qnorm.reshape(DEPTH, 1, -1), g_kvnorm.reshape(DEPTH, 1, -1),
      w_qb_re, w_kvb, st_ckv, st_kr)


def _mla_dec_kernel(qa_ref, ckv_ref, kr_ref, cckv_ref, ckr_ref, cq_ref, sq_ref, ck_ref, sk_ref,
                    gq_ref, gkv_ref, wqb_ref, wkvb_ref, mix_hbm, o_ref, kf_ref, vf_ref):
    del mix_hbm
    scale = (MLA_NOPE + MLA_ROPE) ** -0.5

    @pl.when(pl.program_id(1) == 0)
    def _():
        ckv = _rms(ckv_ref[...], gkv_ref[...])
        ckv_all = jnp.concatenate([cckv_ref[...], ckv], axis=0).astype(BF)
        kr = kr_ref[:, :LANES]
        kr_rot = kr * ck_ref[...] + _rope_swap(kr) * sk_ref[...]
        kr_all = jnp.concatenate([ckr_ref[...], kr_rot], axis=0).astype(BF)
        for h in range(MLA_HEADS):
            w_h = wkvb_ref[:, h * MLA_KV_HEAD:(h + 1) * MLA_KV_HEAD].astype(BF)
            kvh = jnp.dot(ckv_all, w_h, preferred_element_type=F32)
            kf_ref[h, :, 0:MLA_NOPE] = kvh[:, :MLA_NOPE].astype(BF)
            kf_ref[h, :, MLA_NOPE:MLA_HEAD_PAD] = kr_all
            vf_ref[h] = kvh[:, MLA_NOPE:].astype(BF)

    qn = _rms(qa_ref[...], gq_ref[...]).astype(BF)
    q = jnp.dot(qn, wqb_ref[...], preferred_element_type=F32)
    cq = cq_ref[...]
    sq = sq_ref[...]

    def qh(h):
        x = q[:, h * MLA_HEAD_PAD:(h + 1) * MLA_HEAD_PAD]
        return (x * cq + _rope_swap(x) * sq).astype(BF)

    _mla_heads(qh, lambda h: kf_ref[h], lambda h: vf_ref[h], o_ref, scale)


def _mla_dec(p_small, cache_ckv, cache_kr_pad, tabs, g_qnorm, g_kvnorm, w_qb_re, w_kvb, mix, l):
    tq = 512
    nq = DEC_SEQ // tq
    row_blk = N_CTX // DEC_SEQ
    cq, sq, ck, sk = tabs
    return pl.pallas_call(
        _mla_dec_kernel,
        grid=(DEC_BATCH, nq),
        in_specs=[
            pl.BlockSpec((tq, MLA_Q_RANK), lambda b, i: (N_CTX // tq + b * nq + i, 0)),
            pl.BlockSpec((DEC_SEQ, MLA_KV_RANK), lambda b, i: (row_blk + b, C_CKV // MLA_KV_RANK)),
            pl.BlockSpec((DEC_SEQ, KR_BLOCK), lambda b, i: (row_blk + b, C_KR // KR_BLOCK)),
            pl.BlockSpec((None, None, PAST_LEN, MLA_KV_RANK), lambda b, i: (b, l, 0, 0)),
            pl.BlockSpec((None, None, PAST_LEN, LANES), lambda b, i: (b, l, 0, 0)),
            pl.BlockSpec((tq, MLA_HEAD_PAD), lambda b, i: (i, 0)),
            pl.BlockSpec((tq, MLA_HEAD_PAD), lambda b, i: (i, 0)),
            pl.BlockSpec((DEC_SEQ, LANES), lambda b, i: (0, 0)),
            pl.BlockSpec((DEC_SEQ, LANES), lambda b, i: (0, 0)),
            pl.BlockSpec((None, 1, MLA_Q_RANK), lambda b, i: (l, 0, 0)),
            pl.BlockSpec((None, 1, MLA_KV_RANK), lambda b, i: (l, 0, 0)),
            pl.BlockSpec((None, MLA_Q_RANK, MLA_HEADS * MLA_HEAD_PAD), lambda b, i: (l, 0, 0)),
            pl.BlockSpec((None, MLA_KV_RANK, MLA_HEADS * MLA_KV_HEAD), lambda b, i: (l, 0, 0)),
            pl.BlockSpec(memory_space=pl.ANY),
        ],
        out_specs=pl.BlockSpec((tq, MLA_HEADS * MLA_V), lambda b, i: (N_CTX // tq + b * nq + i, 0)),
        out_shape=jax.ShapeDtypeStruct((N_TOK, MLA_HEADS * MLA_V), BF),
        input_output_aliases={13: 0},
        scratch_shapes=[
            pltpu.VMEM((MLA_HEADS, KEYS_DEC, MLA_HEAD_PAD), BF),
            pltpu.VMEM((MLA_HEADS, KEYS_DEC, MLA_V), BF),
        ],
        compiler_params=_cparams(("arbitrary", "arbitrary")),
        name="mla_dec",
    )(p_small, p_small, p_small, cache_ckv, cache_kr_pad, cq, sq, ck, sk,
      g_qnorm.reshape(DEPTH, 1, -1), g_kvnorm.reshape(DEPTH, 1, -1), w_qb_re, w_kvb, mix)


def _diff_lambda(lv):
    t1 = jnp.sum(lv[0:1] * lv[1:2], axis=-1, keepdims=True)
    t2 = jnp.sum(lv[2:3] * lv[3:4], axis=-1, keepdims=True)
    return jnp.exp(t1) - jnp.exp(t2)


def _diff_heads(q, kfun, vfun, lam, g, o_ref, lam_init, rows=slice(None)):
    scale = DIFF_QK ** -0.5
    lane = lax.broadcasted_iota(jnp.int32, (q.shape[0], DIFF_V), 1)
    for h in range(DIFF_HEADS):
        qh = q[:, h * DIFF_V:(h + 1) * DIFF_V]
        q1 = jnp.where(lane < DIFF_QK, qh, 0.0).astype(BF)
        q2 = jnp.where(lane >= DIFF_QK, qh, 0.0).astype(BF)
        k = kfun(h)
        v = vfun(h)
        p1 = _softmax_rows(_dot_nt(q1, k) * scale).astype(BF)
        p2 = _softmax_rows(_dot_nt(q2, k) * scale).astype(BF)
        a1 = jnp.dot(p1, v, preferred_element_type=F32)
        a2 = jnp.dot(p2, v, preferred_element_type=F32)
        o = _rms(a1 - lam * a2, g) * (1.0 - lam_init)
        o_ref[rows, h * DIFF_V:(h + 1) * DIFF_V] = o.astype(o_ref.dtype)


def _diff_ctx_kernel(q_ref, k_ref, v_ref, lam_ref, g_ref, st_k_hbm, st_v_hbm,
                     o_ref, k_out_ref, v_out_ref, *, lam_init):
    del st_k_hbm, st_v_hbm
    lam = _diff_lambda(lam_ref[...]) + lam_init
    for s in range(CTX_SEQS):
        rows = slice(s * SEQ, (s + 1) * SEQ)
        k_out_ref[s] = k_ref[rows, :]
        v_out_ref[s] = v_ref[rows, :]
        k = k_ref[rows, :].astype(BF)
        v = v_ref[rows, :].astype(BF)
        _diff_heads(q_ref[rows, :],
                    lambda h, k=k: k[:, h * DIFF_V:(h + 1) * DIFF_V],
                    lambda h, v=v: v[:, h * DIFF_V:(h + 1) * DIFF_V],
                    lam, g_ref[...], o_ref, lam_init, rows)


def _lam_init(l):
    return 0.8 - 0.6 * math.exp(-0.3 * l)


def _diff_ctx(p_small, diff_lambda, g_diffnorm, st_k, st_v, l):
    return pl.pallas_call(
        functools.partial(_diff_ctx_kernel, lam_init=_lam_init(l)),
        grid=(BATCH // CTX_SEQS,),
        in_specs=[
            pl.BlockSpec((CTX_SEQS * SEQ, DIFF_WIDTH), lambda b: (b, C_DQ // DIFF_WIDTH)),
            pl.BlockSpec((CTX_SEQS * SEQ, DIFF_WIDTH), lambda b: (b, C_DK // DIFF_WIDTH)),
            pl.BlockSpec((CTX_SEQS * SEQ, DIFF_WIDTH), lambda b: (b, C_DV // DIFF_WIDTH)),
            pl.BlockSpec((None, 4, DIFF_QK), lambda b: (l, 0, 0)),
            pl.BlockSpec((None, 1, DIFF_V), lambda b: (l, 0, 0)),
            pl.BlockSpec(memory_space=pl.ANY),
            pl.BlockSpec(memory_space=pl.ANY),
        ],
        out_specs=[
            pl.BlockSpec((CTX_SEQS * SEQ, DIFF_WIDTH), lambda b: (b, 0)),
            pl.BlockSpec((CTX_SEQS, None, SEQ, DIFF_WIDTH), lambda b: (b, l, 0, 0)),
            pl.BlockSpec((CTX_SEQS, None, SEQ, DIFF_WIDTH), lambda b: (b, l, 0, 0)),
        ],
        out_shape=[
            jax.ShapeDtypeStruct((N_TOK, DIFF_WIDTH), BF),
            jax.ShapeDtypeStruct(st_k.shape, F32),
            jax.ShapeDtypeStruct(st_v.shape, F32),
        ],
        input_output_aliases={5: 1, 6: 2},
        compiler_params=_cparams(("arbitrary",)),
        name="diff_ctx",
    )(p_small, p_small, p_small, diff_lambda, g_diffnorm.reshape(DEPTH, 1, -1), st_k, st_v)


def _diff_dec_kernel(q_ref, k_ref, v_ref, ck_ref, cv_ref, cq_ref, sq_ref, cfull_ref, sfull_ref,
                     lam_ref, g_ref, mix_hbm, o_ref, kf_ref, vf_ref, *, lam_init):
    del mix_hbm

    @pl.when(pl.program_id(1) == 0)
    def _():
        k = k_ref[...]
        k_rot = k * cfull_ref[...] + _rope_swap(k) * sfull_ref[...]
        kf_ref[0:PAST_LEN, :] = ck_ref[...].astype(BF)
        kf_ref[PAST_LEN:KEYS_DEC, :] = k_rot.astype(BF)
        vf_ref[0:PAST_LEN, :] = cv_ref[...].astype(BF)
        vf_ref[PAST_LEN:KEYS_DEC, :] = v_ref[...].astype(BF)

    lam = _diff_lambda(lam_ref[...]) + lam_init
    q = q_ref[...]
    q = q * cq_ref[...] + _rope_swap(q) * sq_ref[...]
    _diff_heads(q,
                lambda h: kf_ref[:, h * DIFF_V:(h + 1) * DIFF_V],
                lambda h: vf_ref[:, h * DIFF_V:(h + 1) * DIFF_V],
                lam, g_ref[...], o_ref, lam_init)


def _diff_dec(p_small, cache_k, cache_v, tabs, diff_lambda, g_diffnorm, mix, l):
    tq = 512
    nq = DEC_SEQ // tq
    row_blk = N_CTX // DEC_SEQ
    c512, s512 = tabs
    return pl.pallas_call(
        functools.partial(_diff_dec_kernel, lam_init=_lam_init(l)),
        grid=(DEC_BATCH, nq),
        in_specs=[
            pl.BlockSpec((tq, DIFF_WIDTH), lambda b, i: (N_CTX // tq + b * nq + i, C_DQ // DIFF_WIDTH)),
            pl.BlockSpec((DEC_SEQ, DIFF_WIDTH), lambda b, i: (row_blk + b, C_DK // DIFF_WIDTH)),
            pl.BlockSpec((DEC_SEQ, DIFF_WIDTH), lambda b, i: (row_blk + b, C_DV // DIFF_WIDTH)),
            pl.BlockSpec((None, None, PAST_LEN, DIFF_WIDTH), lambda b, i: (b, l, 0, 0)),
            pl.BlockSpec((None, None, PAST_LEN, DIFF_WIDTH), lambda b, i: (b, l, 0, 0)),
            pl.BlockSpec((tq, DIFF_WIDTH), lambda b, i: (i, 0)),
            pl.BlockSpec((tq, DIFF_WIDTH), lambda b, i: (i, 0)),
            pl.BlockSpec((DEC_SEQ, DIFF_WIDTH), lambda b, i: (0, 0)),
            pl.BlockSpec((DEC_SEQ, DIFF_WIDTH), lambda b, i: (0, 0)),
            pl.BlockSpec((None, 4, DIFF_QK), lambda b, i: (l, 0, 0)),
            pl.BlockSpec((None, 1, DIFF_V), lambda b, i: (l, 0, 0)),
            pl.BlockSpec(memory_space=pl.ANY),
        ],
        out_specs=pl.BlockSpec((tq, DIFF_WIDTH), lambda b, i: (N_CTX // tq + b * nq + i, 0)),
        out_shape=jax.ShapeDtypeStruct((N_TOK, DIFF_WIDTH), BF),
        input_output_aliases={11: 0},
        scratch_shapes=[
            pltpu.VMEM((KEYS_DEC, DIFF_WIDTH), BF),
            pltpu.VMEM((KEYS_DEC, DIFF_WIDTH), BF),
        ],
        compiler_params=_cparams(("arbitrary", "arbitrary")),
        name="diff_dec",
    )(p_small, p_small, p_small, cache_k, cache_v, c512, s512, c512, s512,
      diff_lambda, g_diffnorm.reshape(DEPTH, 1, -1), mix)


def _pool_kernel(z_ref, w_ref, ps_ref, o_ref, *, tm):
    seq_m1 = jnp.where(pl.program_id(0) * tm < N_CTX, SEQ - 1, DEC_SEQ - 1)
    t = lax.broadcasted_iota(jnp.int32, (tm, POOL_GROUP), 0) & seq_m1
    for gi, w in enumerate(POOL_WINDOWS):
        z = z_ref[:, gi * POOL_GROUP:(gi + 1) * POOL_GROUP]
        acc = jnp.zeros_like(z)
        for k in range(-(w // 2), w // 2):
            zs = z if k == 0 else pltpu.roll(z, (-k) % tm, 0)
            ok = (t + k >= 0) & (t + k <= seq_m1)
            acc = acc + jnp.where(ok, zs, 0.0)
        lo = jnp.maximum(t - w // 2, 0)
        hi = jnp.minimum(t + (w - 1) // 2, seq_m1)
        d = (acc / (hi - lo + 1).astype(F32) - z).astype(BF)
        y = jnp.dot(d, w_ref[gi], preferred_element_type=F32)
        y = y * ps_ref[:, gi * POOL_GROUP:(gi + 1) * POOL_GROUP]
        o_ref[:, gi * POOL_GROUP:(gi + 1) * POOL_GROUP] = y.astype(o_ref.dtype)


def _pool(p_small, w_pool_bf, pool_scale, l):
    tm = DEC_SEQ
    return pl.pallas_call(
        functools.partial(_pool_kernel, tm=tm),
        grid=(N_TOK // tm,),
        in_specs=[
            pl.BlockSpec((tm, POOL_WIDTH), lambda i: (i, C_PZ // POOL_WIDTH)),
            pl.BlockSpec((None, len(POOL_WINDOWS), POOL_GROUP, POOL_GROUP), lambda i: (l, 0, 0, 0)),
            pl.BlockSpec((None, 1, POOL_WIDTH), lambda i: (l, 0, 0)),
        ],
        out_specs=pl.BlockSpec((tm, POOL_WIDTH), lambda i: (i, 0)),
        out_shape=jax.ShapeDtypeStruct((N_TOK, POOL_WIDTH), BF),
        compiler_params=_cparams(("arbitrary",)),
        name="pool",
    )(p_small, w_pool_bf, pool_scale.reshape(DEPTH, 1, POOL_WIDTH))


def _merge_kernel(a_ref, p_ref, d_ref, g0_ref, g1_ref, g2_ref, wa_ref, wp_ref, wd_ref, o_ref):
    def sig(r):
        return jax.nn.sigmoid(r[...].astype(F32))

    def proj(x_ref, w_ref):
        return jnp.dot(x_ref[...], w_ref[...].astype(BF), preferred_element_type=F32)

    m = sig(g0_ref) * proj(a_ref, wa_ref)
    m = m + sig(g1_ref) * proj(p_ref, wp_ref)
    m = m + sig(g2_ref) * proj(d_ref, wd_ref)
    o_ref[...] = m.astype(o_ref.dtype)


def _merge(mla_o, pool_o, diff_o, gl, w_br_mla, w_br_pool, w_br_diff, l):
    tm, tn = 1024, 512
    nj = D_MODEL // tn
    return pl.pallas_call(
        _merge_kernel,
        grid=(N_TOK // tm, nj),
        in_specs=[
            pl.BlockSpec((tm, MLA_HEADS * MLA_V), lambda i, j: (i, 0)),
            pl.BlockSpec((tm, POOL_WIDTH), lambda i, j: (i, 0)),
            pl.BlockSpec((tm, DIFF_WIDTH), lambda i, j: (i, 0)),
            pl.BlockSpec((tm, tn), lambda i, j: (i, j)),
            pl.BlockSpec((tm, tn), lambda i, j: (i, nj + j)),
            pl.BlockSpec((tm, tn), lambda i, j: (i, 2 * nj + j)),
            pl.BlockSpec((None, MLA_HEADS * MLA_V, tn), lambda i, j: (l, 0, j)),
            pl.BlockSpec((None, POOL_WIDTH, tn), lambda i, j: (l, 0, j)),
            pl.BlockSpec((None, DIFF_WIDTH, tn), lambda i, j: (l, 0, j)),
        ],
        out_specs=pl.BlockSpec((tm, tn), lambda i, j: (i, j)),
        out_shape=jax.ShapeDtypeStruct((N_TOK, D_MODEL), BF),
        compiler_params=_cparams(("arbitrary", "arbitrary")),
        name="merge",
    )(mla_o, pool_o, diff_o, gl, gl, gl, w_br_mla, w_br_pool, w_br_diff)


def _outproj_kernel(x_ref, m_ref, w_ref, ga_ref, g2_ref, sh_ref, sc_ref, xo_ref, ht_ref, wb_ref, *, tm):
    @pl.when(pl.program_id(0) == 0)
    def _():
        wb_ref[...] = w_ref[...].astype(BF)

    seg = _segment(pl.program_id(0) * tm)
    y = jnp.dot(m_ref[...], wb_ref[...], preferred_element_type=F32)
    xn = x_ref[...] + ga_ref[pl.ds(seg, 1), :] * y
    xo_ref[...] = xn
    h2 = _rms(xn, g2_ref[...]) * (1.0 + sc_ref[pl.ds(seg, 1), :]) + sh_ref[pl.ds(seg, 1), :]
    ht_ref[...] = h2.T.astype(BF)


def _out_projection(x, merged, w_out, mod, g_norm2, l):
    tm = 256
    return pl.pallas_call(
        functools.partial(_outproj_kernel, tm=tm),
        grid=(N_TOK // tm,),
        in_specs=[
            pl.BlockSpec((tm, D_MODEL), lambda i: (i, 0)),
            pl.BlockSpec((tm, D_MODEL), lambda i: (i, 0)),
            pl.BlockSpec((None, D_MODEL, D_MODEL), lambda i: (l, 0, 0), pipeline_mode=pl.Buffered(1)),
            pl.BlockSpec((None, 8, D_MODEL), lambda i: (l, 0, 2)),
            pl.BlockSpec((None, 1, D_MODEL), lambda i: (l, 0, 0)),
            pl.BlockSpec((None, 8, D_MODEL), lambda i: (l, 0, 3)),
            pl.BlockSpec((None, 8, D_MODEL), lambda i: (l, 0, 4)),
        ],
        out_specs=[
            pl.BlockSpec((tm, D_MODEL), lambda i: (i, 0)),
            pl.BlockSpec((D_MODEL, tm), lambda i: (0, i)),
        ],
        out_shape=[
            jax.ShapeDtypeStruct((N_TOK, D_MODEL), F32),
            jax.ShapeDtypeStruct((D_MODEL, N_TOK), BF),
        ],
        scratch_shapes=[pltpu.VMEM((D_MODEL, D_MODEL), BF)],
        compiler_params=_cparams(("arbitrary",)),
        name="out_projection",
    )(x, merged, w_out, mod, g_norm2.reshape(DEPTH, 1, D_MODEL), mod, mod)


ROUTE_LANES = 128
NOT_SELECTED = 127


SLAB = 8
NET_WIDTH = 16


def _sorting_network(n):
    def merge(lo, hi, r):
        step = r * 2
        if step < hi - lo:
            yield from merge(lo, hi, step)
            yield from merge(lo + r, hi, step)
            yield from ((i, i + r) for i in range(lo + r, hi - r, step))
        else:
            yield (lo, lo + r)

    def sort(lo, hi):
        if hi - lo >= 1:
            mid = lo + (hi - lo) // 2
            yield from sort(lo, mid)
            yield from sort(mid + 1, hi)
            yield from merge(lo, hi, 1)

    return tuple(sort(0, n - 1))


def _extract_top(x, n):
    slabs = [x[SLAB * v:SLAB * (v + 1)] for v in range(x.shape[0] // SLAB)]
    slabs += [None] * (NET_WIDTH - len(slabs))
    for i, j in _sorting_network(NET_WIDTH):
        hi, lo = slabs[i], slabs[j]
        if lo is None:
            continue
        if hi is None:
            slabs[i], slabs[j] = lo, None
        else:
            slabs[i], slabs[j] = jnp.maximum(hi, lo), jnp.minimum(hi, lo)
    stack = [s for s in slabs if s is not None]
    sub = lax.broadcasted_iota(jnp.int32, stack[0].shape, 0)
    vals = []
    for r in range(n):
        m = jnp.max(stack[0], axis=0, keepdims=True)
        vals.append(m)
        hit = stack[0] == m
        popped = sub == jnp.min(jnp.where(hit, sub, SLAB), axis=0, keepdims=True)
        for d in range(min(n - r - 1, len(stack))):
            below = stack[d + 1] if d + 1 < len(stack) else -jnp.inf
            stack[d] = jnp.where(popped, below, stack[d])
    return vals


def _stack_rows(rows):
    idx = lax.broadcasted_iota(jnp.int32, (len(rows), rows[0].shape[1]), 0)
    m = jnp.zeros((len(rows), rows[0].shape[1]), F32)
    for r, row in enumerate(rows):
        m = jnp.where(idx == r, row, m)
    return m


def _route_chunk(s1, s2):
    k = PEER_TOPK
    a = _extract_top(s1, k)
    b = _extract_top(s2, k)
    am, bm = _stack_rows(a), _stack_rows(b)
    slabs = [a[0] + bm[0:8], a[0] + bm[8:16]]
    slabs += [a[r] + bm[0:8] for r in range(1, 8)]
    slabs += [am[8:16] + b[0]]
    cand = jnp.concatenate(slabs, axis=0)
    tau = _extract_top(cand, k)[-1]
    sel = cand >= tau
    top = a[0] + b[0]
    cnt = jnp.where(sel, 1.0, 0.0)
    low = jnp.where(sel, cand, jnp.inf)

    def row_rows(r):
        return slice(0, 16) if r == 0 else slice(8 + 8 * r, 16 + 8 * r) if r < 8 else slice(64 + r, 65 + r)

    n_r = [jnp.sum(cnt[row_rows(r)], axis=0, keepdims=True) for r in range(k)]
    low_r = [jnp.min(low[row_rows(r)], axis=0, keepdims=True) for r in range(k)]
    total = n_r[0]
    for r in range(1, k):
        total = total + n_r[r]
    excess = total - float(k)
    for r in reversed(range(k)):
        drop = jnp.where(excess > 0.0, jnp.where(low_r[r] == tau, 1.0, 0.0), 0.0)
        n_r[r] = n_r[r] - drop
        excess = excess - drop
    dropped = total - float(k) - excess
    z = jnp.sum(jnp.where(sel, jnp.exp(cand - top), 0.0), axis=0, keepdims=True) - dropped * jnp.exp(tau - top)

    c1 = jnp.full(s1.shape, -1.0, F32)
    r2 = jnp.full(s2.shape, float(NOT_SELECTED), F32)
    for r in range(k - 1):
        c1 = jnp.where(s1 == a[r], n_r[r] - 1.0, c1)
        r2 = jnp.where(s2 == b[r], float(r), r2)
    key = lax.broadcasted_iota(jnp.int32, s1.shape, 0)

    def last_key_limit(s, vals):
        match = s == vals[k - 1]
        first = jnp.min(jnp.where(match, key, PEER_KEYS), axis=0, keepdims=True)
        return match, jnp.where(vals[k - 2] != vals[k - 1], first, PEER_KEYS)

    match1, limit1 = last_key_limit(s1, a)
    c1 = jnp.where(match1, jnp.where(key <= limit1, n_r[k - 1] - 1.0, c1), c1)
    match2, limit2 = last_key_limit(s2, b)
    r2 = jnp.where(match2, jnp.where(key <= limit2, float(k - 1), r2), r2)
    e1 = jnp.exp(s1 - a[0]) / z
    e2 = jnp.exp(s2 - b[0])
    return c1, e1, r2, e2


ROUTE_HEADS = 8


def _route_kernel(ht_ref, wq_ref, sk_ref, c1_ref, e1_ref, r2_ref, e2_ref):
    def pack_row_pairs(x):
        return pltpu.bitcast(x.astype(BF), jnp.int32)

    def duplicate_halves(x):
        hi = pltpu.bitcast(x.astype(BF).astype(F32), jnp.int32)
        return hi | lax.shift_right_logical(hi, 16)

    half = PEER_KEYS // 2
    for hh in range(ROUTE_HEADS):
        wq = wq_ref[:, hh * PEER_QDIM:(hh + 1) * PEER_QDIM].astype(BF)
        qt = lax.dot_general(wq, ht_ref[...], (((0,), (0,)), ((), ())), preferred_element_type=F32)
        s1 = jnp.dot(sk_ref[0], qt[0:PEER_KEYS].astype(BF), preferred_element_type=F32)
        s2 = jnp.dot(sk_ref[1], qt[PEER_KEYS:].astype(BF), preferred_element_type=F32)
        for c in range(s1.shape[1] // ROUTE_LANES):
            cs = slice(c * ROUTE_LANES, (c + 1) * ROUTE_LANES)
            c1, e1, r2, e2 = _route_chunk(s1[:, cs], s2[:, cs])
            c1_ref[hh * PEER_KEYS:(hh + 1) * PEER_KEYS, cs] = duplicate_halves(c1)
            e1_ref[hh * PEER_KEYS:(hh + 1) * PEER_KEYS, cs] = duplicate_halves(e1)
            r2_ref[hh * half:(hh + 1) * half, cs] = pack_row_pairs(r2)
            e2_ref[hh * half:(hh + 1) * half, cs] = pack_row_pairs(e2)


def _route(h2t, w_peer_q, subkeys_bf, l):
    tn = 512
    rows = PEER_HEADS * PEER_KEYS
    spec = pl.BlockSpec((ROUTE_HEADS * PEER_KEYS, tn), lambda t, h: (h, t))
    pair_spec = pl.BlockSpec((ROUTE_HEADS * PEER_KEYS // 2, tn), lambda t, h: (h, t))
    dup = jax.ShapeDtypeStruct((rows, N_TOK), jnp.int32)
    pairs = jax.ShapeDtypeStruct((rows // 2, N_TOK), jnp.int32)
    return pl.pallas_call(
        _route_kernel,
        grid=(N_TOK // tn, PEER_HEADS // ROUTE_HEADS),
        in_specs=[
            pl.BlockSpec((D_MODEL, tn), lambda t, h: (0, t)),
            pl.BlockSpec((None, D_MODEL, ROUTE_HEADS * PEER_QDIM), lambda t, h: (l, 0, h)),
            pl.BlockSpec((None, 2, PEER_KEYS, PEER_KEYS), lambda t, h: (l, 0, 0, 0)),
        ],
        out_specs=[spec, spec, pair_spec, pair_spec],
        out_shape=[dup, dup, pairs, pairs],
        compiler_params=_cparams(("arbitrary", "arbitrary")),
        name="peer_route",
    )(h2t, w_peer_q, subkeys_bf)


PEER_LANES = 128
PEER_BLK = 256
PEER_FIRST_TOKENS = 1024


def _peer_kernel(*refs, tm, te, convert):
    if convert:
        (ht_ref, u_ref, v_ref, c1_ref, e1_ref, r2_ref, e2_ref, o_ref, ub_ref, vtb_ref,
         at0_ref, at1_ref, wg0_ref, wg1_ref) = refs
    else:
        (ht_ref, u_ref, vt_ref, c1_ref, e1_ref, r2_ref, e2_ref, prev_hbm, o_ref,
         at0_ref, at1_ref, wg0_ref, wg1_ref) = refs
        del prev_hbm
    e = pl.program_id(1)
    nblk = te // PEER_BLK
    half = PEER_KEYS // 2

    @pl.when(e == 0)
    def _():
        o_ref[...] = jnp.zeros_like(o_ref)

    at_bufs = (at0_ref, at1_ref)
    wg_bufs = (wg0_ref, wg1_ref)

    def scores(q):
        rows = slice(q * PEER_BLK, (q + 1) * PEER_BLK)
        u = u_ref[rows, :]
        if convert:
            u = u.astype(BF)
            ub_ref[rows, :] = u
        at_bufs[q % 2][...] = jnp.dot(u, ht_ref[...], preferred_element_type=F32)

    def weights(q):
        for bb in range(PEER_BLK // PEER_KEYS):
            i = (e * nblk + q) * (PEER_BLK // PEER_KEYS) + bb
            bs = slice(bb * PEER_KEYS, (bb + 1) * PEER_KEYS)
            last_rows = [c1_ref[pl.ds(h * PEER_KEYS + i, 1), :] for h in range(PEER_HEADS)]
            e1_rows = [e1_ref[pl.ds(h * PEER_KEYS + i, 1), :] for h in range(PEER_HEADS)]
            for c in range(tm // PEER_LANES):
                cs = slice(c * PEER_LANES, (c + 1) * PEER_LANES)
                w = jnp.zeros((PEER_KEYS, PEER_LANES), BF)
                for h in range(PEER_HEADS):
                    hs = slice(h * half, (h + 1) * half)
                    last = pltpu.bitcast(jnp.broadcast_to(last_rows[h][:, cs], (half, PEER_LANES)), BF)
                    e1 = pltpu.bitcast(jnp.broadcast_to(e1_rows[h][:, cs], (half, PEER_LANES)), BF)
                    r2 = pltpu.bitcast(r2_ref[hs, cs], BF)
                    e2 = pltpu.bitcast(e2_ref[hs, cs], BF)
                    w = w + jnp.where(r2 <= last, e2, jnp.zeros_like(e2)) * e1
                a = at_bufs[q % 2][bs, cs]
                g = (0.5 * a * (1.0 + lax.erf(a * math.sqrt(0.5)))).astype(BF)
                wg_bufs[q % 2][bs, cs] = w * g

    def update(q):
        cols = slice(q * PEER_BLK, (q + 1) * PEER_BLK)
        if convert:
            vt = v_ref[cols, :].T.astype(BF)
            vtb_ref[:, cols] = vt
        else:
            vt = vt_ref[:, cols]
        o_ref[...] += jnp.dot(vt, wg_bufs[q % 2][...], preferred_element_type=F32)

    scores(0)
    for q in range(nblk):
        if q + 1 < nblk:
            scores(q + 1)
        weights(q)
        if q >= 1:
            update(q - 1)
    update(nblk - 1)


def _peer_scratch(tm):
    return [pltpu.VMEM((PEER_BLK, tm), F32), pltpu.VMEM((PEER_BLK, tm), F32),
            pltpu.VMEM((PEER_BLK, tm), BF), pltpu.VMEM((PEER_BLK, tm), BF)]


def _peer(h2t, peer_u, peer_v, routing, l):
    c1, e1, r2, e2 = routing
    rows = PEER_HEADS * PEER_KEYS
    out_sds = jax.ShapeDtypeStruct((D_MODEL, N_TOK), F32)

    tm, te = PEER_FIRST_TOKENS, 512
    once = pl.Buffered(1)
    rspec = pl.BlockSpec((rows, tm), lambda t, e: (0, 0), pipeline_mode=once)
    pspec = pl.BlockSpec((rows // 2, tm), lambda t, e: (0, 0), pipeline_mode=once)
    first, u_bf, vt_bf = pl.pallas_call(
        functools.partial(_peer_kernel, tm=tm, te=te, convert=True),
        grid=(1, PEER_EXPERTS // te),
        in_specs=[
            pl.BlockSpec((D_MODEL, tm), lambda t, e: (0, 0), pipeline_mode=once),
            pl.BlockSpec((None, te, D_MODEL), lambda t, e: (l, e, 0)),
            pl.BlockSpec((None, te, D_MODEL), lambda t, e: (l, e, 0)),
            rspec, rspec, pspec, pspec,
        ],
        out_specs=[
            pl.BlockSpec((D_MODEL, tm), lambda t, e: (0, 0), pipeline_mode=once),
            pl.BlockSpec((te, D_MODEL), lambda t, e: (e, 0)),
            pl.BlockSpec((D_MODEL, te), lambda t, e: (0, e)),
        ],
        out_shape=[out_sds, jax.ShapeDtypeStruct((PEER_EXPERTS, D_MODEL), BF),
                   jax.ShapeDtypeStruct((D_MODEL, PEER_EXPERTS), BF)],
        scratch_shapes=_peer_scratch(tm),
        compiler_params=_cparams(("arbitrary", "arbitrary")),
        name="peer_dense_first",
    )(h2t, peer_u, peer_v, c1, e1, r2, e2)

    tm, te = 512, 1024
    t0 = PEER_FIRST_TOKENS // tm
    rspec = pl.BlockSpec((rows, tm), lambda t, e: (0, t + t0))
    pspec = pl.BlockSpec((rows // 2, tm), lambda t, e: (0, t + t0))
    return pl.pallas_call(
        functools.partial(_peer_kernel, tm=tm, te=te, convert=False),
        grid=(N_TOK // tm - t0, PEER_EXPERTS // te),
        in_specs=[
            pl.BlockSpec((D_MODEL, tm), lambda t, e: (0, t + t0)),
            pl.BlockSpec((te, D_MODEL), lambda t, e: (e, 0)),
            pl.BlockSpec((D_MODEL, te), lambda t, e: (0, e)),
            rspec, rspec, pspec, pspec,
            pl.BlockSpec(memory_space=pl.ANY),
        ],
        out_specs=pl.BlockSpec((D_MODEL, tm), lambda t, e: (0, t + t0)),
        out_shape=out_sds,
        input_output_aliases={7: 0},
        scratch_shapes=_peer_scratch(tm),
        compiler_params=_cparams(("arbitrary", "arbitrary")),
        name="peer_dense",
    )(h2t, u_bf, vt_bf, c1, e1, r2, e2, first)


def _final_kernel(x_ref, pt_ref, ga_ref, g_ref, ctx_ref, dec_ref, *, tm, n_ctx_tiles):
    seg = _segment(pl.program_id(0) * tm)
    x = x_ref[...] + ga_ref[pl.ds(seg, 1), :] * pt_ref[...].T
    y = _rms(x, g_ref[...])
    is_ctx = pl.program_id(0) < n_ctx_tiles

    @pl.when(is_ctx)
    def _():
        ctx_ref[...] = y

    @pl.when(jnp.logical_not(is_ctx))
    def _():
        dec_ref[...] = y


def _final_norm(x, peer_t, mod, g_final):
    tm = 512
    n_ctx_tiles = N_CTX // tm
    return pl.pallas_call(
        functools.partial(_final_kernel, tm=tm, n_ctx_tiles=n_ctx_tiles),
        grid=(N_TOK // tm,),
        in_specs=[pl.BlockSpec((tm, D_MODEL), lambda i: (i, 0)),
                  pl.BlockSpec((D_MODEL, tm), lambda i: (0, i)),
                  pl.BlockSpec((None, 8, D_MODEL), lambda i: (DEPTH - 1, 0, 5)),
                  pl.BlockSpec((1, D_MODEL), lambda i: (0, 0))],
        out_specs=[
            pl.BlockSpec((tm, D_MODEL), lambda i: (jnp.minimum(i, n_ctx_tiles - 1), 0)),
            pl.BlockSpec((tm, D_MODEL), lambda i: (jnp.maximum(i - n_ctx_tiles, 0), 0)),
        ],
        out_shape=[jax.ShapeDtypeStruct((N_CTX, D_MODEL), F32), jax.ShapeDtypeStruct((N_DEC, D_MODEL), F32)],
        compiler_params=_cparams(("arbitrary",)),
        name="final_norm",
    )(x, peer_t, mod, g_final.reshape(1, D_MODEL))


def _rope_tables():
    t = jnp.arange(DEC_SEQ)
    half = MLA_ROPE // 4
    inv = ROPE_BASE ** (-jnp.arange(half, dtype=F32) / half)
    ang_r = (t // GRID_W).astype(F32)[:, None] * inv
    ang_c = (t % GRID_W).astype(F32)[:, None] * inv
    cos = jnp.concatenate([jnp.cos(ang_r)] * 2 + [jnp.cos(ang_c)] * 2, axis=1)
    sin = jnp.concatenate([-jnp.sin(ang_r), jnp.sin(ang_r), -jnp.sin(ang_c), jnp.sin(ang_c)], axis=1)
    return cos, sin


def _relayout_w_qb(w_qb):
    w = w_qb.reshape(DEPTH, MLA_Q_RANK, MLA_HEADS, MLA_NOPE + MLA_ROPE)
    w = jnp.pad(w, ((0, 0), (0, 0), (0, 0), (0, MLA_HEAD_PAD - MLA_NOPE - MLA_ROPE)))
    return w.reshape(DEPTH, MLA_Q_RANK, MLA_HEADS * MLA_HEAD_PAD).astype(BF)


def kernel(x_prompt, x_sample, cache_mla_ckv, cache_mla_krope, cache_diff_k, cache_diff_v, c, c_ctx,
           w_mod, b_mod, g_norm1, w_in, g_qnorm, w_qb, g_kvnorm, w_kvb, w_pool, pool_scale,
           diff_lambda, g_diffnorm, w_br_mla, w_br_pool, w_br_diff, w_out, g_norm2,
           w_peer_q, peer_subkeys, peer_u, peer_v, g_final):
    x = jnp.concatenate([x_prompt.reshape(N_CTX, D_MODEL), x_sample.reshape(N_DEC, D_MODEL)], axis=0)
    cond8 = jnp.concatenate([c_ctx[None, :], c, jnp.zeros((8 - 1 - DEC_BATCH, D_MODEL), F32)], axis=0)

    w_in_t = jnp.swapaxes(w_in, 1, 2)
    w_qb_re = _relayout_w_qb(w_qb)
    w_pool_bf = w_pool.astype(BF)
    subkeys_bf = peer_subkeys.astype(BF)

    cache_kr_pad = jnp.pad(cache_mla_krope, ((0, 0), (0, 0), (0, 0), (0, LANES - MLA_ROPE)))
    cache_k = cache_diff_k.reshape(DEC_BATCH, DEPTH, PAST_LEN, DIFF_WIDTH)
    cache_v = cache_diff_v.reshape(DEC_BATCH, DEPTH, PAST_LEN, DIFF_WIDTH)

    cos64, sin64 = _rope_tables()
    ones, zeros = jnp.ones_like(cos64), jnp.zeros_like(cos64)
    mla_tabs = (
        jnp.concatenate([ones, ones, cos64, ones], axis=1),
        jnp.concatenate([zeros, zeros, sin64, zeros], axis=1),
        jnp.concatenate([cos64, ones], axis=1),
        jnp.concatenate([sin64, zeros], axis=1),
    )
    diff_tabs = (jnp.tile(cos64, (1, DIFF_WIDTH // DIFF_QK)), jnp.tile(sin64, (1, DIFF_WIDTH // DIFF_QK)))

    mod = _modulation(cond8, w_mod, b_mod)

    st_ckv = jnp.zeros((BATCH, DEPTH, SEQ, MLA_KV_RANK), F32)
    st_kr = jnp.zeros((BATCH, DEPTH, SEQ, MLA_ROPE), F32)
    st_k = jnp.zeros((BATCH, DEPTH, SEQ, DIFF_WIDTH), F32)
    st_v = jnp.zeros((BATCH, DEPTH, SEQ, DIFF_WIDTH), F32)
    peer_t = None
    for l in range(DEPTH):
        if l == 0:
            p_small = _in_projection(x, g_norm1, mod, w_in_t, l, 0, SMALL_WIDTH, F32)
        else:
            p_small, x = _in_projection_res(x, peer_t, g_norm1, mod, w_in_t, l, SMALL_WIDTH)
        gl = _in_projection(x, g_norm1, mod, w_in_t, l, SMALL_WIDTH, GATE_WIDTH, BF)
        mla_c, st_ckv, st_kr = _mla_ctx(p_small, g_qnorm, g_kvnorm, w_qb_re, w_kvb, st_ckv, st_kr, l)
        mla_o = _mla_dec(p_small, cache_mla_ckv, cache_kr_pad, mla_tabs, g_qnorm, g_kvnorm, w_qb_re, w_kvb,
                         mla_c, l)
        diff_c, st_k, st_v = _diff_ctx(p_small, diff_lambda, g_diffnorm, st_k, st_v, l)
        diff_o = _diff_dec(p_small, cache_k, cache_v, diff_tabs, diff_lambda, g_diffnorm, diff_c, l)
        pool_o = _pool(p_small, w_pool_bf, pool_scale, l)
        merged = _merge(mla_o, pool_o, diff_o, gl, w_br_mla, w_br_pool, w_br_diff, l)
        x, h2t = _out_projection(x, merged, w_out, mod, g_norm2, l)
        routing = _route(h2t, w_peer_q, subkeys_bf, l)
        peer_t = _peer(h2t, peer_u, peer_v, routing, l)

    y_ctx, y_dec = _final_norm(x, peer_t, mod, g_final)
    return (y_ctx.reshape(BATCH, SEQ, D_MODEL), y_dec.reshape(DEC_BATCH, DEC_SEQ, D_MODEL), st_ckv, st_kr,
            st_k.reshape(BATCH, DEPTH, SEQ, DIFF_HEADS, 2 * DIFF_QK),
            st_v.reshape(BATCH, DEPTH, SEQ, DIFF_HEADS, DIFF_V))
```

```python
import functools
import math

import jax
import jax.numpy as jnp
from jax import lax
from jax.experimental import pallas as pl
from jax.experimental.pallas import tpu as pltpu

BF = jnp.bfloat16
F32 = jnp.float32

D_MODEL = 2048
BATCH = 16
SEQ = 256
DEPTH = 4
DEC_BATCH = 2
DEC_SEQ = 1024
PAST_LEN = 256
GRID_W = 64
ROPE_BASE = 10000.0
EPS = 1e-6

MLA_HEADS = 8
MLA_NOPE = 128
MLA_ROPE = 64
MLA_V = 128
MLA_Q_RANK = 512
MLA_KV_RANK = 256
MLA_HEAD_PAD = 256
MLA_KV_HEAD = MLA_NOPE + MLA_V
LANES = 128
CTX_SEQS = 4
ROPE_HALF = MLA_ROPE // 4

POOL_WINDOWS = (2, 4, 8, 16)
POOL_GROUP = 128
POOL_WIDTH = POOL_GROUP * len(POOL_WINDOWS)

DIFF_HEADS = 4
DIFF_QK = 64
DIFF_V = 2 * DIFF_QK
DIFF_WIDTH = DIFF_HEADS * DIFF_V

PEER_HEADS = 8
PEER_KEYS = 128
PEER_EXPERTS = PEER_KEYS * PEER_KEYS
PEER_TOPK = 16
PEER_QDIM = 256

N_CTX = BATCH * SEQ
N_DEC = DEC_BATCH * DEC_SEQ
N_TOK = N_CTX + N_DEC
KEYS_DEC = PAST_LEN + DEC_SEQ

C_QA = 0
C_CKV = 512
C_KR = 768
KR_BLOCK = 256
C_PZ = 1024
C_DQ = 1536
C_DK = 2048
C_DV = 2560
SMALL_WIDTH = 3072
GATE_WIDTH = 3 * D_MODEL

KV_END = 512 + 320
INPROJ_TN = 512

VMEM_LIMIT = 56 * 1024 * 1024


def _cparams(sem):
    return pltpu.CompilerParams(dimension_semantics=sem, vmem_limit_bytes=VMEM_LIMIT)


def _segment(row0):
    return jnp.where(row0 < N_CTX, 0, 1 + (row0 - N_CTX) // DEC_SEQ)


def _rms(x, g):
    return x * lax.rsqrt(jnp.mean(x * x, axis=-1, keepdims=True) + EPS) * g


def _softmax_rows(s):
    m = jnp.max(s, axis=-1, keepdims=True)
    p = jnp.exp(s - m)
    return p / jnp.sum(p, axis=-1, keepdims=True)


def _dot_nt(a, b):
    return lax.dot_general(a, b, (((1,), (1,)), ((), ())), preferred_element_type=F32)


def _rope_swap(x):
    n = x.shape[-1]
    up = pltpu.roll(x, n - ROPE_HALF, 1)
    dn = pltpu.roll(x, ROPE_HALF, 1)
    lane = lax.broadcasted_iota(jnp.int32, x.shape, 1)
    return jnp.where((lane & ROPE_HALF) == 0, up, dn)


def _mod_kernel(c_ref, w_ref, b_ref, o_ref):
    c = c_ref[...]
    a = (c * jax.nn.sigmoid(c)).astype(BF)
    o_ref[...] = jnp.dot(a, w_ref[...].astype(BF), preferred_element_type=F32) + b_ref[...]


def _modulation(cond8, w_mod, b_mod):
    tn = 1024
    return pl.pallas_call(
        _mod_kernel,
        grid=(DEPTH, 6 * D_MODEL // tn),
        in_specs=[
            pl.BlockSpec((8, D_MODEL), lambda l, j: (0, 0)),
            pl.BlockSpec((None, D_MODEL, tn), lambda l, j: (l, 0, j)),
            pl.BlockSpec((None, 1, tn), lambda l, j: (l, 0, j)),
        ],
        out_specs=pl.BlockSpec((None, 8, tn), lambda l, j: (l, 0, j)),
        out_shape=jax.ShapeDtypeStruct((DEPTH, 8, 6 * D_MODEL), F32),
        compiler_params=_cparams(("arbitrary", "arbitrary")),
        name="modulation",
    )(cond8, w_mod, b_mod.reshape(DEPTH, 1, 6 * D_MODEL))


def _w_in_row(tile):
    per_tile = INPROJ_TN // 64
    return 64 * jnp.where(tile < 2, per_tile * tile, per_tile * tile - (C_PZ - KV_END) // 64)


def _inproj_tile(h_ref, w_ref, tile):
    y = _dot_nt(h_ref[...], w_ref[0].astype(BF))
    lane = lax.broadcasted_iota(jnp.int32, y.shape, 1)
    return jnp.where(lane >= jnp.where(tile == 1, KV_END - INPROJ_TN, INPROJ_TN), 0.0, y)


def _inproj_kernel(x_ref, g_ref, sh_ref, sc_ref, w_ref, o_ref, h_ref, *, tm, tile0):
    i = pl.program_id(0)

    @pl.when(pl.program_id(1) == 0)
    def _():
        seg = _segment(i * tm)
        y = _rms(x_ref[...], g_ref[...])
        h_ref[...] = (y * (1.0 + sc_ref[pl.ds(seg, 1), :]) + sh_ref[pl.ds(seg, 1), :]).astype(BF)

    o_ref[...] = _inproj_tile(h_ref, w_ref, pl.program_id(1) + tile0).astype(o_ref.dtype)


def _w_in_spec(l, tile0):
    return pl.BlockSpec((pl.Element(1), pl.Element(INPROJ_TN), pl.Element(D_MODEL)),
                        lambda i, j: (l, _w_in_row(j + tile0), 0))


def _in_projection(x, g_norm1, mod, w_in_t, l, col0, width, out_dtype, tm=1024):
    tn = INPROJ_TN
    tile0 = col0 // tn
    return pl.pallas_call(
        functools.partial(_inproj_kernel, tm=tm, tile0=tile0),
        grid=(N_TOK // tm, width // tn),
        in_specs=[
            pl.BlockSpec((tm, D_MODEL), lambda i, j: (i, 0)),
            pl.BlockSpec((None, 1, D_MODEL), lambda i, j: (l, 0, 0)),
            pl.BlockSpec((None, 8, D_MODEL), lambda i, j: (l, 0, 0)),
            pl.BlockSpec((None, 8, D_MODEL), lambda i, j: (l, 0, 1)),
            _w_in_spec(l, tile0),
        ],
        out_specs=pl.BlockSpec((tm, tn), lambda i, j: (i, j)),
        out_shape=jax.ShapeDtypeStruct((N_TOK, width), out_dtype),
        scratch_shapes=[pltpu.VMEM((tm, D_MODEL), BF)],
        compiler_params=_cparams(("arbitrary", "arbitrary")),
        name="in_projection",
    )(x, g_norm1.reshape(DEPTH, 1, D_MODEL), mod, mod, w_in_t)


def _inproj_res_kernel(x_ref, pt_ref, ga_ref, g_ref, sh_ref, sc_ref, w_ref, o_ref, xo_ref, h_ref, *, tm):
    i = pl.program_id(0)

    @pl.when(pl.program_id(1) == 0)
    def _():
        seg = _segment(i * tm)
        x = x_ref[...] + ga_ref[pl.ds(seg, 1), :] * pt_ref[...].T
        xo_ref[...] = x
        y = _rms(x, g_ref[...])
        h_ref[...] = (y * (1.0 + sc_ref[pl.ds(seg, 1), :]) + sh_ref[pl.ds(seg, 1), :]).astype(BF)

    o_ref[...] = _inproj_tile(h_ref, w_ref, pl.program_id(1)).astype(o_ref.dtype)


def _in_projection_res(x, peer_t, g_norm1, mod, w_in_t, l, width):
    tm, tn = 1024, INPROJ_TN
    once = pl.Buffered(1)
    return pl.pallas_call(
        functools.partial(_inproj_res_kernel, tm=tm),
        grid=(N_TOK // tm, width // tn),
        in_specs=[
            pl.BlockSpec((tm, D_MODEL), lambda i, j: (i, 0), pipeline_mode=once),
            pl.BlockSpec((D_MODEL, tm), lambda i, j: (0, i), pipeline_mode=once),
            pl.BlockSpec((None, 8, D_MODEL), lambda i, j: (l - 1, 0, 5)),
            pl.BlockSpec((None, 1, D_MODEL), lambda i, j: (l, 0, 0)),
            pl.BlockSpec((None, 8, D_MODEL), lambda i, j: (l, 0, 0)),
            pl.BlockSpec((None, 8, D_MODEL), lambda i, j: (l, 0, 1)),
            _w_in_spec(l, 0),
        ],
        out_specs=[
            pl.BlockSpec((tm, tn), lambda i, j: (i, j)),
            pl.BlockSpec((tm, D_MODEL), lambda i, j: (i, 0)),
        ],
        out_shape=[jax.ShapeDtypeStruct((N_TOK, width), F32), jax.ShapeDtypeStruct((N_TOK, D_MODEL), F32)],
        scratch_shapes=[pltpu.VMEM((tm, D_MODEL), BF)],
        compiler_params=_cparams(("arbitrary", "arbitrary")),
        name="in_projection_res",
    )(x, peer_t, mod, g_norm1.reshape(DEPTH, 1, D_MODEL), mod, mod, w_in_t)


def _mla_heads(q, kfun, vfun, o_ref, scale, rows=slice(None)):
    for h in range(MLA_HEADS):
        s = _dot_nt(q(h), kfun(h)) * scale
        p = _softmax_rows(s).astype(BF)
        o = jnp.dot(p, vfun(h), preferred_element_type=F32)
        o_ref[rows, h * MLA_V:(h + 1) * MLA_V] = o.astype(o_ref.dtype)


def _mla_ctx_kernel(qa_ref, ckv_ref, kr_ref, gq_ref, gkv_ref, wqb_ref, wkvb_ref, st_ckv_hbm, st_kr_hbm,
                    o_ref, ckv_out_ref, kr_out_ref):
    del st_ckv_hbm, st_kr_hbm
    scale = (MLA_NOPE + MLA_ROPE) ** -0.5
    qn = _rms(qa_ref[...], gq_ref[...]).astype(BF)
    q = jnp.dot(qn, wqb_ref[...], preferred_element_type=F32).astype(BF)
    ckv = _rms(ckv_ref[...], gkv_ref[...])
    kr = kr_ref[...]
    kv = jnp.dot(ckv.astype(BF), wkvb_ref[...].astype(BF), preferred_element_type=F32).astype(BF)
    krp = kr[:, :LANES].astype(BF)
    for s in range(CTX_SEQS):
        rows = slice(s * SEQ, (s + 1) * SEQ)
        ckv_out_ref[s] = ckv[rows]
        kr_out_ref[s] = kr[rows, :MLA_ROPE]

        def qh(h, rows=rows):
            return q[rows, h * MLA_HEAD_PAD:(h + 1) * MLA_HEAD_PAD]

        def kh(h, rows=rows):
            return jnp.concatenate([kv[rows, h * MLA_KV_HEAD:h * MLA_KV_HEAD + MLA_NOPE], krp[rows]], axis=1)

        def vh(h, rows=rows):
            return kv[rows, h * MLA_KV_HEAD + MLA_NOPE:(h + 1) * MLA_KV_HEAD]

        _mla_heads(qh, kh, vh, o_ref, scale, rows)


def _mla_ctx(p_small, g_qnorm, g_kvnorm, w_qb_re, w_kvb, st_ckv, st_kr, l):
    return pl.pallas_call(
        _mla_ctx_kernel,
        grid=(BATCH // CTX_SEQS,),
        in_specs=[
            pl.BlockSpec((CTX_SEQS * SEQ, MLA_Q_RANK), lambda b: (b, C_QA // MLA_Q_RANK)),
            pl.BlockSpec((CTX_SEQS * SEQ, MLA_KV_RANK), lambda b: (b, C_CKV // MLA_KV_RANK)),
            pl.BlockSpec((CTX_SEQS * SEQ, KR_BLOCK), lambda b: (b, C_KR // KR_BLOCK)),
            pl.BlockSpec((None, 1, MLA_Q_RANK), lambda b: (l, 0, 0)),
            pl.BlockSpec((None, 1, MLA_KV_RANK), lambda b: (l, 0, 0)),
            pl.BlockSpec((None, MLA_Q_RANK, MLA_HEADS * MLA_HEAD_PAD), lambda b: (l, 0, 0)),
            pl.BlockSpec((None, MLA_KV_RANK, MLA_HEADS * MLA_KV_HEAD), lambda b: (l, 0, 0)),
            pl.BlockSpec(memory_space=pl.ANY),
            pl.BlockSpec(memory_space=pl.ANY),
        ],
        out_specs=[
            pl.BlockSpec((CTX_SEQS * SEQ, MLA_HEADS * MLA_V), lambda b: (b, 0)),
            pl.BlockSpec((CTX_SEQS, None, SEQ, MLA_KV_RANK), lambda b: (b, l, 0, 0)),
            pl.BlockSpec((CTX_SEQS, None, SEQ, MLA_ROPE), lambda b: (b, l, 0, 0)),
        ],
        out_shape=[
            jax.ShapeDtypeStruct((N_TOK, MLA_HEADS * MLA_V), BF),
            jax.ShapeDtypeStruct(st_ckv.shape, F32),
            jax.ShapeDtypeStruct(st_kr.shape, F32),
        ],
        input_output_aliases={7: 1, 8: 2},
        compiler_params=_cparams(("arbitrary",)),
        name="mla_ctx",
    )(p_small, p_small, p_small, g_qnorm.reshape(DEPTH, 1, -1), g_kvnorm.reshape(DEPTH, 1, -1),
      w_qb_re, w_kvb, st_ckv, st_kr)


def _mla_dec_kernel(qa_ref, ckv_ref, kr_ref, cckv_ref, ckr_ref, cq_ref, sq_ref, ck_ref, sk_ref,
                    gq_ref, gkv_ref, wqb_ref, wkvb_ref, mix_hbm, o_ref, kf_ref, vf_ref):
    del mix_hbm
    scale = (MLA_NOPE + MLA_ROPE) ** -0.5

    @pl.when(pl.program_id(1) == 0)
    def _():
        ckv = _rms(ckv_ref[...], gkv_ref[...])
        ckv_all = jnp.concatenate([cckv_ref[...], ckv], axis=0).astype(BF)
        kr = kr_ref[:, :LANES]
        kr_rot = kr * ck_ref[...] + _rope_swap(kr) * sk_ref[...]
        kr_all = jnp.concatenate([ckr_ref[...], kr_rot], axis=0).astype(BF)
        for h in range(MLA_HEADS):
            w_h = wkvb_ref[:, h * MLA_KV_HEAD:(h + 1) * MLA_KV_HEAD].astype(BF)
            kvh = jnp.dot(ckv_all, w_h, preferred_element_type=F32)
            kf_ref[h, :, 0:MLA_NOPE] = kvh[:, :MLA_NOPE].astype(BF)
            kf_ref[h, :, MLA_NOPE:MLA_HEAD_PAD] = kr_all
            vf_ref[h] = kvh[:, MLA_NOPE:].astype(BF)

    qn = _rms(qa_ref[...], gq_ref[...]).astype(BF)
    q = jnp.dot(qn, wqb_ref[...], preferred_element_type=F32)
    cq = cq_ref[...]
    sq = sq_ref[...]

    def qh(h):
        x = q[:, h * MLA_HEAD_PAD:(h + 1) * MLA_HEAD_PAD]
        return (x * cq + _rope_swap(x) * sq).astype(BF)

    _mla_heads(qh, lambda h: kf_ref[h], lambda h: vf_ref[h], o_ref, scale)


def _mla_dec(p_small, cache_ckv, cache_kr_pad, tabs, g_qnorm, g_kvnorm, w_qb_re, w_kvb, mix, l):
    tq = 512
    nq = DEC_SEQ // tq
    row_blk = N_CTX // DEC_SEQ
    cq, sq, ck, sk = tabs
    return pl.pallas_call(
        _mla_dec_kernel,
        grid=(DEC_BATCH, nq),
        in_specs=[
            pl.BlockSpec((tq, MLA_Q_RANK), lambda b, i: (N_CTX // tq + b * nq + i, 0)),
            pl.BlockSpec((DEC_SEQ, MLA_KV_RANK), lambda b, i: (row_blk + b, C_CKV // MLA_KV_RANK)),
            pl.BlockSpec((DEC_SEQ, KR_BLOCK), lambda b, i: (row_blk + b, C_KR // KR_BLOCK)),
            pl.BlockSpec((None, None, PAST_LEN, MLA_KV_RANK), lambda b, i: (b, l, 0, 0)),
            pl.BlockSpec((None, None, PAST_LEN, LANES), lambda b, i: (b, l, 0, 0)),
            pl.BlockSpec((tq, MLA_HEAD_PAD), lambda b, i: (i, 0)),
            pl.BlockSpec((tq, MLA_HEAD_PAD), lambda b, i: (i, 0)),
            pl.BlockSpec((DEC_SEQ, LANES), lambda b, i: (0, 0)),
            pl.BlockSpec((DEC_SEQ, LANES), lambda b, i: (0, 0)),
            pl.BlockSpec((None, 1, MLA_Q_RANK), lambda b, i: (l, 0, 0)),
            pl.BlockSpec((None, 1, MLA_KV_RANK), lambda b, i: (l, 0, 0)),
            pl.BlockSpec((None, MLA_Q_RANK, MLA_HEADS * MLA_HEAD_PAD), lambda b, i: (l, 0, 0)),
            pl.BlockSpec((None, MLA_KV_RANK, MLA_HEADS * MLA_KV_HEAD), lambda b, i: (l, 0, 0)),
            pl.BlockSpec(memory_space=pl.ANY),
        ],
        out_specs=pl.BlockSpec((tq, MLA_HEADS * MLA_V), lambda b, i: (N_CTX // tq + b * nq + i, 0)),
        out_shape=jax.ShapeDtypeStruct((N_TOK, MLA_HEADS * MLA_V), BF),
        input_output_aliases={13: 0},
        scratch_shapes=[
            pltpu.VMEM((MLA_HEADS, KEYS_DEC, MLA_HEAD_PAD), BF),
            pltpu.VMEM((MLA_HEADS, KEYS_DEC, MLA_V), BF),
        ],
        compiler_params=_cparams(("arbitrary", "arbitrary")),
        name="mla_dec",
    )(p_small, p_small, p_small, cache_ckv, cache_kr_pad, cq, sq, ck, sk,
      g_qnorm.reshape(DEPTH, 1, -1), g_kvnorm.reshape(DEPTH, 1, -1), w_qb_re, w_kvb, mix)


def _diff_lambda(lv):
    t1 = jnp.sum(lv[0:1] * lv[1:2], axis=-1, keepdims=True)
    t2 = jnp.sum(lv[2:3] * lv[3:4], axis=-1, keepdims=True)
    return jnp.exp(t1) - jnp.exp(t2)


def _diff_heads(q, kfun, vfun, lam, g, o_ref, lam_init, rows=slice(None)):
    scale = DIFF_QK ** -0.5
    lane = lax.broadcasted_iota(jnp.int32, (q.shape[0], DIFF_V), 1)
    for h in range(DIFF_HEADS):
        qh = q[:, h * DIFF_V:(h + 1) * DIFF_V]
        q1 = jnp.where(lane < DIFF_QK, qh, 0.0).astype(BF)
        q2 = jnp.where(lane >= DIFF_QK, qh, 0.0).astype(BF)
        k = kfun(h)
        v = vfun(h)
        p1 = _softmax_rows(_dot_nt(q1, k) * scale).astype(BF)
        p2 = _softmax_rows(_dot_nt(q2, k) * scale).astype(BF)
        a1 = jnp.dot(p1, v, preferred_element_type=F32)
        a2 = jnp.dot(p2, v, preferred_element_type=F32)
        o = _rms(a1 - lam * a2, g) * (1.0 - lam_init)
        o_ref[rows, h * DIFF_V:(h + 1) * DIFF_V] = o.astype(o_ref.dtype)


def _diff_ctx_kernel(q_ref, k_ref, v_ref, lam_ref, g_ref, st_k_hbm, st_v_hbm,
                     o_ref, k_out_ref, v_out_ref, *, lam_init):
    del st_k_hbm, st_v_hbm
    lam = _diff_lambda(lam_ref[...]) + lam_init
    for s in range(CTX_SEQS):
        rows = slice(s * SEQ, (s + 1) * SEQ)
        for h in range(DIFF_HEADS):
            k_out_ref[s, :, h, :] = k_ref[rows, h * DIFF_V:(h + 1) * DIFF_V]
            v_out_ref[s, :, h, :] = v_ref[rows, h * DIFF_V:(h + 1) * DIFF_V]
        k = k_ref[rows, :].astype(BF)
        v = v_ref[rows, :].astype(BF)
        _diff_heads(q_ref[rows, :],
                    lambda h, k=k: k[:, h * DIFF_V:(h + 1) * DIFF_V],
                    lambda h, v=v: v[:, h * DIFF_V:(h + 1) * DIFF_V],
                    lam, g_ref[...], o_ref, lam_init, rows)


def _lam_init(l):
    return 0.8 - 0.6 * math.exp(-0.3 * l)


def _diff_ctx(p_small, diff_lambda, g_diffnorm, st_k, st_v, l):
    return pl.pallas_call(
        functools.partial(_diff_ctx_kernel, lam_init=_lam_init(l)),
        grid=(BATCH // CTX_SEQS,),
        in_specs=[
            pl.BlockSpec((CTX_SEQS * SEQ, DIFF_WIDTH), lambda b: (b, C_DQ // DIFF_WIDTH)),
            pl.BlockSpec((CTX_SEQS * SEQ, DIFF_WIDTH), lambda b: (b, C_DK // DIFF_WIDTH)),
            pl.BlockSpec((CTX_SEQS * SEQ, DIFF_WIDTH), lambda b: (b, C_DV // DIFF_WIDTH)),
            pl.BlockSpec((None, 4, DIFF_QK), lambda b: (l, 0, 0)),
            pl.BlockSpec((None, 1, DIFF_V), lambda b: (l, 0, 0)),
            pl.BlockSpec(memory_space=pl.ANY),
            pl.BlockSpec(memory_space=pl.ANY),
        ],
        out_specs=[
            pl.BlockSpec((CTX_SEQS * SEQ, DIFF_WIDTH), lambda b: (b, 0)),
            pl.BlockSpec((CTX_SEQS, None, SEQ, DIFF_HEADS, DIFF_V), lambda b: (b, l, 0, 0, 0)),
            pl.BlockSpec((CTX_SEQS, None, SEQ, DIFF_HEADS, DIFF_V), lambda b: (b, l, 0, 0, 0)),
        ],
        out_shape=[
            jax.ShapeDtypeStruct((N_TOK, DIFF_WIDTH), BF),
            jax.ShapeDtypeStruct(st_k.shape, F32),
            jax.ShapeDtypeStruct(st_v.shape, F32),
        ],
        input_output_aliases={5: 1, 6: 2},
        compiler_params=_cparams(("arbitrary",)),
        name="diff_ctx",
    )(p_small, p_small, p_small, diff_lambda, g_diffnorm.reshape(DEPTH, 1, -1), st_k, st_v)


def _diff_dec_kernel(q_ref, k_ref, v_ref, ck_ref, cv_ref, cq_ref, sq_ref, cfull_ref, sfull_ref,
                     lam_ref, g_ref, mix_hbm, o_ref, kf_ref, vf_ref, *, lam_init):
    del mix_hbm

    @pl.when(pl.program_id(1) == 0)
    def _():
        k = k_ref[...]
        k_rot = k * cfull_ref[...] + _rope_swap(k) * sfull_ref[...]
        kf_ref[0:PAST_LEN, :] = ck_ref[...].astype(BF)
        kf_ref[PAST_LEN:KEYS_DEC, :] = k_rot.astype(BF)
        vf_ref[0:PAST_LEN, :] = cv_ref[...].astype(BF)
        vf_ref[PAST_LEN:KEYS_DEC, :] = v_ref[...].astype(BF)

    lam = _diff_lambda(lam_ref[...]) + lam_init
    q = q_ref[...]
    q = q * cq_ref[...] + _rope_swap(q) * sq_ref[...]
    _diff_heads(q,
                lambda h: kf_ref[:, h * DIFF_V:(h + 1) * DIFF_V],
                lambda h: vf_ref[:, h * DIFF_V:(h + 1) * DIFF_V],
                lam, g_ref[...], o_ref, lam_init)


def _diff_dec(p_small, cache_k, cache_v, tabs, diff_lambda, g_diffnorm, mix, l):
    tq = 512
    nq = DEC_SEQ // tq
    row_blk = N_CTX // DEC_SEQ
    c512, s512 = tabs
    return pl.pallas_call(
        functools.partial(_diff_dec_kernel, lam_init=_lam_init(l)),
        grid=(DEC_BATCH, nq),
        in_specs=[
            pl.BlockSpec((tq, DIFF_WIDTH), lambda b, i: (N_CTX // tq + b * nq + i, C_DQ // DIFF_WIDTH)),
            pl.BlockSpec((DEC_SEQ, DIFF_WIDTH), lambda b, i: (row_blk + b, C_DK // DIFF_WIDTH)),
            pl.BlockSpec((DEC_SEQ, DIFF_WIDTH), lambda b, i: (row_blk + b, C_DV // DIFF_WIDTH)),
            pl.BlockSpec((None, None, PAST_LEN, DIFF_WIDTH), lambda b, i: (b, l, 0, 0)),
            pl.BlockSpec((None, None, PAST_LEN, DIFF_WIDTH), lambda b, i: (b, l, 0, 0)),
            pl.BlockSpec((tq, DIFF_WIDTH), lambda b, i: (i, 0)),
            pl.BlockSpec((tq, DIFF_WIDTH), lambda b, i: (i, 0)),
            pl.BlockSpec((DEC_SEQ, DIFF_WIDTH), lambda b, i: (0, 0)),
            pl.BlockSpec((DEC_SEQ, DIFF_WIDTH), lambda b, i: (0, 0)),
            pl.BlockSpec((None, 4, DIFF_QK), lambda b, i: (l, 0, 0)),
            pl.BlockSpec((None, 1, DIFF_V), lambda b, i: (l, 0, 0)),
            pl.BlockSpec(memory_space=pl.ANY),
        ],
        out_specs=pl.BlockSpec((tq, DIFF_WIDTH), lambda b, i: (N_CTX // tq + b * nq + i, 0)),
        out_shape=jax.ShapeDtypeStruct((N_TOK, DIFF_WIDTH), BF),
        input_output_aliases={11: 0},
        scratch_shapes=[
            pltpu.VMEM((KEYS_DEC, DIFF_WIDTH), BF),
            pltpu.VMEM((KEYS_DEC, DIFF_WIDTH), BF),
        ],
        compiler_params=_cparams(("arbitrary", "arbitrary")),
        name="diff_dec",
    )(p_small, p_small, p_small, cache_k, cache_v, c512, s512, c512, s512,
      diff_lambda, g_diffnorm.reshape(DEPTH, 1, -1), mix)


def _pool_kernel(z_ref, w_ref, ps_ref, o_ref, *, tm):
    seq_m1 = jnp.where(pl.program_id(0) * tm < N_CTX, SEQ - 1, DEC_SEQ - 1)
    t = lax.broadcasted_iota(jnp.int32, (tm, POOL_GROUP), 0) & seq_m1

    def ahead(x, m):
        return jnp.where(t + m <= seq_m1, pltpu.roll(x, tm - m, 0), 0.0)

    def behind(x, m):
        return jnp.where(t - m >= 0, pltpu.roll(x, m, 0), 0.0)

    for gi, w in enumerate(POOL_WINDOWS):
        z = z_ref[:, gi * POOL_GROUP:(gi + 1) * POOL_GROUP]
        fwd, bwd, span = z, z, 1
        while span < w // 2:
            fwd = fwd + ahead(fwd, span)
            bwd = bwd + behind(bwd, span)
            span *= 2
        acc = fwd + behind(bwd, 1)
        lo = jnp.maximum(t - w // 2, 0)
        hi = jnp.minimum(t + (w - 1) // 2, seq_m1)
        d = (acc / (hi - lo + 1).astype(F32) - z).astype(BF)
        y = jnp.dot(d, w_ref[gi], preferred_element_type=F32)
        y = y * ps_ref[:, gi * POOL_GROUP:(gi + 1) * POOL_GROUP]
        o_ref[:, gi * POOL_GROUP:(gi + 1) * POOL_GROUP] = y.astype(o_ref.dtype)


def _pool(p_small, w_pool_bf, pool_scale, l):
    tm = DEC_SEQ
    return pl.pallas_call(
        functools.partial(_pool_kernel, tm=tm),
        grid=(N_TOK // tm,),
        in_specs=[
            pl.BlockSpec((tm, POOL_WIDTH), lambda i: (i, C_PZ // POOL_WIDTH)),
            pl.BlockSpec((None, len(POOL_WINDOWS), POOL_GROUP, POOL_GROUP), lambda i: (l, 0, 0, 0)),
            pl.BlockSpec((None, 1, POOL_WIDTH), lambda i: (l, 0, 0)),
        ],
        out_specs=pl.BlockSpec((tm, POOL_WIDTH), lambda i: (i, 0)),
        out_shape=jax.ShapeDtypeStruct((N_TOK, POOL_WIDTH), BF),
        compiler_params=_cparams(("arbitrary",)),
        name="pool",
    )(p_small, w_pool_bf, pool_scale.reshape(DEPTH, 1, POOL_WIDTH))


def _merge_kernel(a_ref, p_ref, d_ref, g0_ref, g1_ref, g2_ref, wa_ref, wp_ref, wd_ref, o_ref):
    def sig(r):
        return jax.nn.sigmoid(r[...].astype(F32))

    def proj(x_ref, w_ref):
        return jnp.dot(x_ref[...], w_ref[...].astype(BF), preferred_element_type=F32)

    m = sig(g0_ref) * proj(a_ref, wa_ref)
    m = m + sig(g1_ref) * proj(p_ref, wp_ref)
    m = m + sig(g2_ref) * proj(d_ref, wd_ref)
    o_ref[...] = m.astype(o_ref.dtype)


def _merge(mla_o, pool_o, diff_o, gl, w_br_mla, w_br_pool, w_br_diff, l):
    tm, tn = 1024, 512
    nj = D_MODEL // tn
    return pl.pallas_call(
        _merge_kernel,
        grid=(N_TOK // tm, nj),
        in_specs=[
            pl.BlockSpec((tm, MLA_HEADS * MLA_V), lambda i, j: (i, 0)),
            pl.BlockSpec((tm, POOL_WIDTH), lambda i, j: (i, 0)),
            pl.BlockSpec((tm, DIFF_WIDTH), lambda i, j: (i, 0)),
            pl.BlockSpec((tm, tn), lambda i, j: (i, j)),
            pl.BlockSpec((tm, tn), lambda i, j: (i, nj + j)),
            pl.BlockSpec((tm, tn), lambda i, j: (i, 2 * nj + j)),
            pl.BlockSpec((None, MLA_HEADS * MLA_V, tn), lambda i, j: (l, 0, j)),
            pl.BlockSpec((None, POOL_WIDTH, tn), lambda i, j: (l, 0, j)),
            pl.BlockSpec((None, DIFF_WIDTH, tn), lambda i, j: (l, 0, j)),
        ],
        out_specs=pl.BlockSpec((tm, tn), lambda i, j: (i, j)),
        out_shape=jax.ShapeDtypeStruct((N_TOK, D_MODEL), BF),
        compiler_params=_cparams(("arbitrary", "arbitrary")),
        name="merge",
    )(mla_o, pool_o, diff_o, gl, gl, gl, w_br_mla, w_br_pool, w_br_diff)


def _outproj_kernel(x_ref, m_ref, w_ref, ga_ref, g2_ref, sh_ref, sc_ref, xo_ref, ht_ref, wb_ref, *, tm):
    @pl.when(pl.program_id(0) == 0)
    def _():
        wb_ref[...] = w_ref[...].astype(BF)

    seg = _segment(pl.program_id(0) * tm)
    y = jnp.dot(m_ref[...], wb_ref[...], preferred_element_type=F32)
    xn = x_ref[...] + ga_ref[pl.ds(seg, 1), :] * y
    xo_ref[...] = xn
    h2 = _rms(xn, g2_ref[...]) * (1.0 + sc_ref[pl.ds(seg, 1), :]) + sh_ref[pl.ds(seg, 1), :]
    ht_ref[...] = h2.T.astype(BF)


def _out_projection(x, merged, w_out, mod, g_norm2, l):
    tm = 256
    return pl.pallas_call(
        functools.partial(_outproj_kernel, tm=tm),
        grid=(N_TOK // tm,),
        in_specs=[
            pl.BlockSpec((tm, D_MODEL), lambda i: (i, 0)),
            pl.BlockSpec((tm, D_MODEL), lambda i: (i, 0)),
            pl.BlockSpec((None, D_MODEL, D_MODEL), lambda i: (l, 0, 0), pipeline_mode=pl.Buffered(1)),
            pl.BlockSpec((None, 8, D_MODEL), lambda i: (l, 0, 2)),
            pl.BlockSpec((None, 1, D_MODEL), lambda i: (l, 0, 0)),
            pl.BlockSpec((None, 8, D_MODEL), lambda i: (l, 0, 3)),
            pl.BlockSpec((None, 8, D_MODEL), lambda i: (l, 0, 4)),
        ],
        out_specs=[
            pl.BlockSpec((tm, D_MODEL), lambda i: (i, 0)),
            pl.BlockSpec((D_MODEL, tm), lambda i: (0, i)),
        ],
        out_shape=[
            jax.ShapeDtypeStruct((N_TOK, D_MODEL), F32),
            jax.ShapeDtypeStruct((D_MODEL, N_TOK), BF),
        ],
        scratch_shapes=[pltpu.VMEM((D_MODEL, D_MODEL), BF)],
        compiler_params=_cparams(("arbitrary",)),
        name="out_projection",
    )(x, merged, w_out, mod, g_norm2.reshape(DEPTH, 1, D_MODEL), mod, mod)


ROUTE_LANES = 128
NOT_SELECTED = 127


SLAB = 8
NET_WIDTH = 16


def _sorting_network(n):
    def merge(lo, hi, r):
        step = r * 2
        if step < hi - lo:
            yield from merge(lo, hi, step)
            yield from merge(lo + r, hi, step)
            yield from ((i, i + r) for i in range(lo + r, hi - r, step))
        else:
            yield (lo, lo + r)

    def sort(lo, hi):
        if hi - lo >= 1:
            mid = lo + (hi - lo) // 2
            yield from sort(lo, mid)
            yield from sort(mid + 1, hi)
            yield from merge(lo, hi, 1)

    return tuple(sort(0, n - 1))


def _extract_top(x, n):
    slabs = [x[SLAB * v:SLAB * (v + 1)] for v in range(x.shape[0] // SLAB)]
    slabs += [None] * (NET_WIDTH - len(slabs))
    for i, j in _sorting_network(NET_WIDTH):
        hi, lo = slabs[i], slabs[j]
        if lo is None:
            continue
        if hi is None:
            slabs[i], slabs[j] = lo, None
        else:
            slabs[i], slabs[j] = jnp.maximum(hi, lo), jnp.minimum(hi, lo)
    stack = [s for s in slabs if s is not None]
    sub = lax.broadcasted_iota(jnp.int32, stack[0].shape, 0)
    vals = []
    for r in range(n):
        m = jnp.max(stack[0], axis=0, keepdims=True)
        vals.append(m)
        hit = stack[0] == m
        popped = sub == jnp.min(jnp.where(hit, sub, SLAB), axis=0, keepdims=True)
        for d in range(min(n - r - 1, len(stack))):
            below = stack[d + 1] if d + 1 < len(stack) else -jnp.inf
            stack[d] = jnp.where(popped, below, stack[d])
    return vals


def _stack_rows(rows):
    idx = lax.broadcasted_iota(jnp.int32, (len(rows), rows[0].shape[1]), 0)
    m = jnp.zeros((len(rows), rows[0].shape[1]), F32)
    for r, row in enumerate(rows):
        m = jnp.where(idx == r, row, m)
    return m


def _route_chunk(s1, s2):
    k = PEER_TOPK
    a = _extract_top(s1, k)
    b = _extract_top(s2, k)
    am, bm = _stack_rows(a), _stack_rows(b)
    slabs = [a[0] + bm[0:8], a[0] + bm[8:16]]
    slabs += [a[r] + bm[0:8] for r in range(1, 8)]
    slabs += [am[8:16] + b[0]]
    cand = jnp.concatenate(slabs, axis=0)
    tau = _extract_top(cand, k)[-1]
    sel = cand >= tau
    top = a[0] + b[0]
    cnt = jnp.where(sel, 1.0, 0.0)
    low = jnp.where(sel, cand, jnp.inf)

    def row_rows(r):
        return slice(0, 16) if r == 0 else slice(8 + 8 * r, 16 + 8 * r) if r < 8 else slice(64 + r, 65 + r)

    n_r = [jnp.sum(cnt[row_rows(r)], axis=0, keepdims=True) for r in range(k)]
    low_r = [jnp.min(low[row_rows(r)], axis=0, keepdims=True) for r in range(k)]
    total = n_r[0]
    for r in range(1, k):
        total = total + n_r[r]
    excess = total - float(k)
    for r in reversed(range(k)):
        drop = jnp.where(excess > 0.0, jnp.where(low_r[r] == tau, 1.0, 0.0), 0.0)
        n_r[r] = n_r[r] - drop
        excess = excess - drop
    dropped = total - float(k) - excess
    z = jnp.sum(jnp.where(sel, jnp.exp(cand - top), 0.0), axis=0, keepdims=True) - dropped * jnp.exp(tau - top)

    c1 = jnp.full(s1.shape, -1.0, F32)
    r2 = jnp.full(s2.shape, float(NOT_SELECTED), F32)
    for r in range(k - 1):
        c1 = jnp.where(s1 == a[r], n_r[r] - 1.0, c1)
        r2 = jnp.where(s2 == b[r], float(r), r2)
    key = lax.broadcasted_iota(jnp.int32, s1.shape, 0)

    def last_key_limit(s, vals):
        match = s == vals[k - 1]
        first = jnp.min(jnp.where(match, key, PEER_KEYS), axis=0, keepdims=True)
        return match, jnp.where(vals[k - 2] != vals[k - 1], first, PEER_KEYS)

    match1, limit1 = last_key_limit(s1, a)
    c1 = jnp.where(match1, jnp.where(key <= limit1, n_r[k - 1] - 1.0, c1), c1)
    match2, limit2 = last_key_limit(s2, b)
    r2 = jnp.where(match2, jnp.where(key <= limit2, float(k - 1), r2), r2)
    e1 = jnp.exp(s1 - a[0]) / z
    e2 = jnp.exp(s2 - b[0])
    return c1, e1, r2, e2


ROUTE_HEADS = 8


def _route_kernel(ht_ref, wq_ref, sk_ref, c1_ref, e1_ref, r2_ref, e2_ref):
    def pack_row_pairs(x):
        return pltpu.bitcast(x.astype(BF), jnp.int32)

    def duplicate_halves(x):
        hi = pltpu.bitcast(x.astype(BF).astype(F32), jnp.int32)
        return hi | lax.shift_right_logical(hi, 16)

    half = PEER_KEYS // 2
    for hh in range(ROUTE_HEADS):
        wq = wq_ref[:, hh * PEER_QDIM:(hh + 1) * PEER_QDIM].astype(BF)
        qt = lax.dot_general(wq, ht_ref[...], (((0,), (0,)), ((), ())), preferred_element_type=F32)
        s1 = jnp.dot(sk_ref[0], qt[0:PEER_KEYS].astype(BF), preferred_element_type=F32)
        s2 = jnp.dot(sk_ref[1], qt[PEER_KEYS:].astype(BF), preferred_element_type=F32)
        for c in range(s1.shape[1] // ROUTE_LANES):
            cs = slice(c * ROUTE_LANES, (c + 1) * ROUTE_LANES)
            c1, e1, r2, e2 = _route_chunk(s1[:, cs], s2[:, cs])
            c1_ref[hh * PEER_KEYS:(hh + 1) * PEER_KEYS, cs] = duplicate_halves(c1)
            e1_ref[hh * PEER_KEYS:(hh + 1) * PEER_KEYS, cs] = duplicate_halves(e1)
            r2_ref[hh * half:(hh + 1) * half, cs] = pack_row_pairs(r2)
            e2_ref[hh * half:(hh + 1) * half, cs] = pack_row_pairs(e2)


def _route(h2t, w_peer_q, subkeys_bf, l):
    tn = 512
    rows = PEER_HEADS * PEER_KEYS
    spec = pl.BlockSpec((ROUTE_HEADS * PEER_KEYS, tn), lambda t, h: (h, t))
    pair_spec = pl.BlockSpec((ROUTE_HEADS * PEER_KEYS // 2, tn), lambda t, h: (h, t))
    dup = jax.ShapeDtypeStruct((rows, N_TOK), jnp.int32)
    pairs = jax.ShapeDtypeStruct((rows // 2, N_TOK), jnp.int32)
    return pl.pallas_call(
        _route_kernel,
        grid=(N_TOK // tn, PEER_HEADS // ROUTE_HEADS),
        in_specs=[
            pl.BlockSpec((D_MODEL, tn), lambda t, h: (0, t)),
            pl.BlockSpec((None, D_MODEL, ROUTE_HEADS * PEER_QDIM), lambda t, h: (l, 0, h)),
            pl.BlockSpec((None, 2, PEER_KEYS, PEER_KEYS), lambda t, h: (l, 0, 0, 0)),
        ],
        out_specs=[spec, spec, pair_spec, pair_spec],
        out_shape=[dup, dup, pairs, pairs],
        compiler_params=_cparams(("arbitrary", "arbitrary")),
        name="peer_route",
    )(h2t, w_peer_q, subkeys_bf)


PEER_LANES = 128
PEER_BLK = 256
PEER_FIRST_TOKENS = 1024


def _peer_kernel(*refs, tm, te, convert):
    if convert:
        (ht_ref, u_ref, v_ref, c1_ref, e1_ref, r2_ref, e2_ref, o_ref, ub_ref, vtb_ref,
         at0_ref, at1_ref, wg0_ref, wg1_ref) = refs
    else:
        (ht_ref, u_ref, vt_ref, c1_ref, e1_ref, r2_ref, e2_ref, prev_hbm, o_ref,
         at0_ref, at1_ref, wg0_ref, wg1_ref) = refs
        del prev_hbm
    e = pl.program_id(1)
    nblk = te // PEER_BLK
    half = PEER_KEYS // 2

    @pl.when(e == 0)
    def _():
        o_ref[...] = jnp.zeros_like(o_ref)

    at_bufs = (at0_ref, at1_ref)
    wg_bufs = (wg0_ref, wg1_ref)

    def scores(q):
        rows = slice(q * PEER_BLK, (q + 1) * PEER_BLK)
        u = u_ref[rows, :]
        if convert:
            u = u.astype(BF)
            ub_ref[rows, :] = u
        at_bufs[q % 2][...] = jnp.dot(u, ht_ref[...], preferred_element_type=F32)

    def weights(q):
        for bb in range(PEER_BLK // PEER_KEYS):
            i = (e * nblk + q) * (PEER_BLK // PEER_KEYS) + bb
            bs = slice(bb * PEER_KEYS, (bb + 1) * PEER_KEYS)
            last_rows = [c1_ref[pl.ds(h * PEER_KEYS + i, 1), :] for h in range(PEER_HEADS)]
            e1_rows = [e1_ref[pl.ds(h * PEER_KEYS + i, 1), :] for h in range(PEER_HEADS)]
            for c in range(tm // PEER_LANES):
                cs = slice(c * PEER_LANES, (c + 1) * PEER_LANES)
                w = jnp.zeros((PEER_KEYS, PEER_LANES), BF)
                for h in range(PEER_HEADS):
                    hs = slice(h * half, (h + 1) * half)
                    last = pltpu.bitcast(jnp.broadcast_to(last_rows[h][:, cs], (half, PEER_LANES)), BF)
                    e1 = pltpu.bitcast(jnp.broadcast_to(e1_rows[h][:, cs], (half, PEER_LANES)), BF)
                    r2 = pltpu.bitcast(r2_ref[hs, cs], BF)
                    e2 = pltpu.bitcast(e2_ref[hs, cs], BF)
                    w = w + jnp.where(r2 <= last, e2, jnp.zeros_like(e2)) * e1
                a = at_bufs[q % 2][bs, cs]
                g = (0.5 * a * (1.0 + lax.erf(a * math.sqrt(0.5)))).astype(BF)
                wg_bufs[q % 2][bs, cs] = w * g

    def update(q):
        cols = slice(q * PEER_BLK, (q + 1) * PEER_BLK)
        if convert:
            vt = v_ref[cols, :].T.astype(BF)
            vtb_ref[:, cols] = vt
        else:
            vt = vt_ref[:, cols]
        o_ref[...] += jnp.dot(vt, wg_bufs[q % 2][...], preferred_element_type=F32)

    scores(0)
    for q in range(nblk):
        if q + 1 < nblk:
            scores(q + 1)
        weights(q)
        if q >= 1:
            update(q - 1)
    update(nblk - 1)


def _peer_scratch(tm):
    return [pltpu.VMEM((PEER_BLK, tm), F32), pltpu.VMEM((PEER_BLK, tm), F32),
            pltpu.VMEM((PEER_BLK, tm), BF), pltpu.VMEM((PEER_BLK, tm), BF)]


def _peer(h2t, peer_u, peer_v, routing, l):
    c1, e1, r2, e2 = routing
    rows = PEER_HEADS * PEER_KEYS
    out_sds = jax.ShapeDtypeStruct((D_MODEL, N_TOK), F32)

    tm, te = PEER_FIRST_TOKENS, 512
    once = pl.Buffered(1)
    rspec = pl.BlockSpec((rows, tm), lambda t, e: (0, 0), pipeline_mode=once)
    pspec = pl.BlockSpec((rows // 2, tm), lambda t, e: (0, 0), pipeline_mode=once)
    first, u_bf, vt_bf = pl.pallas_call(
        functools.partial(_peer_kernel, tm=tm, te=te, convert=True),
        grid=(1, PEER_EXPERTS // te),
        in_specs=[
            pl.BlockSpec((D_MODEL, tm), lambda t, e: (0, 0), pipeline_mode=once),
            pl.BlockSpec((None, te, D_MODEL), lambda t, e: (l, e, 0)),
            pl.BlockSpec((None, te, D_MODEL), lambda t, e: (l, e, 0)),
            rspec, rspec, pspec, pspec,
        ],
        out_specs=[
            pl.BlockSpec((D_MODEL, tm), lambda t, e: (0, 0), pipeline_mode=once),
            pl.BlockSpec((te, D_MODEL), lambda t, e: (e, 0)),
            pl.BlockSpec((D_MODEL, te), lambda t, e: (0, e)),
        ],
        out_shape=[out_sds, jax.ShapeDtypeStruct((PEER_EXPERTS, D_MODEL), BF),
                   jax.ShapeDtypeStruct((D_MODEL, PEER_EXPERTS), BF)],
        scratch_shapes=_peer_scratch(tm),
        compiler_params=_cparams(("arbitrary", "arbitrary")),
        name="peer_dense_first",
    )(h2t, peer_u, peer_v, c1, e1, r2, e2)

    tm, te = 512, 1024
    t0 = PEER_FIRST_TOKENS // tm
    rspec = pl.BlockSpec((rows, tm), lambda t, e: (0, t + t0))
    pspec = pl.BlockSpec((rows // 2, tm), lambda t, e: (0, t + t0))
    return pl.pallas_call(
        functools.partial(_peer_kernel, tm=tm, te=te, convert=False),
        grid=(N_TOK // tm - t0, PEER_EXPERTS // te),
        in_specs=[
            pl.BlockSpec((D_MODEL, tm), lambda t, e: (0, t + t0)),
            pl.BlockSpec((te, D_MODEL), lambda t, e: (e, 0)),
            pl.BlockSpec((D_MODEL, te), lambda t, e: (0, e)),
            rspec, rspec, pspec, pspec,
            pl.BlockSpec(memory_space=pl.ANY),
        ],
        out_specs=pl.BlockSpec((D_MODEL, tm), lambda t, e: (0, t + t0)),
        out_shape=out_sds,
        input_output_aliases={7: 0},
        scratch_shapes=_peer_scratch(tm),
        compiler_params=_cparams(("arbitrary", "arbitrary")),
        name="peer_dense",
    )(h2t, u_bf, vt_bf, c1, e1, r2, e2, first)


def _final_kernel(x_ref, pt_ref, ga_ref, g_ref, ctx_ref, dec_ref, *, tm, n_ctx_tiles):
    seg = _segment(pl.program_id(0) * tm)
    x = x_ref[...] + ga_ref[pl.ds(seg, 1), :] * pt_ref[...].T
    y = _rms(x, g_ref[...])
    is_ctx = pl.program_id(0) < n_ctx_tiles

    @pl.when(is_ctx)
    def _():
        ctx_ref[...] = y

    @pl.when(jnp.logical_not(is_ctx))
    def _():
        dec_ref[...] = y


def _final_norm(x, peer_t, mod, g_final):
    tm = 512
    n_ctx_tiles = N_CTX // tm
    return pl.pallas_call(
        functools.partial(_final_kernel, tm=tm, n_ctx_tiles=n_ctx_tiles),
        grid=(N_TOK // tm,),
        in_specs=[pl.BlockSpec((tm, D_MODEL), lambda i: (i, 0)),
                  pl.BlockSpec((D_MODEL, tm), lambda i: (0, i)),
                  pl.BlockSpec((None, 8, D_MODEL), lambda i: (DEPTH - 1, 0, 5)),
                  pl.BlockSpec((1, D_MODEL), lambda i: (0, 0))],
        out_specs=[
            pl.BlockSpec((tm, D_MODEL), lambda i: (jnp.minimum(i, n_ctx_tiles - 1), 0)),
            pl.BlockSpec((tm, D_MODEL), lambda i: (jnp.maximum(i - n_ctx_tiles, 0), 0)),
        ],
        out_shape=[jax.ShapeDtypeStruct((N_CTX, D_MODEL), F32), jax.ShapeDtypeStruct((N_DEC, D_MODEL), F32)],
        compiler_params=_cparams(("arbitrary",)),
        name="final_norm",
    )(x, peer_t, mod, g_final.reshape(1, D_MODEL))


def _rope_tables():
    t = jnp.arange(DEC_SEQ)
    half = MLA_ROPE // 4
    inv = ROPE_BASE ** (-jnp.arange(half, dtype=F32) / half)
    ang_r = (t // GRID_W).astype(F32)[:, None] * inv
    ang_c = (t % GRID_W).astype(F32)[:, None] * inv
    cos = jnp.concatenate([jnp.cos(ang_r)] * 2 + [jnp.cos(ang_c)] * 2, axis=1)
    sin = jnp.concatenate([-jnp.sin(ang_r), jnp.sin(ang_r), -jnp.sin(ang_c), jnp.sin(ang_c)], axis=1)
    return cos, sin


def _relayout_w_qb(w_qb):
    w = w_qb.reshape(DEPTH, MLA_Q_RANK, MLA_HEADS, MLA_NOPE + MLA_ROPE)
    w = jnp.pad(w, ((0, 0), (0, 0), (0, 0), (0, MLA_HEAD_PAD - MLA_NOPE - MLA_ROPE)))
    return w.reshape(DEPTH, MLA_Q_RANK, MLA_HEADS * MLA_HEAD_PAD).astype(BF)


def kernel(x_prompt, x_sample, cache_mla_ckv, cache_mla_krope, cache_diff_k, cache_diff_v, c, c_ctx,
           w_mod, b_mod, g_norm1, w_in, g_qnorm, w_qb, g_kvnorm, w_kvb, w_pool, pool_scale,
           diff_lambda, g_diffnorm, w_br_mla, w_br_pool, w_br_diff, w_out, g_norm2,
           w_peer_q, peer_subkeys, peer_u, peer_v, g_final):
    x = jnp.concatenate([x_prompt.reshape(N_CTX, D_MODEL), x_sample.reshape(N_DEC, D_MODEL)], axis=0)
    cond8 = jnp.concatenate([c_ctx[None, :], c, jnp.zeros((8 - 1 - DEC_BATCH, D_MODEL), F32)], axis=0)

    w_in_t = jnp.swapaxes(w_in, 1, 2)
    w_qb_re = _relayout_w_qb(w_qb)
    w_pool_bf = w_pool.astype(BF)
    subkeys_bf = peer_subkeys.astype(BF)

    cache_kr_pad = jnp.pad(cache_mla_krope, ((0, 0), (0, 0), (0, 0), (0, LANES - MLA_ROPE)))
    cache_k = cache_diff_k.reshape(DEC_BATCH, DEPTH, PAST_LEN, DIFF_WIDTH)
    cache_v = cache_diff_v.reshape(DEC_BATCH, DEPTH, PAST_LEN, DIFF_WIDTH)

    cos64, sin64 = _rope_tables()
    ones, zeros = jnp.ones_like(cos64), jnp.zeros_like(cos64)
    mla_tabs = (
        jnp.concatenate([ones, ones, cos64, ones], axis=1),
        jnp.concatenate([zeros, zeros, sin64, zeros], axis=1),
        jnp.concatenate([cos64, ones], axis=1),
        jnp.concatenate([sin64, zeros], axis=1),
    )
    diff_tabs = (jnp.tile(cos64, (1, DIFF_WIDTH // DIFF_QK)), jnp.tile(sin64, (1, DIFF_WIDTH // DIFF_QK)))

    mod = _modulation(cond8, w_mod, b_mod)

    st_ckv = jnp.zeros((BATCH, DEPTH, SEQ, MLA_KV_RANK), F32)
    st_kr = jnp.zeros((BATCH, DEPTH, SEQ, MLA_ROPE), F32)
    st_k = jnp.zeros((BATCH, DEPTH, SEQ, DIFF_HEADS, 2 * DIFF_QK), F32)
    st_v = jnp.zeros((BATCH, DEPTH, SEQ, DIFF_HEADS, DIFF_V), F32)
    peer_t = None
    for l in range(DEPTH):
        if l == 0:
            p_small = _in_projection(x, g_norm1, mod, w_in_t, l, 0, SMALL_WIDTH, F32)
        else:
            p_small, x = _in_projection_res(x, peer_t, g_norm1, mod, w_in_t, l, SMALL_WIDTH)
        gl = _in_projection(x, g_norm1, mod, w_in_t, l, SMALL_WIDTH, GATE_WIDTH, BF)
        mla_c, st_ckv, st_kr = _mla_ctx(p_small, g_qnorm, g_kvnorm, w_qb_re, w_kvb, st_ckv, st_kr, l)
        mla_o = _mla_dec(p_small, cache_mla_ckv, cache_kr_pad, mla_tabs, g_qnorm, g_kvnorm, w_qb_re, w_kvb,
                         mla_c, l)
        diff_c, st_k, st_v = _diff_ctx(p_small, diff_lambda, g_diffnorm, st_k, st_v, l)
        diff_o = _diff_dec(p_small, cache_k, cache_v, diff_tabs, diff_lambda, g_diffnorm, diff_c, l)
        pool_o = _pool(p_small, w_pool_bf, pool_scale, l)
        merged = _merge(mla_o, pool_o, diff_o, gl, w_br_mla, w_br_pool, w_br_diff, l)
        x, h2t = _out_projection(x, merged, w_out, mod, g_norm2, l)
        routing = _route(h2t, w_peer_q, subkeys_bf, l)
        peer_t = _peer(h2t, peer_u, peer_v, routing, l)

    y_ctx, y_dec = _final_norm(x, peer_t, mod, g_final)
    return y_ctx.reshape(BATCH, SEQ, D_MODEL), y_dec.reshape(DEC_BATCH, DEC_SEQ, D_MODEL), st_ckv, st_kr, st_k, st_v
```

```python
import functools
import math

import jax
import jax.numpy as jnp
from jax import lax
from jax.experimental import pallas as pl
from jax.experimental.pallas import tpu as pltpu

BF = jnp.bfloat16
F32 = jnp.float32

D_MODEL = 2048
BATCH = 16
SEQ = 256
DEPTH = 4
DEC_BATCH = 2
DEC_SEQ = 1024
PAST_LEN = 256
GRID_W = 64
ROPE_BASE = 10000.0
EPS = 1e-6

MLA_HEADS = 8
MLA_NOPE = 128
MLA_ROPE = 64
MLA_V = 128
MLA_Q_RANK = 512
MLA_KV_RANK = 256
MLA_HEAD_PAD = 256
MLA_KV_HEAD = MLA_NOPE + MLA_V
LANES = 128
CTX_SEQS = 4
ROPE_HALF = MLA_ROPE // 4

POOL_WINDOWS = (2, 4, 8, 16)
POOL_GROUP = 128
POOL_WIDTH = POOL_GROUP * len(POOL_WINDOWS)

DIFF_HEADS = 4
DIFF_QK = 64
DIFF_V = 2 * DIFF_QK
DIFF_WIDTH = DIFF_HEADS * DIFF_V

PEER_HEADS = 8
PEER_KEYS = 128
PEER_EXPERTS = PEER_KEYS * PEER_KEYS
PEER_TOPK = 16
PEER_QDIM = 256

N_CTX = BATCH * SEQ
N_DEC = DEC_BATCH * DEC_SEQ
N_TOK = N_CTX + N_DEC
KEYS_DEC = PAST_LEN + DEC_SEQ

C_QA = 0
C_CKV = 512
C_KR = 768
KR_BLOCK = 256
C_PZ = 1024
C_DQ = 1536
C_DK = 2048
C_DV = 2560
SMALL_WIDTH = 3072
GATE_WIDTH = 3 * D_MODEL

KV_END = 512 + 320
INPROJ_TN = 512

VMEM_LIMIT = 56 * 1024 * 1024


def _cparams(sem):
    return pltpu.CompilerParams(dimension_semantics=sem, vmem_limit_bytes=VMEM_LIMIT)


def _segment(row0):
    return jnp.where(row0 < N_CTX, 0, 1 + (row0 - N_CTX) // DEC_SEQ)


def _rms(x, g):
    return x * lax.rsqrt(jnp.mean(x * x, axis=-1, keepdims=True) + EPS) * g


def _softmax_rows(s):
    m = jnp.max(s, axis=-1, keepdims=True)
    p = jnp.exp(s - m)
    return p / jnp.sum(p, axis=-1, keepdims=True)


def _dot_nt(a, b):
    return lax.dot_general(a, b, (((1,), (1,)), ((), ())), preferred_element_type=F32)


def _rope_swap(x):
    n = x.shape[-1]
    up = pltpu.roll(x, n - ROPE_HALF, 1)
    dn = pltpu.roll(x, ROPE_HALF, 1)
    lane = lax.broadcasted_iota(jnp.int32, x.shape, 1)
    return jnp.where((lane & ROPE_HALF) == 0, up, dn)


def _mod_kernel(c_ref, w_ref, b_ref, o_ref):
    c = c_ref[...]
    a = (c * jax.nn.sigmoid(c)).astype(BF)
    o_ref[...] = jnp.dot(a, w_ref[...].astype(BF), preferred_element_type=F32) + b_ref[...]


def _modulation(cond8, w_mod, b_mod):
    tn = 1024
    return pl.pallas_call(
        _mod_kernel,
        grid=(DEPTH, 6 * D_MODEL // tn),
        in_specs=[
            pl.BlockSpec((8, D_MODEL), lambda l, j: (0, 0)),
            pl.BlockSpec((None, D_MODEL, tn), lambda l, j: (l, 0, j)),
            pl.BlockSpec((None, 1, tn), lambda l, j: (l, 0, j)),
        ],
        out_specs=pl.BlockSpec((None, 8, tn), lambda l, j: (l, 0, j)),
        out_shape=jax.ShapeDtypeStruct((DEPTH, 8, 6 * D_MODEL), F32),
        compiler_params=_cparams(("arbitrary", "arbitrary")),
        name="modulation",
    )(cond8, w_mod, b_mod.reshape(DEPTH, 1, 6 * D_MODEL))


def _w_in_row(tile):
    per_tile = INPROJ_TN // 64
    return 64 * jnp.where(tile < 2, per_tile * tile, per_tile * tile - (C_PZ - KV_END) // 64)


def _inproj_tile(h_ref, w_ref, tile):
    y = _dot_nt(h_ref[...], w_ref[0].astype(BF))
    lane = lax.broadcasted_iota(jnp.int32, y.shape, 1)
    return jnp.where(lane >= jnp.where(tile == 1, KV_END - INPROJ_TN, INPROJ_TN), 0.0, y)


def _inproj_kernel(x_ref, g_ref, sh_ref, sc_ref, w_ref, o_ref, h_ref, *, tm, tile0):
    i = pl.program_id(0)

    @pl.when(pl.program_id(1) == 0)
    def _():
        seg = _segment(i * tm)
        y = _rms(x_ref[...], g_ref[...])
        h_ref[...] = (y * (1.0 + sc_ref[pl.ds(seg, 1), :]) + sh_ref[pl.ds(seg, 1), :]).astype(BF)

    o_ref[...] = _inproj_tile(h_ref, w_ref, pl.program_id(1) + tile0).astype(o_ref.dtype)


def _w_in_spec(l, tile0):
    return pl.BlockSpec((pl.Element(1), pl.Element(INPROJ_TN), pl.Element(D_MODEL)),
                        lambda i, j: (l, _w_in_row(j + tile0), 0))


def _in_projection(x, g_norm1, mod, w_in_t, l, col0, width, out_dtype, tm=1024):
    tn = INPROJ_TN
    tile0 = col0 // tn
    return pl.pallas_call(
        functools.partial(_inproj_kernel, tm=tm, tile0=tile0),
        grid=(N_TOK // tm, width // tn),
        in_specs=[
            pl.BlockSpec((tm, D_MODEL), lambda i, j: (i, 0)),
            pl.BlockSpec((None, 1, D_MODEL), lambda i, j: (l, 0, 0)),
            pl.BlockSpec((None, 8, D_MODEL), lambda i, j: (l, 0, 0)),
            pl.BlockSpec((None, 8, D_MODEL), lambda i, j: (l, 0, 1)),
            _w_in_spec(l, tile0),
        ],
        out_specs=pl.BlockSpec((tm, tn), lambda i, j: (i, j)),
        out_shape=jax.ShapeDtypeStruct((N_TOK, width), out_dtype),
        scratch_shapes=[pltpu.VMEM((tm, D_MODEL), BF)],
        compiler_params=_cparams(("arbitrary", "arbitrary")),
        name="in_projection",
    )(x, g_norm1.reshape(DEPTH, 1, D_MODEL), mod, mod, w_in_t)


def _inproj_res_kernel(x_ref, pt_ref, ga_ref, g_ref, sh_ref, sc_ref, w_ref, o_ref, xo_ref, h_ref, *, tm):
    i = pl.program_id(0)

    @pl.when(pl.program_id(1) == 0)
    def _():
        seg = _segment(i * tm)
        x = x_ref[...] + ga_ref[pl.ds(seg, 1), :] * pt_ref[...].T
        xo_ref[...] = x
        y = _rms(x, g_ref[...])
        h_ref[...] = (y * (1.0 + sc_ref[pl.ds(seg, 1), :]) + sh_ref[pl.ds(seg, 1), :]).astype(BF)

    o_ref[...] = _inproj_tile(h_ref, w_ref, pl.program_id(1)).astype(o_ref.dtype)


def _in_projection_res(x, peer_t, g_norm1, mod, w_in_t, l, width):
    tm, tn = 1024, INPROJ_TN
    once = pl.Buffered(1)
    return pl.pallas_call(
        functools.partial(_inproj_res_kernel, tm=tm),
        grid=(N_TOK // tm, width // tn),
        in_specs=[
            pl.BlockSpec((tm, D_MODEL), lambda i, j: (i, 0), pipeline_mode=once),
            pl.BlockSpec((D_MODEL, tm), lambda i, j: (0, i), pipeline_mode=once),
            pl.BlockSpec((None, 8, D_MODEL), lambda i, j: (l - 1, 0, 5)),
            pl.BlockSpec((None, 1, D_MODEL), lambda i, j: (l, 0, 0)),
            pl.BlockSpec((None, 8, D_MODEL), lambda i, j: (l, 0, 0)),
            pl.BlockSpec((None, 8, D_MODEL), lambda i, j: (l, 0, 1)),
            _w_in_spec(l, 0),
        ],
        out_specs=[
            pl.BlockSpec((tm, tn), lambda i, j: (i, j)),
            pl.BlockSpec((tm, D_MODEL), lambda i, j: (i, 0)),
        ],
        out_shape=[jax.ShapeDtypeStruct((N_TOK, width), F32), jax.ShapeDtypeStruct((N_TOK, D_MODEL), F32)],
        scratch_shapes=[pltpu.VMEM((tm, D_MODEL), BF)],
        compiler_params=_cparams(("arbitrary", "arbitrary")),
        name="in_projection_res",
    )(x, peer_t, mod, g_norm1.reshape(DEPTH, 1, D_MODEL), mod, mod, w_in_t)


def _mla_heads(q, kfun, vfun, o_ref, scale, rows=slice(None)):
    for h in range(MLA_HEADS):
        s = _dot_nt(q(h), kfun(h)) * scale
        p = _softmax_rows(s).astype(BF)
        o = jnp.dot(p, vfun(h), preferred_element_type=F32)
        o_ref[rows, h * MLA_V:(h + 1) * MLA_V] = o.astype(o_ref.dtype)


def _mla_ctx_kernel(qa_ref, ckv_ref, kr_ref, gq_ref, gkv_ref, wqb_ref, wkvb_ref, st_ckv_hbm, st_kr_hbm,
                    o_ref, ckv_out_ref, kr_out_ref):
    del st_ckv_hbm, st_kr_hbm
    scale = (MLA_NOPE + MLA_ROPE) ** -0.5
    qn = _rms(qa_ref[...], gq_ref[...]).astype(BF)
    q = jnp.dot(qn, wqb_ref[...], preferred_element_type=F32).astype(BF)
    ckv = _rms(ckv_ref[...], gkv_ref[...])
    kr = kr_ref[...]
    kv = jnp.dot(ckv.astype(BF), wkvb_ref[...].astype(BF), preferred_element_type=F32).astype(BF)
    krp = kr[:, :LANES].astype(BF)
    for s in range(CTX_SEQS):
        rows = slice(s * SEQ, (s + 1) * SEQ)
        ckv_out_ref[s] = ckv[rows]
        kr_out_ref[s] = kr[rows, :MLA_ROPE]

        def qh(h, rows=rows):
            return q[rows, h * MLA_HEAD_PAD:(h + 1) * MLA_HEAD_PAD]

        def kh(h, rows=rows):
            return jnp.concatenate([kv[rows, h * MLA_KV_HEAD:h * MLA_KV_HEAD + MLA_NOPE], krp[rows]], axis=1)

        def vh(h, rows=rows):
            return kv[rows, h * MLA_KV_HEAD + MLA_NOPE:(h + 1) * MLA_KV_HEAD]

        _mla_heads(qh, kh, vh, o_ref, scale, rows)


def _mla_ctx(p_small, g_qnorm, g_kvnorm, w_qb_re, w_kvb, st_ckv, st_kr, l):
    return pl.pallas_call(
        _mla_ctx_kernel,
        grid=(BATCH // CTX_SEQS,),
        in_specs=[
            pl.BlockSpec((CTX_SEQS * SEQ, MLA_Q_RANK), lambda b: (b, C_QA // MLA_Q_RANK)),
            pl.BlockSpec((CTX_SEQS * SEQ, MLA_KV_RANK), lambda b: (b, C_CKV // MLA_KV_RANK)),
            pl.BlockSpec((CTX_SEQS * SEQ, KR_BLOCK), lambda b: (b, C_KR // KR_BLOCK)),
            pl.BlockSpec((None, 1, MLA_Q_RANK), lambda b: (l, 0, 0)),
            pl.BlockSpec((None, 1, MLA_KV_RANK), lambda b: (l, 0, 0)),
            pl.BlockSpec((None, MLA_Q_RANK, MLA_HEADS * MLA_HEAD_PAD), lambda b: (l, 0, 0)),
            pl.BlockSpec((None, MLA_KV_RANK, MLA_HEADS * MLA_KV_HEAD), lambda b: (l, 0, 0)),
            pl.BlockSpec(memory_space=pl.ANY),
            pl.BlockSpec(memory_space=pl.ANY),
        ],
        out_specs=[
            pl.BlockSpec((CTX_SEQS * SEQ, MLA_HEADS * MLA_V), lambda b: (b, 0)),
            pl.BlockSpec((CTX_SEQS, None, SEQ, MLA_KV_RANK), lambda b: (b, l, 0, 0)),
            pl.BlockSpec((CTX_SEQS, None, SEQ, MLA_ROPE), lambda b: (b, l, 0, 0)),
        ],
        out_shape=[
            jax.ShapeDtypeStruct((N_TOK, MLA_HEADS * MLA_V), BF),
            jax.ShapeDtypeStruct(st_ckv.shape, F32),
            jax.ShapeDtypeStruct(st_kr.shape, F32),
        ],
        input_output_aliases={7: 1, 8: 2},
        compiler_params=_cparams(("arbitrary",)),
        name="mla_ctx",
    )(p_small, p_small, p_small, g_qnorm.reshape(DEPTH, 1, -1), g_kvnorm.reshape(DEPTH, 1, -1),
      w_qb_re, w_kvb, st_ckv, st_kr)


def _mla_dec_kernel(qa_ref, ckv_ref, kr_ref, cckv_ref, ckr_ref, cq_ref, sq_ref, ck_ref, sk_ref,
                    gq_ref, gkv_ref, wqb_ref, wkvb_ref, mix_hbm, o_ref, kf_ref, vf_ref):
    del mix_hbm
    scale = (MLA_NOPE + MLA_ROPE) ** -0.5

    @pl.when(pl.program_id(1) == 0)
    def _():
        ckv = _rms(ckv_ref[...], gkv_ref[...])
        ckv_all = jnp.concatenate([cckv_ref[...], ckv], axis=0).astype(BF)
        kr = kr_ref[:, :LANES]
        kr_rot = kr * ck_ref[...] + _rope_swap(kr) * sk_ref[...]
        kr_all = jnp.concatenate([ckr_ref[...], kr_rot], axis=0).astype(BF)
        for h in range(MLA_HEADS):
            w_h = wkvb_ref[:, h * MLA_KV_HEAD:(h + 1) * MLA_KV_HEAD].astype(BF)
            kvh = jnp.dot(ckv_all, w_h, preferred_element_type=F32)
            kf_ref[h, :, 0:MLA_NOPE] = kvh[:, :MLA_NOPE].astype(BF)
            kf_ref[h, :, MLA_NOPE:MLA_HEAD_PAD] = kr_all
            vf_ref[h] = kvh[:, MLA_NOPE:].astype(BF)

    qn = _rms(qa_ref[...], gq_ref[...]).astype(BF)
    q = jnp.dot(qn, wqb_ref[...], preferred_element_type=F32)
    cq = cq_ref[...]
    sq = sq_ref[...]

    def qh(h):
        x = q[:, h * MLA_HEAD_PAD:(h + 1) * MLA_HEAD_PAD]
        return (x * cq + _rope_swap(x) * sq).astype(BF)

    _mla_heads(qh, lambda h: kf_ref[h], lambda h: vf_ref[h], o_ref, scale)


def _mla_dec(p_small, cache_ckv, cache_kr_pad, tabs, g_qnorm, g_kvnorm, w_qb_re, w_kvb, mix, l):
    tq = 512
    nq = DEC_SEQ // tq
    row_blk = N_CTX // DEC_SEQ
    cq, sq, ck, sk = tabs
    return pl.pallas_call(
        _mla_dec_kernel,
        grid=(DEC_BATCH, nq),
        in_specs=[
            pl.BlockSpec((tq, MLA_Q_RANK), lambda b, i: (N_CTX // tq + b * nq + i, 0)),
            pl.BlockSpec((DEC_SEQ, MLA_KV_RANK), lambda b, i: (row_blk + b, C_CKV // MLA_KV_RANK)),
            pl.BlockSpec((DEC_SEQ, KR_BLOCK), lambda b, i: (row_blk + b, C_KR // KR_BLOCK)),
            pl.BlockSpec((None, None, PAST_LEN, MLA_KV_RANK), lambda b, i: (b, l, 0, 0)),
            pl.BlockSpec((None, None, PAST_LEN, LANES), lambda b, i: (b, l, 0, 0)),
            pl.BlockSpec((tq, MLA_HEAD_PAD), lambda b, i: (i, 0)),
            pl.BlockSpec((tq, MLA_HEAD_PAD), lambda b, i: (i, 0)),
            pl.BlockSpec((DEC_SEQ, LANES), lambda b, i: (0, 0)),
            pl.BlockSpec((DEC_SEQ, LANES), lambda b, i: (0, 0)),
            pl.BlockSpec((None, 1, MLA_Q_RANK), lambda b, i: (l, 0, 0)),
            pl.BlockSpec((None, 1, MLA_KV_RANK), lambda b, i: (l, 0, 0)),
            pl.BlockSpec((None, MLA_Q_RANK, MLA_HEADS * MLA_HEAD_PAD), lambda b, i: (l, 0, 0)),
            pl.BlockSpec((None, MLA_KV_RANK, MLA_HEADS * MLA_KV_HEAD), lambda b, i: (l, 0, 0)),
            pl.BlockSpec(memory_space=pl.ANY),
        ],
        out_specs=pl.BlockSpec((tq, MLA_HEADS * MLA_V), lambda b, i: (N_CTX // tq + b * nq + i, 0)),
        out_shape=jax.ShapeDtypeStruct((N_TOK, MLA_HEADS * MLA_V), BF),
        input_output_aliases={13: 0},
        scratch_shapes=[
            pltpu.VMEM((MLA_HEADS, KEYS_DEC, MLA_HEAD_PAD), BF),
            pltpu.VMEM((MLA_HEADS, KEYS_DEC, MLA_V), BF),
        ],
        compiler_params=_cparams(("arbitrary", "arbitrary")),
        name="mla_dec",
    )(p_small, p_small, p_small, cache_ckv, cache_kr_pad, cq, sq, ck, sk,
      g_qnorm.reshape(DEPTH, 1, -1), g_kvnorm.reshape(DEPTH, 1, -1), w_qb_re, w_kvb, mix)


def _diff_lambda(lv):
    t1 = jnp.sum(lv[0:1] * lv[1:2], axis=-1, keepdims=True)
    t2 = jnp.sum(lv[2:3] * lv[3:4], axis=-1, keepdims=True)
    return jnp.exp(t1) - jnp.exp(t2)


def _diff_heads(q, kfun, vfun, lam, g, o_ref, lam_init, rows=slice(None)):
    scale = DIFF_QK ** -0.5
    lane = lax.broadcasted_iota(jnp.int32, (q.shape[0], DIFF_V), 1)
    for h in range(DIFF_HEADS):
        qh = q[:, h * DIFF_V:(h + 1) * DIFF_V]
        q1 = jnp.where(lane < DIFF_QK, qh, 0.0).astype(BF)
        q2 = jnp.where(lane >= DIFF_QK, qh, 0.0).astype(BF)
        k = kfun(h)
        v = vfun(h)
        p1 = _softmax_rows(_dot_nt(q1, k) * scale).astype(BF)
        p2 = _softmax_rows(_dot_nt(q2, k) * scale).astype(BF)
        a1 = jnp.dot(p1, v, preferred_element_type=F32)
        a2 = jnp.dot(p2, v, preferred_element_type=F32)
        o = _rms(a1 - lam * a2, g) * (1.0 - lam_init)
        o_ref[rows, h * DIFF_V:(h + 1) * DIFF_V] = o.astype(o_ref.dtype)


def _diff_ctx_kernel(q_ref, k_ref, v_ref, lam_ref, g_ref, st_k_hbm, st_v_hbm,
                     o_ref, k_out_ref, v_out_ref, *, lam_init):
    del st_k_hbm, st_v_hbm
    lam = _diff_lambda(lam_ref[...]) + lam_init
    for s in range(CTX_SEQS):
        rows = slice(s * SEQ, (s + 1) * SEQ)
        for h in range(DIFF_HEADS):
            k_out_ref[s, :, h, :] = k_ref[rows, h * DIFF_V:(h + 1) * DIFF_V]
            v_out_ref[s, :, h, :] = v_ref[rows, h * DIFF_V:(h + 1) * DIFF_V]
        k = k_ref[rows, :].astype(BF)
        v = v_ref[rows, :].astype(BF)
        _diff_heads(q_ref[rows, :],
                    lambda h, k=k: k[:, h * DIFF_V:(h + 1) * DIFF_V],
                    lambda h, v=v: v[:, h * DIFF_V:(h + 1) * DIFF_V],
                    lam, g_ref[...], o_ref, lam_init, rows)


def _lam_init(l):
    return 0.8 - 0.6 * math.exp(-0.3 * l)


def _diff_ctx(p_small, diff_lambda, g_diffnorm, st_k, st_v, l):
    return pl.pallas_call(
        functools.partial(_diff_ctx_kernel, lam_init=_lam_init(l)),
        grid=(BATCH // CTX_SEQS,),
        in_specs=[
            pl.BlockSpec((CTX_SEQS * SEQ, DIFF_WIDTH), lambda b: (b, C_DQ // DIFF_WIDTH)),
            pl.BlockSpec((CTX_SEQS * SEQ, DIFF_WIDTH), lambda b: (b, C_DK // DIFF_WIDTH)),
            pl.BlockSpec((CTX_SEQS * SEQ, DIFF_WIDTH), lambda b: (b, C_DV // DIFF_WIDTH)),
            pl.BlockSpec((None, 4, DIFF_QK), lambda b: (l, 0, 0)),
            pl.BlockSpec((None, 1, DIFF_V), lambda b: (l, 0, 0)),
            pl.BlockSpec(memory_space=pl.ANY),
            pl.BlockSpec(memory_space=pl.ANY),
        ],
        out_specs=[
            pl.BlockSpec((CTX_SEQS * SEQ, DIFF_WIDTH), lambda b: (b, 0)),
            pl.BlockSpec((CTX_SEQS, None, SEQ, DIFF_HEADS, DIFF_V), lambda b: (b, l, 0, 0, 0)),
            pl.BlockSpec((CTX_SEQS, None, SEQ, DIFF_HEADS, DIFF_V), lambda b: (b, l, 0, 0, 0)),
        ],
        out_shape=[
            jax.ShapeDtypeStruct((N_TOK, DIFF_WIDTH), BF),
            jax.ShapeDtypeStruct(st_k.shape, F32),
            jax.ShapeDtypeStruct(st_v.shape, F32),
        ],
        input_output_aliases={5: 1, 6: 2},
        compiler_params=_cparams(("arbitrary",)),
        name="diff_ctx",
    )(p_small, p_small, p_small, diff_lambda, g_diffnorm.reshape(DEPTH, 1, -1), st_k, st_v)


def _diff_dec_kernel(q_ref, k_ref, v_ref, ck_ref, cv_ref, cq_ref, sq_ref, cfull_ref, sfull_ref,
                     lam_ref, g_ref, mix_hbm, o_ref, kf_ref, vf_ref, *, lam_init):
    del mix_hbm

    @pl.when(pl.program_id(1) == 0)
    def _():
        k = k_ref[...]
        k_rot = k * cfull_ref[...] + _rope_swap(k) * sfull_ref[...]
        kf_ref[0:PAST_LEN, :] = ck_ref[...].astype(BF)
        kf_ref[PAST_LEN:KEYS_DEC, :] = k_rot.astype(BF)
        vf_ref[0:PAST_LEN, :] = cv_ref[...].astype(BF)
        vf_ref[PAST_LEN:KEYS_DEC, :] = v_ref[...].astype(BF)

    lam = _diff_lambda(lam_ref[...]) + lam_init
    q = q_ref[...]
    q = q * cq_ref[...] + _rope_swap(q) * sq_ref[...]
    _diff_heads(q,
                lambda h: kf_ref[:, h * DIFF_V:(h + 1) * DIFF_V],
                lambda h: vf_ref[:, h * DIFF_V:(h + 1) * DIFF_V],
                lam, g_ref[...], o_ref, lam_init)


def _diff_dec(p_small, cache_k, cache_v, tabs, diff_lambda, g_diffnorm, mix, l):
    tq = 512
    nq = DEC_SEQ // tq
    row_blk = N_CTX // DEC_SEQ
    c512, s512 = tabs
    return pl.pallas_call(
        functools.partial(_diff_dec_kernel, lam_init=_lam_init(l)),
        grid=(DEC_BATCH, nq),
        in_specs=[
            pl.BlockSpec((tq, DIFF_WIDTH), lambda b, i: (N_CTX // tq + b * nq + i, C_DQ // DIFF_WIDTH)),
            pl.BlockSpec((DEC_SEQ, DIFF_WIDTH), lambda b, i: (row_blk + b, C_DK // DIFF_WIDTH)),
            pl.BlockSpec((DEC_SEQ, DIFF_WIDTH), lambda b, i: (row_blk + b, C_DV // DIFF_WIDTH)),
            pl.BlockSpec((None, None, PAST_LEN, DIFF_WIDTH), lambda b, i: (b, l, 0, 0)),
            pl.BlockSpec((None, None, PAST_LEN, DIFF_WIDTH), lambda b, i: (b, l, 0, 0)),
            pl.BlockSpec((tq, DIFF_WIDTH), lambda b, i: (i, 0)),
            pl.BlockSpec((tq, DIFF_WIDTH), lambda b, i: (i, 0)),
            pl.BlockSpec((DEC_SEQ, DIFF_WIDTH), lambda b, i: (0, 0)),
            pl.BlockSpec((DEC_SEQ, DIFF_WIDTH), lambda b, i: (0, 0)),
            pl.BlockSpec((None, 4, DIFF_QK), lambda b, i: (l, 0, 0)),
            pl.BlockSpec((None, 1, DIFF_V), lambda b, i: (l, 0, 0)),
            pl.BlockSpec(memory_space=pl.ANY),
        ],
        out_specs=pl.BlockSpec((tq, DIFF_WIDTH), lambda b, i: (N_CTX // tq + b * nq + i, 0)),
        out_shape=jax.ShapeDtypeStruct((N_TOK, DIFF_WIDTH), BF),
        input_output_aliases={11: 0},
        scratch_shapes=[
            pltpu.VMEM((KEYS_DEC, DIFF_WIDTH), BF),
            pltpu.VMEM((KEYS_DEC, DIFF_WIDTH), BF),
        ],
        compiler_params=_cparams(("arbitrary", "arbitrary")),
        name="diff_dec",
    )(p_small, p_small, p_small, cache_k, cache_v, c512, s512, c512, s512,
      diff_lambda, g_diffnorm.reshape(DEPTH, 1, -1), mix)


def _pool_kernel(z_ref, w_ref, ps_ref, o_ref, *, tm):
    seq_m1 = jnp.where(pl.program_id(0) * tm < N_CTX, SEQ - 1, DEC_SEQ - 1)
    t = lax.broadcasted_iota(jnp.int32, (tm, POOL_GROUP), 0) & seq_m1

    def ahead(x, m):
        return jnp.where(t + m <= seq_m1, pltpu.roll(x, tm - m, 0), 0.0)

    def behind(x, m):
        return jnp.where(t - m >= 0, pltpu.roll(x, m, 0), 0.0)

    for gi, w in enumerate(POOL_WINDOWS):
        z = z_ref[:, gi * POOL_GROUP:(gi + 1) * POOL_GROUP]
        fwd, bwd, span = z, z, 1
        while span < w // 2:
            fwd = fwd + ahead(fwd, span)
            bwd = bwd + behind(bwd, span)
            span *= 2
        acc = fwd + behind(bwd, 1)
        lo = jnp.maximum(t - w // 2, 0)
        hi = jnp.minimum(t + (w - 1) // 2, seq_m1)
        d = (acc / (hi - lo + 1).astype(F32) - z).astype(BF)
        y = jnp.dot(d, w_ref[gi], preferred_element_type=F32)
        y = y * ps_ref[:, gi * POOL_GROUP:(gi + 1) * POOL_GROUP]
        o_ref[:, gi * POOL_GROUP:(gi + 1) * POOL_GROUP] = y.astype(o_ref.dtype)


def _pool(p_small, w_pool_bf, pool_scale, l):
    tm = DEC_SEQ
    return pl.pallas_call(
        functools.partial(_pool_kernel, tm=tm),
        grid=(N_TOK // tm,),
        in_specs=[
            pl.BlockSpec((tm, POOL_WIDTH), lambda i: (i, C_PZ // POOL_WIDTH)),
            pl.BlockSpec((None, len(POOL_WINDOWS), POOL_GROUP, POOL_GROUP), lambda i: (l, 0, 0, 0)),
            pl.BlockSpec((None, 1, POOL_WIDTH), lambda i: (l, 0, 0)),
        ],
        out_specs=pl.BlockSpec((tm, POOL_WIDTH), lambda i: (i, 0)),
        out_shape=jax.ShapeDtypeStruct((N_TOK, POOL_WIDTH), BF),
        compiler_params=_cparams(("arbitrary",)),
        name="pool",
    )(p_small, w_pool_bf, pool_scale.reshape(DEPTH, 1, POOL_WIDTH))


def _merge_kernel(a_ref, p_ref, d_ref, g0_ref, g1_ref, g2_ref, wa_ref, wp_ref, wd_ref, o_ref):
    def sig(r):
        return jax.nn.sigmoid(r[...].astype(F32))

    def proj(x_ref, w_ref):
        return jnp.dot(x_ref[...], w_ref[...].astype(BF), preferred_element_type=F32)

    m = sig(g0_ref) * proj(a_ref, wa_ref)
    m = m + sig(g1_ref) * proj(p_ref, wp_ref)
    m = m + sig(g2_ref) * proj(d_ref, wd_ref)
    o_ref[...] = m.astype(o_ref.dtype)


def _merge(mla_o, pool_o, diff_o, gl, w_br_mla, w_br_pool, w_br_diff, l):
    tm, tn = 1024, 512
    nj = D_MODEL // tn
    return pl.pallas_call(
        _merge_kernel,
        grid=(N_TOK // tm, nj),
        in_specs=[
            pl.BlockSpec((tm, MLA_HEADS * MLA_V), lambda i, j: (i, 0)),
            pl.BlockSpec((tm, POOL_WIDTH), lambda i, j: (i, 0)),
            pl.BlockSpec((tm, DIFF_WIDTH), lambda i, j: (i, 0)),
            pl.BlockSpec((tm, tn), lambda i, j: (i, j)),
            pl.BlockSpec((tm, tn), lambda i, j: (i, nj + j)),
            pl.BlockSpec((tm, tn), lambda i, j: (i, 2 * nj + j)),
            pl.BlockSpec((None, MLA_HEADS * MLA_V, tn), lambda i, j: (l, 0, j)),
            pl.BlockSpec((None, POOL_WIDTH, tn), lambda i, j: (l, 0, j)),
            pl.BlockSpec((None, DIFF_WIDTH, tn), lambda i, j: (l, 0, j)),
        ],
        out_specs=pl.BlockSpec((tm, tn), lambda i, j: (i, j)),
        out_shape=jax.ShapeDtypeStruct((N_TOK, D_MODEL), BF),
        compiler_params=_cparams(("arbitrary", "arbitrary")),
        name="merge",
    )(mla_o, pool_o, diff_o, gl, gl, gl, w_br_mla, w_br_pool, w_br_diff)


def _outproj_kernel(x_ref, m_ref, w_ref, ga_ref, g2_ref, sh_ref, sc_ref, xo_ref, ht_ref, wb_ref, *, tm):
    @pl.when(pl.program_id(0) == 0)
    def _():
        wb_ref[...] = w_ref[...].astype(BF)

    seg = _segment(pl.program_id(0) * tm)
    y = jnp.dot(m_ref[...], wb_ref[...], preferred_element_type=F32)
    xn = x_ref[...] + ga_ref[pl.ds(seg, 1), :] * y
    xo_ref[...] = xn
    h2 = _rms(xn, g2_ref[...]) * (1.0 + sc_ref[pl.ds(seg, 1), :]) + sh_ref[pl.ds(seg, 1), :]
    ht_ref[...] = h2.T.astype(BF)


def _out_projection(x, merged, w_out, mod, g_norm2, l):
    tm = 256
    return pl.pallas_call(
        functools.partial(_outproj_kernel, tm=tm),
        grid=(N_TOK // tm,),
        in_specs=[
            pl.BlockSpec((tm, D_MODEL), lambda i: (i, 0)),
            pl.BlockSpec((tm, D_MODEL), lambda i: (i, 0)),
            pl.BlockSpec((None, D_MODEL, D_MODEL), lambda i: (l, 0, 0), pipeline_mode=pl.Buffered(1)),
            pl.BlockSpec((None, 8, D_MODEL), lambda i: (l, 0, 2)),
            pl.BlockSpec((None, 1, D_MODEL), lambda i: (l, 0, 0)),
            pl.BlockSpec((None, 8, D_MODEL), lambda i: (l, 0, 3)),
            pl.BlockSpec((None, 8, D_MODEL), lambda i: (l, 0, 4)),
        ],
        out_specs=[
            pl.BlockSpec((tm, D_MODEL), lambda i: (i, 0)),
            pl.BlockSpec((D_MODEL, tm), lambda i: (0, i)),
        ],
        out_shape=[
            jax.ShapeDtypeStruct((N_TOK, D_MODEL), F32),
            jax.ShapeDtypeStruct((D_MODEL, N_TOK), BF),
        ],
        scratch_shapes=[pltpu.VMEM((D_MODEL, D_MODEL), BF)],
        compiler_params=_cparams(("arbitrary",)),
        name="out_projection",
    )(x, merged, w_out, mod, g_norm2.reshape(DEPTH, 1, D_MODEL), mod, mod)


ROUTE_LANES = 128
NOT_SELECTED = 127


SLAB = 8
NET_WIDTH = 16


def _sorting_network(n):
    def merge(lo, hi, r):
        step = r * 2
        if step < hi - lo:
            yield from merge(lo, hi, step)
            yield from merge(lo + r, hi, step)
            yield from ((i, i + r) for i in range(lo + r, hi - r, step))
        else:
            yield (lo, lo + r)

    def sort(lo, hi):
        if hi - lo >= 1:
            mid = lo + (hi - lo) // 2
            yield from sort(lo, mid)
            yield from sort(mid + 1, hi)
            yield from merge(lo, hi, 1)

    return tuple(sort(0, n - 1))


def _extract_top(x, n):
    slabs = [x[SLAB * v:SLAB * (v + 1)] for v in range(x.shape[0] // SLAB)]
    slabs += [None] * (NET_WIDTH - len(slabs))
    for i, j in _sorting_network(NET_WIDTH):
        hi, lo = slabs[i], slabs[j]
        if lo is None:
            continue
        if hi is None:
            slabs[i], slabs[j] = lo, None
        else:
            slabs[i], slabs[j] = jnp.maximum(hi, lo), jnp.minimum(hi, lo)
    stack = [s for s in slabs if s is not None]
    sub = lax.broadcasted_iota(jnp.int32, stack[0].shape, 0)
    vals = []
    for r in range(n):
        m = jnp.max(stack[0], axis=0, keepdims=True)
        vals.append(m)
        hit = stack[0] == m
        popped = sub == jnp.min(jnp.where(hit, sub, SLAB), axis=0, keepdims=True)
        for d in range(min(n - r - 1, len(stack))):
            below = stack[d + 1] if d + 1 < len(stack) else -jnp.inf
            stack[d] = jnp.where(popped, below, stack[d])
    return vals


def _stack_rows(rows):
    idx = lax.broadcasted_iota(jnp.int32, (len(rows), rows[0].shape[1]), 0)
    m = jnp.zeros((len(rows), rows[0].shape[1]), F32)
    for r, row in enumerate(rows):
        m = jnp.where(idx == r, row, m)
    return m


def _route_chunk(s1, s2):
    k = PEER_TOPK
    a = _extract_top(s1, k)
    b = _extract_top(s2, k)
    am, bm = _stack_rows(a), _stack_rows(b)
    slabs = [a[0] + bm[0:8], a[0] + bm[8:16]]
    slabs += [a[r] + bm[0:8] for r in range(1, 8)]
    slabs += [am[8:16] + b[0]]
    cand = jnp.concatenate(slabs, axis=0)
    tau = _extract_top(cand, k)[-1]
    sel = cand >= tau
    top = a[0] + b[0]
    cnt = jnp.where(sel, 1.0, 0.0)
    low = jnp.where(sel, cand, jnp.inf)

    def row_rows(r):
        return slice(0, 16) if r == 0 else slice(8 + 8 * r, 16 + 8 * r) if r < 8 else slice(64 + r, 65 + r)

    n_r = [jnp.sum(cnt[row_rows(r)], axis=0, keepdims=True) for r in range(k)]
    low_r = [jnp.min(low[row_rows(r)], axis=0, keepdims=True) for r in range(k)]
    total = n_r[0]
    for r in range(1, k):
        total = total + n_r[r]
    excess = total - float(k)
    for r in reversed(range(k)):
        drop = jnp.where(excess > 0.0, jnp.where(low_r[r] == tau, 1.0, 0.0), 0.0)
        n_r[r] = n_r[r] - drop
        excess = excess - drop
    dropped = total - float(k) - excess
    z = jnp.sum(jnp.where(sel, jnp.exp(cand - top), 0.0), axis=0, keepdims=True) - dropped * jnp.exp(tau - top)

    c1 = jnp.full(s1.shape, -1.0, F32)
    r2 = jnp.full(s2.shape, float(NOT_SELECTED), F32)
    for r in range(k - 1):
        c1 = jnp.where(s1 == a[r], n_r[r] - 1.0, c1)
        r2 = jnp.where(s2 == b[r], float(r), r2)
    key = lax.broadcasted_iota(jnp.int32, s1.shape, 0)

    def last_key_limit(s, vals):
        match = s == vals[k - 1]
        first = jnp.min(jnp.where(match, key, PEER_KEYS), axis=0, keepdims=True)
        return match, jnp.where(vals[k - 2] != vals[k - 1], first, PEER_KEYS)

    match1, limit1 = last_key_limit(s1, a)
    c1 = jnp.where(match1, jnp.where(key <= limit1, n_r[k - 1] - 1.0, c1), c1)
    match2, limit2 = last_key_limit(s2, b)
    r2 = jnp.where(match2, jnp.where(key <= limit2, float(k - 1), r2), r2)
    e1 = jnp.exp(s1 - a[0]) / z
    e2 = jnp.exp(s2 - b[0])
    return c1, e1, r2, e2


ROUTE_HEADS = 8


def _route_kernel(ht_ref, wq_ref, sk_ref, c1_ref, e1_ref, r2_ref, e2_ref):
    def pack_row_pairs(x):
        return pltpu.bitcast(x.astype(BF), jnp.int32)

    def duplicate_halves(x):
        hi = pltpu.bitcast(x.astype(BF).astype(F32), jnp.int32)
        return hi | lax.shift_right_logical(hi, 16)

    half = PEER_KEYS // 2
    for hh in range(ROUTE_HEADS):
        wq = wq_ref[:, hh * PEER_QDIM:(hh + 1) * PEER_QDIM].astype(BF)
        qt = lax.dot_general(wq, ht_ref[...], (((0,), (0,)), ((), ())), preferred_element_type=F32)
        s1 = jnp.dot(sk_ref[0], qt[0:PEER_KEYS].astype(BF), preferred_element_type=F32)
        s2 = jnp.dot(sk_ref[1], qt[PEER_KEYS:].astype(BF), preferred_element_type=F32)
        for c in range(s1.shape[1] // ROUTE_LANES):
            cs = slice(c * ROUTE_LANES, (c + 1) * ROUTE_LANES)
            c1, e1, r2, e2 = _route_chunk(s1[:, cs], s2[:, cs])
            c1_ref[hh * PEER_KEYS:(hh + 1) * PEER_KEYS, cs] = duplicate_halves(c1)
            e1_ref[hh * PEER_KEYS:(hh + 1) * PEER_KEYS, cs] = duplicate_halves(e1)
            r2_ref[hh * half:(hh + 1) * half, cs] = pack_row_pairs(r2)
            e2_ref[hh * half:(hh + 1) * half, cs] = pack_row_pairs(e2)


def _route(h2t, w_peer_q, subkeys_bf, l):
    tn = 512
    rows = PEER_HEADS * PEER_KEYS
    spec = pl.BlockSpec((ROUTE_HEADS * PEER_KEYS, tn), lambda t, h: (h, t))
    pair_spec = pl.BlockSpec((ROUTE_HEADS * PEER_KEYS // 2, tn), lambda t, h: (h, t))
    dup = jax.ShapeDtypeStruct((rows, N_TOK), jnp.int32)
    pairs = jax.ShapeDtypeStruct((rows // 2, N_TOK), jnp.int32)
    return pl.pallas_call(
        _route_kernel,
        grid=(N_TOK // tn, PEER_HEADS // ROUTE_HEADS),
        in_specs=[
            pl.BlockSpec((D_MODEL, tn), lambda t, h: (0, t)),
            pl.BlockSpec((None, D_MODEL, ROUTE_HEADS * PEER_QDIM), lambda t, h: (l, 0, h)),
            pl.BlockSpec((None, 2, PEER_KEYS, PEER_KEYS), lambda t, h: (l, 0, 0, 0)),
        ],
        out_specs=[spec, spec, pair_spec, pair_spec],
        out_shape=[dup, dup, pairs, pairs],
        compiler_params=_cparams(("arbitrary", "arbitrary")),
        name="peer_route",
    )(h2t, w_peer_q, subkeys_bf)


PEER_LANES = 128
PEER_BLK = 256
PEER_FIRST_TOKENS = 1024


def _peer_kernel(*refs, tm, te, convert):
    if convert:
        (ht_ref, u_ref, v_ref, c1_ref, e1_ref, r2_ref, e2_ref, o_ref, ub_ref, vtb_ref,
         at0_ref, at1_ref, wg0_ref, wg1_ref) = refs
    else:
        (ht_ref, u_ref, vt_ref, c1_ref, e1_ref, r2_ref, e2_ref, prev_hbm, o_ref,
         at0_ref, at1_ref, wg0_ref, wg1_ref) = refs
        del prev_hbm
    e = pl.program_id(1)
    nblk = te // PEER_BLK
    half = PEER_KEYS // 2

    @pl.when(e == 0)
    def _():
        o_ref[...] = jnp.zeros_like(o_ref)

    at_bufs = (at0_ref, at1_ref)
    wg_bufs = (wg0_ref, wg1_ref)

    def scores(q):
        rows = slice(q * PEER_BLK, (q + 1) * PEER_BLK)
        u = u_ref[rows, :]
        if convert:
            u = u.astype(BF)
            ub_ref[rows, :] = u
        at_bufs[q % 2][...] = jnp.dot(u, ht_ref[...], preferred_element_type=F32)

    def weights(q):
        for bb in range(PEER_BLK // PEER_KEYS):
            i = (e * nblk + q) * (PEER_BLK // PEER_KEYS) + bb
            bs = slice(bb * PEER_KEYS, (bb + 1) * PEER_KEYS)
            last_rows = [c1_ref[pl.ds(h * PEER_KEYS + i, 1), :] for h in range(PEER_HEADS)]
            e1_rows = [e1_ref[pl.ds(h * PEER_KEYS + i, 1), :] for h in range(PEER_HEADS)]
            for c in range(tm // PEER_LANES):
                cs = slice(c * PEER_LANES, (c + 1) * PEER_LANES)
                w = jnp.zeros((PEER_KEYS, PEER_LANES), BF)
                for h in range(PEER_HEADS):
                    hs = slice(h * half, (h + 1) * half)
                    last = pltpu.bitcast(jnp.broadcast_to(last_rows[h][:, cs], (half, PEER_LANES)), BF)
                    e1 = pltpu.bitcast(jnp.broadcast_to(e1_rows[h][:, cs], (half, PEER_LANES)), BF)
                    r2 = pltpu.bitcast(r2_ref[hs, cs], BF)
                    e2 = pltpu.bitcast(e2_ref[hs, cs], BF)
                    w = w + jnp.where(r2 <= last, e2, jnp.zeros_like(e2)) * e1
                a = at_bufs[q % 2][bs, cs]
                g = (0.5 * a * (1.0 + lax.erf(a * math.sqrt(0.5)))).astype(BF)
                wg_bufs[q % 2][bs, cs] = w * g

    def update(q):
        cols = slice(q * PEER_BLK, (q + 1) * PEER_BLK)
        if convert:
            vt = v_ref[cols, :].T.astype(BF)
            vtb_ref[:, cols] = vt
        else:
            vt = vt_ref[:, cols]
        o_ref[...] += jnp.dot(vt, wg_bufs[q % 2][...], preferred_element_type=F32)

    scores(0)
    for q in range(nblk):
        if q + 1 < nblk:
            scores(q + 1)
        weights(q)
        if q >= 1:
            update(q - 1)
    update(nblk - 1)


def _peer_scratch(tm):
    return [pltpu.VMEM((PEER_BLK, tm), F32), pltpu.VMEM((PEER_BLK, tm), F32),
            pltpu.VMEM((PEER_BLK, tm), BF), pltpu.VMEM((PEER_BLK, tm), BF)]


def _peer(h2t, peer_u, peer_v, routing, l):
    c1, e1, r2, e2 = routing
    rows = PEER_HEADS * PEER_KEYS
    out_sds = jax.ShapeDtypeStruct((D_MODEL, N_TOK), F32)

    tm, te = PEER_FIRST_TOKENS, 512
    once = pl.Buffered(1)
    rspec = pl.BlockSpec((rows, tm), lambda t, e: (0, 0), pipeline_mode=once)
    pspec = pl.BlockSpec((rows // 2, tm), lambda t, e: (0, 0), pipeline_mode=once)
    first, u_bf, vt_bf = pl.pallas_call(
        functools.partial(_peer_kernel, tm=tm, te=te, convert=True),
        grid=(1, PEER_EXPERTS // te),
        in_specs=[
            pl.BlockSpec((D_MODEL, tm), lambda t, e: (0, 0), pipeline_mode=once),
            pl.BlockSpec((None, te, D_MODEL), lambda t, e: (l, e, 0)),
            pl.BlockSpec((None, te, D_MODEL), lambda t, e: (l, e, 0)),
            rspec, rspec, pspec, pspec,
        ],
        out_specs=[
            pl.BlockSpec((D_MODEL, tm), lambda t, e: (0, 0), pipeline_mode=once),
            pl.BlockSpec((te, D_MODEL), lambda t, e: (e, 0)),
            pl.BlockSpec((D_MODEL, te), lambda t, e: (0, e)),
        ],
        out_shape=[out_sds, jax.ShapeDtypeStruct((PEER_EXPERTS, D_MODEL), BF),
                   jax.ShapeDtypeStruct((D_MODEL, PEER_EXPERTS), BF)],
        scratch_shapes=_peer_scratch(tm),
        compiler_params=_cparams(("arbitrary", "arbitrary")),
        name="peer_dense_first",
    )(h2t, peer_u, peer_v, c1, e1, r2, e2)

    tm, te = 512, 2048
    t0 = PEER_FIRST_TOKENS // tm
    rspec = pl.BlockSpec((rows, tm), lambda t, e: (0, t + t0), pipeline_mode=once)
    pspec = pl.BlockSpec((rows // 2, tm), lambda t, e: (0, t + t0), pipeline_mode=once)
    return pl.pallas_call(
        functools.partial(_peer_kernel, tm=tm, te=te, convert=False),
        grid=(N_TOK // tm - t0, PEER_EXPERTS // te),
        in_specs=[
            pl.BlockSpec((D_MODEL, tm), lambda t, e: (0, t + t0), pipeline_mode=once),
            pl.BlockSpec((te, D_MODEL), lambda t, e: (e, 0)),
            pl.BlockSpec((D_MODEL, te), lambda t, e: (0, e)),
            rspec, rspec, pspec, pspec,
            pl.BlockSpec(memory_space=pl.ANY),
        ],
        out_specs=pl.BlockSpec((D_MODEL, tm), lambda t, e: (0, t + t0)),
        out_shape=out_sds,
        input_output_aliases={7: 0},
        scratch_shapes=_peer_scratch(tm),
        compiler_params=_cparams(("arbitrary", "arbitrary")),
        name="peer_dense",
    )(h2t, u_bf, vt_bf, c1, e1, r2, e2, first)


def _final_kernel(x_ref, pt_ref, ga_ref, g_ref, ctx_ref, dec_ref, *, tm, n_ctx_tiles):
    seg = _segment(pl.program_id(0) * tm)
    x = x_ref[...] + ga_ref[pl.ds(seg, 1), :] * pt_ref[...].T
    y = _rms(x, g_ref[...])
    is_ctx = pl.program_id(0) < n_ctx_tiles

    @pl.when(is_ctx)
    def _():
        ctx_ref[...] = y

    @pl.when(jnp.logical_not(is_ctx))
    def _():
        dec_ref[...] = y


def _final_norm(x, peer_t, mod, g_final):
    tm = 512
    n_ctx_tiles = N_CTX // tm
    return pl.pallas_call(
        functools.partial(_final_kernel, tm=tm, n_ctx_tiles=n_ctx_tiles),
        grid=(N_TOK // tm,),
        in_specs=[pl.BlockSpec((tm, D_MODEL), lambda i: (i, 0)),
                  pl.BlockSpec((D_MODEL, tm), lambda i: (0, i)),
                  pl.BlockSpec((None, 8, D_MODEL), lambda i: (DEPTH - 1, 0, 5)),
                  pl.BlockSpec((1, D_MODEL), lambda i: (0, 0))],
        out_specs=[
            pl.BlockSpec((tm, D_MODEL), lambda i: (jnp.minimum(i, n_ctx_tiles - 1), 0)),
            pl.BlockSpec((tm, D_MODEL), lambda i: (jnp.maximum(i - n_ctx_tiles, 0), 0)),
        ],
        out_shape=[jax.ShapeDtypeStruct((N_CTX, D_MODEL), F32), jax.ShapeDtypeStruct((N_DEC, D_MODEL), F32)],
        compiler_params=_cparams(("arbitrary",)),
        name="final_norm",
    )(x, peer_t, mod, g_final.reshape(1, D_MODEL))


def _rope_tables():
    t = jnp.arange(DEC_SEQ)
    half = MLA_ROPE // 4
    inv = ROPE_BASE ** (-jnp.arange(half, dtype=F32) / half)
    ang_r = (t // GRID_W).astype(F32)[:, None] * inv
    ang_c = (t % GRID_W).astype(F32)[:, None] * inv
    cos = jnp.concatenate([jnp.cos(ang_r)] * 2 + [jnp.cos(ang_c)] * 2, axis=1)
    sin = jnp.concatenate([-jnp.sin(ang_r), jnp.sin(ang_r), -jnp.sin(ang_c), jnp.sin(ang_c)], axis=1)
    return cos, sin


def _relayout_w_qb(w_qb):
    w = w_qb.reshape(DEPTH, MLA_Q_RANK, MLA_HEADS, MLA_NOPE + MLA_ROPE)
    w = jnp.pad(w, ((0, 0), (0, 0), (0, 0), (0, MLA_HEAD_PAD - MLA_NOPE - MLA_ROPE)))
    return w.reshape(DEPTH, MLA_Q_RANK, MLA_HEADS * MLA_HEAD_PAD).astype(BF)


def kernel(x_prompt, x_sample, cache_mla_ckv, cache_mla_krope, cache_diff_k, cache_diff_v, c, c_ctx,
           w_mod, b_mod, g_norm1, w_in, g_qnorm, w_qb, g_kvnorm, w_kvb, w_pool, pool_scale,
           diff_lambda, g_diffnorm, w_br_mla, w_br_pool, w_br_diff, w_out, g_norm2,
           w_peer_q, peer_subkeys, peer_u, peer_v, g_final):
    x = jnp.concatenate([x_prompt.reshape(N_CTX, D_MODEL), x_sample.reshape(N_DEC, D_MODEL)], axis=0)
    cond8 = jnp.concatenate([c_ctx[None, :], c, jnp.zeros((8 - 1 - DEC_BATCH, D_MODEL), F32)], axis=0)

    w_in_t = jnp.swapaxes(w_in, 1, 2)
    w_qb_re = _relayout_w_qb(w_qb)
    w_pool_bf = w_pool.astype(BF)
    subkeys_bf = peer_subkeys.astype(BF)

    cache_kr_pad = jnp.pad(cache_mla_krope, ((0, 0), (0, 0), (0, 0), (0, LANES - MLA_ROPE)))
    cache_k = cache_diff_k.reshape(DEC_BATCH, DEPTH, PAST_LEN, DIFF_WIDTH)
    cache_v = cache_diff_v.reshape(DEC_BATCH, DEPTH, PAST_LEN, DIFF_WIDTH)

    cos64, sin64 = _rope_tables()
    ones, zeros = jnp.ones_like(cos64), jnp.zeros_like(cos64)
    mla_tabs = (
        jnp.concatenate([ones, ones, cos64, ones], axis=1),
        jnp.concatenate([zeros, zeros, sin64, zeros], axis=1),
        jnp.concatenate([cos64, ones], axis=1),
        jnp.concatenate([sin64, zeros], axis=1),
    )
    diff_tabs = (jnp.tile(cos64, (1, DIFF_WIDTH // DIFF_QK)), jnp.tile(sin64, (1, DIFF_WIDTH // DIFF_QK)))

    mod = _modulation(cond8, w_mod, b_mod)

    st_ckv = jnp.zeros((BATCH, DEPTH, SEQ, MLA_KV_RANK), F32)
    st_kr = jnp.zeros((BATCH, DEPTH, SEQ, MLA_ROPE), F32)
    st_k = jnp.zeros((BATCH, DEPTH, SEQ, DIFF_HEADS, 2 * DIFF_QK), F32)
    st_v = jnp.zeros((BATCH, DEPTH, SEQ, DIFF_HEADS, DIFF_V), F32)
    peer_t = None
    for l in range(DEPTH):
        if l == 0:
            p_small = _in_projection(x, g_norm1, mod, w_in_t, l, 0, SMALL_WIDTH, F32)
        else:
            p_small, x = _in_projection_res(x, peer_t, g_norm1, mod, w_in_t, l, SMALL_WIDTH)
        gl = _in_projection(x, g_norm1, mod, w_in_t, l, SMALL_WIDTH, GATE_WIDTH, BF)
        mla_c, st_ckv, st_kr = _mla_ctx(p_small, g_qnorm, g_kvnorm, w_qb_re, w_kvb, st_ckv, st_kr, l)
        mla_o = _mla_dec(p_small, cache_mla_ckv, cache_kr_pad, mla_tabs, g_qnorm, g_kvnorm, w_qb_re, w_kvb,
                         mla_c, l)
        diff_c, st_k, st_v = _diff_ctx(p_small, diff_lambda, g_diffnorm, st_k, st_v, l)
        diff_o = _diff_dec(p_small, cache_k, cache_v, diff_tabs, diff_lambda, g_diffnorm, diff_c, l)
        pool_o = _pool(p_small, w_pool_bf, pool_scale, l)
        merged = _merge(mla_o, pool_o, diff_o, gl, w_br_mla, w_br_pool, w_br_diff, l)
        x, h2t = _out_projection(x, merged, w_out, mod, g_norm2, l)
        routing = _route(h2t, w_peer_q, subkeys_bf, l)
        peer_t = _peer(h2t, peer_u, peer_v, routing, l)

    y_ctx, y_dec = _final_norm(x, peer_t, mod, g_final)
    return y_ctx.reshape(BATCH, SEQ, D_MODEL), y_dec.reshape(DEC_BATCH, DEC_SEQ, D_MODEL), st_ckv, st_kr, st_k, st_v
```
